```python
import jax, jax.numpy as jnp
from jax import lax
import numpy as np

D_MODEL = 4096
BATCH = 2
SEQ = 8192
DEPTH = 2

N_BRANCH = 4
BRANCH_W = D_MODEL // 4

MLA_NOPE = 128
MLA_ROPE = 64
MLA_V = 128
MLA_HEADS = BRANCH_W // MLA_V
MLA_Q_RANK = 768
MLA_KV_RANK = 256
ROPE_THETA = 10000.0

DSA_DH = 128
DSA_HEADS = BRANCH_W // DSA_DH
IDX_HEADS = 16
IDX_DIM = 64
IDX_TOPK_MAX = 256

GLA_DV = 256
GLA_HEADS = BRANCH_W // GLA_DV
GLA_DK = GLA_DV // 2
GLA_GATE_RANK = 16
GLA_TAU = 16.0
GLA_CHUNK = 64

RWKV_HS = 64
RWKV_HEADS = BRANCH_W // RWKV_HS
RWKV_LORA = 64
RWKV_MV_RANK = 32
RWKV_GN_EPS = 64e-5

N_EXPERTS = 64
TOP_K = 8
EXPERT_HIDDEN = 384
SHARED_HIDDEN = 384
ROUTED_SCALE = 2.5

ALPHA = (2 * DEPTH) ** 0.25
BETA = (8 * DEPTH) ** -0.25
LN_EPS = 1e-5
RMS_EPS = 1e-6
Q_BLOCK = 128

IN_SPLIT = (
    ('gate', N_BRANCH * D_MODEL),
    ('mla_cq', MLA_Q_RANK), ('mla_ckv', MLA_KV_RANK), ('mla_kr', MLA_ROPE),
    ('dsa_q', DSA_HEADS * DSA_DH), ('dsa_k', DSA_DH), ('dsa_v', DSA_DH),
    ('idx_q', IDX_HEADS * IDX_DIM), ('idx_k', IDX_DIM), ('idx_w', IDX_HEADS),
    ('gla_q', GLA_HEADS * GLA_DK), ('gla_k', GLA_HEADS * GLA_DK), ('gla_v', GLA_HEADS * GLA_DV),
    ('gla_a', GLA_GATE_RANK), ('gla_r', GLA_HEADS * GLA_DV),
    ('rwkv_r', BRANCH_W), ('rwkv_k', BRANCH_W), ('rwkv_v', BRANCH_W),
    ('rwkv_w', RWKV_LORA), ('rwkv_a', RWKV_LORA), ('rwkv_g', RWKV_LORA),
)
IN_TOTAL = sum(width for _, width in IN_SPLIT)

kernel_name = 'hybrid_mla_dsa_gla_rwkv7_moe_deepnorm'


def layer_norm(x, g, b, eps=LN_EPS):
    xf = x.astype(jnp.float32)
    mu = jnp.mean(xf, -1, keepdims=True)
    var = jnp.mean(jnp.square(xf - mu), -1, keepdims=True)
    return ((xf - mu) * lax.rsqrt(var + eps)).astype(x.dtype) * g + b


def rms_norm(x, g, eps=RMS_EPS):
    xf = x.astype(jnp.float32)
    return (xf * lax.rsqrt(jnp.mean(xf * xf, -1, keepdims=True) + eps)).astype(x.dtype) * g


def rope(x, positions):
    d = x.shape[-1]
    inv = ROPE_THETA ** (-jnp.arange(0, d, 2, dtype=jnp.float32) / d)
    ang = positions.astype(jnp.float32)[..., None] * inv
    ang = ang.reshape(ang.shape[:2] + (1,) * (x.ndim - 3) + (d // 2,))
    cos, sin = jnp.cos(ang).astype(x.dtype), jnp.sin(ang).astype(x.dtype)
    x1, x2 = x[..., : d // 2], x[..., d // 2:]
    return jnp.concatenate([x1 * cos - x2 * sin, x1 * sin + x2 * cos], -1)


def split_proj(proj):
    parts, off = {}, 0
    for name, width in IN_SPLIT:
        parts[name] = proj[..., off:off + width]
        off += width
    return parts


def token_shift(p):
    return jnp.pad(p[:, :-1], ((0, 0), (1, 0), (0, 0)))


def shift_lerp(p, mu):
    return p + (token_shift(p) - p) * mu


def causal_block_attention(q, k, v, scale):
    B, S, H, _ = q.shape
    nb = S // Q_BLOCK
    qb = q.reshape(B, nb, Q_BLOCK, H, q.shape[-1]).swapaxes(0, 1)
    kpos = jnp.arange(S)

    def one_block(args):
        i, qi = args
        qpos = i * Q_BLOCK + jnp.arange(Q_BLOCK)
        s = jnp.einsum('bqhd,bkhd->bhqk', qi, k).astype(jnp.float32) * scale
        s = jnp.where(kpos[None, :] <= qpos[:, None], s, -jnp.inf)
        p = jax.nn.softmax(s, axis=-1).astype(v.dtype)
        return jnp.einsum('bhqk,bkhd->bqhd', p, v)

    out = lax.map(one_block, (jnp.arange(nb), qb))
    return out.swapaxes(0, 1).reshape(B, S, H, v.shape[-1])


def mla_mixer(cq, ckv, kr, positions, q_norm, w_uq, kv_norm, w_ukv):
    B, S, _ = cq.shape
    q = (rms_norm(cq, q_norm) @ w_uq).reshape(B, S, MLA_HEADS, MLA_NOPE + MLA_ROPE)
    q = jnp.concatenate([q[..., :MLA_NOPE], rope(q[..., MLA_NOPE:], positions)], -1)
    kv = (rms_norm(ckv, kv_norm) @ w_ukv).reshape(B, S, MLA_HEADS, MLA_NOPE + MLA_V)
    k_rope = jnp.broadcast_to(rope(kr, positions)[:, :, None, :], (B, S, MLA_HEADS, MLA_ROPE))
    k = jnp.concatenate([kv[..., :MLA_NOPE], k_rope], -1)
    o = causal_block_attention(q, k, kv[..., MLA_NOPE:], (MLA_NOPE + MLA_ROPE) ** -0.5)
    return o.reshape(B, S, MLA_HEADS * MLA_V)


def dsa_mixer(q, k, v, iq, ik, iw):
    B, S, _ = q.shape
    n_sel = min(IDX_TOPK_MAX, S // 4)
    nb = S // Q_BLOCK
    blocks = lambda t: t.reshape((B, nb, Q_BLOCK) + t.shape[2:]).swapaxes(0, 1)
    q = q.reshape(B, S, DSA_HEADS, DSA_DH)
    iq = iq.reshape(B, S, IDX_HEADS, IDX_DIM)
    kpos = jnp.arange(S)
    gather = jax.vmap(lambda seq, idx: seq[idx])

    def one_block(args):
        i, qb, iqb, iwb = args
        qpos = i * Q_BLOCK + jnp.arange(Q_BLOCK)
        logits = jnp.einsum('bqhd,bsd->bqhs', iqb, ik).astype(jnp.float32) * IDX_DIM ** -0.5
        score = jnp.einsum('bqh,bqhs->bqs', iwb.astype(jnp.float32) * IDX_HEADS ** -0.5, jax.nn.relu(logits))
        score = jnp.where((kpos[None, :] <= qpos[:, None])[None], score, -jnp.inf)
        _, idx = lax.top_k(score, n_sel)
        valid = idx <= qpos[None, :, None]
        k_sel = gather(k, idx)
        v_sel = gather(v, idx)
        s = jnp.einsum('bqhd,bqnd->bhqn', qb, k_sel).astype(jnp.float32) * DSA_DH ** -0.5
        s = jnp.where(valid[:, None], s, -jnp.inf)
        p = jax.nn.softmax(s, axis=-1).astype(v.dtype)
        return jnp.einsum('bhqn,bqnd->bqhd', p, v_sel)

    out = lax.map(one_block, (jnp.arange(nb), blocks(q), blocks(iq), blocks(iw)))
    return out.swapaxes(0, 1).reshape(B, S, BRANCH_W)


def gla_chunked(q, k, v, log_a):
    B, S, H, dk = q.shape
    dv = v.shape[-1]
    C = GLA_CHUNK
    nc = S // C
    to_chunks = lambda t: t.astype(jnp.float32).reshape(B, nc, C, H, t.shape[-1]).transpose(1, 0, 3, 2, 4)
    qc, kc, vc, gc = to_chunks(q * dk ** -0.5), to_chunks(k), to_chunks(v), to_chunks(log_a)
    causal = jnp.tril(jnp.ones((C, C), dtype=bool))

    def step(state, inp):
        qi, ki, vi, gi = inp
        b = jnp.cumsum(gi, axis=-2)
        rel = b[..., :, None, :] - b[..., None, :, :]
        decay = jnp.exp(jnp.where(causal[:, :, None], rel, -jnp.inf))
        attn = jnp.einsum('bhtk,bhsk,bhtsk->bhts', qi, ki, decay)
        o = jnp.einsum('bhts,bhsv->bhtv', attn, vi) + jnp.einsum('bhtk,bhkv->bhtv', qi * jnp.exp(b), state)
        b_last = b[..., -1:, :]
        state = jnp.exp(b_last[..., 0, :])[..., None] * state + jnp.einsum('bhsk,bhsv->bhkv', ki * jnp.exp(b_last - b), vi)
        return state, o

    _, o = lax.scan(step, jnp.zeros((B, H, dk, dv), jnp.float32), (qc, kc, vc, gc))
    return o.transpose(1, 0, 3, 2, 4).reshape(B, S, H, dv).astype(v.dtype)


def gla_mixer(q, k, v, ga, r, w_gate2, b_gate, norm_g):
    B, S, _ = q.shape
    heads = lambda t, d: t.reshape(B, S, GLA_HEADS, d)
    log_a = jax.nn.log_sigmoid((ga @ w_gate2 + b_gate).astype(jnp.float32)) / GLA_TAU
    o = gla_chunked(heads(q, GLA_DK), heads(k, GLA_DK), heads(v, GLA_DV), heads(log_a, GLA_DK))
    o = rms_norm(o, norm_g.reshape(GLA_HEADS, GLA_DV))
    return o.reshape(B, S, BRANCH_W) * jax.nn.silu(r)


def rwkv7_scan(r, w, k, v, kk, a):
    B, S, H, N = r.shape
    seq_first = lambda t: t.astype(jnp.float32).transpose(1, 0, 2, 3)

    def step(state, inp):
        rt, wt, kt, vt, kkt, at = inp
        sa = jnp.einsum('bhvk,bhk->bhv', state, -kkt)
        state = state * wt[:, :, None, :] + sa[..., None] * (kkt * at)[:, :, None, :] + vt[..., None] * kt[:, :, None, :]
        return state, jnp.einsum('bhvk,bhk->bhv', state, rt)

    xs = (seq_first(r), seq_first(w), seq_first(k), seq_first(v), seq_first(kk), seq_first(a))
    _, y = lax.scan(step, jnp.zeros((B, H, N, N), jnp.float32), xs)
    return y.transpose(1, 0, 2, 3).astype(r.dtype)


def rwkv7_mixer(pr, pk, pv, pw, pa, pg, mu_rkv, mu_lora, w0, w2, a0, a2, g2, k_k, k_a, r_k, lnx_g, lnx_b, v_first, vres):
    B, S, _ = pr.shape
    H, N = RWKV_HEADS, RWKV_HS
    heads = lambda t: t.reshape(B, S, H, N)
    r = shift_lerp(pr, mu_rkv[0])
    k = shift_lerp(pk, mu_rkv[1])
    v = shift_lerp(pv, mu_rkv[2])
    xw = shift_lerp(pw, mu_lora[0])
    xa = shift_lerp(pa, mu_lora[1])
    xg = shift_lerp(pg, mu_lora[2])
    w_log = -jax.nn.softplus(-(w0 + jnp.tanh(xw) @ w2).astype(jnp.float32)) - 0.5
    decay = jnp.exp(-jnp.exp(w_log))
    a = jax.nn.sigmoid(a0 + xa @ a2)
    g = jax.nn.sigmoid(xg) @ g2
    if vres is None:
        v_first = v
    else:
        v0, v1, v2 = vres
        v = v + (v_first - v) * jax.nn.sigmoid(v0 + (v @ v1) @ v2)
    kk = heads(k * k_k)
    kkf = kk.astype(jnp.float32)
    kk = (kkf * lax.rsqrt(jnp.maximum(jnp.sum(kkf * kkf, -1, keepdims=True), 1e-24))).astype(k.dtype)
    k = k * (1.0 + (a - 1.0) * k_a)
    rh, kh, vh = heads(r), heads(k), heads(v)
    y = rwkv7_scan(rh, heads(decay), kh, vh, kk, heads(a))
    y = layer_norm(y, lnx_g.reshape(H, N), lnx_b.reshape(H, N), eps=RWKV_GN_EPS)
    y = y + jnp.sum(rh * kh * r_k, -1, keepdims=True) * vh
    return y.reshape(B, S, BRANCH_W) * g, v_first


def swiglu(t, wg, wu, wd):
    return (jax.nn.silu(t @ wg) * (t @ wu)) @ wd


def moe(h, router_w, router_bias, w_gate, w_up, w_down, s_gate, s_up, s_down):
    B, S, D = h.shape
    t = h.reshape(B * S, D)
    scores = jax.nn.sigmoid((t @ router_w).astype(jnp.float32))
    _, top_i = lax.top_k(scores + router_bias, TOP_K)
    top_s = jnp.take_along_axis(scores, top_i, axis=-1)
    top_w = top_s / jnp.sum(top_s, -1, keepdims=True) * ROUTED_SCALE
    combine = jnp.einsum('tk,tke->te', top_w, jax.nn.one_hot(top_i, N_EXPERTS, dtype=jnp.float32)).astype(h.dtype)
    out = swiglu(t, s_gate, s_up, s_down)
    for e in range(N_EXPERTS):
        out = out + combine[:, e:e + 1] * swiglu(t, w_gate[e], w_up[e], w_down[e])
    return out.reshape(B, S, D)


def setup_inputs(seed: int = 0) -> dict:
    key = jax.random.key(seed)
    ks = iter(jax.random.split(key, 64))
    nrm = lambda shape, scale: jax.random.normal(next(ks), shape, jnp.float32) * scale
    unif = lambda shape, lo, hi: jax.random.uniform(next(ks), shape, jnp.float32, lo, hi)
    L, D, W = DEPTH, D_MODEL, BRANCH_W
    return {
        'x': nrm((BATCH, SEQ, D), 1.0),
        'positions': jnp.broadcast_to(jnp.arange(SEQ, dtype=jnp.int32), (BATCH, SEQ)),
        'ln_in_g': 1.0 + nrm((D,), 0.02),
        'ln_in_b': nrm((D,), 0.02),
        'w_in': nrm((L, D, IN_TOTAL), D ** -0.5),
        'w_branch': nrm((L, N_BRANCH, W, D), BETA * W ** -0.5),
        'w_out': nrm((L, D, D), BETA * D ** -0.5),
        'mla_q_norm': 1.0 + nrm((L, MLA_Q_RANK), 0.02),
        'mla_w_uq': nrm((L, MLA_Q_RANK, MLA_HEADS * (MLA_NOPE + MLA_ROPE)), MLA_Q_RANK ** -0.5),
        'mla_kv_norm': 1.0 + nrm((L, MLA_KV_RANK), 0.02),
        'mla_w_ukv': nrm((L, MLA_KV_RANK, MLA_HEADS * (MLA_NOPE + MLA_V)), MLA_KV_RANK ** -0.5),
        'gla_w_gate2': nrm((L, GLA_GATE_RANK, GLA_HEADS * GLA_DK), GLA_GATE_RANK ** -0.5),
        'gla_b_gate': nrm((L, GLA_HEADS * GLA_DK), 0.1),
        'gla_norm_g': 1.0 + nrm((L, W), 0.02),
        'rwkv_mu_rkv': unif((L, 3, W), 0.0, 1.0),
        'rwkv_mu_lora': unif((L, 3, RWKV_LORA), 0.0, 1.0),
        'rwkv_w0': unif((L, W), -6.0, -1.0),
        'rwkv_w2': nrm((L, RWKV_LORA, W), 0.1 * RWKV_LORA ** -0.5),
        'rwkv_a0': nrm((L, W), 0.1),
        'rwkv_a2': nrm((L, RWKV_LORA, W), RWKV_LORA ** -0.5),
        'rwkv_g2': nrm((L, RWKV_LORA, W), RWKV_LORA ** -0.5),
        'rwkv_k_k': 0.85 + nrm((L, W), 0.02),
        'rwkv_k_a': 1.0 + nrm((L, W), 0.02),
        'rwkv_r_k': nrm((L, RWKV_HEADS, RWKV_HS), 0.1),
        'rwkv_lnx_g': 1.0 + nrm((L, W), 0.02),
        'rwkv_lnx_b': nrm((L, W), 0.02),
        'rwkv_v0': 1.0 + nrm((L - 1, W), 0.1),
        'rwkv_v1': nrm((L - 1, W, RWKV_MV_RANK), W ** -0.5),
        'rwkv_v2': nrm((L - 1, RWKV_MV_RANK, W), RWKV_MV_RANK ** -0.5),
        'ln_mix_g': 1.0 + nrm((L, D), 0.02),
        'ln_mix_b': nrm((L, D), 0.02),
        'router_w': nrm((L, D, N_EXPERTS), D ** -0.5),
        'router_bias': nrm((L, N_EXPERTS), 0.01),
        'exp_w_gate': nrm((L, N_EXPERTS, D, EXPERT_HIDDEN), D ** -0.5),
        'exp_w_up': nrm((L, N_EXPERTS, D, EXPERT_HIDDEN), D ** -0.5),
        'exp_w_down': nrm((L, N_EXPERTS, EXPERT_HIDDEN, D), BETA * EXPERT_HIDDEN ** -0.5),
        'sh_w_gate': nrm((L, D, SHARED_HIDDEN), D ** -0.5),
        'sh_w_up': nrm((L, D, SHARED_HIDDEN), D ** -0.5),
        'sh_w_down': nrm((L, SHARED_HIDDEN, D), BETA * SHARED_HIDDEN ** -0.5),
        'ln_ffn_g': 1.0 + nrm((L, D), 0.02),
        'ln_ffn_b': nrm((L, D), 0.02),
    }


def reference(x, positions, ln_in_g, ln_in_b, w_in, w_branch, w_out, mla_q_norm, mla_w_uq, mla_kv_norm, mla_w_ukv,
              gla_w_gate2, gla_b_gate, gla_norm_g, rwkv_mu_rkv, rwkv_mu_lora, rwkv_w0, rwkv_w2, rwkv_a0, rwkv_a2,
              rwkv_g2, rwkv_k_k, rwkv_k_a, rwkv_r_k, rwkv_lnx_g, rwkv_lnx_b, rwkv_v0, rwkv_v1, rwkv_v2,
              ln_mix_g, ln_mix_b, router_w, router_bias, exp_w_gate, exp_w_up, exp_w_down,
              sh_w_gate, sh_w_up, sh_w_down, ln_ffn_g, ln_ffn_b):
    B, S, D = x.shape
    x = layer_norm(x, ln_in_g, ln_in_b)
    v_first = None
    for l in range(DEPTH):
        p = split_proj(x @ w_in[l])
        y_mla = mla_mixer(p['mla_cq'], p['mla_ckv'], p['mla_kr'], positions,
                          mla_q_norm[l], mla_w_uq[l], mla_kv_norm[l], mla_w_ukv[l])
        y_dsa = dsa_mixer(p['dsa_q'], p['dsa_k'], p['dsa_v'], p['idx_q'], p['idx_k'], p['idx_w'])
        y_gla = gla_mixer(p['gla_q'], p['gla_k'], p['gla_v'], p['gla_a'], p['gla_r'],
                          gla_w_gate2[l], gla_b_gate[l], gla_norm_g[l])
        vres = None if l == 0 else (rwkv_v0[l - 1], rwkv_v1[l - 1], rwkv_v2[l - 1])
        y_rwkv, v_first = rwkv7_mixer(p['rwkv_r'], p['rwkv_k'], p['rwkv_v'], p['rwkv_w'], p['rwkv_a'], p['rwkv_g'],
                                      rwkv_mu_rkv[l], rwkv_mu_lora[l], rwkv_w0[l], rwkv_w2[l], rwkv_a0[l],
                                      rwkv_a2[l], rwkv_g2[l], rwkv_k_k[l], rwkv_k_a[l], rwkv_r_k[l],
                                      rwkv_lnx_g[l], rwkv_lnx_b[l], v_first, vres)
        merged = jnp.zeros_like(x)
        for i, y in enumerate((y_mla, y_dsa, y_gla, y_rwkv)):
            gate = jax.nn.sigmoid(p['gate'][..., i * D:(i + 1) * D])
            merged = merged + gate * (y @ w_branch[l, i])
        x = layer_norm(ALPHA * x + merged @ w_out[l], ln_mix_g[l], ln_mix_b[l])
        f = moe(x, router_w[l], router_bias[l], exp_w_gate[l], exp_w_up[l], exp_w_down[l],
                sh_w_gate[l], sh_w_up[l], sh_w_down[l])
        x = layer_norm(ALPHA * x + f, ln_ffn_g[l], ln_ffn_b[l])
    return x
```

```python
import functools
import math

import jax
import jax.numpy as jnp
from jax import lax
from jax.experimental import pallas as pl
from jax.experimental.pallas import tpu as pltpu

F32 = jnp.float32
BF16 = jnp.bfloat16

LANES = 128

MLA_NOPE, MLA_ROPE, MLA_V = 128, 64, 128
MLA_Q_RANK, MLA_KV_RANK = 768, 256
ROPE_THETA = 10000.0
DSA_DH = 128
IDX_HEADS, IDX_DIM, IDX_TOPK_MAX = 16, 64, 256
GLA_DV, GLA_DK, GLA_GATE_RANK, GLA_TAU, GLA_CHUNK = 256, 128, 16, 16.0, 64
GLA_SUB = 16
RWKV_HS, RWKV_LORA, RWKV_GN_EPS = 64, 64, 64e-5
RWKV_CHUNK = 64
TOP_K, ROUTED_SCALE = 8, 2.5
LN_EPS, RMS_EPS = 1e-5, 1e-6
NEG_BIG = -1e30

VMEM_LIMIT = 56 * 1024 * 1024


def _cparams(*sem):
    return pltpu.CompilerParams(dimension_semantics=sem, vmem_limit_bytes=VMEM_LIMIT)


def _tile(n, pref, unit=LANES):
    if n <= pref:
        return n
    t = (pref // unit) * unit
    while t > unit and n % t:
        t -= unit
    assert n % t == 0, (n, pref, unit)
    return t


def _dot(a, b):
    return jnp.dot(a, b, preferred_element_type=F32)


def _dot_t(a, b):
    return lax.dot_general(a, b, (((1,), (1,)), ((), ())), preferred_element_type=F32)


def _dot_f32(a, b):
    return jnp.dot(a, b, preferred_element_type=F32, precision=lax.Precision.HIGHEST)


def _dot_t_f32(a, b):
    return lax.dot_general(a, b, (((1,), (1,)), ((), ())), preferred_element_type=F32,
                           precision=lax.Precision.HIGHEST)


def _dot_tl_f32(a, b):
    return lax.dot_general(a, b, (((0,), (0,)), ((), ())), preferred_element_type=F32,
                           precision=lax.Precision.HIGHEST)


def _mm_kernel(a_ref, b_ref, o_ref):
    o_ref[...] = _dot(a_ref[...], b_ref[...]).astype(o_ref.dtype)


def _matmul(a, b, out_dtype, tm=1024, tn=512):
    m, k = a.shape
    n = b.shape[1]
    tm, tn = _tile(m, tm, 8), _tile(n, tn)
    return pl.pallas_call(
        _mm_kernel,
        grid=(m // tm, n // tn),
        in_specs=[pl.BlockSpec((tm, k), lambda i, j: (i, 0)), pl.BlockSpec((k, tn), lambda i, j: (0, j))],
        out_specs=pl.BlockSpec((tm, tn), lambda i, j: (i, j)),
        out_shape=jax.ShapeDtypeStruct((m, n), out_dtype),
        compiler_params=_cparams("parallel", "parallel"),
        name="matmul",
    )(a, b)


def _ln_kernel(*refs, alpha, has_res):
    if has_res:
        x_ref, f_ref, g_ref, b_ref, o_ref, ob_ref = refs
        x = alpha * x_ref[...] + f_ref[...]
    else:
        x_ref, g_ref, b_ref, o_ref, ob_ref = refs
        x = x_ref[...]
    mu = jnp.mean(x, axis=-1, keepdims=True)
    xc = x - mu
    var = jnp.mean(xc * xc, axis=-1, keepdims=True)
    y = xc * lax.rsqrt(var + LN_EPS) * g_ref[...] + b_ref[...]
    o_ref[...] = y
    ob_ref[...] = y.astype(BF16)


def _layer_norm(x, f, g, b, alpha=1.0):
    t, d = x.shape
    tm = _tile(t, 128, 8)
    row = pl.BlockSpec((tm, d), lambda i: (i, 0))
    vec = pl.BlockSpec((1, d), lambda i: (0, 0))
    has_res = f is not None
    args = (x, f) if has_res else (x,)
    return pl.pallas_call(
        functools.partial(_ln_kernel, alpha=alpha, has_res=has_res),
        grid=(t // tm,),
        in_specs=[row] * len(args) + [vec, vec],
        out_specs=[row, row],
        out_shape=[jax.ShapeDtypeStruct((t, d), F32), jax.ShapeDtypeStruct((t, d), BF16)],
        compiler_params=_cparams("parallel"),
        name="layer_norm",
    )(*args, g.reshape(1, d), b.reshape(1, d))


def _pad_cols(w, width):
    return jnp.pad(w, ((0, 0), (0, width - w.shape[1])))


def _rms(x, g):
    return x * lax.rsqrt(jnp.mean(x * x, axis=-1, keepdims=True) + RMS_EPS) * g


def _mla_prep_kernel(lat_ref, ct_ref, st_ref, qg_ref, kg_ref, wqn_ref, wqr_ref, wqs_ref, wkn_ref, wv_ref,
                     qn_ref, qr_ref, kn_ref, kr_ref, v_ref, *, heads, scale):
    lat = lat_ref[...]
    cq = lat[:, :MLA_Q_RANK]
    ckv = lat[:, MLA_Q_RANK:MLA_Q_RANK + MLA_KV_RANK]
    kr = lat[:, MLA_Q_RANK + MLA_KV_RANK:MLA_Q_RANK + MLA_KV_RANK + LANES]
    krs = lat[:, MLA_Q_RANK + MLA_KV_RANK + LANES:MLA_Q_RANK + MLA_KV_RANK + 2 * LANES]
    ct, st = ct_ref[...], st_ref[...]
    nq = _rms(cq, qg_ref[...]).astype(BF16)
    nkv = _rms(ckv, kg_ref[...]).astype(BF16)
    qn_ref[...] = (_dot(nq, wqn_ref[...]) * scale).astype(BF16)
    cth = jnp.concatenate([ct] * heads, axis=1)
    sth = jnp.concatenate([st] * heads, axis=1)
    qr_ref[...] = ((_dot(nq, wqr_ref[...]) * cth + _dot(nq, wqs_ref[...]) * sth) * scale).astype(BF16)
    kn_ref[...] = _dot(nkv, wkn_ref[...]).astype(BF16)
    v_ref[...] = _dot(nkv, wv_ref[...]).astype(BF16)
    kr_ref[...] = (kr * ct + krs * st).astype(BF16)


def _flash_kernel(qn_ref, qr_ref, kn_ref, kr_ref, v_ref, o_ref, m_ref, l_ref, acc_ref, *, tq, tk):
    i, j = pl.program_id(2), pl.program_id(3)

    @pl.when(j == 0)
    def _():
        m_ref[...] = jnp.full_like(m_ref, NEG_BIG)
        l_ref[...] = jnp.zeros_like(l_ref)
        acc_ref[...] = jnp.zeros_like(acc_ref)

    def step(masked):
        q = jnp.concatenate([qn_ref[...], qr_ref[...]], axis=1)
        k = jnp.concatenate([kn_ref[...], kr_ref[...]], axis=1)
        s = _dot_t(q, k)
        if masked:
            qpos = i * tq + lax.broadcasted_iota(jnp.int32, (tq, tk), 0)
            kpos = j * tk + lax.broadcasted_iota(jnp.int32, (tq, tk), 1)
            s = jnp.where(kpos <= qpos, s, -jnp.inf)
        m_prev = m_ref[...]
        m_new = jnp.maximum(m_prev, jnp.max(s, axis=1, keepdims=True))
        alpha = jnp.exp(m_prev - m_new)
        p = jnp.exp(s - jnp.concatenate([m_new] * (tk // LANES), axis=1))
        l_ref[...] = alpha * l_ref[...] + jnp.sum(p, axis=1, keepdims=True)
        acc_ref[...] = alpha * acc_ref[...] + _dot(p.astype(BF16), v_ref[...])
        m_ref[...] = m_new

    below = j * tk + tk - 1 <= i * tq
    touches = j * tk <= i * tq + tq - 1

    @pl.when(below)
    def _():
        step(False)

    @pl.when(jnp.logical_and(touches, jnp.logical_not(below)))
    def _():
        step(True)

    @pl.when(j == pl.num_programs(3) - 1)
    def _():
        o_ref[...] = (acc_ref[...] / l_ref[...]).astype(o_ref.dtype)


def _rope_tables(positions):
    half = MLA_ROPE // 2
    inv = ROPE_THETA ** (-jnp.arange(0, MLA_ROPE, 2, dtype=F32) / MLA_ROPE)
    ang = positions.reshape(-1).astype(F32)[:, None] * inv
    cos, sin = jnp.cos(ang), jnp.sin(ang)
    zero = jnp.zeros((ang.shape[0], LANES - 2 * half), F32)
    return jnp.concatenate([cos, cos, zero], 1), jnp.concatenate([-sin, sin, zero], 1)


def _swap_halves(w):
    half = w.shape[-1] // 2
    return jnp.concatenate([w[..., half:], w[..., :half]], -1)


def _mla(lat, ct, st, q_norm, w_uq, kv_norm, w_ukv, batch, seq):
    t = lat.shape[0]
    heads = w_uq.shape[1] // (MLA_NOPE + MLA_ROPE)
    hw = heads * LANES
    wq = w_uq.reshape(MLA_Q_RANK, heads, MLA_NOPE + MLA_ROPE)
    wqn = wq[:, :, :MLA_NOPE].reshape(MLA_Q_RANK, hw).astype(BF16)
    rope_pad = ((0, 0), (0, 0), (0, LANES - MLA_ROPE))
    wqr = jnp.pad(wq[:, :, MLA_NOPE:], rope_pad).reshape(MLA_Q_RANK, hw).astype(BF16)
    wqs = jnp.pad(_swap_halves(wq[:, :, MLA_NOPE:]), rope_pad).reshape(MLA_Q_RANK, hw).astype(BF16)
    wkv = w_ukv.reshape(MLA_KV_RANK, heads, MLA_NOPE + MLA_V)
    wkn = wkv[:, :, :MLA_NOPE].reshape(MLA_KV_RANK, hw).astype(BF16)
    wv = wkv[:, :, MLA_NOPE:].reshape(MLA_KV_RANK, hw).astype(BF16)

    tm = _tile(t, 512, 8)
    row = lambda w: pl.BlockSpec((tm, w), lambda i: (i, 0))
    full = lambda a: pl.BlockSpec(a.shape, lambda i: (0, 0))
    qg, kg = q_norm.reshape(1, -1), kv_norm.reshape(1, -1)
    qn, qr, kn, kr, v = pl.pallas_call(
        functools.partial(_mla_prep_kernel, heads=heads, scale=(MLA_NOPE + MLA_ROPE) ** -0.5),
        grid=(t // tm,),
        in_specs=[row(lat.shape[1]), row(LANES), row(LANES), full(qg), full(kg),
                  full(wqn), full(wqr), full(wqs), full(wkn), full(wv)],
        out_specs=[row(hw), row(hw), row(hw), row(LANES), row(hw)],
        out_shape=[jax.ShapeDtypeStruct((t, hw), BF16)] * 3 + [jax.ShapeDtypeStruct((t, LANES), BF16),
                                                               jax.ShapeDtypeStruct((t, hw), BF16)],
        compiler_params=_cparams("parallel"),
        name="mla_prep",
    )(lat, ct, st, qg, kg, wqn, wqr, wqs, wkn, wv)

    tq = tk = _tile(seq, 1024, 8)
    nq, nk = seq // tq, seq // tk
    r3 = lambda a: a.reshape(batch, seq, a.shape[1])
    last = lambda i, j: jnp.minimum(j, ((i + 1) * tq - 1) // tk)
    q_spec = pl.BlockSpec((None, tq, LANES), lambda b, h, i, j: (b, i, h))
    k_spec = pl.BlockSpec((None, tk, LANES), lambda b, h, i, j: (b, last(i, j), h))
    kr_spec = pl.BlockSpec((None, tk, LANES), lambda b, h, i, j: (b, last(i, j), 0))
    out = pl.pallas_call(
        functools.partial(_flash_kernel, tq=tq, tk=tk),
        grid=(batch, heads, nq, nk),
        in_specs=[q_spec, q_spec, k_spec, kr_spec, k_spec],
        out_specs=q_spec,
        out_shape=jax.ShapeDtypeStruct((batch, seq, hw), BF16),
        scratch_shapes=[pltpu.VMEM((tq, LANES), F32), pltpu.VMEM((tq, LANES), F32), pltpu.VMEM((tq, LANES), F32)],
        compiler_params=_cparams("parallel", "parallel", "parallel", "arbitrary"),
        name="mla_flash",
    )(r3(qn), r3(qr), r3(kn), r3(kr), r3(v))
    return out.reshape(t, hw)


DSA_TQ = 128
DSA_TK = 512
INT_MIN = -2 ** 31


def _dsa_kernel(iq_ref, q_ref, k_ref, v_ref, ik_ref, iw_ref, o_ref, key_ref, m_ref, l_ref, acc_ref,
                *, tq, tk, n_sel, heads, pos_bits):
    i = pl.program_id(1)
    n_kt = (i * tq + tq - 1) // tk + 1
    reps = tk // LANES
    wide = lambda a: jnp.concatenate([a] * reps, axis=1)
    qpos = i * tq + lax.broadcasted_iota(jnp.int32, (tq, tk), 0)
    lane_pos = lax.broadcasted_iota(jnp.int32, (tq, tk), 1)

    iq = iq_ref[...]
    iw = iw_ref[...].astype(F32) * IDX_HEADS ** -0.5
    iw_b = [jnp.broadcast_to(iw[:, h:h + 1], (tq, LANES)) for h in range(IDX_HEADS)]

    def score_tile(j, carry):
        ikj = ik_ref[pl.ds(pl.multiple_of(j * tk, tk), tk), :]
        sc = jnp.zeros((tq, tk), F32)
        for h in range(IDX_HEADS):
            logit = _dot_t(iq[:, h * LANES:(h + 1) * LANES], ikj) * IDX_DIM ** -0.5
            sc = sc + wide(iw_b[h]) * jnp.maximum(logit, 0.0)
        sc = jnp.where(j * tk + lane_pos <= qpos, sc + 0.0, -jnp.inf)
        bits = lax.bitcast_convert_type(sc, jnp.int32)
        key_ref[j] = jnp.where(bits >= 0, bits, bits ^ 0x7FFFFFFF)
        return carry

    lax.fori_loop(0, n_kt, score_tile, 0)

    def count(pred):
        def body(j, acc):
            c = pred(key_ref[j], j * tk + lane_pos).astype(jnp.int32)
            for rep in range(reps):
                acc = acc + c[:, rep * LANES:(rep + 1) * LANES]
            return acc
        acc = lax.fori_loop(0, n_kt, body, jnp.zeros((tq, LANES), jnp.int32))
        return jnp.broadcast_to(jnp.sum(acc, axis=1, keepdims=True), (tq, LANES))

    def thr_bit(bit_i, thr):
        cand = thr ^ jnp.left_shift(jnp.int32(1), 31 - bit_i)
        cnt = count(lambda key, pos: key >= wide(cand))
        return jnp.where(cnt >= n_sel, cand, thr)

    thr = lax.fori_loop(0, 32, thr_bit, jnp.full((tq, LANES), INT_MIN, jnp.int32))
    cnt_gt = count(lambda key, pos: key > wide(thr))
    cnt_ge = count(lambda key, pos: key >= wide(thr))
    need = n_sel - cnt_gt

    def tie_bit(bit_i, cut):
        cand = cut | jnp.left_shift(jnp.int32(1), pos_bits - 1 - bit_i)
        cnt = count(lambda key, pos: jnp.logical_and(key == wide(thr), pos < wide(cand)))
        return jnp.where(cnt < need, cand, cut)

    surplus = jnp.max(cnt_ge - cnt_gt - need) > 0
    cut = lax.cond(surplus,
                   lambda: lax.fori_loop(0, pos_bits, tie_bit, jnp.zeros((tq, LANES), jnp.int32)),
                   lambda: jnp.full((tq, LANES), 2 ** 31 - 1, jnp.int32))

    q = q_ref[...]
    q_all = jnp.concatenate([q[:, h * LANES:(h + 1) * LANES] for h in range(heads)], axis=0)
    m_ref[...] = jnp.full_like(m_ref, NEG_BIG)
    l_ref[...] = jnp.zeros_like(l_ref)
    acc_ref[...] = jnp.zeros_like(acc_ref)

    def attend(j, carry):
        rows = pl.ds(pl.multiple_of(j * tk, tk), tk)
        s = _dot_t(q_all, k_ref[rows, :]) * DSA_DH ** -0.5
        key, pos = key_ref[j], j * tk + lane_pos
        keep = jnp.where(key > wide(thr), 0.0,
                         jnp.where(jnp.logical_and(key == wide(thr), pos <= wide(cut)), 0.0, -jnp.inf))
        keep = jnp.where(pos <= qpos, keep, -jnp.inf)
        s = s + jnp.concatenate([keep] * heads, axis=0)
        m_prev = m_ref[...]
        m_new = jnp.maximum(m_prev, jnp.max(s, axis=1, keepdims=True))
        alpha = jnp.exp(m_prev - m_new)
        p = jnp.exp(s - jnp.concatenate([m_new] * reps, axis=1))
        l_ref[...] = alpha * l_ref[...] + jnp.sum(p, axis=1, keepdims=True)
        acc_ref[...] = alpha * acc_ref[...] + _dot(p.astype(BF16), v_ref[rows, :])
        m_ref[...] = m_new
        return carry

    lax.fori_loop(0, n_kt, attend, 0)
    out = acc_ref[...] / l_ref[...]
    o_ref[...] = jnp.concatenate([out[h * tq:(h + 1) * tq] for h in range(heads)], axis=1).astype(o_ref.dtype)


def _dsa(pb, batch, seq, width):
    t, wb = pb.shape
    heads = width // DSA_DH
    iqw = IDX_HEADS * LANES
    tq, tk = _tile(seq, DSA_TQ, 8), _tile(seq, DSA_TK)
    n_sel = min(IDX_TOPK_MAX, seq // 4)
    c0 = (iqw + width) // LANES
    pb3 = pb.reshape(batch, seq, wb)
    qblk = lambda w, idx: pl.BlockSpec((None, tq, w), lambda b, i: (b, i, idx))
    seqblk = lambda idx: pl.BlockSpec((None, seq, LANES), lambda b, i: (b, 0, idx))
    out = pl.pallas_call(
        functools.partial(_dsa_kernel, tq=tq, tk=tk, n_sel=n_sel, heads=heads,
                          pos_bits=max(1, (seq - 1).bit_length())),
        grid=(batch, seq // tq),
        in_specs=[qblk(iqw, 0), qblk(width, iqw // width), seqblk(c0), seqblk(c0 + 1), seqblk(c0 + 2),
                  qblk(LANES, c0 + 3)],
        out_specs=qblk(width, 0),
        out_shape=jax.ShapeDtypeStruct((batch, seq, width), BF16),
        scratch_shapes=[pltpu.VMEM((seq // tk, tq, tk), jnp.int32), pltpu.VMEM((heads * tq, LANES), F32),
                        pltpu.VMEM((heads * tq, LANES), F32), pltpu.VMEM((heads * tq, LANES), F32)],
        compiler_params=_cparams("parallel", "arbitrary"),
        name="dsa",
    )(pb3, pb3, pb3, pb3, pb3, pb3)
    return out.reshape(t, width)


def _gla_kernel(q_ref, k_ref, v_ref, a_ref, r_ref, wg_ref, bg_ref, ng_ref, o_ref, state_ref):
    c_len, sub = GLA_CHUNK, GLA_SUB

    @pl.when(pl.program_id(2) == 0)
    def _():
        state_ref[...] = jnp.zeros_like(state_ref)

    q = q_ref[...] * GLA_DK ** -0.5
    k = k_ref[...]
    vb = v_ref[...].astype(BF16)
    z = _dot(a_ref[...].astype(BF16), wg_ref[...]) + bg_ref[...]
    g = jax.nn.log_sigmoid(z) / GLA_TAU
    ri = lax.broadcasted_iota(jnp.int32, (c_len, c_len), 0)
    ci = lax.broadcasted_iota(jnp.int32, (c_len, c_len), 1)
    b = _dot_f32((ri >= ci).astype(F32), g)
    state_t = state_ref[...]
    o_inter = _dot_t((q * jnp.exp(b)).astype(BF16), state_t.astype(BF16))

    ones = jnp.ones((GLA_DK, LANES), F32)
    trow = lax.broadcasted_iota(jnp.int32, (sub, GLA_DK), 0)
    lane = lax.broadcasted_iota(jnp.int32, (sub, LANES), 1)
    outs = []
    for blk in range(c_len // sub):
        lo = blk * sub
        qi, bi, ki = q[lo:lo + sub], b[lo:lo + sub], k[lo:lo + sub]
        rows = [qi * ki[s:s + 1] * jnp.exp(jnp.where(trow >= s, bi - bi[s:s + 1], -jnp.inf)) for s in range(sub)]
        sums = _dot_f32(jnp.concatenate(rows, axis=0), ones)
        attn = jnp.zeros((sub, LANES), F32)
        for s in range(sub):
            attn = attn + jnp.where(lane == s, sums[s * sub:(s + 1) * sub], 0.0)
        o_blk = _dot(attn[:, :sub].astype(BF16), vb[lo:lo + sub])
        if blk:
            qa = qi * jnp.exp(bi - bi[0:1])
            ka = k[:lo] * jnp.exp(bi[0:1] - b[:lo])
            o_blk = o_blk + _dot(_dot_t(qa.astype(BF16), ka.astype(BF16)).astype(BF16), vb[:lo])
        outs.append(o_blk)
    o = o_inter + jnp.concatenate(outs, axis=0)

    b_last = b[c_len - 1:c_len]
    k_dec = (k * jnp.exp(b_last - b)).astype(BF16)
    state_ref[...] = state_t * jnp.exp(b_last) + lax.dot_general(
        vb, k_dec, (((0,), (0,)), ((), ())), preferred_element_type=F32)

    r = r_ref[...]
    o_ref[...] = (_rms(o, ng_ref[...]) * (r * jax.nn.sigmoid(r))).astype(o_ref.dtype)


def _gla(pc, w_gate2, b_gate, norm_g, batch, seq):
    t = pc.shape[0]
    heads = norm_g.shape[0] // GLA_DV
    wg = jnp.pad(w_gate2, ((0, LANES - GLA_GATE_RANK), (0, 0))).astype(BF16)
    nc = seq // GLA_CHUNK
    blk = lambda w, off: pl.BlockSpec((None, GLA_CHUNK, w), lambda b, h, c: (b, c, off + h))
    par = lambda rows, w: pl.BlockSpec((rows, w), lambda b, h, c: (0, h))
    h2 = heads * GLA_DV // GLA_DK
    pc3 = pc.reshape(batch, seq, pc.shape[1])
    out = pl.pallas_call(
        _gla_kernel,
        grid=(batch, heads, nc),
        in_specs=[blk(GLA_DK, 2 * h2), blk(GLA_DK, 2 * h2 + heads), blk(GLA_DV, 0),
                  pl.BlockSpec((None, GLA_CHUNK, LANES), lambda b, h, c: (b, c, 2 * h2 + 2 * heads)),
                  blk(GLA_DV, heads), par(LANES, GLA_DK), par(1, GLA_DK), par(1, GLA_DV)],
        out_specs=blk(GLA_DV, 0),
        out_shape=jax.ShapeDtypeStruct((batch, seq, heads * GLA_DV), BF16),
        scratch_shapes=[pltpu.VMEM((GLA_DV, GLA_DK), F32)],
        compiler_params=_cparams("parallel", "parallel", "arbitrary"),
        name="gla",
    )(pc3, pc3, pc3, pc3, pc3, wg, b_gate.reshape(1, -1), norm_g.reshape(1, -1))
    return out.reshape(t, heads * GLA_DV)


def _seg_sum(x, width):
    gi = lax.broadcasted_iota(jnp.int32, (LANES, LANES), 0) // width
    gj = lax.broadcasted_iota(jnp.int32, (LANES, LANES), 1) // width
    ones = (gi == gj).astype(F32)
    return jnp.concatenate([_dot_f32(x[:, c:c + LANES], ones) for c in range(0, x.shape[1], LANES)], axis=1)


def _rwkv_prep_kernel(*refs, width, has_vres):
    if has_vres:
        (p_ref, pp_ref, prm_ref, ml_ref, w2_ref, a2_ref, g2_ref, vf_ref, v0_ref, v1_ref, v2_ref,
         r_ref, lw_ref, k_ref, v_ref, kk_ref, bb_ref, bonus_ref, g_ref) = refs
    else:
        (p_ref, pp_ref, prm_ref, ml_ref, w2_ref, a2_ref, g2_ref,
         r_ref, lw_ref, k_ref, v_ref, kk_ref, bb_ref, bonus_ref, g_ref) = refs
    w = width
    p, pp, prm, ml = p_ref[...], pp_ref[...], prm_ref[...], ml_ref[...]
    lerp = lambda lo, hi, mu: p[:, lo:hi] + (pp[:, lo:hi] - p[:, lo:hi]) * mu
    r = lerp(0, w, prm[0:1])
    k = lerp(w, 2 * w, prm[1:2])
    v = lerp(2 * w, 3 * w, prm[2:3])
    x_wa = lerp(3 * w, 3 * w + LANES, ml[0:1])
    x_g = lerp(3 * w + LANES, 3 * w + 2 * LANES, ml[1:2])
    w_log = -jax.nn.softplus(-(prm[3:4] + _dot(jnp.tanh(x_wa).astype(BF16), w2_ref[...]))) - 0.5
    lw_ref[...] = -jnp.exp(w_log)
    a = jax.nn.sigmoid(prm[4:5] + _dot(x_wa.astype(BF16), a2_ref[...]))
    g_ref[...] = _dot(jax.nn.sigmoid(x_g).astype(BF16), g2_ref[...])
    if has_vres:
        mix = jax.nn.sigmoid(v0_ref[...] + _dot(_dot(v.astype(BF16), v1_ref[...]).astype(BF16), v2_ref[...]))
        v = v + (vf_ref[...] - v) * mix
    kk = k * prm[5:6]
    kk = kk * lax.rsqrt(jnp.maximum(_seg_sum(kk * kk, RWKV_HS), 1e-24))
    k2 = k * (1.0 + (a - 1.0) * prm[6:7])
    r_ref[...] = r
    k_ref[...] = k2
    v_ref[...] = v
    kk_ref[...] = kk
    bb_ref[...] = kk * a
    bonus_ref[...] = _seg_sum(r * k2 * prm[7:8], RWKV_HS) * v


def _rwkv_chunk_kernel(r_ref, lw_ref, k_ref, v_ref, kk_ref, bb_ref, y_ref, s_ref, *, n_chunk):
    c_len, hs = RWKV_CHUNK, RWKV_HS
    rows_n = n_chunk * c_len

    @pl.when(pl.program_id(2) == 0)
    def _():
        s_ref[...] = jnp.zeros_like(s_ref)

    ri = lax.broadcasted_iota(jnp.int32, (rows_n, rows_n), 0)
    ci = lax.broadcasted_iota(jnp.int32, (rows_n, rows_n), 1)
    same = (ri // c_len) == (ci // c_len)
    incl = jnp.logical_and(same, ci <= ri)
    strict = jnp.logical_and(same, ci < ri)
    eye = (ri == ci).astype(F32)
    e_r = lax.broadcasted_iota(jnp.int32, (hs, hs), 0)
    e_c = lax.broadcasted_iota(jnp.int32, (hs, hs), 1)
    ys = []
    for hh in range(LANES // hs):
        sl = slice(hh * hs, (hh + 1) * hs)
        r, lw, k, v, kk, bb = (ref[...][:, sl] for ref in (r_ref, lw_ref, k_ref, v_ref, kk_ref, bb_ref))
        cum = _dot_f32(incl.astype(F32), lw)
        cum_c = _dot_f32(same.astype(F32), lw)
        g_inv, g_end = jnp.exp(-cum), jnp.exp(cum_c - cum)
        kap, rt = kk * jnp.exp(cum - lw), r * jnp.exp(cum)
        bet, kt = bb * g_inv, k * g_inv
        bet_c, kt_c = bb * g_end, k * g_end
        gram = _dot_t_f32(jnp.concatenate([kap, rt], axis=0), jnp.concatenate([bet, kt], axis=0))
        n_m = jnp.where(strict, gram[:rows_n, :rows_n], 0.0)
        a_kk = jnp.where(strict, gram[:rows_n, rows_n:], 0.0)
        a_rb = jnp.where(incl, gram[rows_n:, :rows_n], 0.0)
        a_rk = jnp.where(incl, gram[rows_n:, rows_n:], 0.0)
        t_inv, pw = eye - n_m, -n_m
        for _ in range(int(math.log2(c_len)) - 1):
            pw = _dot_f32(pw, pw)
            t_inv = t_inv + _dot_f32(t_inv, pw)
        av = _dot_f32(jnp.concatenate([a_kk, a_rk], axis=0), v)
        z = _dot_f32(t_inv, jnp.concatenate([kap, av[:rows_n]], axis=1))
        az = _dot_f32(a_rb, z)
        r_p, y0 = rt - az[:, :hs], av[rows_n:] - az[:, hs:]
        s = s_ref[hh]
        outs = []
        for c in range(n_chunk):
            rows = slice(c * c_len, (c + 1) * c_len)
            outs.append(_dot_t_f32(r_p[rows], s) + y0[rows])
            zb = _dot_tl_f32(z[rows], bet_c[rows])
            g_chunk = jnp.exp(cum_c[c * c_len:c * c_len + 1])
            m = jnp.where(e_r == e_c, jnp.broadcast_to(g_chunk, (hs, hs)), 0.0) - zb[:hs]
            s = _dot_f32(s, m) + _dot_tl_f32(v[rows], kt_c[rows]) - zb[hs:]
        s_ref[hh] = s
        ys.append(jnp.concatenate(outs, axis=0))
    y_ref[...] = jnp.concatenate(ys, axis=1)


def _rwkv_post_kernel(y_ref, bonus_ref, g_ref, lg_ref, lb_ref, o_ref):
    y = y_ref[...]
    mu = _seg_sum(y, RWKV_HS) * (1.0 / RWKV_HS)
    yc = y - mu
    var = _seg_sum(yc * yc, RWKV_HS) * (1.0 / RWKV_HS)
    yn = yc * lax.rsqrt(var + RWKV_GN_EPS) * lg_ref[...] + lb_ref[...]
    o_ref[...] = ((yn + bonus_ref[...]) * g_ref[...]).astype(o_ref.dtype)


def _rwkv(pd, mu_rkv, mu_lora, w0, w2, a0, a2, g2, k_k, k_a, r_k, lnx_g, lnx_b, v_first, vres, batch, seq):
    t, wd = pd.shape
    w = w0.shape[0]
    pd3 = pd.reshape(batch, seq, wd)
    prev = jnp.pad(pd3[:, :-1], ((0, 0), (1, 0), (0, 0))).reshape(t, wd)
    prm = jnp.stack([mu_rkv[0], mu_rkv[1], mu_rkv[2], w0, a0, k_k, k_a, r_k.reshape(-1)])
    zl = jnp.zeros((RWKV_LORA,), F32)
    ml = jnp.stack([jnp.concatenate([mu_lora[0], mu_lora[1]]), jnp.concatenate([mu_lora[2], zl])])
    zw = jnp.zeros((LANES - RWKV_LORA, w), F32)
    w2p = jnp.concatenate([w2, zw]).astype(BF16)
    a2p = jnp.concatenate([zw, a2]).astype(BF16)
    g2p = jnp.concatenate([g2, zw]).astype(BF16)
    has_vres = vres is not None
    tm = _tile(t, 256, 8)
    row = lambda width: pl.BlockSpec((tm, width), lambda i: (i, 0))
    full = lambda a: pl.BlockSpec(a.shape, lambda i: (0, 0))
    args = [pd, prev, prm, ml, w2p, a2p, g2p]
    specs = [row(wd), row(wd), full(prm), full(ml), full(w2p), full(a2p), full(g2p)]
    if has_vres:
        v0, v1, v2 = vres
        v1p = _pad_cols(v1, LANES).astype(BF16)
        v2p = jnp.pad(v2, ((0, LANES - v2.shape[0]), (0, 0))).astype(BF16)
        v0r = v0.reshape(1, w)
        args += [v_first, v0r, v1p, v2p]
        specs += [row(w), full(v0r), full(v1p), full(v2p)]
    r, lw, k2, v, kk, bb, bonus, g = pl.pallas_call(
        functools.partial(_rwkv_prep_kernel, width=w, has_vres=has_vres),
        grid=(t // tm,),
        in_specs=specs,
        out_specs=[row(w)] * 8,
        out_shape=[jax.ShapeDtypeStruct((t, w), F32)] * 8,
        compiler_params=_cparams("parallel"),
        name="rwkv_prep",
    )(*args)

    n_chunk = 4 if seq % (4 * RWKV_CHUNK) == 0 else 1
    rows_n = n_chunk * RWKV_CHUNK
    pairs = w // LANES
    blk = pl.BlockSpec((None, rows_n, LANES), lambda b, h, c: (b, c, h))
    r3 = lambda a: a.reshape(batch, seq, w)
    y = pl.pallas_call(
        functools.partial(_rwkv_chunk_kernel, n_chunk=n_chunk),
        grid=(batch, pairs, seq // rows_n),
        in_specs=[blk] * 6,
        out_specs=blk,
        out_shape=jax.ShapeDtypeStruct((batch, seq, w), F32),
        scratch_shapes=[pltpu.VMEM((LANES // RWKV_HS, RWKV_HS, RWKV_HS), F32)],
        compiler_params=_cparams("parallel", "parallel", "arbitrary"),
        name="rwkv_chunk",
    )(r3(r), r3(lw), r3(k2), r3(v), r3(kk), r3(bb))

    lg, lb = lnx_g.reshape(1, w), lnx_b.reshape(1, w)
    out = pl.pallas_call(
        _rwkv_post_kernel,
        grid=(t // tm,),
        in_specs=[row(w), row(w), row(w), full(lg), full(lb)],
        out_specs=row(w),
        out_shape=jax.ShapeDtypeStruct((t, w), BF16),
        compiler_params=_cparams("parallel"),
        name="rwkv_post",
    )(y.reshape(t, w), bonus, g, lg, lb)
    return out, (v_first if has_vres else v)


N_BRANCH = 4


def _merge_kernel(*refs):
    x_ref = refs[0]
    y_refs = refs[1:1 + N_BRANCH]
    wg_refs = refs[1 + N_BRANCH:1 + 2 * N_BRANCH]
    wb_refs = refs[1 + 2 * N_BRANCH:1 + 3 * N_BRANCH]
    o_ref = refs[-1]
    x = x_ref[...]
    acc = None
    for y_ref, wg_ref, wb_ref in zip(y_refs, wg_refs, wb_refs):
        term = jax.nn.sigmoid(_dot(x, wg_ref[...])) * _dot(y_ref[...], wb_ref[...])
        acc = term if acc is None else acc + term
    o_ref[...] = acc.astype(o_ref.dtype)


def _merge(xb, ys, w_gate, w_branch):
    t, d = xb.shape
    w = ys[0].shape[1]
    tm, tn = _tile(t, 512, 8), _tile(d, 256)
    nj = d // tn
    gate_spec = lambda i: pl.BlockSpec((d, tn), lambda r, j: (0, i * nj + j))
    br_spec = lambda i: pl.BlockSpec((None, w, tn), lambda r, j: (i, 0, j))
    return pl.pallas_call(
        _merge_kernel,
        grid=(t // tm, nj),
        in_specs=[pl.BlockSpec((tm, d), lambda r, j: (r, 0))] + [pl.BlockSpec((tm, w), lambda r, j: (r, 0))] * N_BRANCH
        + [gate_spec(i) for i in range(N_BRANCH)] + [br_spec(i) for i in range(N_BRANCH)],
        out_specs=pl.BlockSpec((tm, tn), lambda r, j: (r, j)),
        out_shape=jax.ShapeDtypeStruct((t, d), BF16),
        compiler_params=_cparams("parallel", "parallel"),
        name="merge",
    )(xb, *ys, *([w_gate] * N_BRANCH), *([w_branch] * N_BRANCH))


def _router_kernel(x_ref, w_ref, b_ref, o_ref, *, n_exp):
    tm = x_ref.shape[0]
    scores = jax.nn.sigmoid(_dot(x_ref[...], w_ref[...]))
    lane = lax.broadcasted_iota(jnp.int32, (tm, LANES), 1)
    work = jnp.where(lane < n_exp, scores + b_ref[...], -jnp.inf)
    chosen = jnp.zeros((tm, LANES), F32)
    for _ in range(TOP_K):
        best = jnp.max(work, axis=1, keepdims=True)
        first = jnp.min(jnp.where(work == best, lane, LANES), axis=1, keepdims=True)
        hit = lane == first
        chosen = jnp.where(hit, scores, chosen)
        work = jnp.where(hit, -jnp.inf, work)
    comb = chosen / jnp.sum(chosen, axis=1, keepdims=True) * ROUTED_SCALE
    o_ref[...] = jnp.where(lane == n_exp, 1.0, comb)


def _router(xb, router_w, router_bias):
    t, d = xb.shape
    n_exp = router_w.shape[1]
    assert n_exp < LANES
    tm = _tile(t, 512, 8)
    wr = _pad_cols(router_w, LANES).astype(BF16)
    br = _pad_cols(router_bias.reshape(1, n_exp), LANES)
    return pl.pallas_call(
        functools.partial(_router_kernel, n_exp=n_exp),
        grid=(t // tm,),
        in_specs=[pl.BlockSpec((tm, d), lambda i: (i, 0)), pl.BlockSpec((d, LANES), lambda i: (0, 0)),
                  pl.BlockSpec((1, LANES), lambda i: (0, 0))],
        out_specs=pl.BlockSpec((tm, LANES), lambda i: (i, 0)),
        out_shape=jax.ShapeDtypeStruct((t, LANES), F32),
        compiler_params=_cparams("parallel"),
        name="router",
    )(xb, wr, br)


MOE_COLS = 1024


def _moe_kernel(x_ref, c_ref, wg_ref, wu_ref, wd_ref, o_ref):
    e = pl.program_id(1)
    tm, d = o_ref.shape

    @pl.when(e == 0)
    def _():
        o_ref[...] = jnp.zeros_like(o_ref)

    x = x_ref[...]
    lane = lax.broadcasted_iota(jnp.int32, (tm, LANES), 1)
    c = jnp.sum(jnp.where(lane == e, c_ref[...], 0.0), axis=1, keepdims=True)
    gate = _dot(x, wg_ref[...])
    h = (gate * jax.nn.sigmoid(gate) * _dot(x, wu_ref[...])).astype(BF16)
    cols = _tile(d, MOE_COLS)
    for lo in range(0, d, cols):
        o_ref[:, lo:lo + cols] += c * _dot(h, wd_ref[:, lo:lo + cols])


def _moe(xb, comb, w_gate, w_up, w_down):
    t, d = xb.shape
    n_e, _, hid = w_gate.shape
    tm = _tile(t, 512, 8)
    return pl.pallas_call(
        _moe_kernel,
        grid=(t // tm, n_e),
        in_specs=[pl.BlockSpec((tm, d), lambda i, e: (i, 0)), pl.BlockSpec((tm, LANES), lambda i, e: (i, 0)),
                  pl.BlockSpec((None, d, hid), lambda i, e: (e, 0, 0)),
                  pl.BlockSpec((None, d, hid), lambda i, e: (e, 0, 0)),
                  pl.BlockSpec((None, hid, d), lambda i, e: (e, 0, 0))],
        out_specs=pl.BlockSpec((tm, d), lambda i, e: (i, 0)),
        out_shape=jax.ShapeDtypeStruct((t, d), F32),
        compiler_params=_cparams("parallel", "arbitrary"),
        name="moe",
    )(xb, comb, w_gate, w_up, w_down)


PROJ_ALIGN = 512


def _split_w_in(w, d):
    wdt = d // N_BRANCH
    gla_heads = wdt // GLA_DV
    names = (('gate', N_BRANCH * d), ('mla_cq', MLA_Q_RANK), ('mla_ckv', MLA_KV_RANK), ('mla_kr', MLA_ROPE),
             ('dsa_q', wdt), ('dsa_k', DSA_DH), ('dsa_v', DSA_DH),
             ('idx_q', IDX_HEADS * IDX_DIM), ('idx_k', IDX_DIM), ('idx_w', IDX_HEADS),
             ('gla_q', gla_heads * GLA_DK), ('gla_k', gla_heads * GLA_DK), ('gla_v', wdt),
             ('gla_a', GLA_GATE_RANK), ('gla_r', wdt),
             ('rwkv_r', wdt), ('rwkv_k', wdt), ('rwkv_v', wdt),
             ('rwkv_w', RWKV_LORA), ('rwkv_a', RWKV_LORA), ('rwkv_g', RWKV_LORA))
    parts, off = {}, 0
    for name, width in names:
        parts[name] = w[:, off:off + width]
        off += width
    assert off == w.shape[1], (off, w.shape)
    return parts


def _group(cols, dtype=BF16):
    wcat = jnp.concatenate(cols, axis=1)
    return _pad_cols(wcat, -(-wcat.shape[1] // PROJ_ALIGN) * PROJ_ALIGN).astype(dtype)


def kernel(x, positions, ln_in_g, ln_in_b, w_in, w_branch, w_out, mla_q_norm, mla_w_uq, mla_kv_norm, mla_w_ukv,
           gla_w_gate2, gla_b_gate, gla_norm_g, rwkv_mu_rkv, rwkv_mu_lora, rwkv_w0, rwkv_w2, rwkv_a0, rwkv_a2,
           rwkv_g2, rwkv_k_k, rwkv_k_a, rwkv_r_k, rwkv_lnx_g, rwkv_lnx_b, rwkv_v0, rwkv_v1, rwkv_v2,
           ln_mix_g, ln_mix_b, router_w, router_bias, exp_w_gate, exp_w_up, exp_w_down,
           sh_w_gate, sh_w_up, sh_w_down, ln_ffn_g, ln_ffn_b):
    batch, seq, d = x.shape
    depth = w_in.shape[0]
    t, wdt = batch * seq, d // N_BRANCH
    alpha = (2 * depth) ** 0.25
    ct, st = _rope_tables(positions)
    xf, xb = _layer_norm(x.reshape(t, d), None, ln_in_g, ln_in_b)
    v_first = None
    for l in range(depth):
        p = _split_w_in(w_in[l], d)
        pad128 = lambda w: _pad_cols(w, LANES)
        idx_q = jnp.pad(p['idx_q'].reshape(d, IDX_HEADS, IDX_DIM),
                        ((0, 0), (0, 0), (0, LANES - IDX_DIM))).reshape(d, IDX_HEADS * LANES)
        w_lat = _group([p['mla_cq'], p['mla_ckv'], pad128(p['mla_kr']), pad128(_swap_halves(p['mla_kr']))])
        w_dsa = _group([idx_q, p['dsa_q'], p['dsa_k'], p['dsa_v'], pad128(p['idx_k']), pad128(p['idx_w'])])
        w_gla = _group([p['gla_v'], p['gla_r'], p['gla_q'], p['gla_k'], pad128(p['gla_a'])])
        w_rwkv = _group([p['rwkv_r'], p['rwkv_k'], p['rwkv_v'], p['rwkv_w'], p['rwkv_a'], pad128(p['rwkv_g'])])
        lat = _matmul(xb, w_lat, F32)
        pb = _matmul(xb, w_dsa, BF16)
        pc = _matmul(xb, w_gla, F32)
        pd = _matmul(xb, w_rwkv, F32)

        y_mla = _mla(lat, ct, st, mla_q_norm[l], mla_w_uq[l], mla_kv_norm[l], mla_w_ukv[l], batch, seq)
        y_dsa = _dsa(pb, batch, seq, wdt)
        y_gla = _gla(pc, gla_w_gate2[l], gla_b_gate[l], gla_norm_g[l], batch, seq)
        vres = None if l == 0 else (rwkv_v0[l - 1], rwkv_v1[l - 1], rwkv_v2[l - 1])
        y_rwkv, v_first = _rwkv(pd, rwkv_mu_rkv[l], rwkv_mu_lora[l], rwkv_w0[l], rwkv_w2[l], rwkv_a0[l], rwkv_a2[l],
                                rwkv_g2[l], rwkv_k_k[l], rwkv_k_a[l], rwkv_r_k[l], rwkv_lnx_g[l], rwkv_lnx_b[l],
                                v_first, vres, batch, seq)

        merged = _merge(xb, (y_mla, y_dsa, y_gla, y_rwkv), p['gate'].astype(BF16), w_branch[l].astype(BF16))
        xf, xb = _layer_norm(xf, _matmul(merged, w_out[l].astype(BF16), F32), ln_mix_g[l], ln_mix_b[l], alpha)

        comb = _router(xb, router_w[l], router_bias[l])
        stack = lambda routed, shared: jnp.concatenate([routed.astype(BF16), shared[None].astype(BF16)], axis=0)
        f = _moe(xb, comb, stack(exp_w_gate[l], sh_w_gate[l]), stack(exp_w_up[l], sh_w_up[l]),
                 stack(exp_w_down[l], sh_w_down[l]))
        xf, xb = _layer_norm(xf, f, ln_ffn_g[l], ln_ffn_b[l], alpha)
    return xf.reshape(batch, seq, d)
```

```python
import functools
import math

import jax
import jax.numpy as jnp
from jax import lax
from jax.experimental import pallas as pl
from jax.experimental.pallas import tpu as pltpu

F32 = jnp.float32
BF16 = jnp.bfloat16

LANES = 128

MLA_NOPE, MLA_ROPE, MLA_V = 128, 64, 128
MLA_Q_RANK, MLA_KV_RANK = 768, 256
ROPE_THETA = 10000.0
DSA_DH = 128
IDX_HEADS, IDX_DIM, IDX_TOPK_MAX = 16, 64, 256
GLA_DV, GLA_DK, GLA_GATE_RANK, GLA_TAU, GLA_CHUNK = 256, 128, 16, 16.0, 64
GLA_SUB = 16
RWKV_HS, RWKV_LORA, RWKV_GN_EPS = 64, 64, 64e-5
RWKV_CHUNK = 64
TOP_K, ROUTED_SCALE = 8, 2.5
LN_EPS, RMS_EPS = 1e-5, 1e-6
NEG_BIG = -1e30
LOG2E = math.log2(math.e)

VMEM_LIMIT = 56 * 1024 * 1024


def _cparams(*sem):
    return pltpu.CompilerParams(dimension_semantics=sem, vmem_limit_bytes=VMEM_LIMIT)


def _tile(n, pref, unit=LANES):
    if n <= pref:
        return n
    t = (pref // unit) * unit
    while t > unit and n % t:
        t -= unit
    assert n % t == 0, (n, pref, unit)
    return t


def _dot(a, b):
    return jnp.dot(a, b, preferred_element_type=F32)


def _dot_t(a, b):
    return lax.dot_general(a, b, (((1,), (1,)), ((), ())), preferred_element_type=F32)


def _dot_f32(a, b):
    return jnp.dot(a, b, preferred_element_type=F32, precision=lax.Precision.HIGHEST)


def _masked_sums(mask, x):
    w = x.shape[1]
    hi = x.astype(BF16)
    rest = x - hi.astype(F32)
    mid = rest.astype(BF16)
    lo = (rest - mid.astype(F32)).astype(BF16)
    s = _dot(mask, jnp.concatenate([hi, mid, lo], axis=1))
    return s[:, :w] + s[:, w:2 * w] + s[:, 2 * w:]


def _mm_kernel(a_ref, b_ref, o_ref):
    o_ref[...] = _dot(a_ref[...], b_ref[...]).astype(o_ref.dtype)


def _matmul(a, b, out_dtype, tm=1024, tn=512):
    m, k = a.shape
    n = b.shape[1]
    tm, tn = _tile(m, tm, 8), _tile(n, tn)
    return pl.pallas_call(
        _mm_kernel,
        grid=(m // tm, n // tn),
        in_specs=[pl.BlockSpec((tm, k), lambda i, j: (i, 0)), pl.BlockSpec((k, tn), lambda i, j: (0, j))],
        out_specs=pl.BlockSpec((tm, tn), lambda i, j: (i, j)),
        out_shape=jax.ShapeDtypeStruct((m, n), out_dtype),
        compiler_params=_cparams("parallel", "parallel"),
        name="matmul",
    )(a, b)


def _ln_kernel(*refs, alpha, has_res):
    if has_res:
        x_ref, f_ref, g_ref, b_ref, o_ref, ob_ref = refs
        x = alpha * x_ref[...] + f_ref[...]
    else:
        x_ref, g_ref, b_ref, o_ref, ob_ref = refs
        x = x_ref[...]
    mu = jnp.mean(x, axis=-1, keepdims=True)
    xc = x - mu
    var = jnp.mean(xc * xc, axis=-1, keepdims=True)
    y = xc * lax.rsqrt(var + LN_EPS) * g_ref[...] + b_ref[...]
    o_ref[...] = y
    ob_ref[...] = y.astype(BF16)


def _layer_norm(x, f, g, b, alpha=1.0):
    t, d = x.shape
    tm = _tile(t, 128, 8)
    row = pl.BlockSpec((tm, d), lambda i: (i, 0))
    vec = pl.BlockSpec((1, d), lambda i: (0, 0))
    has_res = f is not None
    args = (x, f) if has_res else (x,)
    return pl.pallas_call(
        functools.partial(_ln_kernel, alpha=alpha, has_res=has_res),
        grid=(t // tm,),
        in_specs=[row] * len(args) + [vec, vec],
        out_specs=[row, row],
        out_shape=[jax.ShapeDtypeStruct((t, d), F32), jax.ShapeDtypeStruct((t, d), BF16)],
        compiler_params=_cparams("parallel"),
        name="layer_norm",
    )(*args, g.reshape(1, d), b.reshape(1, d))


def _pad_cols(w, width):
    return jnp.pad(w, ((0, 0), (0, width - w.shape[1])))


def _rms(x, g):
    return x * lax.rsqrt(jnp.mean(x * x, axis=-1, keepdims=True) + RMS_EPS) * g


def _mla_prep_kernel(lat_ref, ct_ref, st_ref, qg_ref, kg_ref, wqn_ref, wqr_ref, wqs_ref, wkn_ref, wv_ref,
                     qn_ref, qr_ref, kn_ref, kr_ref, v_ref, *, heads, scale):
    lat = lat_ref[...]
    cq = lat[:, :MLA_Q_RANK]
    ckv = lat[:, MLA_Q_RANK:MLA_Q_RANK + MLA_KV_RANK]
    kr = lat[:, MLA_Q_RANK + MLA_KV_RANK:MLA_Q_RANK + MLA_KV_RANK + LANES]
    krs = lat[:, MLA_Q_RANK + MLA_KV_RANK + LANES:MLA_Q_RANK + MLA_KV_RANK + 2 * LANES]
    ct, st = ct_ref[...], st_ref[...]
    nq = _rms(cq, qg_ref[...]).astype(BF16)
    nkv = _rms(ckv, kg_ref[...]).astype(BF16)
    qn_ref[...] = (_dot(nq, wqn_ref[...]) * scale).astype(BF16)
    cth = jnp.concatenate([ct] * heads, axis=1)
    sth = jnp.concatenate([st] * heads, axis=1)
    qr_ref[...] = ((_dot(nq, wqr_ref[...]) * cth + _dot(nq, wqs_ref[...]) * sth) * scale).astype(BF16)
    kn_ref[...] = _dot(nkv, wkn_ref[...]).astype(BF16)
    v_ref[...] = _dot(nkv, wv_ref[...]).astype(BF16)
    kr_ref[...] = (kr * ct + krs * st).astype(BF16)


def _flash_kernel(qn_ref, qr_ref, kn_ref, kr_ref, v_ref, o_ref, m_ref, l_ref, acc_ref, *, tq, tk):
    i, j = pl.program_id(2), pl.program_id(3)

    @pl.when(j == 0)
    def _():
        m_ref[...] = jnp.full_like(m_ref, NEG_BIG)
        l_ref[...] = jnp.zeros_like(l_ref)
        acc_ref[...] = jnp.zeros_like(acc_ref)

    def step(masked):
        q = jnp.concatenate([qn_ref[...], qr_ref[...]], axis=1)
        k = jnp.concatenate([kn_ref[...], kr_ref[...]], axis=1)
        s = _dot_t(q, k)
        if masked:
            qpos = i * tq + lax.broadcasted_iota(jnp.int32, (tq, tk), 0)
            kpos = j * tk + lax.broadcasted_iota(jnp.int32, (tq, tk), 1)
            s = jnp.where(kpos <= qpos, s, -jnp.inf)
        m_prev = m_ref[...]
        m_new = jnp.maximum(m_prev, jnp.max(s, axis=1, keepdims=True))
        alpha = jnp.exp2(m_prev - m_new)
        p = jnp.exp2(s - jnp.concatenate([m_new] * (tk // LANES), axis=1))
        l_ref[...] = alpha * l_ref[...] + jnp.sum(p, axis=1, keepdims=True)
        acc_ref[...] = alpha * acc_ref[...] + _dot(p.astype(BF16), v_ref[...])
        m_ref[...] = m_new

    below = j * tk + tk - 1 <= i * tq
    touches = j * tk <= i * tq + tq - 1

    @pl.when(below)
    def _():
        step(False)

    @pl.when(jnp.logical_and(touches, jnp.logical_not(below)))
    def _():
        step(True)

    @pl.when(j == pl.num_programs(3) - 1)
    def _():
        o_ref[...] = (acc_ref[...] / l_ref[...]).astype(o_ref.dtype)


def _rope_tables(positions):
    half = MLA_ROPE // 2
    inv = ROPE_THETA ** (-jnp.arange(0, MLA_ROPE, 2, dtype=F32) / MLA_ROPE)
    ang = positions.reshape(-1).astype(F32)[:, None] * inv
    cos, sin = jnp.cos(ang), jnp.sin(ang)
    zero = jnp.zeros((ang.shape[0], LANES - 2 * half), F32)
    return jnp.concatenate([cos, cos, zero], 1), jnp.concatenate([-sin, sin, zero], 1)


def _swap_halves(w):
    half = w.shape[-1] // 2
    return jnp.concatenate([w[..., half:], w[..., :half]], -1)


def _mla(lat, ct, st, q_norm, w_uq, kv_norm, w_ukv, batch, seq):
    t = lat.shape[0]
    heads = w_uq.shape[1] // (MLA_NOPE + MLA_ROPE)
    hw = heads * LANES
    wq = w_uq.reshape(MLA_Q_RANK, heads, MLA_NOPE + MLA_ROPE)
    wqn = wq[:, :, :MLA_NOPE].reshape(MLA_Q_RANK, hw).astype(BF16)
    rope_pad = ((0, 0), (0, 0), (0, LANES - MLA_ROPE))
    wqr = jnp.pad(wq[:, :, MLA_NOPE:], rope_pad).reshape(MLA_Q_RANK, hw).astype(BF16)
    wqs = jnp.pad(_swap_halves(wq[:, :, MLA_NOPE:]), rope_pad).reshape(MLA_Q_RANK, hw).astype(BF16)
    wkv = w_ukv.reshape(MLA_KV_RANK, heads, MLA_NOPE + MLA_V)
    wkn = wkv[:, :, :MLA_NOPE].reshape(MLA_KV_RANK, hw).astype(BF16)
    wv = wkv[:, :, MLA_NOPE:].reshape(MLA_KV_RANK, hw).astype(BF16)

    tm = _tile(t, 512, 8)
    row = lambda w: pl.BlockSpec((tm, w), lambda i: (i, 0))
    full = lambda a: pl.BlockSpec(a.shape, lambda i: (0, 0))
    qg, kg = q_norm.reshape(1, -1), kv_norm.reshape(1, -1)
    qn, qr, kn, kr, v = pl.pallas_call(
        functools.partial(_mla_prep_kernel, heads=heads, scale=(MLA_NOPE + MLA_ROPE) ** -0.5 * LOG2E),
        grid=(t // tm,),
        in_specs=[row(lat.shape[1]), row(LANES), row(LANES), full(qg), full(kg),
                  full(wqn), full(wqr), full(wqs), full(wkn), full(wv)],
        out_specs=[row(hw), row(hw), row(hw), row(LANES), row(hw)],
        out_shape=[jax.ShapeDtypeStruct((t, hw), BF16)] * 3 + [jax.ShapeDtypeStruct((t, LANES), BF16),
                                                               jax.ShapeDtypeStruct((t, hw), BF16)],
        compiler_params=_cparams("parallel"),
        name="mla_prep",
    )(lat, ct, st, qg, kg, wqn, wqr, wqs, wkn, wv)

    tq = tk = _tile(seq, 1024, 8)
    nq, nk = seq // tq, seq // tk
    r3 = lambda a: a.reshape(batch, seq, a.shape[1])
    last = lambda i, j: jnp.minimum(j, ((i + 1) * tq - 1) // tk)
    q_spec = pl.BlockSpec((None, tq, LANES), lambda b, h, i, j: (b, i, h))
    k_spec = pl.BlockSpec((None, tk, LANES), lambda b, h, i, j: (b, last(i, j), h))
    kr_spec = pl.BlockSpec((None, tk, LANES), lambda b, h, i, j: (b, last(i, j), 0))
    out = pl.pallas_call(
        functools.partial(_flash_kernel, tq=tq, tk=tk),
        grid=(batch, heads, nq, nk),
        in_specs=[q_spec, q_spec, k_spec, kr_spec, k_spec],
        out_specs=q_spec,
        out_shape=jax.ShapeDtypeStruct((batch, seq, hw), BF16),
        scratch_shapes=[pltpu.VMEM((tq, LANES), F32), pltpu.VMEM((tq, LANES), F32), pltpu.VMEM((tq, LANES), F32)],
        compiler_params=_cparams("parallel", "parallel", "parallel", "arbitrary"),
        name="mla_flash",
    )(r3(qn), r3(qr), r3(kn), r3(kr), r3(v))
    return out.reshape(t, hw)


DSA_TQ = 128
DSA_TK = 512
INT_MIN = -2 ** 31


def _dsa_kernel(iq_ref, q_ref, k_ref, v_ref, ik_ref, iw_ref, o_ref, key_ref, m_ref, l_ref, acc_ref,
                *, tq, tk, n_sel, heads, pos_bits):
    i = pl.program_id(1)
    n_kt = (i * tq + tq - 1) // tk + 1
    reps = tk // LANES
    wide = lambda a: jnp.concatenate([a] * reps, axis=1)
    qpos = i * tq + lax.broadcasted_iota(jnp.int32, (tq, tk), 0)
    lane_pos = lax.broadcasted_iota(jnp.int32, (tq, tk), 1)

    iq = iq_ref[...]
    iw = iw_ref[...].astype(F32) * (IDX_HEADS ** -0.5 * IDX_DIM ** -0.5)
    iw_b = [jnp.broadcast_to(iw[:, h:h + 1], (tq, LANES)) for h in range(IDX_HEADS)]

    def score_tile(j, carry):
        ikj = ik_ref[pl.ds(pl.multiple_of(j * tk, tk), tk), :]
        sc = jnp.zeros((tq, tk), F32)
        for h in range(IDX_HEADS):
            logit = _dot_t(iq[:, h * LANES:(h + 1) * LANES], ikj)
            sc = sc + wide(iw_b[h]) * jnp.maximum(logit, 0.0)
        sc = jnp.where(j * tk + lane_pos <= qpos, sc + 0.0, -jnp.inf)
        bits = lax.bitcast_convert_type(sc, jnp.int32)
        key_ref[j] = jnp.where(bits >= 0, bits, bits ^ 0x7FFFFFFF)
        return carry

    lax.fori_loop(0, n_kt, score_tile, 0)

    def count(pred):
        def body(j, acc):
            c = pred(key_ref[j], j * tk + lane_pos).astype(jnp.int32)
            for rep in range(reps):
                acc = acc + c[:, rep * LANES:(rep + 1) * LANES]
            return acc
        acc = lax.fori_loop(0, n_kt, body, jnp.zeros((tq, LANES), jnp.int32))
        return jnp.broadcast_to(jnp.sum(acc, axis=1, keepdims=True), (tq, LANES))

    def thr_bit(bit_i, thr):
        cand = thr ^ jnp.left_shift(jnp.int32(1), 31 - bit_i)
        cnt = count(lambda key, pos: key >= wide(cand))
        return jnp.where(cnt >= n_sel, cand, thr)

    thr = lax.fori_loop(0, 32, thr_bit, jnp.full((tq, LANES), INT_MIN, jnp.int32))
    cnt_gt = count(lambda key, pos: key > wide(thr))
    cnt_ge = count(lambda key, pos: key >= wide(thr))
    need = n_sel - cnt_gt

    def tie_bit(bit_i, cut):
        cand = cut | jnp.left_shift(jnp.int32(1), pos_bits - 1 - bit_i)
        cnt = count(lambda key, pos: jnp.logical_and(key == wide(thr), pos < wide(cand)))
        return jnp.where(cnt < need, cand, cut)

    surplus = jnp.max(cnt_ge - cnt_gt - need) > 0
    cut = lax.cond(surplus,
                   lambda: lax.fori_loop(0, pos_bits, tie_bit, jnp.zeros((tq, LANES), jnp.int32)),
                   lambda: jnp.full((tq, LANES), 2 ** 31 - 1, jnp.int32))

    q = q_ref[...]
    q_all = jnp.concatenate([q[:, h * LANES:(h + 1) * LANES] for h in range(heads)], axis=0)
    m_ref[...] = jnp.full_like(m_ref, NEG_BIG)
    l_ref[...] = jnp.zeros_like(l_ref)
    acc_ref[...] = jnp.zeros_like(acc_ref)

    def attend(j, carry):
        rows = pl.ds(pl.multiple_of(j * tk, tk), tk)
        s = _dot_t(q_all, k_ref[rows, :])
        key, pos = key_ref[j], j * tk + lane_pos
        keep = jnp.where(key > wide(thr), 0.0,
                         jnp.where(jnp.logical_and(key == wide(thr), pos <= wide(cut)), 0.0, -jnp.inf))
        keep = jnp.where(pos <= qpos, keep, -jnp.inf)
        s = s + jnp.concatenate([keep] * heads, axis=0)
        m_prev = m_ref[...]
        m_new = jnp.maximum(m_prev, jnp.max(s, axis=1, keepdims=True))
        alpha = jnp.exp2(m_prev - m_new)
        p = jnp.exp2(s - jnp.concatenate([m_new] * reps, axis=1))
        l_ref[...] = alpha * l_ref[...] + jnp.sum(p, axis=1, keepdims=True)
        acc_ref[...] = alpha * acc_ref[...] + _dot(p.astype(BF16), v_ref[rows, :])
        m_ref[...] = m_new
        return carry

    lax.fori_loop(0, n_kt, attend, 0)
    out = acc_ref[...] / l_ref[...]
    o_ref[...] = jnp.concatenate([out[h * tq:(h + 1) * tq] for h in range(heads)], axis=1).astype(o_ref.dtype)


def _dsa(pb, batch, seq, width):
    t, wb = pb.shape
    heads = width // DSA_DH
    iqw = IDX_HEADS * LANES
    tq, tk = _tile(seq, DSA_TQ, 8), _tile(seq, DSA_TK)
    n_sel = min(IDX_TOPK_MAX, seq // 4)
    c0 = (iqw + width) // LANES
    pb3 = pb.reshape(batch, seq, wb)
    qblk = lambda w, idx: pl.BlockSpec((None, tq, w), lambda b, i: (b, i, idx))
    seqblk = lambda idx: pl.BlockSpec((None, seq, LANES), lambda b, i: (b, 0, idx))
    out = pl.pallas_call(
        functools.partial(_dsa_kernel, tq=tq, tk=tk, n_sel=n_sel, heads=heads,
                          pos_bits=max(1, (seq - 1).bit_length())),
        grid=(batch, seq // tq),
        in_specs=[qblk(iqw, 0), qblk(width, iqw // width), seqblk(c0), seqblk(c0 + 1), seqblk(c0 + 2),
                  qblk(LANES, c0 + 3)],
        out_specs=qblk(width, 0),
        out_shape=jax.ShapeDtypeStruct((batch, seq, width), BF16),
        scratch_shapes=[pltpu.VMEM((seq // tk, tq, tk), jnp.int32), pltpu.VMEM((heads * tq, LANES), F32),
                        pltpu.VMEM((heads * tq, LANES), F32), pltpu.VMEM((heads * tq, LANES), F32)],
        compiler_params=_cparams("parallel", "arbitrary"),
        name="dsa",
    )(pb3, pb3, pb3, pb3, pb3, pb3)
    return out.reshape(t, width)


def _gla_kernel(q_ref, k_ref, v_ref, a_ref, r_ref, wg_ref, bg_ref, ng_ref, o_ref, state_ref):
    c_len, sub = GLA_CHUNK, GLA_SUB

    @pl.when(pl.program_id(2) == 0)
    def _():
        state_ref[...] = jnp.zeros_like(state_ref)

    q = q_ref[...] * GLA_DK ** -0.5
    k = k_ref[...]
    vb = v_ref[...].astype(BF16)
    z = _dot(a_ref[...].astype(BF16), wg_ref[...]) + bg_ref[...]
    g = jax.nn.log_sigmoid(z) / GLA_TAU
    ri = lax.broadcasted_iota(jnp.int32, (c_len, c_len), 0)
    ci = lax.broadcasted_iota(jnp.int32, (c_len, c_len), 1)
    b = _masked_sums((ri >= ci).astype(BF16), g)
    state_t = state_ref[...]
    o_inter = _dot_t((q * jnp.exp(b)).astype(BF16), state_t.astype(BF16))

    ones = jnp.ones((GLA_DK, LANES), BF16)
    trow = lax.broadcasted_iota(jnp.int32, (sub, GLA_DK), 0)
    lane = lax.broadcasted_iota(jnp.int32, (sub, LANES), 1)
    outs = []
    for blk in range(c_len // sub):
        lo = blk * sub
        qi, bi, ki = q[lo:lo + sub], b[lo:lo + sub], k[lo:lo + sub]
        rows = [qi * ki[s:s + 1] * jnp.exp(jnp.where(trow >= s, bi - bi[s:s + 1], -jnp.inf)) for s in range(sub)]
        prod = jnp.concatenate(rows, axis=0)
        p_hi = prod.astype(BF16)
        p_lo = (prod - p_hi.astype(F32)).astype(BF16)
        sums = _dot(jnp.concatenate([p_hi, p_lo], axis=0), ones)
        sums = sums[:sub * sub] + sums[sub * sub:]
        attn = jnp.zeros((sub, LANES), F32)
        for s in range(sub):
            attn = attn + jnp.where(lane == s, sums[s * sub:(s + 1) * sub], 0.0)
        o_blk = _dot(attn[:, :sub].astype(BF16), vb[lo:lo + sub])
        if blk:
            qa = qi * jnp.exp(bi - bi[0:1])
            ka = k[:lo] * jnp.exp(bi[0:1] - b[:lo])
            o_blk = o_blk + _dot(_dot_t(qa.astype(BF16), ka.astype(BF16)).astype(BF16), vb[:lo])
        outs.append(o_blk)
    o = o_inter + jnp.concatenate(outs, axis=0)

    b_last = b[c_len - 1:c_len]
    k_dec = (k * jnp.exp(b_last - b)).astype(BF16)
    state_ref[...] = state_t * jnp.exp(b_last) + lax.dot_general(
        vb, k_dec, (((0,), (0,)), ((), ())), preferred_element_type=F32)

    r = r_ref[...]
    o_ref[...] = (_rms(o, ng_ref[...]) * (r * jax.nn.sigmoid(r))).astype(o_ref.dtype)


def _gla(pc, w_gate2, b_gate, norm_g, batch, seq):
    t = pc.shape[0]
    heads = norm_g.shape[0] // GLA_DV
    wg = jnp.pad(w_gate2, ((0, LANES - GLA_GATE_RANK), (0, 0))).astype(BF16)
    nc = seq // GLA_CHUNK
    blk = lambda w, off: pl.BlockSpec((None, GLA_CHUNK, w), lambda b, h, c: (b, c, off + h))
    par = lambda rows, w: pl.BlockSpec((rows, w), lambda b, h, c: (0, h))
    h2 = heads * GLA_DV // GLA_DK
    pc3 = pc.reshape(batch, seq, pc.shape[1])
    out = pl.pallas_call(
        _gla_kernel,
        grid=(batch, heads, nc),
        in_specs=[blk(GLA_DK, 2 * h2), blk(GLA_DK, 2 * h2 + heads), blk(GLA_DV, 0),
                  pl.BlockSpec((None, GLA_CHUNK, LANES), lambda b, h, c: (b, c, 2 * h2 + 2 * heads)),
                  blk(GLA_DV, heads), par(LANES, GLA_DK), par(1, GLA_DK), par(1, GLA_DV)],
        out_specs=blk(GLA_DV, 0),
        out_shape=jax.ShapeDtypeStruct((batch, seq, heads * GLA_DV), BF16),
        scratch_shapes=[pltpu.VMEM((GLA_DV, GLA_DK), F32)],
        compiler_params=_cparams("parallel", "parallel", "arbitrary"),
        name="gla",
    )(pc3, pc3, pc3, pc3, pc3, wg, b_gate.reshape(1, -1), norm_g.reshape(1, -1))
    return out.reshape(t, heads * GLA_DV)


_DIMS = {'nn': (((1,), (0,)), ((), ())), 'nt': (((1,), (1,)), ((), ())), 'tn': (((0,), (0,)), ((), ()))}


def _mm(a, b, form):
    return lax.dot_general(a.astype(BF16), b.astype(BF16), _DIMS[form], preferred_element_type=F32)


def _seg_sum(x, width):
    gi = lax.broadcasted_iota(jnp.int32, (LANES, LANES), 0) // width
    gj = lax.broadcasted_iota(jnp.int32, (LANES, LANES), 1) // width
    ones = (gi == gj).astype(F32)
    return jnp.concatenate([_dot_f32(x[:, c:c + LANES], ones) for c in range(0, x.shape[1], LANES)], axis=1)


def _rwkv_prep_kernel(*refs, width, has_vres):
    if has_vres:
        (p_ref, pp_ref, prm_ref, ml_ref, w2_ref, a2_ref, g2_ref, vf_ref, v0_ref, v1_ref, v2_ref,
         r_ref, lw_ref, k_ref, v_ref, kk_ref, bb_ref, bonus_ref, g_ref) = refs
    else:
        (p_ref, pp_ref, prm_ref, ml_ref, w2_ref, a2_ref, g2_ref,
         r_ref, lw_ref, k_ref, v_ref, kk_ref, bb_ref, bonus_ref, g_ref) = refs
    w = width
    p, pp, prm, ml = p_ref[...], pp_ref[...], prm_ref[...], ml_ref[...]
    lerp = lambda lo, hi, mu: p[:, lo:hi] + (pp[:, lo:hi] - p[:, lo:hi]) * mu
    r = lerp(0, w, prm[0:1])
    k = lerp(w, 2 * w, prm[1:2])
    v = lerp(2 * w, 3 * w, prm[2:3])
    x_wa = lerp(3 * w, 3 * w + LANES, ml[0:1])
    x_g = lerp(3 * w + LANES, 3 * w + 2 * LANES, ml[1:2])
    w_log = -jax.nn.softplus(-(prm[3:4] + _dot(jnp.tanh(x_wa).astype(BF16), w2_ref[...]))) - 0.5
    lw_ref[...] = -jnp.exp(w_log)
    a = jax.nn.sigmoid(prm[4:5] + _dot(x_wa.astype(BF16), a2_ref[...]))
    g_ref[...] = _dot(jax.nn.sigmoid(x_g).astype(BF16), g2_ref[...])
    if has_vres:
        mix = jax.nn.sigmoid(v0_ref[...] + _dot(_dot(v.astype(BF16), v1_ref[...]).astype(BF16), v2_ref[...]))
        v = v + (vf_ref[...] - v) * mix
    kk = k * prm[5:6]
    kk = kk * lax.rsqrt(jnp.maximum(_seg_sum(kk * kk, RWKV_HS), 1e-24))
    k2 = k * (1.0 + (a - 1.0) * prm[6:7])
    r_ref[...] = r
    k_ref[...] = k2
    v_ref[...] = v
    kk_ref[...] = kk
    bb_ref[...] = kk * a
    bonus_ref[...] = _seg_sum(r * k2 * prm[7:8], RWKV_HS) * v


def _rwkv_chunk_kernel(r_ref, lw_ref, k_ref, v_ref, kk_ref, bb_ref, y_ref, s_ref, *, n_chunk):
    c_len, hs = RWKV_CHUNK, RWKV_HS
    rows_n = n_chunk * c_len

    @pl.when(pl.program_id(2) == 0)
    def _():
        s_ref[...] = jnp.zeros_like(s_ref)

    ri = lax.broadcasted_iota(jnp.int32, (rows_n, rows_n), 0)
    ci = lax.broadcasted_iota(jnp.int32, (rows_n, rows_n), 1)
    same = (ri // c_len) == (ci // c_len)
    incl = jnp.logical_and(same, ci <= ri)
    strict = jnp.logical_and(same, ci < ri)
    eye = (ri == ci).astype(F32)
    e_r = lax.broadcasted_iota(jnp.int32, (hs, hs), 0)
    e_c = lax.broadcasted_iota(jnp.int32, (hs, hs), 1)
    sums = _masked_sums(jnp.concatenate([incl, same], axis=0).astype(BF16), lw_ref[...])
    cum_all, cum_c_all = sums[:rows_n], sums[rows_n:]
    ys = []
    for hh in range(LANES // hs):
        sl = slice(hh * hs, (hh + 1) * hs)
        r, lw, k, v, kk, bb = (ref[...][:, sl] for ref in (r_ref, lw_ref, k_ref, v_ref, kk_ref, bb_ref))
        cum, cum_c = cum_all[:, sl], cum_c_all[:, sl]
        g_inv, g_end = jnp.exp(-cum), jnp.exp(cum_c - cum)
        kap, rt = kk * jnp.exp(cum - lw), r * jnp.exp(cum)
        bet, kt = bb * g_inv, k * g_inv
        bet_c, kt_c = bb * g_end, k * g_end
        gram = _mm(jnp.concatenate([kap, rt], axis=0), jnp.concatenate([bet, kt], axis=0), 'nt')
        n_m = jnp.where(strict, gram[:rows_n, :rows_n], 0.0)
        a_kk = jnp.where(strict, gram[:rows_n, rows_n:], 0.0)
        a_rb = jnp.where(incl, gram[rows_n:, :rows_n], 0.0)
        a_rk = jnp.where(incl, gram[rows_n:, rows_n:], 0.0)
        t_inv, pw = eye - n_m, -n_m
        for _ in range(int(math.log2(c_len)) - 1):
            pw = _mm(pw, pw, 'nn')
            t_inv = t_inv + _mm(t_inv, pw, 'nn')
        av = _mm(jnp.concatenate([a_kk, a_rk], axis=0), v, 'nn')
        z = _mm(t_inv, jnp.concatenate([kap, av[:rows_n]], axis=1), 'nn')
        az = _mm(a_rb, z, 'nn')
        r_p, y0 = rt - az[:, :hs], av[rows_n:] - az[:, hs:]
        s = s_ref[hh]
        outs = []
        for c in range(n_chunk):
            rows = slice(c * c_len, (c + 1) * c_len)
            outs.append(_mm(r_p[rows], s, 'nt') + y0[rows])
            zb = _mm(z[rows], bet_c[rows], 'tn')
            g_chunk = jnp.exp(cum_c[c * c_len:c * c_len + 1])
            m = jnp.where(e_r == e_c, jnp.broadcast_to(g_chunk, (hs, hs)), 0.0) - zb[:hs]
            s = _mm(s, m, 'nn') + _mm(v[rows], kt_c[rows], 'tn') - zb[hs:]
        s_ref[hh] = s
        ys.append(jnp.concatenate(outs, axis=0))
    y_ref[...] = jnp.concatenate(ys, axis=1)


def _rwkv_post_kernel(y_ref, bonus_ref, g_ref, lg_ref, lb_ref, o_ref):
    y = y_ref[...]
    mu = _seg_sum(y, RWKV_HS) * (1.0 / RWKV_HS)
    yc = y - mu
    var = _seg_sum(yc * yc, RWKV_HS) * (1.0 / RWKV_HS)
    yn = yc * lax.rsqrt(var + RWKV_GN_EPS) * lg_ref[...] + lb_ref[...]
    o_ref[...] = ((yn + bonus_ref[...]) * g_ref[...]).astype(o_ref.dtype)


def _rwkv(pd, mu_rkv, mu_lora, w0, w2, a0, a2, g2, k_k, k_a, r_k, lnx_g, lnx_b, v_first, vres, batch, seq):
    t, wd = pd.shape
    w = w0.shape[0]
    pd3 = pd.reshape(batch, seq, wd)
    prev = jnp.pad(pd3[:, :-1], ((0, 0), (1, 0), (0, 0))).reshape(t, wd)
    prm = jnp.stack([mu_rkv[0], mu_rkv[1], mu_rkv[2], w0, a0, k_k, k_a, r_k.reshape(-1)])
    zl = jnp.zeros((RWKV_LORA,), F32)
    ml = jnp.stack([jnp.concatenate([mu_lora[0], mu_lora[1]]), jnp.concatenate([mu_lora[2], zl])])
    zw = jnp.zeros((LANES - RWKV_LORA, w), F32)
    w2p = jnp.concatenate([w2, zw]).astype(BF16)
    a2p = jnp.concatenate([zw, a2]).astype(BF16)
    g2p = jnp.concatenate([g2, zw]).astype(BF16)
    has_vres = vres is not None
    tm = _tile(t, 256, 8)
    row = lambda width: pl.BlockSpec((tm, width), lambda i: (i, 0))
    full = lambda a: pl.BlockSpec(a.shape, lambda i: (0, 0))
    args = [pd, prev, prm, ml, w2p, a2p, g2p]
    specs = [row(wd), row(wd), full(prm), full(ml), full(w2p), full(a2p), full(g2p)]
    if has_vres:
        v0, v1, v2 = vres
        v1p = _pad_cols(v1, LANES).astype(BF16)
        v2p = jnp.pad(v2, ((0, LANES - v2.shape[0]), (0, 0))).astype(BF16)
        v0r = v0.reshape(1, w)
        args += [v_first, v0r, v1p, v2p]
        specs += [row(w), full(v0r), full(v1p), full(v2p)]
    r, lw, k2, v, kk, bb, bonus, g = pl.pallas_call(
        functools.partial(_rwkv_prep_kernel, width=w, has_vres=has_vres),
        grid=(t // tm,),
        in_specs=specs,
        out_specs=[row(w)] * 8,
        out_shape=[jax.ShapeDtypeStruct((t, w), F32)] * 8,
        compiler_params=_cparams("parallel"),
        name="rwkv_prep",
    )(*args)

    n_chunk = 4 if seq % (4 * RWKV_CHUNK) == 0 else 1
    rows_n = n_chunk * RWKV_CHUNK
    pairs = w // LANES
    blk = pl.BlockSpec((None, rows_n, LANES), lambda b, h, c: (b, c, h))
    r3 = lambda a: a.reshape(batch, seq, w)
    y = pl.pallas_call(
        functools.partial(_rwkv_chunk_kernel, n_chunk=n_chunk),
        grid=(batch, pairs, seq // rows_n),
        in_specs=[blk] * 6,
        out_specs=blk,
        out_shape=jax.ShapeDtypeStruct((batch, seq, w), F32),
        scratch_shapes=[pltpu.VMEM((LANES // RWKV_HS, RWKV_HS, RWKV_HS), F32)],
        compiler_params=_cparams("parallel", "parallel", "arbitrary"),
        name="rwkv_chunk",
    )(r3(r), r3(lw), r3(k2), r3(v), r3(kk), r3(bb))

    lg, lb = lnx_g.reshape(1, w), lnx_b.reshape(1, w)
    out = pl.pallas_call(
        _rwkv_post_kernel,
        grid=(t // tm,),
        in_specs=[row(w), row(w), row(w), full(lg), full(lb)],
        out_specs=row(w),
        out_shape=jax.ShapeDtypeStruct((t, w), BF16),
        compiler_params=_cparams("parallel"),
        name="rwkv_post",
    )(y.reshape(t, w), bonus, g, lg, lb)
    return out, (v_first if has_vres else v)


N_BRANCH = 4


def _merge_kernel(*refs):
    x_ref = refs[0]
    y_refs = refs[1:1 + N_BRANCH]
    wg_refs = refs[1 + N_BRANCH:1 + 2 * N_BRANCH]
    wb_refs = refs[1 + 2 * N_BRANCH:1 + 3 * N_BRANCH]
    o_ref = refs[-1]
    x = x_ref[...]
    acc = None
    for y_ref, wg_ref, wb_ref in zip(y_refs, wg_refs, wb_refs):
        term = jax.nn.sigmoid(_dot(x, wg_ref[...])) * _dot(y_ref[...], wb_ref[...])
        acc = term if acc is None else acc + term
    o_ref[...] = acc.astype(o_ref.dtype)


def _merge(xb, ys, w_gate, w_branch):
    t, d = xb.shape
    w = ys[0].shape[1]
    tm, tn = _tile(t, 512, 8), _tile(d, 256)
    nj = d // tn
    gate_spec = lambda i: pl.BlockSpec((d, tn), lambda r, j: (0, i * nj + j))
    br_spec = lambda i: pl.BlockSpec((None, w, tn), lambda r, j: (i, 0, j))
    return pl.pallas_call(
        _merge_kernel,
        grid=(t // tm, nj),
        in_specs=[pl.BlockSpec((tm, d), lambda r, j: (r, 0))] + [pl.BlockSpec((tm, w), lambda r, j: (r, 0))] * N_BRANCH
        + [gate_spec(i) for i in range(N_BRANCH)] + [br_spec(i) for i in range(N_BRANCH)],
        out_specs=pl.BlockSpec((tm, tn), lambda r, j: (r, j)),
        out_shape=jax.ShapeDtypeStruct((t, d), BF16),
        compiler_params=_cparams("parallel", "parallel"),
        name="merge",
    )(xb, *ys, *([w_gate] * N_BRANCH), *([w_branch] * N_BRANCH))


def _router_kernel(x_ref, w_ref, b_ref, o_ref, *, n_exp):
    tm = x_ref.shape[0]
    scores = jax.nn.sigmoid(_dot(x_ref[...], w_ref[...]))
    lane = lax.broadcasted_iota(jnp.int32, (tm, LANES), 1)
    work = jnp.where(lane < n_exp, scores + b_ref[...], -jnp.inf)
    chosen = jnp.zeros((tm, LANES), F32)
    for _ in range(TOP_K):
        best = jnp.max(work, axis=1, keepdims=True)
        first = jnp.min(jnp.where(work == best, lane, LANES), axis=1, keepdims=True)
        hit = lane == first
        chosen = jnp.where(hit, scores, chosen)
        work = jnp.where(hit, -jnp.inf, work)
    comb = chosen / jnp.sum(chosen, axis=1, keepdims=True) * ROUTED_SCALE
    o_ref[...] = jnp.where(lane == n_exp, 1.0, comb)


def _router(xb, router_w, router_bias):
    t, d = xb.shape
    n_exp = router_w.shape[1]
    assert n_exp < LANES
    tm = _tile(t, 512, 8)
    wr = _pad_cols(router_w, LANES).astype(BF16)
    br = _pad_cols(router_bias.reshape(1, n_exp), LANES)
    return pl.pallas_call(
        functools.partial(_router_kernel, n_exp=n_exp),
        grid=(t // tm,),
        in_specs=[pl.BlockSpec((tm, d), lambda i: (i, 0)), pl.BlockSpec((d, LANES), lambda i: (0, 0)),
                  pl.BlockSpec((1, LANES), lambda i: (0, 0))],
        out_specs=pl.BlockSpec((tm, LANES), lambda i: (i, 0)),
        out_shape=jax.ShapeDtypeStruct((t, LANES), F32),
        compiler_params=_cparams("parallel"),
        name="router",
    )(xb, wr, br)


MOE_COLS = 1024


MOE_GROUP = 2


def _moe_kernel(x_ref, c_ref, wg_ref, wu_ref, wd_ref, o_ref, *, hid):
    g = pl.program_id(1)
    tm, d = o_ref.shape

    @pl.when(g == 0)
    def _():
        o_ref[...] = jnp.zeros_like(o_ref)

    x = x_ref[...]
    comb = c_ref[...]
    lane = lax.broadcasted_iota(jnp.int32, (tm, LANES), 1)
    c_wide = jnp.concatenate(
        [jnp.broadcast_to(jnp.sum(jnp.where(lane == g * MOE_GROUP + u, comb, 0.0), axis=1, keepdims=True), (tm, hid))
         for u in range(MOE_GROUP)], axis=1)
    gate = _dot(x, wg_ref[...])
    h = (gate * jax.nn.sigmoid(gate) * _dot(x, wu_ref[...]) * c_wide).astype(BF16)
    cols = _tile(d, MOE_COLS)
    for lo in range(0, d, cols):
        o_ref[:, lo:lo + cols] += _dot(h, wd_ref[:, lo:lo + cols])


def _moe_weights(routed, shared, transpose):
    w = jnp.concatenate([routed.astype(BF16), shared[None].astype(BF16)], axis=0)
    n_e, a, b = w.shape
    n_g = -(-n_e // MOE_GROUP)
    w = jnp.pad(w, ((0, n_g * MOE_GROUP - n_e), (0, 0), (0, 0))).reshape(n_g, MOE_GROUP, a, b)
    if transpose:
        return w.transpose(0, 2, 1, 3).reshape(n_g, a, MOE_GROUP * b)
    return w.reshape(n_g, MOE_GROUP * a, b)


def _moe(xb, comb, w_gate, w_up, w_down):
    t, d = xb.shape
    n_g, _, ghid = w_gate.shape
    tm = _tile(t, 512, 8)
    once = pl.Buffered(1)
    return pl.pallas_call(
        functools.partial(_moe_kernel, hid=ghid // MOE_GROUP),
        grid=(t // tm, n_g),
        in_specs=[pl.BlockSpec((tm, d), lambda i, e: (i, 0), pipeline_mode=once),
                  pl.BlockSpec((tm, LANES), lambda i, e: (i, 0), pipeline_mode=once),
                  pl.BlockSpec((None, d, ghid), lambda i, e: (e, 0, 0)),
                  pl.BlockSpec((None, d, ghid), lambda i, e: (e, 0, 0)),
                  pl.BlockSpec((None, ghid, d), lambda i, e: (e, 0, 0))],
        out_specs=pl.BlockSpec((tm, d), lambda i, e: (i, 0), pipeline_mode=once),
        out_shape=jax.ShapeDtypeStruct((t, d), F32),
        compiler_params=_cparams("parallel", "arbitrary"),
        name="moe",
    )(xb, comb, w_gate, w_up, w_down)


PROJ_ALIGN = 512


def _split_w_in(w, d):
    wdt = d // N_BRANCH
    gla_heads = wdt // GLA_DV
    names = (('gate', N_BRANCH * d), ('mla_cq', MLA_Q_RANK), ('mla_ckv', MLA_KV_RANK), ('mla_kr', MLA_ROPE),
             ('dsa_q', wdt), ('dsa_k', DSA_DH), ('dsa_v', DSA_DH),
             ('idx_q', IDX_HEADS * IDX_DIM), ('idx_k', IDX_DIM), ('idx_w', IDX_HEADS),
             ('gla_q', gla_heads * GLA_DK), ('gla_k', gla_heads * GLA_DK), ('gla_v', wdt),
             ('gla_a', GLA_GATE_RANK), ('gla_r', wdt),
             ('rwkv_r', wdt), ('rwkv_k', wdt), ('rwkv_v', wdt),
             ('rwkv_w', RWKV_LORA), ('rwkv_a', RWKV_LORA), ('rwkv_g', RWKV_LORA))
    parts, off = {}, 0
    for name, width in names:
        parts[name] = w[:, off:off + width]
        off += width
    assert off == w.shape[1], (off, w.shape)
    return parts


def _group(cols, dtype=BF16):
    wcat = jnp.concatenate(cols, axis=1)
    return _pad_cols(wcat, -(-wcat.shape[1] // PROJ_ALIGN) * PROJ_ALIGN).astype(dtype)


def kernel(x, positions, ln_in_g, ln_in_b, w_in, w_branch, w_out, mla_q_norm, mla_w_uq, mla_kv_norm, mla_w_ukv,
           gla_w_gate2, gla_b_gate, gla_norm_g, rwkv_mu_rkv, rwkv_mu_lora, rwkv_w0, rwkv_w2, rwkv_a0, rwkv_a2,
           rwkv_g2, rwkv_k_k, rwkv_k_a, rwkv_r_k, rwkv_lnx_g, rwkv_lnx_b, rwkv_v0, rwkv_v1, rwkv_v2,
           ln_mix_g, ln_mix_b, router_w, router_bias, exp_w_gate, exp_w_up, exp_w_down,
           sh_w_gate, sh_w_up, sh_w_down, ln_ffn_g, ln_ffn_b):
    batch, seq, d = x.shape
    depth = w_in.shape[0]
    t, wdt = batch * seq, d // N_BRANCH
    alpha = (2 * depth) ** 0.25
    ct, st = _rope_tables(positions)
    xf, xb = _layer_norm(x.reshape(t, d), None, ln_in_g, ln_in_b)
    v_first = None
    for l in range(depth):
        p = _split_w_in(w_in[l], d)
        pad128 = lambda w: _pad_cols(w, LANES)
        idx_q = jnp.pad(p['idx_q'].reshape(d, IDX_HEADS, IDX_DIM),
                        ((0, 0), (0, 0), (0, LANES - IDX_DIM))).reshape(d, IDX_HEADS * LANES)
        w_lat = _group([p['mla_cq'], p['mla_ckv'], pad128(p['mla_kr']), pad128(_swap_halves(p['mla_kr']))])
        dsa_q = p['dsa_q'] * (DSA_DH ** -0.5 * LOG2E)
        w_dsa = _group([idx_q, dsa_q, p['dsa_k'], p['dsa_v'], pad128(p['idx_k']), pad128(p['idx_w'])])
        w_gla = _group([p['gla_v'], p['gla_r'], p['gla_q'], p['gla_k'], pad128(p['gla_a'])])
        w_rwkv = _group([p['rwkv_r'], p['rwkv_k'], p['rwkv_v'], p['rwkv_w'], p['rwkv_a'], pad128(p['rwkv_g'])])
        lat = _matmul(xb, w_lat, F32)
        pb = _matmul(xb, w_dsa, BF16)
        pc = _matmul(xb, w_gla, F32)
        pd = _matmul(xb, w_rwkv, F32)

        y_mla = _mla(lat, ct, st, mla_q_norm[l], mla_w_uq[l], mla_kv_norm[l], mla_w_ukv[l], batch, seq)
        y_dsa = _dsa(pb, batch, seq, wdt)
        y_gla = _gla(pc, gla_w_gate2[l], gla_b_gate[l], gla_norm_g[l], batch, seq)
        vres = None if l == 0 else (rwkv_v0[l - 1], rwkv_v1[l - 1], rwkv_v2[l - 1])
        y_rwkv, v_first = _rwkv(pd, rwkv_mu_rkv[l], rwkv_mu_lora[l], rwkv_w0[l], rwkv_w2[l], rwkv_a0[l], rwkv_a2[l],
                                rwkv_g2[l], rwkv_k_k[l], rwkv_k_a[l], rwkv_r_k[l], rwkv_lnx_g[l], rwkv_lnx_b[l],
                                v_first, vres, batch, seq)

        merged = _merge(xb, (y_mla, y_dsa, y_gla, y_rwkv), p['gate'].astype(BF16), w_branch[l].astype(BF16))
        xf, xb = _layer_norm(xf, _matmul(merged, w_out[l].astype(BF16), F32), ln_mix_g[l], ln_mix_b[l], alpha)

        comb = _router(xb, router_w[l], router_bias[l])
        f = _moe(xb, comb, _moe_weights(exp_w_gate[l], sh_w_gate[l], True), _moe_weights(exp_w_up[l], sh_w_up[l], True),
                 _moe_weights(exp_w_down[l], sh_w_down[l], False))
        xf, xb = _layer_norm(xf, f, ln_ffn_g[l], ln_ffn_b[l], alpha)
    return xf.reshape(batch, seq, d)
```

```python
import functools
import math

import jax
import jax.numpy as jnp
from jax import lax
from jax.experimental import pallas as pl
from jax.experimental.pallas import tpu as pltpu

F32 = jnp.float32
BF16 = jnp.bfloat16

LANES = 128

MLA_NOPE, MLA_ROPE, MLA_V = 128, 64, 128
MLA_Q_RANK, MLA_KV_RANK = 768, 256
ROPE_THETA = 10000.0
DSA_DH = 128
IDX_HEADS, IDX_DIM, IDX_TOPK_MAX = 16, 64, 256
GLA_DV, GLA_DK, GLA_GATE_RANK, GLA_TAU, GLA_CHUNK = 256, 128, 16, 16.0, 64
GLA_SUB = 16
RWKV_HS, RWKV_LORA, RWKV_GN_EPS = 64, 64, 64e-5
RWKV_CHUNK = 64
TOP_K, ROUTED_SCALE = 8, 2.5
LN_EPS, RMS_EPS = 1e-5, 1e-6
NEG_BIG = -1e30
LOG2E = math.log2(math.e)

VMEM_LIMIT = 56 * 1024 * 1024


def _cparams(*sem):
    return pltpu.CompilerParams(dimension_semantics=sem, vmem_limit_bytes=VMEM_LIMIT)


def _tile(n, pref, unit=LANES):
    if n <= pref:
        return n
    t = (pref // unit) * unit
    while t > unit and n % t:
        t -= unit
    assert n % t == 0, (n, pref, unit)
    return t


def _dot(a, b):
    return jnp.dot(a, b, preferred_element_type=F32)


def _dot_t(a, b):
    return lax.dot_general(a, b, (((1,), (1,)), ((), ())), preferred_element_type=F32)


def _dot_f32(a, b):
    return jnp.dot(a, b, preferred_element_type=F32, precision=lax.Precision.HIGHEST)


def _masked_sums(mask, x):
    w = x.shape[1]
    hi = x.astype(BF16)
    rest = x - hi.astype(F32)
    mid = rest.astype(BF16)
    lo = (rest - mid.astype(F32)).astype(BF16)
    s = _dot(mask, jnp.concatenate([hi, mid, lo], axis=1))
    return s[:, :w] + s[:, w:2 * w] + s[:, 2 * w:]


def _mm_kernel(a_ref, b_ref, o_ref):
    o_ref[...] = _dot(a_ref[...], b_ref[...]).astype(o_ref.dtype)


def _matmul(a, b, out_dtype, tm=1024, tn=512):
    m, k = a.shape
    n = b.shape[1]
    tm, tn = _tile(m, tm, 8), _tile(n, tn)
    return pl.pallas_call(
        _mm_kernel,
        grid=(m // tm, n // tn),
        in_specs=[pl.BlockSpec((tm, k), lambda i, j: (i, 0)), pl.BlockSpec((k, tn), lambda i, j: (0, j))],
        out_specs=pl.BlockSpec((tm, tn), lambda i, j: (i, j)),
        out_shape=jax.ShapeDtypeStruct((m, n), out_dtype),
        compiler_params=_cparams("parallel", "parallel"),
        name="matmul",
    )(a, b)


def _ln_kernel(*refs, alpha, n_res):
    x_ref, res_refs = refs[0], refs[1:1 + n_res]
    g_ref, b_ref, o_ref, ob_ref = refs[1 + n_res:]
    x = x_ref[...]
    if n_res:
        x = alpha * x
        for f_ref in res_refs:
            x = x + f_ref[...]
    mu = jnp.mean(x, axis=-1, keepdims=True)
    xc = x - mu
    var = jnp.mean(xc * xc, axis=-1, keepdims=True)
    y = xc * lax.rsqrt(var + LN_EPS) * g_ref[...] + b_ref[...]
    o_ref[...] = y
    ob_ref[...] = y.astype(BF16)


def _layer_norm(x, res, g, b, alpha=1.0):
    t, d = x.shape
    tm = _tile(t, 128, 8)
    row = pl.BlockSpec((tm, d), lambda i: (i, 0))
    vec = pl.BlockSpec((1, d), lambda i: (0, 0))
    args = (x,) + tuple(res)
    return pl.pallas_call(
        functools.partial(_ln_kernel, alpha=alpha, n_res=len(res)),
        grid=(t // tm,),
        in_specs=[row] * len(args) + [vec, vec],
        out_specs=[row, row],
        out_shape=[jax.ShapeDtypeStruct((t, d), F32), jax.ShapeDtypeStruct((t, d), BF16)],
        compiler_params=_cparams("parallel"),
        name="layer_norm",
    )(*args, g.reshape(1, d), b.reshape(1, d))


def _pad_cols(w, width):
    return jnp.pad(w, ((0, 0), (0, width - w.shape[1])))


def _rms(x, g):
    return x * lax.rsqrt(jnp.mean(x * x, axis=-1, keepdims=True) + RMS_EPS) * g


def _mla_prep_kernel(lat_ref, ct_ref, st_ref, qg_ref, kg_ref, wqn_ref, wqr_ref, wqs_ref, wkn_ref, wv_ref,
                     qn_ref, qr_ref, kn_ref, kr_ref, v_ref, *, heads, scale):
    lat = lat_ref[...]
    cq = lat[:, :MLA_Q_RANK]
    ckv = lat[:, MLA_Q_RANK:MLA_Q_RANK + MLA_KV_RANK]
    kr = lat[:, MLA_Q_RANK + MLA_KV_RANK:MLA_Q_RANK + MLA_KV_RANK + LANES]
    krs = lat[:, MLA_Q_RANK + MLA_KV_RANK + LANES:MLA_Q_RANK + MLA_KV_RANK + 2 * LANES]
    ct, st = ct_ref[...], st_ref[...]
    nq = _rms(cq, qg_ref[...]).astype(BF16)
    nkv = _rms(ckv, kg_ref[...]).astype(BF16)
    qn_ref[...] = (_dot(nq, wqn_ref[...]) * scale).astype(BF16)
    cth = jnp.concatenate([ct] * heads, axis=1)
    sth = jnp.concatenate([st] * heads, axis=1)
    qr_ref[...] = ((_dot(nq, wqr_ref[...]) * cth + _dot(nq, wqs_ref[...]) * sth) * scale).astype(BF16)
    kn_ref[...] = _dot(nkv, wkn_ref[...]).astype(BF16)
    v_ref[...] = _dot(nkv, wv_ref[...]).astype(BF16)
    kr_ref[...] = (kr * ct + krs * st).astype(BF16)


def _flash_kernel(qn_ref, qr_ref, kn_ref, kr_ref, v_ref, o_ref, m_ref, l_ref, acc_ref, *, tq, tk):
    i, j = pl.program_id(2), pl.program_id(3)

    @pl.when(j == 0)
    def _():
        m_ref[...] = jnp.full_like(m_ref, NEG_BIG)
        l_ref[...] = jnp.zeros_like(l_ref)
        acc_ref[...] = jnp.zeros_like(acc_ref)

    def step(masked):
        q = jnp.concatenate([qn_ref[...], qr_ref[...]], axis=1)
        k = jnp.concatenate([kn_ref[...], kr_ref[...]], axis=1)
        s = _dot_t(q, k)
        if masked:
            qpos = i * tq + lax.broadcasted_iota(jnp.int32, (tq, tk), 0)
            kpos = j * tk + lax.broadcasted_iota(jnp.int32, (tq, tk), 1)
            s = jnp.where(kpos <= qpos, s, -jnp.inf)
        m_prev = m_ref[...]
        m_new = jnp.maximum(m_prev, jnp.max(s, axis=1, keepdims=True))
        alpha = jnp.exp2(m_prev - m_new)
        p = jnp.exp2(s - jnp.concatenate([m_new] * (tk // LANES), axis=1))
        l_ref[...] = alpha * l_ref[...] + jnp.sum(p, axis=1, keepdims=True)
        acc_ref[...] = alpha * acc_ref[...] + _dot(p.astype(BF16), v_ref[...])
        m_ref[...] = m_new

    below = j * tk + tk - 1 <= i * tq
    touches = j * tk <= i * tq + tq - 1

    @pl.when(below)
    def _():
        step(False)

    @pl.when(jnp.logical_and(touches, jnp.logical_not(below)))
    def _():
        step(True)

    @pl.when(j == pl.num_programs(3) - 1)
    def _():
        o_ref[...] = (acc_ref[...] / l_ref[...]).astype(o_ref.dtype)


def _rope_tables(positions):
    half = MLA_ROPE // 2
    inv = ROPE_THETA ** (-jnp.arange(0, MLA_ROPE, 2, dtype=F32) / MLA_ROPE)
    ang = positions.reshape(-1).astype(F32)[:, None] * inv
    cos, sin = jnp.cos(ang), jnp.sin(ang)
    zero = jnp.zeros((ang.shape[0], LANES - 2 * half), F32)
    return jnp.concatenate([cos, cos, zero], 1), jnp.concatenate([-sin, sin, zero], 1)


def _swap_halves(w):
    half = w.shape[-1] // 2
    return jnp.concatenate([w[..., half:], w[..., :half]], -1)


def _mla(lat, ct, st, q_norm, w_uq, kv_norm, w_ukv, batch, seq):
    t = lat.shape[0]
    heads = w_uq.shape[1] // (MLA_NOPE + MLA_ROPE)
    hw = heads * LANES
    wq = w_uq.reshape(MLA_Q_RANK, heads, MLA_NOPE + MLA_ROPE)
    wqn = wq[:, :, :MLA_NOPE].reshape(MLA_Q_RANK, hw).astype(BF16)
    rope_pad = ((0, 0), (0, 0), (0, LANES - MLA_ROPE))
    wqr = jnp.pad(wq[:, :, MLA_NOPE:], rope_pad).reshape(MLA_Q_RANK, hw).astype(BF16)
    wqs = jnp.pad(_swap_halves(wq[:, :, MLA_NOPE:]), rope_pad).reshape(MLA_Q_RANK, hw).astype(BF16)
    wkv = w_ukv.reshape(MLA_KV_RANK, heads, MLA_NOPE + MLA_V)
    wkn = wkv[:, :, :MLA_NOPE].reshape(MLA_KV_RANK, hw).astype(BF16)
    wv = wkv[:, :, MLA_NOPE:].reshape(MLA_KV_RANK, hw).astype(BF16)

    tm = _tile(t, 512, 8)
    row = lambda w: pl.BlockSpec((tm, w), lambda i: (i, 0))
    full = lambda a: pl.BlockSpec(a.shape, lambda i: (0, 0))
    qg, kg = q_norm.reshape(1, -1), kv_norm.reshape(1, -1)
    qn, qr, kn, kr, v = pl.pallas_call(
        functools.partial(_mla_prep_kernel, heads=heads, scale=(MLA_NOPE + MLA_ROPE) ** -0.5 * LOG2E),
        grid=(t // tm,),
        in_specs=[row(lat.shape[1]), row(LANES), row(LANES), full(qg), full(kg),
                  full(wqn), full(wqr), full(wqs), full(wkn), full(wv)],
        out_specs=[row(hw), row(hw), row(hw), row(LANES), row(hw)],
        out_shape=[jax.ShapeDtypeStruct((t, hw), BF16)] * 3 + [jax.ShapeDtypeStruct((t, LANES), BF16),
                                                               jax.ShapeDtypeStruct((t, hw), BF16)],
        compiler_params=_cparams("parallel"),
        name="mla_prep",
    )(lat, ct, st, qg, kg, wqn, wqr, wqs, wkn, wv)

    tq = tk = _tile(seq, 1024, 8)
    nq, nk = seq // tq, seq // tk
    r3 = lambda a: a.reshape(batch, seq, a.shape[1])
    last = lambda i, j: jnp.minimum(j, ((i + 1) * tq - 1) // tk)
    q_spec = pl.BlockSpec((None, tq, LANES), lambda b, h, i, j: (b, i, h))
    k_spec = pl.BlockSpec((None, tk, LANES), lambda b, h, i, j: (b, last(i, j), h))
    kr_spec = pl.BlockSpec((None, tk, LANES), lambda b, h, i, j: (b, last(i, j), 0))
    out = pl.pallas_call(
        functools.partial(_flash_kernel, tq=tq, tk=tk),
        grid=(batch, heads, nq, nk),
        in_specs=[q_spec, q_spec, k_spec, kr_spec, k_spec],
        out_specs=q_spec,
        out_shape=jax.ShapeDtypeStruct((batch, seq, hw), BF16),
        scratch_shapes=[pltpu.VMEM((tq, LANES), F32), pltpu.VMEM((tq, LANES), F32), pltpu.VMEM((tq, LANES), F32)],
        compiler_params=_cparams("parallel", "parallel", "parallel", "arbitrary"),
        name="mla_flash",
    )(r3(qn), r3(qr), r3(kn), r3(kr), r3(v))
    return out.reshape(t, hw)


DSA_TQ = 128
DSA_TK = 512
INT_MIN = -2 ** 31


def _dsa_kernel(iq_ref, q_ref, k_ref, v_ref, ik_ref, iw_ref, o_ref, key_ref, m_ref, l_ref, acc_ref,
                *, tq, tk, n_sel, heads, pos_bits):
    i = pl.program_id(1)
    n_kt = (i * tq + tq - 1) // tk + 1
    reps = tk // LANES
    wide = lambda a: jnp.concatenate([a] * reps, axis=1)
    qpos = i * tq + lax.broadcasted_iota(jnp.int32, (tq, tk), 0)
    lane_pos = lax.broadcasted_iota(jnp.int32, (tq, tk), 1)

    iq = iq_ref[...]
    iw = iw_ref[...].astype(F32) * (IDX_HEADS ** -0.5 * IDX_DIM ** -0.5)
    iw_b = [jnp.broadcast_to(iw[:, h:h + 1], (tq, LANES)) for h in range(IDX_HEADS)]

    def score_tile(j, carry):
        ikj = ik_ref[pl.ds(pl.multiple_of(j * tk, tk), tk), :]
        sc = jnp.zeros((tq, tk), F32)
        for h in range(IDX_HEADS):
            logit = _dot_t(iq[:, h * LANES:(h + 1) * LANES], ikj)
            sc = sc + wide(iw_b[h]) * jnp.maximum(logit, 0.0)
        sc = jnp.where(j * tk + lane_pos <= qpos, sc + 0.0, -jnp.inf)
        bits = lax.bitcast_convert_type(sc, jnp.int32)
        key_ref[j] = jnp.where(bits >= 0, bits, bits ^ 0x7FFFFFFF)
        return carry

    lax.fori_loop(0, n_kt, score_tile, 0)

    def count(pred):
        def body(j, acc):
            c = pred(key_ref[j], j * tk + lane_pos).astype(jnp.int32)
            for rep in range(reps):
                acc = acc + c[:, rep * LANES:(rep + 1) * LANES]
            return acc
        acc = lax.fori_loop(0, n_kt, body, jnp.zeros((tq, LANES), jnp.int32))
        return jnp.broadcast_to(jnp.sum(acc, axis=1, keepdims=True), (tq, LANES))

    def thr_bit(bit_i, thr):
        cand = thr ^ jnp.left_shift(jnp.int32(1), 31 - bit_i)
        cnt = count(lambda key, pos: key >= wide(cand))
        return jnp.where(cnt >= n_sel, cand, thr)

    thr = lax.fori_loop(0, 32, thr_bit, jnp.full((tq, LANES), INT_MIN, jnp.int32))
    cnt_gt = count(lambda key, pos: key > wide(thr))
    cnt_ge = count(lambda key, pos: key >= wide(thr))
    need = n_sel - cnt_gt

    def tie_bit(bit_i, cut):
        cand = cut | jnp.left_shift(jnp.int32(1), pos_bits - 1 - bit_i)
        cnt = count(lambda key, pos: jnp.logical_and(key == wide(thr), pos < wide(cand)))
        return jnp.where(cnt < need, cand, cut)

    surplus = jnp.max(cnt_ge - cnt_gt - need) > 0
    cut = lax.cond(surplus,
                   lambda: lax.fori_loop(0, pos_bits, tie_bit, jnp.zeros((tq, LANES), jnp.int32)),
                   lambda: jnp.full((tq, LANES), 2 ** 31 - 1, jnp.int32))

    q = q_ref[...]
    q_all = jnp.concatenate([q[:, h * LANES:(h + 1) * LANES] for h in range(heads)], axis=0)
    m_ref[...] = jnp.full_like(m_ref, NEG_BIG)
    l_ref[...] = jnp.zeros_like(l_ref)
    acc_ref[...] = jnp.zeros_like(acc_ref)

    def attend(j, carry):
        rows = pl.ds(pl.multiple_of(j * tk, tk), tk)
        s = _dot_t(q_all, k_ref[rows, :])
        key, pos = key_ref[j], j * tk + lane_pos
        keep = jnp.where(key > wide(thr), 0.0,
                         jnp.where(jnp.logical_and(key == wide(thr), pos <= wide(cut)), 0.0, -jnp.inf))
        keep = jnp.where(pos <= qpos, keep, -jnp.inf)
        s = s + jnp.concatenate([keep] * heads, axis=0)
        m_prev = m_ref[...]
        m_new = jnp.maximum(m_prev, jnp.max(s, axis=1, keepdims=True))
        alpha = jnp.exp2(m_prev - m_new)
        p = jnp.exp2(s - jnp.concatenate([m_new] * reps, axis=1))
        l_ref[...] = alpha * l_ref[...] + jnp.sum(p, axis=1, keepdims=True)
        acc_ref[...] = alpha * acc_ref[...] + _dot(p.astype(BF16), v_ref[rows, :])
        m_ref[...] = m_new
        return carry

    lax.fori_loop(0, n_kt, attend, 0)
    out = acc_ref[...] / l_ref[...]
    o_ref[...] = jnp.concatenate([out[h * tq:(h + 1) * tq] for h in range(heads)], axis=1).astype(o_ref.dtype)


def _dsa(pb, batch, seq, width):
    t, wb = pb.shape
    heads = width // DSA_DH
    iqw = IDX_HEADS * LANES
    tq, tk = _tile(seq, DSA_TQ, 8), _tile(seq, DSA_TK)
    n_sel = min(IDX_TOPK_MAX, seq // 4)
    c0 = (iqw + width) // LANES
    pb3 = pb.reshape(batch, seq, wb)
    qblk = lambda w, idx: pl.BlockSpec((None, tq, w), lambda b, i: (b, i, idx))
    seqblk = lambda idx: pl.BlockSpec((None, seq, LANES), lambda b, i: (b, 0, idx))
    out = pl.pallas_call(
        functools.partial(_dsa_kernel, tq=tq, tk=tk, n_sel=n_sel, heads=heads,
                          pos_bits=max(1, (seq - 1).bit_length())),
        grid=(batch, seq // tq),
        in_specs=[qblk(iqw, 0), qblk(width, iqw // width), seqblk(c0), seqblk(c0 + 1), seqblk(c0 + 2),
                  qblk(LANES, c0 + 3)],
        out_specs=qblk(width, 0),
        out_shape=jax.ShapeDtypeStruct((batch, seq, width), BF16),
        scratch_shapes=[pltpu.VMEM((seq // tk, tq, tk), jnp.int32), pltpu.VMEM((heads * tq, LANES), F32),
                        pltpu.VMEM((heads * tq, LANES), F32), pltpu.VMEM((heads * tq, LANES), F32)],
        compiler_params=_cparams("parallel", "arbitrary"),
        name="dsa",
    )(pb3, pb3, pb3, pb3, pb3, pb3)
    return out.reshape(t, width)


def _gla_kernel(q_ref, k_ref, v_ref, a_ref, r_ref, wg_ref, bg_ref, ng_ref, o_ref, state_ref):
    c_len, sub = GLA_CHUNK, GLA_SUB

    @pl.when(pl.program_id(2) == 0)
    def _():
        state_ref[...] = jnp.zeros_like(state_ref)

    z_all = _dot(a_ref[...].astype(BF16), wg_ref[...]) + bg_ref[...]
    g_all = jax.nn.log_sigmoid(z_all) / GLA_TAU
    ri = lax.broadcasted_iota(jnp.int32, (c_len, c_len), 0)
    ci = lax.broadcasted_iota(jnp.int32, (c_len, c_len), 1)
    b_all = _masked_sums((ri >= ci).astype(BF16), g_all)
    ones = jnp.ones((GLA_DK, LANES), BF16)
    trow = lax.broadcasted_iota(jnp.int32, (sub, GLA_DK), 0)
    lane = lax.broadcasted_iota(jnp.int32, (sub, LANES), 1)
    for hh in range(state_ref.shape[0]):
        ksl = slice(hh * GLA_DK, (hh + 1) * GLA_DK)
        vsl = slice(hh * GLA_DV, (hh + 1) * GLA_DV)
        o = _gla_head(q_ref[:, ksl] * GLA_DK ** -0.5, k_ref[:, ksl], v_ref[:, vsl].astype(BF16), b_all[:, ksl],
                      state_ref.at[hh], ones, trow, lane)
        r = r_ref[:, vsl]
        o_ref[:, vsl] = (_rms(o, ng_ref[:, vsl]) * (r * jax.nn.sigmoid(r))).astype(o_ref.dtype)


def _gla_head(q, k, vb, b, state_ref, ones, trow, lane):
    c_len, sub = GLA_CHUNK, GLA_SUB
    state_t = state_ref[...]
    o_inter = _dot_t((q * jnp.exp(b)).astype(BF16), state_t.astype(BF16))
    outs = []
    for blk in range(c_len // sub):
        lo = blk * sub
        qi, bi, ki = q[lo:lo + sub], b[lo:lo + sub], k[lo:lo + sub]
        rows = [qi * ki[s:s + 1] * jnp.exp(jnp.where(trow >= s, bi - bi[s:s + 1], -jnp.inf)) for s in range(sub)]
        prod = jnp.concatenate(rows, axis=0)
        p_hi = prod.astype(BF16)
        p_lo = (prod - p_hi.astype(F32)).astype(BF16)
        sums = _dot(jnp.concatenate([p_hi, p_lo], axis=0), ones)
        sums = sums[:sub * sub] + sums[sub * sub:]
        attn = jnp.zeros((sub, LANES), F32)
        for s in range(sub):
            attn = attn + jnp.where(lane == s, sums[s * sub:(s + 1) * sub], 0.0)
        o_blk = _dot(attn[:, :sub].astype(BF16), vb[lo:lo + sub])
        if blk:
            qa = qi * jnp.exp(bi - bi[0:1])
            ka = k[:lo] * jnp.exp(bi[0:1] - b[:lo])
            o_blk = o_blk + _dot(_dot_t(qa.astype(BF16), ka.astype(BF16)).astype(BF16), vb[:lo])
        outs.append(o_blk)
    o = o_inter + jnp.concatenate(outs, axis=0)

    b_last = b[c_len - 1:c_len]
    k_dec = (k * jnp.exp(b_last - b)).astype(BF16)
    state_ref[...] = state_t * jnp.exp(b_last) + lax.dot_general(
        vb, k_dec, (((0,), (0,)), ((), ())), preferred_element_type=F32)
    return o


GLA_HEADS_PER_STEP = 4


def _gla(pc, w_gate2, b_gate, norm_g, batch, seq):
    t = pc.shape[0]
    heads = norm_g.shape[0] // GLA_DV
    hps = math.gcd(heads, GLA_HEADS_PER_STEP)
    groups = heads // hps
    wg = jnp.pad(w_gate2, ((0, LANES - GLA_GATE_RANK), (0, 0))).astype(BF16)
    nc = seq // GLA_CHUNK
    blk = lambda w, off: pl.BlockSpec((None, GLA_CHUNK, hps * w), lambda b, g, c: (b, c, off + g))
    par = lambda rows, w: pl.BlockSpec((rows, hps * w), lambda b, g, c: (0, g))
    pc3 = pc.reshape(batch, seq, pc.shape[1])
    out = pl.pallas_call(
        _gla_kernel,
        grid=(batch, groups, nc),
        in_specs=[blk(GLA_DK, 4 * groups), blk(GLA_DK, 5 * groups), blk(GLA_DV, 0),
                  pl.BlockSpec((None, GLA_CHUNK, LANES), lambda b, g, c: (b, c, 6 * heads)),
                  blk(GLA_DV, groups), par(LANES, GLA_DK), par(1, GLA_DK), par(1, GLA_DV)],
        out_specs=blk(GLA_DV, 0),
        out_shape=jax.ShapeDtypeStruct((batch, seq, heads * GLA_DV), BF16),
        scratch_shapes=[pltpu.VMEM((hps, GLA_DV, GLA_DK), F32)],
        compiler_params=_cparams("parallel", "parallel", "arbitrary"),
        name="gla",
    )(pc3, pc3, pc3, pc3, pc3, wg, b_gate.reshape(1, -1), norm_g.reshape(1, -1))
    return out.reshape(t, heads * GLA_DV)


_DIMS = {'nn': (((1,), (0,)), ((), ())), 'nt': (((1,), (1,)), ((), ())), 'tn': (((0,), (0,)), ((), ()))}


def _mm(a, b, form):
    return lax.dot_general(a.astype(BF16), b.astype(BF16), _DIMS[form], preferred_element_type=F32)


def _seg_sum(x, width):
    gi = lax.broadcasted_iota(jnp.int32, (LANES, LANES), 0) // width
    gj = lax.broadcasted_iota(jnp.int32, (LANES, LANES), 1) // width
    ones = (gi == gj).astype(F32)
    return jnp.concatenate([_dot_f32(x[:, c:c + LANES], ones) for c in range(0, x.shape[1], LANES)], axis=1)


def _rwkv_prep_kernel(*refs, width, seq, has_vres):
    if has_vres:
        (p_ref, pp_ref, prm_ref, ml_ref, w2_ref, a2_ref, g2_ref, vf_ref, v0_ref, v1_ref, v2_ref,
         r_ref, lw_ref, k_ref, v_ref, kk_ref, bb_ref, bonus_ref, g_ref) = refs
    else:
        (p_ref, pp_ref, prm_ref, ml_ref, w2_ref, a2_ref, g2_ref,
         r_ref, lw_ref, k_ref, v_ref, kk_ref, bb_ref, bonus_ref, g_ref) = refs
    w = width
    p, prm, ml = p_ref[...], prm_ref[...], ml_ref[...]
    tm = p.shape[0]
    above = jnp.where((pl.program_id(0) * tm) % seq == 0, 0.0, pp_ref[7:8, :])
    first = lax.broadcasted_iota(jnp.int32, p.shape, 0) == 0
    pp = jnp.where(first, jnp.broadcast_to(above, p.shape), pltpu.roll(p, 1, axis=0))
    lerp = lambda lo, hi, mu: p[:, lo:hi] + (pp[:, lo:hi] - p[:, lo:hi]) * mu
    r = lerp(0, w, prm[0:1])
    k = lerp(w, 2 * w, prm[1:2])
    v = lerp(2 * w, 3 * w, prm[2:3])
    x_wa = lerp(3 * w, 3 * w + LANES, ml[0:1])
    x_g = lerp(3 * w + LANES, 3 * w + 2 * LANES, ml[1:2])
    w_log = -jax.nn.softplus(-(prm[3:4] + _dot(jnp.tanh(x_wa).astype(BF16), w2_ref[...]))) - 0.5
    lw_ref[...] = -jnp.exp(w_log)
    a = jax.nn.sigmoid(prm[4:5] + _dot(x_wa.astype(BF16), a2_ref[...]))
    g_ref[...] = _dot(jax.nn.sigmoid(x_g).astype(BF16), g2_ref[...])
    if has_vres:
        mix = jax.nn.sigmoid(v0_ref[...] + _dot(_dot(v.astype(BF16), v1_ref[...]).astype(BF16), v2_ref[...]))
        v = v + (vf_ref[...] - v) * mix
    kk = k * prm[5:6]
    kk = kk * lax.rsqrt(jnp.maximum(_seg_sum(kk * kk, RWKV_HS), 1e-24))
    k2 = k * (1.0 + (a - 1.0) * prm[6:7])
    r_ref[...] = r
    k_ref[...] = k2
    v_ref[...] = v
    kk_ref[...] = kk
    bb_ref[...] = kk * a
    bonus_ref[...] = _seg_sum(r * k2 * prm[7:8], RWKV_HS) * v


RWKV_HEADS_PER_STEP = 8


def _rwkv_chunk_kernel(r_ref, lw_ref, k_ref, v_ref, kk_ref, bb_ref, y_ref, s_ref, *, n_chunk):
    c_len, hs = RWKV_CHUNK, RWKV_HS
    rows_n = n_chunk * c_len

    @pl.when(pl.program_id(2) == 0)
    def _():
        s_ref[...] = jnp.zeros_like(s_ref)

    ri = lax.broadcasted_iota(jnp.int32, (rows_n, rows_n), 0)
    ci = lax.broadcasted_iota(jnp.int32, (rows_n, rows_n), 1)
    same = (ri // c_len) == (ci // c_len)
    incl = jnp.logical_and(same, ci <= ri)
    strict = jnp.logical_and(same, ci < ri)
    eye = (ri == ci).astype(F32)
    e_r = lax.broadcasted_iota(jnp.int32, (hs, hs), 0)
    e_c = lax.broadcasted_iota(jnp.int32, (hs, hs), 1)
    sums = _masked_sums(jnp.concatenate([incl, same], axis=0).astype(BF16), lw_ref[...])
    cum_all, cum_c_all = sums[:rows_n], sums[rows_n:]
    heads = range(s_ref.shape[0])
    hd = []
    for hh in heads:
        sl = slice(hh * hs, (hh + 1) * hs)
        r, lw, k, v, kk, bb = (ref[...][:, sl] for ref in (r_ref, lw_ref, k_ref, v_ref, kk_ref, bb_ref))
        cum, cum_c = cum_all[:, sl], cum_c_all[:, sl]
        g_inv, g_end = jnp.exp(-cum), jnp.exp(cum_c - cum)
        kap, rt = kk * jnp.exp(cum - lw), r * jnp.exp(cum)
        bet, kt = bb * g_inv, k * g_inv
        gram = _mm(jnp.concatenate([kap, rt], axis=0), jnp.concatenate([bet, kt], axis=0), 'nt')
        hd.append(dict(v=v, kap=kap, rt=rt, bet_c=bb * g_end, kt_c=k * g_end, cum_c=cum_c,
                       n_m=jnp.where(strict, gram[:rows_n, :rows_n], 0.0),
                       a_kk=jnp.where(strict, gram[:rows_n, rows_n:], 0.0),
                       a_rb=jnp.where(incl, gram[rows_n:, :rows_n], 0.0),
                       a_rk=jnp.where(incl, gram[rows_n:, rows_n:], 0.0)))
    t_inv = [eye - h['n_m'] for h in hd]
    pw = [-h['n_m'] for h in hd]
    for _ in range(int(math.log2(c_len)) - 1):
        pw = [_mm(p, p, 'nn') for p in pw]
        t_inv = [t + _mm(t, p, 'nn') for t, p in zip(t_inv, pw)]
    av = [_mm(jnp.concatenate([h['a_kk'], h['a_rk']], axis=0), h['v'], 'nn') for h in hd]
    z = [_mm(t, jnp.concatenate([h['kap'], a[:rows_n]], axis=1), 'nn') for t, h, a in zip(t_inv, hd, av)]
    az = [_mm(h['a_rb'], zz, 'nn') for h, zz in zip(hd, z)]
    r_p = [h['rt'] - a[:, :hs] for h, a in zip(hd, az)]
    y0 = [a[rows_n:] - b[:, hs:] for a, b in zip(av, az)]
    s = [s_ref[hh] for hh in heads]
    outs = [[] for _ in heads]
    for c in range(n_chunk):
        rows = slice(c * c_len, (c + 1) * c_len)
        for hh in heads:
            h = hd[hh]
            outs[hh].append(_mm(r_p[hh][rows], s[hh], 'nt') + y0[hh][rows])
            zb = _mm(z[hh][rows], h['bet_c'][rows], 'tn')
            g_chunk = jnp.exp(h['cum_c'][c * c_len:c * c_len + 1])
            m = jnp.where(e_r == e_c, jnp.broadcast_to(g_chunk, (hs, hs)), 0.0) - zb[:hs]
            s[hh] = _mm(s[hh], m, 'nn') + _mm(h['v'][rows], h['kt_c'][rows], 'tn') - zb[hs:]
    for hh in heads:
        s_ref[hh] = s[hh]
    y_ref[...] = jnp.concatenate([jnp.concatenate(o, axis=0) for o in outs], axis=1)


def _rwkv_post_kernel(y_ref, bonus_ref, g_ref, lg_ref, lb_ref, o_ref):
    y = y_ref[...]
    mu = _seg_sum(y, RWKV_HS) * (1.0 / RWKV_HS)
    yc = y - mu
    var = _seg_sum(yc * yc, RWKV_HS) * (1.0 / RWKV_HS)
    yn = yc * lax.rsqrt(var + RWKV_GN_EPS) * lg_ref[...] + lb_ref[...]
    o_ref[...] = ((yn + bonus_ref[...]) * g_ref[...]).astype(o_ref.dtype)


def _rwkv(pd, mu_rkv, mu_lora, w0, w2, a0, a2, g2, k_k, k_a, r_k, lnx_g, lnx_b, v_first, vres, batch, seq):
    t, wd = pd.shape
    w = w0.shape[0]
    prm = jnp.stack([mu_rkv[0], mu_rkv[1], mu_rkv[2], w0, a0, k_k, k_a, r_k.reshape(-1)])
    zl = jnp.zeros((RWKV_LORA,), F32)
    ml = jnp.stack([jnp.concatenate([mu_lora[0], mu_lora[1]]), jnp.concatenate([mu_lora[2], zl])])
    zw = jnp.zeros((LANES - RWKV_LORA, w), F32)
    w2p = jnp.concatenate([w2, zw]).astype(BF16)
    a2p = jnp.concatenate([zw, a2]).astype(BF16)
    g2p = jnp.concatenate([g2, zw]).astype(BF16)
    has_vres = vres is not None
    tm = _tile(seq, 256, 8)
    row = lambda width: pl.BlockSpec((tm, width), lambda i: (i, 0))
    full = lambda a: pl.BlockSpec(a.shape, lambda i: (0, 0))
    above = pl.BlockSpec((8, wd), lambda i: (jnp.maximum(i * (tm // 8) - 1, 0), 0))
    args = [pd, pd, prm, ml, w2p, a2p, g2p]
    specs = [row(wd), above, full(prm), full(ml), full(w2p), full(a2p), full(g2p)]
    if has_vres:
        v0, v1, v2 = vres
        v1p = _pad_cols(v1, LANES).astype(BF16)
        v2p = jnp.pad(v2, ((0, LANES - v2.shape[0]), (0, 0))).astype(BF16)
        v0r = v0.reshape(1, w)
        args += [v_first, v0r, v1p, v2p]
        specs += [row(w), full(v0r), full(v1p), full(v2p)]
    r, lw, k2, v, kk, bb, bonus, g = pl.pallas_call(
        functools.partial(_rwkv_prep_kernel, width=w, seq=seq, has_vres=has_vres),
        grid=(t // tm,),
        in_specs=specs,
        out_specs=[row(w)] * 8,
        out_shape=[jax.ShapeDtypeStruct((t, w), F32)] * 8,
        compiler_params=_cparams("parallel"),
        name="rwkv_prep",
    )(*args)

    n_chunk = 4 if seq % (4 * RWKV_CHUNK) == 0 else 1
    rows_n = n_chunk * RWKV_CHUNK
    bw = math.gcd(w, RWKV_HEADS_PER_STEP * RWKV_HS)
    blk = pl.BlockSpec((None, rows_n, bw), lambda b, h, c: (b, c, h))
    r3 = lambda a: a.reshape(batch, seq, w)
    y = pl.pallas_call(
        functools.partial(_rwkv_chunk_kernel, n_chunk=n_chunk),
        grid=(batch, w // bw, seq // rows_n),
        in_specs=[blk] * 6,
        out_specs=blk,
        out_shape=jax.ShapeDtypeStruct((batch, seq, w), F32),
        scratch_shapes=[pltpu.VMEM((bw // RWKV_HS, RWKV_HS, RWKV_HS), F32)],
        compiler_params=_cparams("parallel", "parallel", "arbitrary"),
        name="rwkv_chunk",
    )(r3(r), r3(lw), r3(k2), r3(v), r3(kk), r3(bb))

    lg, lb = lnx_g.reshape(1, w), lnx_b.reshape(1, w)
    out = pl.pallas_call(
        _rwkv_post_kernel,
        grid=(t // tm,),
        in_specs=[row(w), row(w), row(w), full(lg), full(lb)],
        out_specs=row(w),
        out_shape=jax.ShapeDtypeStruct((t, w), BF16),
        compiler_params=_cparams("parallel"),
        name="rwkv_post",
    )(y.reshape(t, w), bonus, g, lg, lb)
    return out, (v_first if has_vres else v)


N_BRANCH = 4


def _merge_kernel(*refs):
    x_ref = refs[0]
    y_refs = refs[1:1 + N_BRANCH]
    wg_refs = refs[1 + N_BRANCH:1 + 2 * N_BRANCH]
    wb_refs = refs[1 + 2 * N_BRANCH:1 + 3 * N_BRANCH]
    o_ref = refs[-1]
    x = x_ref[...]
    acc = None
    for y_ref, wg_ref, wb_ref in zip(y_refs, wg_refs, wb_refs):
        term = jax.nn.sigmoid(_dot(x, wg_ref[...])) * _dot(y_ref[...], wb_ref[...])
        acc = term if acc is None else acc + term
    o_ref[...] = acc.astype(o_ref.dtype)


def _merge(xb, ys, w_gate, w_branch):
    t, d = xb.shape
    w = ys[0].shape[1]
    tm, tn = _tile(t, 512, 8), _tile(d, 256)
    nj = d // tn
    gate_spec = lambda i: pl.BlockSpec((d, tn), lambda r, j: (0, i * nj + j))
    br_spec = lambda i: pl.BlockSpec((None, w, tn), lambda r, j: (i, 0, j))
    return pl.pallas_call(
        _merge_kernel,
        grid=(t // tm, nj),
        in_specs=[pl.BlockSpec((tm, d), lambda r, j: (r, 0))] + [pl.BlockSpec((tm, w), lambda r, j: (r, 0))] * N_BRANCH
        + [gate_spec(i) for i in range(N_BRANCH)] + [br_spec(i) for i in range(N_BRANCH)],
        out_specs=pl.BlockSpec((tm, tn), lambda r, j: (r, j)),
        out_shape=jax.ShapeDtypeStruct((t, d), BF16),
        compiler_params=_cparams("parallel", "parallel"),
        name="merge",
    )(xb, *ys, *([w_gate] * N_BRANCH), *([w_branch] * N_BRANCH))


def _router_kernel(x_ref, w_ref, b_ref, o_ref, *, n_exp):
    tm = x_ref.shape[0]
    scores = jax.nn.sigmoid(_dot(x_ref[...], w_ref[...]))
    lane = lax.broadcasted_iota(jnp.int32, (tm, LANES), 1)
    work = jnp.where(lane < n_exp, scores + b_ref[...], -jnp.inf)
    chosen = jnp.zeros((tm, LANES), F32)
    for _ in range(TOP_K):
        best = jnp.max(work, axis=1, keepdims=True)
        first = jnp.min(jnp.where(work == best, lane, LANES), axis=1, keepdims=True)
        hit = lane == first
        chosen = jnp.where(hit, scores, chosen)
        work = jnp.where(hit, -jnp.inf, work)
    comb = chosen / jnp.sum(chosen, axis=1, keepdims=True) * ROUTED_SCALE
    o_ref[...] = jnp.where(lane == n_exp, 1.0, comb)


def _router(xb, router_w, router_bias):
    t, d = xb.shape
    n_exp = router_w.shape[1]
    assert n_exp < LANES
    tm = _tile(t, 512, 8)
    wr = _pad_cols(router_w, LANES).astype(BF16)
    br = _pad_cols(router_bias.reshape(1, n_exp), LANES)
    return pl.pallas_call(
        functools.partial(_router_kernel, n_exp=n_exp),
        grid=(t // tm,),
        in_specs=[pl.BlockSpec((tm, d), lambda i: (i, 0)), pl.BlockSpec((d, LANES), lambda i: (0, 0)),
                  pl.BlockSpec((1, LANES), lambda i: (0, 0))],
        out_specs=pl.BlockSpec((tm, LANES), lambda i: (i, 0)),
        out_shape=jax.ShapeDtypeStruct((t, LANES), F32),
        compiler_params=_cparams("parallel"),
        name="router",
    )(xb, wr, br)


MOE_COLS = 1024


MOE_GROUP = 2


def _moe_kernel(x_ref, c_ref, wgu_ref, wd_ref, o_ref, *, hid, group, lane0):
    g = pl.program_id(1)
    tm, d = o_ref.shape

    @pl.when(g == 0)
    def _():
        o_ref[...] = jnp.zeros_like(o_ref)

    x = x_ref[...]
    comb = c_ref[...]
    lane = lax.broadcasted_iota(jnp.int32, (tm, LANES), 1)
    hs = []
    for u in range(group):
        c = jnp.sum(jnp.where(lane == lane0 + g * group + u, comb, 0.0), axis=1, keepdims=True)
        gu = _dot(x, wgu_ref[u])
        gate = gu[:, :hid]
        hs.append((gate * jax.nn.sigmoid(gate) * gu[:, hid:] * c).astype(BF16))
    h = jnp.concatenate(hs, axis=1)
    cols = _tile(d, MOE_COLS)
    for lo in range(0, d, cols):
        o_ref[:, lo:lo + cols] += _dot(h, wd_ref[:, lo:lo + cols])


def _experts(xb, comb, w_gate, w_up, w_down, group, lane0):
    t, d = xb.shape
    n_e, _, hid = w_gate.shape
    assert n_e % group == 0
    wgu = jnp.concatenate([w_gate, w_up], axis=2).astype(BF16)
    wd = w_down.astype(BF16).reshape(n_e // group, group * hid, d)
    tm = _tile(t, 512, 8)
    once = pl.Buffered(1)
    tok = lambda width: pl.BlockSpec((tm, width), lambda i, e: (i, 0), pipeline_mode=once)
    return pl.pallas_call(
        functools.partial(_moe_kernel, hid=hid, group=group, lane0=lane0),
        grid=(t // tm, n_e // group),
        in_specs=[tok(d), tok(LANES), pl.BlockSpec((group, d, 2 * hid), lambda i, e: (e, 0, 0)),
                  pl.BlockSpec((None, group * hid, d), lambda i, e: (e, 0, 0))],
        out_specs=tok(d),
        out_shape=jax.ShapeDtypeStruct((t, d), F32),
        compiler_params=_cparams("parallel", "arbitrary"),
        name="moe",
    )(xb, comb, wgu, wd)


PROJ_ALIGN = 512


def _split_w_in(w, d):
    wdt = d // N_BRANCH
    gla_heads = wdt // GLA_DV
    names = (('gate', N_BRANCH * d), ('mla_cq', MLA_Q_RANK), ('mla_ckv', MLA_KV_RANK), ('mla_kr', MLA_ROPE),
             ('dsa_q', wdt), ('dsa_k', DSA_DH), ('dsa_v', DSA_DH),
             ('idx_q', IDX_HEADS * IDX_DIM), ('idx_k', IDX_DIM), ('idx_w', IDX_HEADS),
             ('gla_q', gla_heads * GLA_DK), ('gla_k', gla_heads * GLA_DK), ('gla_v', wdt),
             ('gla_a', GLA_GATE_RANK), ('gla_r', wdt),
             ('rwkv_r', wdt), ('rwkv_k', wdt), ('rwkv_v', wdt),
             ('rwkv_w', RWKV_LORA), ('rwkv_a', RWKV_LORA), ('rwkv_g', RWKV_LORA))
    parts, off = {}, 0
    for name, width in names:
        parts[name] = w[:, off:off + width]
        off += width
    assert off == w.shape[1], (off, w.shape)
    return parts


def _group(cols, dtype=BF16):
    wcat = jnp.concatenate(cols, axis=1)
    return _pad_cols(wcat, -(-wcat.shape[1] // PROJ_ALIGN) * PROJ_ALIGN).astype(dtype)


def kernel(x, positions, ln_in_g, ln_in_b, w_in, w_branch, w_out, mla_q_norm, mla_w_uq, mla_kv_norm, mla_w_ukv,
           gla_w_gate2, gla_b_gate, gla_norm_g, rwkv_mu_rkv, rwkv_mu_lora, rwkv_w0, rwkv_w2, rwkv_a0, rwkv_a2,
           rwkv_g2, rwkv_k_k, rwkv_k_a, rwkv_r_k, rwkv_lnx_g, rwkv_lnx_b, rwkv_v0, rwkv_v1, rwkv_v2,
           ln_mix_g, ln_mix_b, router_w, router_bias, exp_w_gate, exp_w_up, exp_w_down,
           sh_w_gate, sh_w_up, sh_w_down, ln_ffn_g, ln_ffn_b):
    batch, seq, d = x.shape
    depth = w_in.shape[0]
    t, wdt = batch * seq, d // N_BRANCH
    alpha = (2 * depth) ** 0.25
    ct, st = _rope_tables(positions)
    xf, xb = _layer_norm(x.reshape(t, d), (), ln_in_g, ln_in_b)
    v_first = None
    for l in range(depth):
        p = _split_w_in(w_in[l], d)
        pad128 = lambda w: _pad_cols(w, LANES)
        idx_q = jnp.pad(p['idx_q'].reshape(d, IDX_HEADS, IDX_DIM),
                        ((0, 0), (0, 0), (0, LANES - IDX_DIM))).reshape(d, IDX_HEADS * LANES)
        w_lat = _group([p['mla_cq'], p['mla_ckv'], pad128(p['mla_kr']), pad128(_swap_halves(p['mla_kr']))])
        dsa_q = p['dsa_q'] * (DSA_DH ** -0.5 * LOG2E)
        w_dsa = _group([idx_q, dsa_q, p['dsa_k'], p['dsa_v'], pad128(p['idx_k']), pad128(p['idx_w'])])
        w_gla = _group([p['gla_v'], p['gla_r'], p['gla_q'], p['gla_k'], pad128(p['gla_a'])])
        w_rwkv = _group([p['rwkv_r'], p['rwkv_k'], p['rwkv_v'], p['rwkv_w'], p['rwkv_a'], pad128(p['rwkv_g'])])
        lat = _matmul(xb, w_lat, F32)
        pb = _matmul(xb, w_dsa, BF16)
        pc = _matmul(xb, w_gla, F32)
        pd = _matmul(xb, w_rwkv, F32)

        y_mla = _mla(lat, ct, st, mla_q_norm[l], mla_w_uq[l], mla_kv_norm[l], mla_w_ukv[l], batch, seq)
        y_dsa = _dsa(pb, batch, seq, wdt)
        y_gla = _gla(pc, gla_w_gate2[l], gla_b_gate[l], gla_norm_g[l], batch, seq)
        vres = None if l == 0 else (rwkv_v0[l - 1], rwkv_v1[l - 1], rwkv_v2[l - 1])
        y_rwkv, v_first = _rwkv(pd, rwkv_mu_rkv[l], rwkv_mu_lora[l], rwkv_w0[l], rwkv_w2[l], rwkv_a0[l], rwkv_a2[l],
                                rwkv_g2[l], rwkv_k_k[l], rwkv_k_a[l], rwkv_r_k[l], rwkv_lnx_g[l], rwkv_lnx_b[l],
                                v_first, vres, batch, seq)

        merged = _merge(xb, (y_mla, y_dsa, y_gla, y_rwkv), p['gate'].astype(BF16), w_branch[l].astype(BF16))
        xf, xb = _layer_norm(xf, (_matmul(merged, w_out[l].astype(BF16), F32),), ln_mix_g[l], ln_mix_b[l], alpha)

        comb = _router(xb, router_w[l], router_bias[l])
        n_exp = exp_w_gate.shape[1]
        f_shared = _experts(xb, comb, sh_w_gate[l][None], sh_w_up[l][None], sh_w_down[l][None], 1, n_exp)
        f_routed = _experts(xb, comb, exp_w_gate[l], exp_w_up[l], exp_w_down[l], MOE_GROUP, 0)
        xf, xb = _layer_norm(xf, (f_routed, f_shared), ln_ffn_g[l], ln_ffn_b[l], alpha)
    return xf.reshape(batch, seq, d)
```

```python
import functools
import math

import jax
import jax.numpy as jnp
from jax import lax
from jax.experimental import pallas as pl
from jax.experimental.pallas import tpu as pltpu
from jax.experimental.pallas import tpu_sc as plsc

F32 = jnp.float32
BF16 = jnp.bfloat16

LANES = 128

MLA_NOPE, MLA_ROPE, MLA_V = 128, 64, 128
MLA_Q_RANK, MLA_KV_RANK = 768, 256
ROPE_THETA = 10000.0
DSA_DH = 128
IDX_HEADS, IDX_DIM, IDX_TOPK_MAX = 16, 64, 256
GLA_DV, GLA_DK, GLA_GATE_RANK, GLA_TAU, GLA_CHUNK = 256, 128, 16, 16.0, 64
GLA_SUB = 16
RWKV_HS, RWKV_LORA, RWKV_GN_EPS = 64, 64, 64e-5
RWKV_CHUNK = 64
TOP_K, ROUTED_SCALE = 8, 2.5
LN_EPS, RMS_EPS = 1e-5, 1e-6
NEG_BIG = -1e30
LOG2E = math.log2(math.e)

VMEM_LIMIT = 56 * 1024 * 1024


def _cparams(*sem):
    return pltpu.CompilerParams(dimension_semantics=sem, vmem_limit_bytes=VMEM_LIMIT)


def _tile(n, pref, unit=LANES):
    if n <= pref:
        return n
    t = (pref // unit) * unit
    while t > unit and n % t:
        t -= unit
    assert n % t == 0, (n, pref, unit)
    return t


def _dot(a, b):
    return jnp.dot(a, b, preferred_element_type=F32)


def _dot_t(a, b):
    return lax.dot_general(a, b, (((1,), (1,)), ((), ())), preferred_element_type=F32)


def _dot_f32(a, b):
    return jnp.dot(a, b, preferred_element_type=F32, precision=lax.Precision.HIGHEST)


def _masked_sums(mask, x):
    w = x.shape[1]
    hi = x.astype(BF16)
    rest = x - hi.astype(F32)
    mid = rest.astype(BF16)
    lo = (rest - mid.astype(F32)).astype(BF16)
    s = _dot(mask, jnp.concatenate([hi, mid, lo], axis=1))
    return s[:, :w] + s[:, w:2 * w] + s[:, 2 * w:]


def _mm_kernel(a_ref, b_ref, o_ref):
    o_ref[...] = _dot(a_ref[...], b_ref[...]).astype(o_ref.dtype)


def _matmul(a, b, out_dtype, tm=1024, tn=512):
    m, k = a.shape
    n = b.shape[1]
    tm, tn = _tile(m, tm, 8), _tile(n, tn)
    return pl.pallas_call(
        _mm_kernel,
        grid=(m // tm, n // tn),
        in_specs=[pl.BlockSpec((tm, k), lambda i, j: (i, 0)), pl.BlockSpec((k, tn), lambda i, j: (0, j))],
        out_specs=pl.BlockSpec((tm, tn), lambda i, j: (i, j)),
        out_shape=jax.ShapeDtypeStruct((m, n), out_dtype),
        compiler_params=_cparams("parallel", "parallel"),
        name="matmul",
    )(a, b)


def _pack_pairs(v):
    half = v.shape[1] // 2
    bits = lax.bitcast_convert_type(v.astype(BF16).astype(F32), jnp.int32)
    return (bits[:, half:] & jnp.int32(-65536)) | lax.shift_right_logical(bits[:, :half], 16)


def _unpack_pairs(w):
    return (lax.bitcast_convert_type(lax.shift_left(w, 16), F32),
            lax.bitcast_convert_type(w & jnp.int32(-65536), F32))


def _ln_kernel(*refs, alpha, n_res, packed):
    x_ref, res_refs = refs[0], refs[1:1 + n_res]
    g_ref, b_ref, o_ref, ob_ref = refs[1 + n_res:5 + n_res]
    x = x_ref[...]
    if n_res:
        x = alpha * x
        for f_ref in res_refs:
            x = x + f_ref[...]
    mu = jnp.mean(x, axis=-1, keepdims=True)
    xc = x - mu
    var = jnp.mean(xc * xc, axis=-1, keepdims=True)
    y = xc * lax.rsqrt(var + LN_EPS) * g_ref[...] + b_ref[...]
    o_ref[...] = y
    ob_ref[...] = y.astype(BF16)
    if packed:
        refs[-1][...] = _pack_pairs(y)


def _layer_norm(x, res, g, b, alpha=1.0, packed=False):
    t, d = x.shape
    tm = _tile(t, 128, 8)
    row = pl.BlockSpec((tm, d), lambda i: (i, 0))
    vec = pl.BlockSpec((1, d), lambda i: (0, 0))
    args = (x,) + tuple(res)
    extra_spec = [pl.BlockSpec((tm, d // 2), lambda i: (i, 0))] if packed else []
    extra_shape = [jax.ShapeDtypeStruct((t, d // 2), jnp.int32)] if packed else []
    return pl.pallas_call(
        functools.partial(_ln_kernel, alpha=alpha, n_res=len(res), packed=packed),
        grid=(t // tm,),
        in_specs=[row] * len(args) + [vec, vec],
        out_specs=[row, row] + extra_spec,
        out_shape=[jax.ShapeDtypeStruct((t, d), F32), jax.ShapeDtypeStruct((t, d), BF16)] + extra_shape,
        compiler_params=_cparams("parallel"),
        name="layer_norm",
    )(*args, g.reshape(1, d), b.reshape(1, d))


def _pad_cols(w, width):
    return jnp.pad(w, ((0, 0), (0, width - w.shape[1])))


def _rms(x, g):
    return x * lax.rsqrt(jnp.mean(x * x, axis=-1, keepdims=True) + RMS_EPS) * g


def _mla_prep_kernel(lat_ref, ct_ref, st_ref, qg_ref, kg_ref, wqn_ref, wqr_ref, wqs_ref, wkn_ref, wv_ref,
                     qn_ref, qr_ref, kn_ref, kr_ref, v_ref, *, heads, scale):
    lat = lat_ref[...]
    cq = lat[:, :MLA_Q_RANK]
    ckv = lat[:, MLA_Q_RANK:MLA_Q_RANK + MLA_KV_RANK]
    kr = lat[:, MLA_Q_RANK + MLA_KV_RANK:MLA_Q_RANK + MLA_KV_RANK + LANES]
    krs = lat[:, MLA_Q_RANK + MLA_KV_RANK + LANES:MLA_Q_RANK + MLA_KV_RANK + 2 * LANES]
    ct, st = ct_ref[...], st_ref[...]
    nq = _rms(cq, qg_ref[...]).astype(BF16)
    nkv = _rms(ckv, kg_ref[...]).astype(BF16)
    qn_ref[...] = (_dot(nq, wqn_ref[...]) * scale).astype(BF16)
    cth = jnp.concatenate([ct] * heads, axis=1)
    sth = jnp.concatenate([st] * heads, axis=1)
    qr_ref[...] = ((_dot(nq, wqr_ref[...]) * cth + _dot(nq, wqs_ref[...]) * sth) * scale).astype(BF16)
    kn_ref[...] = _dot(nkv, wkn_ref[...]).astype(BF16)
    v_ref[...] = _dot(nkv, wv_ref[...]).astype(BF16)
    kr_ref[...] = (kr * ct + krs * st).astype(BF16)


def _flash_kernel(qn_ref, qr_ref, kn_ref, kr_ref, v_ref, o_ref, m_ref, l_ref, acc_ref, *, tq, tk):
    i, j = pl.program_id(2), pl.program_id(3)

    @pl.when(j == 0)
    def _():
        m_ref[...] = jnp.full_like(m_ref, NEG_BIG)
        l_ref[...] = jnp.zeros_like(l_ref)
        acc_ref[...] = jnp.zeros_like(acc_ref)

    def step(masked):
        q = jnp.concatenate([qn_ref[...], qr_ref[...]], axis=1)
        k = jnp.concatenate([kn_ref[...], kr_ref[...]], axis=1)
        s = _dot_t(q, k)
        if masked:
            qpos = i * tq + lax.broadcasted_iota(jnp.int32, (tq, tk), 0)
            kpos = j * tk + lax.broadcasted_iota(jnp.int32, (tq, tk), 1)
            s = jnp.where(kpos <= qpos, s, -jnp.inf)
        m_prev = m_ref[...]
        m_new = jnp.maximum(m_prev, jnp.max(s, axis=1, keepdims=True))
        alpha = jnp.exp2(m_prev - m_new)
        p = jnp.exp2(s - jnp.concatenate([m_new] * (tk // LANES), axis=1))
        l_ref[...] = alpha * l_ref[...] + jnp.sum(p, axis=1, keepdims=True)
        acc_ref[...] = alpha * acc_ref[...] + _dot(p.astype(BF16), v_ref[...])
        m_ref[...] = m_new

    below = j * tk + tk - 1 <= i * tq
    touches = j * tk <= i * tq + tq - 1

    @pl.when(below)
    def _():
        step(False)

    @pl.when(jnp.logical_and(touches, jnp.logical_not(below)))
    def _():
        step(True)

    @pl.when(j == pl.num_programs(3) - 1)
    def _():
        o_ref[...] = (acc_ref[...] / l_ref[...]).astype(o_ref.dtype)


def _rope_tables(positions):
    half = MLA_ROPE // 2
    inv = ROPE_THETA ** (-jnp.arange(0, MLA_ROPE, 2, dtype=F32) / MLA_ROPE)
    ang = positions.reshape(-1).astype(F32)[:, None] * inv
    cos, sin = jnp.cos(ang), jnp.sin(ang)
    zero = jnp.zeros((ang.shape[0], LANES - 2 * half), F32)
    return jnp.concatenate([cos, cos, zero], 1), jnp.concatenate([-sin, sin, zero], 1)


def _swap_halves(w):
    half = w.shape[-1] // 2
    return jnp.concatenate([w[..., half:], w[..., :half]], -1)


def _mla(lat, ct, st, q_norm, w_uq, kv_norm, w_ukv, batch, seq):
    t = lat.shape[0]
    heads = w_uq.shape[1] // (MLA_NOPE + MLA_ROPE)
    hw = heads * LANES
    wq = w_uq.reshape(MLA_Q_RANK, heads, MLA_NOPE + MLA_ROPE)
    wqn = wq[:, :, :MLA_NOPE].reshape(MLA_Q_RANK, hw).astype(BF16)
    rope_pad = ((0, 0), (0, 0), (0, LANES - MLA_ROPE))
    wqr = jnp.pad(wq[:, :, MLA_NOPE:], rope_pad).reshape(MLA_Q_RANK, hw).astype(BF16)
    wqs = jnp.pad(_swap_halves(wq[:, :, MLA_NOPE:]), rope_pad).reshape(MLA_Q_RANK, hw).astype(BF16)
    wkv = w_ukv.reshape(MLA_KV_RANK, heads, MLA_NOPE + MLA_V)
    wkn = wkv[:, :, :MLA_NOPE].reshape(MLA_KV_RANK, hw).astype(BF16)
    wv = wkv[:, :, MLA_NOPE:].reshape(MLA_KV_RANK, hw).astype(BF16)

    tm = _tile(t, 512, 8)
    row = lambda w: pl.BlockSpec((tm, w), lambda i: (i, 0))
    full = lambda a: pl.BlockSpec(a.shape, lambda i: (0, 0))
    qg, kg = q_norm.reshape(1, -1), kv_norm.reshape(1, -1)
    qn, qr, kn, kr, v = pl.pallas_call(
        functools.partial(_mla_prep_kernel, heads=heads, scale=(MLA_NOPE + MLA_ROPE) ** -0.5 * LOG2E),
        grid=(t // tm,),
        in_specs=[row(lat.shape[1]), row(LANES), row(LANES), full(qg), full(kg),
                  full(wqn), full(wqr), full(wqs), full(wkn), full(wv)],
        out_specs=[row(hw), row(hw), row(hw), row(LANES), row(hw)],
        out_shape=[jax.ShapeDtypeStruct((t, hw), BF16)] * 3 + [jax.ShapeDtypeStruct((t, LANES), BF16),
                                                               jax.ShapeDtypeStruct((t, hw), BF16)],
        compiler_params=_cparams("parallel"),
        name="mla_prep",
    )(lat, ct, st, qg, kg, wqn, wqr, wqs, wkn, wv)

    tq = tk = _tile(seq, 1024, 8)
    nq, nk = seq // tq, seq // tk
    r3 = lambda a: a.reshape(batch, seq, a.shape[1])
    last = lambda i, j: jnp.minimum(j, ((i + 1) * tq - 1) // tk)
    q_spec = pl.BlockSpec((None, tq, LANES), lambda b, h, i, j: (b, i, h))
    k_spec = pl.BlockSpec((None, tk, LANES), lambda b, h, i, j: (b, last(i, j), h))
    kr_spec = pl.BlockSpec((None, tk, LANES), lambda b, h, i, j: (b, last(i, j), 0))
    out = pl.pallas_call(
        functools.partial(_flash_kernel, tq=tq, tk=tk),
        grid=(batch, heads, nq, nk),
        in_specs=[q_spec, q_spec, k_spec, kr_spec, k_spec],
        out_specs=q_spec,
        out_shape=jax.ShapeDtypeStruct((batch, seq, hw), BF16),
        scratch_shapes=[pltpu.VMEM((tq, LANES), F32), pltpu.VMEM((tq, LANES), F32), pltpu.VMEM((tq, LANES), F32)],
        compiler_params=_cparams("parallel", "parallel", "parallel", "arbitrary"),
        name="mla_flash",
    )(r3(qn), r3(qr), r3(kn), r3(kr), r3(v))
    return out.reshape(t, hw)


DSA_TQ = 128
DSA_TK = 512
INT_MIN = -2 ** 31


def _dsa_kernel(iq_ref, q_ref, k_ref, v_ref, ik_ref, iw_ref, o_ref, key_ref, m_ref, l_ref, acc_ref,
                *, tq, tk, n_sel, heads, pos_bits):
    i = pl.program_id(1)
    n_kt = (i * tq + tq - 1) // tk + 1
    reps = tk // LANES
    wide = lambda a: jnp.concatenate([a] * reps, axis=1)
    qpos = i * tq + lax.broadcasted_iota(jnp.int32, (tq, tk), 0)
    lane_pos = lax.broadcasted_iota(jnp.int32, (tq, tk), 1)

    iq = iq_ref[...]
    iw = iw_ref[...].astype(F32) * (IDX_HEADS ** -0.5 * IDX_DIM ** -0.5)
    iw_b = [jnp.broadcast_to(iw[:, h:h + 1], (tq, LANES)) for h in range(IDX_HEADS)]

    def score_tile(j, carry):
        ikj = ik_ref[pl.ds(pl.multiple_of(j * tk, tk), tk), :]
        sc = jnp.zeros((tq, tk), F32)
        for h in range(IDX_HEADS):
            logit = _dot_t(iq[:, h * LANES:(h + 1) * LANES], ikj)
            sc = sc + wide(iw_b[h]) * jnp.maximum(logit, 0.0)
        sc = jnp.where(j * tk + lane_pos <= qpos, sc + 0.0, -jnp.inf)
        bits = lax.bitcast_convert_type(sc, jnp.int32)
        key_ref[j] = jnp.where(bits >= 0, bits, bits ^ 0x7FFFFFFF)
        return carry

    lax.fori_loop(0, n_kt, score_tile, 0)

    def count(pred):
        def body(j, acc):
            c = pred(key_ref[j], j * tk + lane_pos).astype(jnp.int32)
            for rep in range(reps):
                acc = acc + c[:, rep * LANES:(rep + 1) * LANES]
            return acc
        acc = lax.fori_loop(0, n_kt, body, jnp.zeros((tq, LANES), jnp.int32))
        return jnp.broadcast_to(jnp.sum(acc, axis=1, keepdims=True), (tq, LANES))

    def thr_bit(bit_i, thr):
        cand = thr ^ jnp.left_shift(jnp.int32(1), 31 - bit_i)
        cnt = count(lambda key, pos: key >= wide(cand))
        return jnp.where(cnt >= n_sel, cand, thr)

    thr = lax.fori_loop(0, 32, thr_bit, jnp.full((tq, LANES), INT_MIN, jnp.int32))
    cnt_gt = count(lambda key, pos: key > wide(thr))
    cnt_ge = count(lambda key, pos: key >= wide(thr))
    need = n_sel - cnt_gt

    def tie_bit(bit_i, cut):
        cand = cut | jnp.left_shift(jnp.int32(1), pos_bits - 1 - bit_i)
        cnt = count(lambda key, pos: jnp.logical_and(key == wide(thr), pos < wide(cand)))
        return jnp.where(cnt < need, cand, cut)

    surplus = jnp.max(cnt_ge - cnt_gt - need) > 0
    cut = lax.cond(surplus,
                   lambda: lax.fori_loop(0, pos_bits, tie_bit, jnp.zeros((tq, LANES), jnp.int32)),
                   lambda: jnp.full((tq, LANES), 2 ** 31 - 1, jnp.int32))

    q = q_ref[...]
    q_all = jnp.concatenate([q[:, h * LANES:(h + 1) * LANES] for h in range(heads)], axis=0)
    m_ref[...] = jnp.full_like(m_ref, NEG_BIG)
    l_ref[...] = jnp.zeros_like(l_ref)
    acc_ref[...] = jnp.zeros_like(acc_ref)

    def attend(j, carry):
        rows = pl.ds(pl.multiple_of(j * tk, tk), tk)
        s = _dot_t(q_all, k_ref[rows, :])
        key, pos = key_ref[j], j * tk + lane_pos
        keep = jnp.where(key > wide(thr), 0.0,
                         jnp.where(jnp.logical_and(key == wide(thr), pos <= wide(cut)), 0.0, -jnp.inf))
        keep = jnp.where(pos <= qpos, keep, -jnp.inf)
        s = s + jnp.concatenate([keep] * heads, axis=0)
        m_prev = m_ref[...]
        m_new = jnp.maximum(m_prev, jnp.max(s, axis=1, keepdims=True))
        alpha = jnp.exp2(m_prev - m_new)
        p = jnp.exp2(s - jnp.concatenate([m_new] * reps, axis=1))
        l_ref[...] = alpha * l_ref[...] + jnp.sum(p, axis=1, keepdims=True)
        acc_ref[...] = alpha * acc_ref[...] + _dot(p.astype(BF16), v_ref[rows, :])
        m_ref[...] = m_new
        return carry

    lax.fori_loop(0, n_kt, attend, 0)
    out = acc_ref[...] / l_ref[...]
    o_ref[...] = jnp.concatenate([out[h * tq:(h + 1) * tq] for h in range(heads)], axis=1).astype(o_ref.dtype)


def _dsa(pb, batch, seq, width):
    t, wb = pb.shape
    heads = width // DSA_DH
    iqw = IDX_HEADS * LANES
    tq, tk = _tile(seq, DSA_TQ, 8), _tile(seq, DSA_TK)
    n_sel = min(IDX_TOPK_MAX, seq // 4)
    c0 = (iqw + width) // LANES
    pb3 = pb.reshape(batch, seq, wb)
    qblk = lambda w, idx: pl.BlockSpec((None, tq, w), lambda b, i: (b, i, idx))
    seqblk = lambda idx: pl.BlockSpec((None, seq, LANES), lambda b, i: (b, 0, idx))
    out = pl.pallas_call(
        functools.partial(_dsa_kernel, tq=tq, tk=tk, n_sel=n_sel, heads=heads,
                          pos_bits=max(1, (seq - 1).bit_length())),
        grid=(batch, seq // tq),
        in_specs=[qblk(iqw, 0), qblk(width, iqw // width), seqblk(c0), seqblk(c0 + 1), seqblk(c0 + 2),
                  qblk(LANES, c0 + 3)],
        out_specs=qblk(width, 0),
        out_shape=jax.ShapeDtypeStruct((batch, seq, width), BF16),
        scratch_shapes=[pltpu.VMEM((seq // tk, tq, tk), jnp.int32), pltpu.VMEM((heads * tq, LANES), F32),
                        pltpu.VMEM((heads * tq, LANES), F32), pltpu.VMEM((heads * tq, LANES), F32)],
        compiler_params=_cparams("parallel", "arbitrary"),
        name="dsa",
    )(pb3, pb3, pb3, pb3, pb3, pb3)
    return out.reshape(t, width)


def _gla_kernel(q_ref, k_ref, v_ref, a_ref, r_ref, wg_ref, bg_ref, ng_ref, o_ref, state_ref):
    c_len, sub = GLA_CHUNK, GLA_SUB

    @pl.when(pl.program_id(2) == 0)
    def _():
        state_ref[...] = jnp.zeros_like(state_ref)

    z_all = _dot(a_ref[...].astype(BF16), wg_ref[...]) + bg_ref[...]
    g_all = jax.nn.log_sigmoid(z_all) / GLA_TAU
    ri = lax.broadcasted_iota(jnp.int32, (c_len, c_len), 0)
    ci = lax.broadcasted_iota(jnp.int32, (c_len, c_len), 1)
    b_all = _masked_sums((ri >= ci).astype(BF16), g_all)
    ones = jnp.ones((GLA_DK, LANES), BF16)
    trow = lax.broadcasted_iota(jnp.int32, (sub, GLA_DK), 0)
    lane = lax.broadcasted_iota(jnp.int32, (sub, LANES), 1)
    for hh in range(state_ref.shape[0]):
        ksl = slice(hh * GLA_DK, (hh + 1) * GLA_DK)
        vsl = slice(hh * GLA_DV, (hh + 1) * GLA_DV)
        o = _gla_head(q_ref[:, ksl] * GLA_DK ** -0.5, k_ref[:, ksl], v_ref[:, vsl].astype(BF16), b_all[:, ksl],
                      state_ref.at[hh], ones, trow, lane)
        r = r_ref[:, vsl]
        o_ref[:, vsl] = (_rms(o, ng_ref[:, vsl]) * (r * jax.nn.sigmoid(r))).astype(o_ref.dtype)


def _gla_head(q, k, vb, b, state_ref, ones, trow, lane):
    c_len, sub = GLA_CHUNK, GLA_SUB
    state_t = state_ref[...]
    o_inter = _dot_t((q * jnp.exp(b)).astype(BF16), state_t.astype(BF16))
    outs = []
    for blk in range(c_len // sub):
        lo = blk * sub
        qi, bi, ki = q[lo:lo + sub], b[lo:lo + sub], k[lo:lo + sub]
        rows = [qi * ki[s:s + 1] * jnp.exp(jnp.where(trow >= s, bi - bi[s:s + 1], -jnp.inf)) for s in range(sub)]
        prod = jnp.concatenate(rows, axis=0)
        p_hi = prod.astype(BF16)
        p_lo = (prod - p_hi.astype(F32)).astype(BF16)
        sums = _dot(jnp.concatenate([p_hi, p_lo], axis=0), ones)
        sums = sums[:sub * sub] + sums[sub * sub:]
        attn = jnp.zeros((sub, LANES), F32)
        for s in range(sub):
            attn = attn + jnp.where(lane == s, sums[s * sub:(s + 1) * sub], 0.0)
        o_blk = _dot(attn[:, :sub].astype(BF16), vb[lo:lo + sub])
        if blk:
            qa = qi * jnp.exp(bi - bi[0:1])
            ka = k[:lo] * jnp.exp(bi[0:1] - b[:lo])
            o_blk = o_blk + _dot(_dot_t(qa.astype(BF16), ka.astype(BF16)).astype(BF16), vb[:lo])
        outs.append(o_blk)
    o = o_inter + jnp.concatenate(outs, axis=0)

    b_last = b[c_len - 1:c_len]
    k_dec = (k * jnp.exp(b_last - b)).astype(BF16)
    state_ref[...] = state_t * jnp.exp(b_last) + lax.dot_general(
        vb, k_dec, (((0,), (0,)), ((), ())), preferred_element_type=F32)
    return o


GLA_HEADS_PER_STEP = 4


def _gla(pc, w_gate2, b_gate, norm_g, batch, seq):
    t = pc.shape[0]
    heads = norm_g.shape[0] // GLA_DV
    hps = math.gcd(heads, GLA_HEADS_PER_STEP)
    groups = heads // hps
    wg = jnp.pad(w_gate2, ((0, LANES - GLA_GATE_RANK), (0, 0))).astype(BF16)
    nc = seq // GLA_CHUNK
    blk = lambda w, off: pl.BlockSpec((None, GLA_CHUNK, hps * w), lambda b, g, c: (b, c, off + g))
    par = lambda rows, w: pl.BlockSpec((rows, hps * w), lambda b, g, c: (0, g))
    pc3 = pc.reshape(batch, seq, pc.shape[1])
    out = pl.pallas_call(
        _gla_kernel,
        grid=(batch, groups, nc),
        in_specs=[blk(GLA_DK, 4 * groups), blk(GLA_DK, 5 * groups), blk(GLA_DV, 0),
                  pl.BlockSpec((None, GLA_CHUNK, LANES), lambda b, g, c: (b, c, 6 * heads)),
                  blk(GLA_DV, groups), par(LANES, GLA_DK), par(1, GLA_DK), par(1, GLA_DV)],
        out_specs=blk(GLA_DV, 0),
        out_shape=jax.ShapeDtypeStruct((batch, seq, heads * GLA_DV), BF16),
        scratch_shapes=[pltpu.VMEM((hps, GLA_DV, GLA_DK), F32)],
        compiler_params=_cparams("parallel", "parallel", "arbitrary"),
        name="gla",
    )(pc3, pc3, pc3, pc3, pc3, wg, b_gate.reshape(1, -1), norm_g.reshape(1, -1))
    return out.reshape(t, heads * GLA_DV)


_DIMS = {'nn': (((1,), (0,)), ((), ())), 'nt': (((1,), (1,)), ((), ())), 'tn': (((0,), (0,)), ((), ()))}


def _mm(a, b, form):
    return lax.dot_general(a.astype(BF16), b.astype(BF16), _DIMS[form], preferred_element_type=F32)


def _seg_sum(x, width):
    gi = lax.broadcasted_iota(jnp.int32, (LANES, LANES), 0) // width
    gj = lax.broadcasted_iota(jnp.int32, (LANES, LANES), 1) // width
    ones = (gi == gj).astype(F32)
    return jnp.concatenate([_dot_f32(x[:, c:c + LANES], ones) for c in range(0, x.shape[1], LANES)], axis=1)


def _rwkv_prep_kernel(*refs, width, seq, has_vres):
    if has_vres:
        (p_ref, pp_ref, prm_ref, ml_ref, w2_ref, a2_ref, g2_ref, vf_ref, v0_ref, v1_ref, v2_ref,
         r_ref, lw_ref, k_ref, v_ref, kk_ref, bb_ref, bonus_ref, g_ref) = refs
    else:
        (p_ref, pp_ref, prm_ref, ml_ref, w2_ref, a2_ref, g2_ref,
         r_ref, lw_ref, k_ref, v_ref, kk_ref, bb_ref, bonus_ref, g_ref) = refs
    w = width
    p, prm, ml = p_ref[...], prm_ref[...], ml_ref[...]
    tm = p.shape[0]
    above = jnp.where((pl.program_id(0) * tm) % seq == 0, 0.0, pp_ref[7:8, :])
    first = lax.broadcasted_iota(jnp.int32, p.shape, 0) == 0
    pp = jnp.where(first, jnp.broadcast_to(above, p.shape), pltpu.roll(p, 1, axis=0))
    lerp = lambda lo, hi, mu: p[:, lo:hi] + (pp[:, lo:hi] - p[:, lo:hi]) * mu
    r = lerp(0, w, prm[0:1])
    k = lerp(w, 2 * w, prm[1:2])
    v = lerp(2 * w, 3 * w, prm[2:3])
    x_wa = lerp(3 * w, 3 * w + LANES, ml[0:1])
    x_g = lerp(3 * w + LANES, 3 * w + 2 * LANES, ml[1:2])
    w_log = -jax.nn.softplus(-(prm[3:4] + _dot(jnp.tanh(x_wa).astype(BF16), w2_ref[...]))) - 0.5
    lw_ref[...] = -jnp.exp(w_log)
    a = jax.nn.sigmoid(prm[4:5] + _dot(x_wa.astype(BF16), a2_ref[...]))
    g_ref[...] = _dot(jax.nn.sigmoid(x_g).astype(BF16), g2_ref[...])
    if has_vres:
        mix = jax.nn.sigmoid(v0_ref[...] + _dot(_dot(v.astype(BF16), v1_ref[...]).astype(BF16), v2_ref[...]))
        v = v + (vf_ref[...] - v) * mix
    kk = k * prm[5:6]
    kk = kk * lax.rsqrt(jnp.maximum(_seg_sum(kk * kk, RWKV_HS), 1e-24))
    k2 = k * (1.0 + (a - 1.0) * prm[6:7])
    r_ref[...] = r
    k_ref[...] = k2
    v_ref[...] = v
    kk_ref[...] = kk
    bb_ref[...] = kk * a
    bonus_ref[...] = _seg_sum(r * k2 * prm[7:8], RWKV_HS) * v


RWKV_HEADS_PER_STEP = 8


def _rwkv_chunk_kernel(r_ref, lw_ref, k_ref, v_ref, kk_ref, bb_ref, y_ref, s_ref, *, n_chunk):
    c_len, hs = RWKV_CHUNK, RWKV_HS
    rows_n = n_chunk * c_len

    @pl.when(pl.program_id(2) == 0)
    def _():
        s_ref[...] = jnp.zeros_like(s_ref)

    ri = lax.broadcasted_iota(jnp.int32, (rows_n, rows_n), 0)
    ci = lax.broadcasted_iota(jnp.int32, (rows_n, rows_n), 1)
    same = (ri // c_len) == (ci // c_len)
    incl = jnp.logical_and(same, ci <= ri)
    strict = jnp.logical_and(same, ci < ri)
    eye = (ri == ci).astype(F32)
    e_r = lax.broadcasted_iota(jnp.int32, (hs, hs), 0)
    e_c = lax.broadcasted_iota(jnp.int32, (hs, hs), 1)
    sums = _masked_sums(jnp.concatenate([incl, same], axis=0).astype(BF16), lw_ref[...])
    cum_all, cum_c_all = sums[:rows_n], sums[rows_n:]
    heads = range(s_ref.shape[0])
    hd = []
    for hh in heads:
        sl = slice(hh * hs, (hh + 1) * hs)
        r, lw, k, v, kk, bb = (ref[...][:, sl] for ref in (r_ref, lw_ref, k_ref, v_ref, kk_ref, bb_ref))
        cum, cum_c = cum_all[:, sl], cum_c_all[:, sl]
        g_inv, g_end = jnp.exp(-cum), jnp.exp(cum_c - cum)
        kap, rt = kk * jnp.exp(cum - lw), r * jnp.exp(cum)
        bet, kt = bb * g_inv, k * g_inv
        gram = _mm(jnp.concatenate([kap, rt], axis=0), jnp.concatenate([bet, kt], axis=0), 'nt')
        hd.append(dict(v=v, kap=kap, rt=rt, bet_c=bb * g_end, kt_c=k * g_end, cum_c=cum_c,
                       n_m=jnp.where(strict, gram[:rows_n, :rows_n], 0.0),
                       a_kk=jnp.where(strict, gram[:rows_n, rows_n:], 0.0),
                       a_rb=jnp.where(incl, gram[rows_n:, :rows_n], 0.0),
                       a_rk=jnp.where(incl, gram[rows_n:, rows_n:], 0.0)))
    t_inv = [eye - h['n_m'] for h in hd]
    pw = [-h['n_m'] for h in hd]
    for _ in range(int(math.log2(c_len)) - 1):
        pw = [_mm(p, p, 'nn') for p in pw]
        t_inv = [t + _mm(t, p, 'nn') for t, p in zip(t_inv, pw)]
    av = [_mm(jnp.concatenate([h['a_kk'], h['a_rk']], axis=0), h['v'], 'nn') for h in hd]
    z = [_mm(t, jnp.concatenate([h['kap'], a[:rows_n]], axis=1), 'nn') for t, h, a in zip(t_inv, hd, av)]
    az = [_mm(h['a_rb'], zz, 'nn') for h, zz in zip(hd, z)]
    r_p = [h['rt'] - a[:, :hs] for h, a in zip(hd, az)]
    y0 = [a[rows_n:] - b[:, hs:] for a, b in zip(av, az)]
    s = [s_ref[hh] for hh in heads]
    outs = [[] for _ in heads]
    for c in range(n_chunk):
        rows = slice(c * c_len, (c + 1) * c_len)
        for hh in heads:
            h = hd[hh]
            outs[hh].append(_mm(r_p[hh][rows], s[hh], 'nt') + y0[hh][rows])
            zb = _mm(z[hh][rows], h['bet_c'][rows], 'tn')
            g_chunk = jnp.exp(h['cum_c'][c * c_len:c * c_len + 1])
            m = jnp.where(e_r == e_c, jnp.broadcast_to(g_chunk, (hs, hs)), 0.0) - zb[:hs]
            s[hh] = _mm(s[hh], m, 'nn') + _mm(h['v'][rows], h['kt_c'][rows], 'tn') - zb[hs:]
    for hh in heads:
        s_ref[hh] = s[hh]
    y_ref[...] = jnp.concatenate([jnp.concatenate(o, axis=0) for o in outs], axis=1)


def _rwkv_post_kernel(y_ref, bonus_ref, g_ref, lg_ref, lb_ref, o_ref):
    y = y_ref[...]
    mu = _seg_sum(y, RWKV_HS) * (1.0 / RWKV_HS)
    yc = y - mu
    var = _seg_sum(yc * yc, RWKV_HS) * (1.0 / RWKV_HS)
    yn = yc * lax.rsqrt(var + RWKV_GN_EPS) * lg_ref[...] + lb_ref[...]
    o_ref[...] = ((yn + bonus_ref[...]) * g_ref[...]).astype(o_ref.dtype)


def _rwkv(pd, mu_rkv, mu_lora, w0, w2, a0, a2, g2, k_k, k_a, r_k, lnx_g, lnx_b, v_first, vres, batch, seq):
    t, wd = pd.shape
    w = w0.shape[0]
    prm = jnp.stack([mu_rkv[0], mu_rkv[1], mu_rkv[2], w0, a0, k_k, k_a, r_k.reshape(-1)])
    zl = jnp.zeros((RWKV_LORA,), F32)
    ml = jnp.stack([jnp.concatenate([mu_lora[0], mu_lora[1]]), jnp.concatenate([mu_lora[2], zl])])
    zw = jnp.zeros((LANES - RWKV_LORA, w), F32)
    w2p = jnp.concatenate([w2, zw]).astype(BF16)
    a2p = jnp.concatenate([zw, a2]).astype(BF16)
    g2p = jnp.concatenate([g2, zw]).astype(BF16)
    has_vres = vres is not None
    tm = _tile(seq, 256, 8)
    row = lambda width: pl.BlockSpec((tm, width), lambda i: (i, 0))
    full = lambda a: pl.BlockSpec(a.shape, lambda i: (0, 0))
    above = pl.BlockSpec((8, wd), lambda i: (jnp.maximum(i * (tm // 8) - 1, 0), 0))
    args = [pd, pd, prm, ml, w2p, a2p, g2p]
    specs = [row(wd), above, full(prm), full(ml), full(w2p), full(a2p), full(g2p)]
    if has_vres:
        v0, v1, v2 = vres
        v1p = _pad_cols(v1, LANES).astype(BF16)
        v2p = jnp.pad(v2, ((0, LANES - v2.shape[0]), (0, 0))).astype(BF16)
        v0r = v0.reshape(1, w)
        args += [v_first, v0r, v1p, v2p]
        specs += [row(w), full(v0r), full(v1p), full(v2p)]
    r, lw, k2, v, kk, bb, bonus, g = pl.pallas_call(
        functools.partial(_rwkv_prep_kernel, width=w, seq=seq, has_vres=has_vres),
        grid=(t // tm,),
        in_specs=specs,
        out_specs=[row(w)] * 8,
        out_shape=[jax.ShapeDtypeStruct((t, w), F32)] * 8,
        compiler_params=_cparams("parallel"),
        name="rwkv_prep",
    )(*args)

    n_chunk = 4 if seq % (4 * RWKV_CHUNK) == 0 else 1
    rows_n = n_chunk * RWKV_CHUNK
    bw = math.gcd(w, RWKV_HEADS_PER_STEP * RWKV_HS)
    blk = pl.BlockSpec((None, rows_n, bw), lambda b, h, c: (b, c, h))
    r3 = lambda a: a.reshape(batch, seq, w)
    y = pl.pallas_call(
        functools.partial(_rwkv_chunk_kernel, n_chunk=n_chunk),
        grid=(batch, w // bw, seq // rows_n),
        in_specs=[blk] * 6,
        out_specs=blk,
        out_shape=jax.ShapeDtypeStruct((batch, seq, w), F32),
        scratch_shapes=[pltpu.VMEM((bw // RWKV_HS, RWKV_HS, RWKV_HS), F32)],
        compiler_params=_cparams("parallel", "parallel", "arbitrary"),
        name="rwkv_chunk",
    )(r3(r), r3(lw), r3(k2), r3(v), r3(kk), r3(bb))

    lg, lb = lnx_g.reshape(1, w), lnx_b.reshape(1, w)
    out = pl.pallas_call(
        _rwkv_post_kernel,
        grid=(t // tm,),
        in_specs=[row(w), row(w), row(w), full(lg), full(lb)],
        out_specs=row(w),
        out_shape=jax.ShapeDtypeStruct((t, w), BF16),
        compiler_params=_cparams("parallel"),
        name="rwkv_post",
    )(y.reshape(t, w), bonus, g, lg, lb)
    return out, (v_first if has_vres else v)


N_BRANCH = 4


def _merge_kernel(*refs):
    x_ref = refs[0]
    y_refs = refs[1:1 + N_BRANCH]
    wg_refs = refs[1 + N_BRANCH:1 + 2 * N_BRANCH]
    wb_refs = refs[1 + 2 * N_BRANCH:1 + 3 * N_BRANCH]
    o_ref = refs[-1]
    x = x_ref[...]
    acc = None
    for y_ref, wg_ref, wb_ref in zip(y_refs, wg_refs, wb_refs):
        term = jax.nn.sigmoid(_dot(x, wg_ref[...])) * _dot(y_ref[...], wb_ref[...])
        acc = term if acc is None else acc + term
    o_ref[...] = acc.astype(o_ref.dtype)


def _merge(xb, ys, w_gate, w_branch):
    t, d = xb.shape
    w = ys[0].shape[1]
    tm, tn = _tile(t, 512, 8), _tile(d, 256)
    nj = d // tn
    gate_spec = lambda i: pl.BlockSpec((d, tn), lambda r, j: (0, i * nj + j))
    br_spec = lambda i: pl.BlockSpec((None, w, tn), lambda r, j: (i, 0, j))
    return pl.pallas_call(
        _merge_kernel,
        grid=(t // tm, nj),
        in_specs=[pl.BlockSpec((tm, d), lambda r, j: (r, 0))] + [pl.BlockSpec((tm, w), lambda r, j: (r, 0))] * N_BRANCH
        + [gate_spec(i) for i in range(N_BRANCH)] + [br_spec(i) for i in range(N_BRANCH)],
        out_specs=pl.BlockSpec((tm, tn), lambda r, j: (r, j)),
        out_shape=jax.ShapeDtypeStruct((t, d), BF16),
        compiler_params=_cparams("parallel", "parallel"),
        name="merge",
    )(xb, *ys, *([w_gate] * N_BRANCH), *([w_branch] * N_BRANCH))


def _router_kernel(x_ref, w_ref, b_ref, comb_ref, ids_ref, wts_ref, rank_ref, cnt_ref, run_ref, *, n_exp):
    tm = x_ref.shape[0]

    @pl.when(pl.program_id(0) == 0)
    def _():
        run_ref[...] = jnp.zeros_like(run_ref)

    scores = jax.nn.sigmoid(_dot(x_ref[...], w_ref[...]))
    lane = lax.broadcasted_iota(jnp.int32, (tm, LANES), 1)
    work = jnp.where(lane < n_exp, scores + b_ref[...], -jnp.inf)
    chosen = jnp.zeros((tm, LANES), F32)
    sel = jnp.zeros((tm, LANES), F32)
    ids = jnp.zeros((tm, LANES), jnp.int32)
    wts = jnp.zeros((tm, LANES), F32)
    for it in range(TOP_K):
        best = jnp.max(work, axis=1, keepdims=True)
        first = jnp.min(jnp.where(work == best, lane, LANES), axis=1, keepdims=True)
        hit = lane == first
        chosen = jnp.where(hit, scores, chosen)
        sel = jnp.where(hit, 1.0, sel)
        ids = jnp.where(lane == it, first, ids)
        wts = jnp.where(lane == it, jnp.sum(jnp.where(hit, scores, 0.0), axis=1, keepdims=True), wts)
        work = jnp.where(hit, -jnp.inf, work)
    norm = ROUTED_SCALE / jnp.sum(chosen, axis=1, keepdims=True)
    comb_ref[...] = jnp.where(lane == n_exp, 1.0, chosen * norm)
    ids_ref[...] = ids
    wts_ref[...] = wts * norm
    ri = lax.broadcasted_iota(jnp.int32, (tm, tm), 0)
    ci = lax.broadcasted_iota(jnp.int32, (tm, tm), 1)
    before = _dot((ci < ri).astype(BF16), sel.astype(BF16))
    run = run_ref[0:1, :]
    rank_ref[...] = (before + run).astype(jnp.int32)
    run = run + jnp.sum(sel, axis=0, keepdims=True)
    run_ref[...] = jnp.broadcast_to(run, run_ref.shape)
    cnt_ref[...] = jnp.broadcast_to(run, cnt_ref.shape).astype(jnp.int32)


def _router(xb, router_w, router_bias):
    t, d = xb.shape
    n_exp = router_w.shape[1]
    assert n_exp < LANES
    tm = _tile(t, 512, 8)
    wr = _pad_cols(router_w, LANES).astype(BF16)
    br = _pad_cols(router_bias.reshape(1, n_exp), LANES)
    tok = pl.BlockSpec((tm, LANES), lambda i: (i, 0))
    tok_shape = lambda dt: jax.ShapeDtypeStruct((t, LANES), dt)
    return pl.pallas_call(
        functools.partial(_router_kernel, n_exp=n_exp),
        grid=(t // tm,),
        in_specs=[pl.BlockSpec((tm, d), lambda i: (i, 0)), pl.BlockSpec((d, LANES), lambda i: (0, 0)),
                  pl.BlockSpec((1, LANES), lambda i: (0, 0))],
        out_specs=[tok, tok, tok, tok, pl.BlockSpec((8, LANES), lambda i: (0, 0))],
        out_shape=[tok_shape(F32), tok_shape(jnp.int32), tok_shape(F32), tok_shape(jnp.int32),
                   jax.ShapeDtypeStruct((8, LANES), jnp.int32)],
        scratch_shapes=[pltpu.VMEM((8, LANES), F32)],
        compiler_params=_cparams("arbitrary"),
        name="router",
    )(xb, wr, br)


MOE_COLS = 1024


MOE_GROUP = 2


def _moe_kernel(x_ref, c_ref, wgu_ref, wd_ref, o_ref, *, hid, group, lane0):
    g = pl.program_id(1)
    tm, d = o_ref.shape

    @pl.when(g == 0)
    def _():
        o_ref[...] = jnp.zeros_like(o_ref)

    x = x_ref[...]
    comb = c_ref[...]
    lane = lax.broadcasted_iota(jnp.int32, (tm, LANES), 1)
    hs = []
    for u in range(group):
        c = jnp.sum(jnp.where(lane == lane0 + g * group + u, comb, 0.0), axis=1, keepdims=True)
        gu = _dot(x, wgu_ref[u])
        gate = gu[:, :hid]
        hs.append((gate * jax.nn.sigmoid(gate) * gu[:, hid:] * c).astype(BF16))
    h = jnp.concatenate(hs, axis=1)
    cols = _tile(d, MOE_COLS)
    for lo in range(0, d, cols):
        o_ref[:, lo:lo + cols] += _dot(h, wd_ref[:, lo:lo + cols])


def _experts(xb, comb, w_gate, w_up, w_down, group, lane0):
    t, d = xb.shape
    n_e, _, hid = w_gate.shape
    assert n_e % group == 0
    wgu = jnp.concatenate([w_gate, w_up], axis=2).astype(BF16)
    wd = w_down.astype(BF16).reshape(n_e // group, group * hid, d)
    tm = _tile(t, 512, 8)
    once = pl.Buffered(1)
    tok = lambda width: pl.BlockSpec((tm, width), lambda i, e: (i, 0), pipeline_mode=once)
    return pl.pallas_call(
        functools.partial(_moe_kernel, hid=hid, group=group, lane0=lane0),
        grid=(t // tm, n_e // group),
        in_specs=[tok(d), tok(LANES), pl.BlockSpec((group, d, 2 * hid), lambda i, e: (e, 0, 0)),
                  pl.BlockSpec((None, group * hid, d), lambda i, e: (e, 0, 0))],
        out_specs=tok(d),
        out_shape=jax.ShapeDtypeStruct((t, d), F32),
        compiler_params=_cparams("parallel", "arbitrary"),
        name="moe",
    )(xb, comb, wgu, wd)


MOE_ROWS = 512
SC_WINDOW = 16


def _sc_mesh():
    return plsc.VectorSubcoreMesh(core_axis_name="core", subcore_axis_name="subcore")


def _sc_windows(n_items):
    mesh = _sc_mesh()
    workers = mesh.num_cores * mesh.num_subcores
    n_win = n_items // LANES
    assert n_items % LANES == 0
    return mesh, n_win, -(-n_win // workers)


def _sc_worker(mesh):
    return lax.axis_index("core") * mesh.num_subcores + lax.axis_index("subcore")


def _sc_scatter_rows(x, pos, n_rows):
    t, d = x.shape
    k = pos.shape[0]
    mesh, n_win, per = _sc_windows(t)

    @pl.kernel(out_type=jax.ShapeDtypeStruct((n_rows, d), x.dtype), mesh=mesh,
               scratch_types=[pltpu.VMEM((k, LANES), jnp.int32), pltpu.VMEM((SC_WINDOW, d), x.dtype)])
    def scatter(x_hbm, pos_hbm, o_hbm, idx_vmem, buf):
        first = _sc_worker(mesh) * per

        @pl.loop(0, per)
        def _(step):
            win = first + step

            @pl.when(win < n_win)
            def _():
                base = win * LANES
                pltpu.sync_copy(pos_hbm.at[:, pl.ds(base, LANES)], idx_vmem)
                for j in range(LANES // SC_WINDOW):
                    pltpu.sync_copy(x_hbm.at[pl.ds(base + j * SC_WINDOW, SC_WINDOW)], buf)
                    for kk in range(k):
                        pltpu.sync_copy(buf, o_hbm.at[idx_vmem[kk, pl.ds(j * SC_WINDOW, SC_WINDOW)]])

    return scatter(x, pos)


def _sc_gather_rows(y, idx):
    m = idx.shape[1]
    d = y.shape[1]
    mesh, n_win, per = _sc_windows(m)

    @pl.kernel(out_type=jax.ShapeDtypeStruct((m, d), y.dtype), mesh=mesh,
               scratch_types=[pltpu.VMEM((1, LANES), jnp.int32), pltpu.VMEM((SC_WINDOW, d), y.dtype)])
    def gather(y_hbm, i_hbm, o_hbm, idx_vmem, buf):
        first = _sc_worker(mesh) * per

        @pl.loop(0, per)
        def _(step):
            win = first + step

            @pl.when(win < n_win)
            def _():
                base = win * LANES
                pltpu.sync_copy(i_hbm.at[:, pl.ds(base, LANES)], idx_vmem)
                for j in range(LANES // SC_WINDOW):
                    pltpu.sync_copy(y_hbm.at[idx_vmem[0, pl.ds(j * SC_WINDOW, SC_WINDOW)]], buf)
                    pltpu.sync_copy(buf, o_hbm.at[pl.ds(base + j * SC_WINDOW, SC_WINDOW)])

    return gather(y, idx)


def _grouped_kernel(te_ref, nu_ref, x_ref, wgu_ref, wd_ref, o_ref, *, hid):
    @pl.when(pl.program_id(0) < nu_ref[0])
    def _():
        lo, hi = _unpack_pairs(x_ref[...])
        x = jnp.concatenate([lo.astype(BF16), hi.astype(BF16)], axis=1)
        gu = _dot(x, wgu_ref[...])
        gate = gu[:, :hid]
        h = (gate * jax.nn.sigmoid(gate) * gu[:, hid:]).astype(BF16)
        o_ref[...] = _pack_pairs(_dot(h, wd_ref[...]))


def _grouped_swiglu(xs, tile_expert, n_used, wgu, wd):
    n, half = xs.shape
    d = 2 * half
    hid = wd.shape[1]
    grid_spec = pltpu.PrefetchScalarGridSpec(
        num_scalar_prefetch=2,
        grid=(n // MOE_ROWS,),
        in_specs=[pl.BlockSpec((MOE_ROWS, half), lambda i, te, nu: (i, 0)),
                  pl.BlockSpec((None, d, 2 * hid), lambda i, te, nu: (te[i], 0, 0)),
                  pl.BlockSpec((None, hid, d), lambda i, te, nu: (te[i], 0, 0))],
        out_specs=pl.BlockSpec((MOE_ROWS, half), lambda i, te, nu: (i, 0)),
    )
    return pl.pallas_call(
        functools.partial(_grouped_kernel, hid=hid),
        grid_spec=grid_spec,
        out_shape=jax.ShapeDtypeStruct((n, half), jnp.int32),
        compiler_params=_cparams("parallel"),
        name="moe_grouped",
    )(tile_expert, n_used, xs, wgu, wd)


def _combine_ln_kernel(x_ref, yg_ref, w_ref, fs_ref, g_ref, b_ref, o_ref, ob_ref, *, alpha):
    half = x_ref.shape[1] // 2
    w = w_ref[...]
    acc_lo, acc_hi = None, None
    for k in range(TOP_K):
        lo, hi = _unpack_pairs(yg_ref[:, k * half:(k + 1) * half])
        wk = w[:, k:k + 1]
        acc_lo = wk * lo if k == 0 else acc_lo + wk * lo
        acc_hi = wk * hi if k == 0 else acc_hi + wk * hi
    acc = alpha * x_ref[...] + fs_ref[...] + jnp.concatenate([acc_lo, acc_hi], axis=1)
    mu = jnp.mean(acc, axis=-1, keepdims=True)
    xc = acc - mu
    var = jnp.mean(xc * xc, axis=-1, keepdims=True)
    y = xc * lax.rsqrt(var + LN_EPS) * g_ref[...] + b_ref[...]
    o_ref[...] = y
    ob_ref[...] = y.astype(BF16)


def _combine_ln(x, yg, wts, f_shared, g, b, alpha):
    t, d = x.shape
    tm = _tile(t, 64, 8)
    row = lambda width: pl.BlockSpec((tm, width), lambda i: (i, 0))
    vec = pl.BlockSpec((1, d), lambda i: (0, 0))
    return pl.pallas_call(
        functools.partial(_combine_ln_kernel, alpha=alpha),
        grid=(t // tm,),
        in_specs=[row(d), row(TOP_K * d // 2), row(LANES), row(d), vec, vec],
        out_specs=[row(d), row(d)],
        out_shape=[jax.ShapeDtypeStruct((t, d), F32), jax.ShapeDtypeStruct((t, d), BF16)],
        compiler_params=_cparams("parallel"),
        name="moe_combine_ln",
    )(x, yg, wts, f_shared, g.reshape(1, d), b.reshape(1, d))


def _routed_ln(xf, xpk, ids, wts, rank, counts, w_gate, w_up, w_down, f_shared, g, b, alpha):
    t, d = xf.shape
    n_exp = w_gate.shape[0]
    n_rows = t * TOP_K + n_exp * MOE_ROWS
    n_tiles = n_rows // MOE_ROWS
    cnt = counts[0, :n_exp]
    padded = (cnt + MOE_ROWS - 1) // MOE_ROWS * MOE_ROWS
    ends = jnp.cumsum(padded)
    starts = ends - padded
    tile_expert = jnp.minimum(jnp.searchsorted(ends, jnp.arange(n_tiles, dtype=jnp.int32) * MOE_ROWS, side='right'),
                              n_exp - 1).astype(jnp.int32)
    n_used = (ends[-1:] // MOE_ROWS).astype(jnp.int32)
    top = ids[:, :TOP_K]
    pos = (starts[top] + jnp.take_along_axis(rank, top, axis=1)).astype(jnp.int32)
    wgu = jnp.concatenate([w_gate, w_up], axis=2).astype(BF16)
    xs = _sc_scatter_rows(xpk, pos.T, n_rows)
    ys = _grouped_swiglu(xs, tile_expert, n_used, wgu, w_down.astype(BF16))
    yg = _sc_gather_rows(ys, pos.reshape(1, t * TOP_K)).reshape(t, TOP_K * d // 2)
    return _combine_ln(xf, yg, wts, f_shared, g, b, alpha)


PROJ_ALIGN = 512


def _split_w_in(w, d):
    wdt = d // N_BRANCH
    gla_heads = wdt // GLA_DV
    names = (('gate', N_BRANCH * d), ('mla_cq', MLA_Q_RANK), ('mla_ckv', MLA_KV_RANK), ('mla_kr', MLA_ROPE),
             ('dsa_q', wdt), ('dsa_k', DSA_DH), ('dsa_v', DSA_DH),
             ('idx_q', IDX_HEADS * IDX_DIM), ('idx_k', IDX_DIM), ('idx_w', IDX_HEADS),
             ('gla_q', gla_heads * GLA_DK), ('gla_k', gla_heads * GLA_DK), ('gla_v', wdt),
             ('gla_a', GLA_GATE_RANK), ('gla_r', wdt),
             ('rwkv_r', wdt), ('rwkv_k', wdt), ('rwkv_v', wdt),
             ('rwkv_w', RWKV_LORA), ('rwkv_a', RWKV_LORA), ('rwkv_g', RWKV_LORA))
    parts, off = {}, 0
    for name, width in names:
        parts[name] = w[:, off:off + width]
        off += width
    assert off == w.shape[1], (off, w.shape)
    return parts


def _group(cols, dtype=BF16):
    wcat = jnp.concatenate(cols, axis=1)
    return _pad_cols(wcat, -(-wcat.shape[1] // PROJ_ALIGN) * PROJ_ALIGN).astype(dtype)


def kernel(x, positions, ln_in_g, ln_in_b, w_in, w_branch, w_out, mla_q_norm, mla_w_uq, mla_kv_norm, mla_w_ukv,
           gla_w_gate2, gla_b_gate, gla_norm_g, rwkv_mu_rkv, rwkv_mu_lora, rwkv_w0, rwkv_w2, rwkv_a0, rwkv_a2,
           rwkv_g2, rwkv_k_k, rwkv_k_a, rwkv_r_k, rwkv_lnx_g, rwkv_lnx_b, rwkv_v0, rwkv_v1, rwkv_v2,
           ln_mix_g, ln_mix_b, router_w, router_bias, exp_w_gate, exp_w_up, exp_w_down,
           sh_w_gate, sh_w_up, sh_w_down, ln_ffn_g, ln_ffn_b):
    batch, seq, d = x.shape
    depth = w_in.shape[0]
    t, wdt = batch * seq, d // N_BRANCH
    alpha = (2 * depth) ** 0.25
    ct, st = _rope_tables(positions)
    xf, xb = _layer_norm(x.reshape(t, d), (), ln_in_g, ln_in_b)
    v_first = None
    for l in range(depth):
        p = _split_w_in(w_in[l], d)
        pad128 = lambda w: _pad_cols(w, LANES)
        idx_q = jnp.pad(p['idx_q'].reshape(d, IDX_HEADS, IDX_DIM),
                        ((0, 0), (0, 0), (0, LANES - IDX_DIM))).reshape(d, IDX_HEADS * LANES)
        w_lat = _group([p['mla_cq'], p['mla_ckv'], pad128(p['mla_kr']), pad128(_swap_halves(p['mla_kr']))])
        dsa_q = p['dsa_q'] * (DSA_DH ** -0.5 * LOG2E)
        w_dsa = _group([idx_q, dsa_q, p['dsa_k'], p['dsa_v'], pad128(p['idx_k']), pad128(p['idx_w'])])
        w_gla = _group([p['gla_v'], p['gla_r'], p['gla_q'], p['gla_k'], pad128(p['gla_a'])])
        w_rwkv = _group([p['rwkv_r'], p['rwkv_k'], p['rwkv_v'], p['rwkv_w'], p['rwkv_a'], pad128(p['rwkv_g'])])
        lat = _matmul(xb, w_lat, F32)
        pb = _matmul(xb, w_dsa, BF16)
        pc = _matmul(xb, w_gla, F32)
        pd = _matmul(xb, w_rwkv, F32)

        y_mla = _mla(lat, ct, st, mla_q_norm[l], mla_w_uq[l], mla_kv_norm[l], mla_w_ukv[l], batch, seq)
        y_dsa = _dsa(pb, batch, seq, wdt)
        y_gla = _gla(pc, gla_w_gate2[l], gla_b_gate[l], gla_norm_g[l], batch, seq)
        vres = None if l == 0 else (rwkv_v0[l - 1], rwkv_v1[l - 1], rwkv_v2[l - 1])
        y_rwkv, v_first = _rwkv(pd, rwkv_mu_rkv[l], rwkv_mu_lora[l], rwkv_w0[l], rwkv_w2[l], rwkv_a0[l], rwkv_a2[l],
                                rwkv_g2[l], rwkv_k_k[l], rwkv_k_a[l], rwkv_r_k[l], rwkv_lnx_g[l], rwkv_lnx_b[l],
                                v_first, vres, batch, seq)

        merged = _merge(xb, (y_mla, y_dsa, y_gla, y_rwkv), p['gate'].astype(BF16), w_branch[l].astype(BF16))
        xf, xb, xpk = _layer_norm(xf, (_matmul(merged, w_out[l].astype(BF16), F32),), ln_mix_g[l], ln_mix_b[l], alpha,
                                  packed=True)

        comb, ids, wts, rank, counts = _router(xb, router_w[l], router_bias[l])
        n_exp = exp_w_gate.shape[1]
        f_shared = _experts(xb, comb, sh_w_gate[l][None], sh_w_up[l][None], sh_w_down[l][None], 1, n_exp)
        xf, xb = _routed_ln(xf, xpk, ids, wts, rank, counts, exp_w_gate[l], exp_w_up[l], exp_w_down[l],
                            f_shared, ln_ffn_g[l], ln_ffn_b[l], alpha)
    return xf.reshape(batch, seq, d)
```

```python
import functools
import math

import jax
import jax.numpy as jnp
from jax import lax
from jax.experimental import pallas as pl
from jax.experimental.pallas import tpu as pltpu
from jax.experimental.pallas import tpu_sc as plsc

F32 = jnp.float32
BF16 = jnp.bfloat16

LANES = 128

MLA_NOPE, MLA_ROPE, MLA_V = 128, 64, 128
MLA_Q_RANK, MLA_KV_RANK = 768, 256
ROPE_THETA = 10000.0
DSA_DH = 128
IDX_HEADS, IDX_DIM, IDX_TOPK_MAX = 16, 64, 256
GLA_DV, GLA_DK, GLA_GATE_RANK, GLA_TAU, GLA_CHUNK = 256, 128, 16, 16.0, 64
GLA_SUB = 16
RWKV_HS, RWKV_LORA, RWKV_GN_EPS = 64, 64, 64e-5
RWKV_CHUNK = 64
TOP_K, ROUTED_SCALE = 8, 2.5
LN_EPS, RMS_EPS = 1e-5, 1e-6
NEG_BIG = -1e30
LOG2E = math.log2(math.e)

VMEM_LIMIT = 56 * 1024 * 1024


def _cparams(*sem):
    return pltpu.CompilerParams(dimension_semantics=sem, vmem_limit_bytes=VMEM_LIMIT)


def _tile(n, pref, unit=LANES):
    if n <= pref:
        return n
    t = (pref // unit) * unit
    while t > unit and n % t:
        t -= unit
    assert n % t == 0, (n, pref, unit)
    return t


def _dot(a, b):
    return jnp.dot(a, b, preferred_element_type=F32)


def _dot_t(a, b):
    return lax.dot_general(a, b, (((1,), (1,)), ((), ())), preferred_element_type=F32)


def _dot_f32(a, b):
    return jnp.dot(a, b, preferred_element_type=F32, precision=lax.Precision.HIGHEST)


def _masked_sums(mask, x):
    w = x.shape[1]
    hi = x.astype(BF16)
    rest = x - hi.astype(F32)
    mid = rest.astype(BF16)
    lo = (rest - mid.astype(F32)).astype(BF16)
    s = _dot(mask, jnp.concatenate([hi, mid, lo], axis=1))
    return s[:, :w] + s[:, w:2 * w] + s[:, 2 * w:]


def _mm_kernel(a_ref, b_ref, o_ref):
    o_ref[...] = _dot(a_ref[...], b_ref[...]).astype(o_ref.dtype)


def _matmul(a, b, out_dtype, tm=1024, tn=512):
    m, k = a.shape
    n = b.shape[1]
    tm, tn = _tile(m, tm, 8), _tile(n, tn)
    return pl.pallas_call(
        _mm_kernel,
        grid=(m // tm, n // tn),
        in_specs=[pl.BlockSpec((tm, k), lambda i, j: (i, 0)), pl.BlockSpec((k, tn), lambda i, j: (0, j))],
        out_specs=pl.BlockSpec((tm, tn), lambda i, j: (i, j)),
        out_shape=jax.ShapeDtypeStruct((m, n), out_dtype),
        compiler_params=_cparams("parallel", "parallel"),
        name="matmul",
    )(a, b)


def _pack_pairs(v):
    half = v.shape[1] // 2
    bits = lax.bitcast_convert_type(v.astype(BF16).astype(F32), jnp.int32)
    return (bits[:, half:] & jnp.int32(-65536)) | lax.shift_right_logical(bits[:, :half], 16)


def _unpack_pairs(w):
    return (lax.bitcast_convert_type(lax.shift_left(w, 16), F32),
            lax.bitcast_convert_type(w & jnp.int32(-65536), F32))


def _ln_kernel(*refs, alpha, n_res, packed):
    x_ref, res_refs = refs[0], refs[1:1 + n_res]
    g_ref, b_ref, o_ref, ob_ref = refs[1 + n_res:5 + n_res]
    x = x_ref[...]
    if n_res:
        x = alpha * x
        for f_ref in res_refs:
            x = x + f_ref[...]
    mu = jnp.mean(x, axis=-1, keepdims=True)
    xc = x - mu
    var = jnp.mean(xc * xc, axis=-1, keepdims=True)
    y = xc * lax.rsqrt(var + LN_EPS) * g_ref[...] + b_ref[...]
    o_ref[...] = y
    ob_ref[...] = y.astype(BF16)
    if packed:
        refs[-1][...] = _pack_pairs(y)


def _layer_norm(x, res, g, b, alpha=1.0, packed=False):
    t, d = x.shape
    tm = _tile(t, 128, 8)
    row = pl.BlockSpec((tm, d), lambda i: (i, 0))
    vec = pl.BlockSpec((1, d), lambda i: (0, 0))
    args = (x,) + tuple(res)
    extra_spec = [pl.BlockSpec((tm, d // 2), lambda i: (i, 0))] if packed else []
    extra_shape = [jax.ShapeDtypeStruct((t, d // 2), jnp.int32)] if packed else []
    return pl.pallas_call(
        functools.partial(_ln_kernel, alpha=alpha, n_res=len(res), packed=packed),
        grid=(t // tm,),
        in_specs=[row] * len(args) + [vec, vec],
        out_specs=[row, row] + extra_spec,
        out_shape=[jax.ShapeDtypeStruct((t, d), F32), jax.ShapeDtypeStruct((t, d), BF16)] + extra_shape,
        compiler_params=_cparams("parallel"),
        name="layer_norm",
    )(*args, g.reshape(1, d), b.reshape(1, d))


def _pad_cols(w, width):
    return jnp.pad(w, ((0, 0), (0, width - w.shape[1])))


def _rms(x, g):
    return x * lax.rsqrt(jnp.mean(x * x, axis=-1, keepdims=True) + RMS_EPS) * g


def _mla_prep_kernel(lat_ref, ct_ref, st_ref, qg_ref, kg_ref, wqn_ref, wqr_ref, wqs_ref, wkn_ref, wv_ref,
                     qn_ref, qr_ref, kn_ref, kr_ref, v_ref, *, heads, scale):
    lat = lat_ref[...]
    cq = lat[:, :MLA_Q_RANK]
    ckv = lat[:, MLA_Q_RANK:MLA_Q_RANK + MLA_KV_RANK]
    kr = lat[:, MLA_Q_RANK + MLA_KV_RANK:MLA_Q_RANK + MLA_KV_RANK + LANES]
    krs = lat[:, MLA_Q_RANK + MLA_KV_RANK + LANES:MLA_Q_RANK + MLA_KV_RANK + 2 * LANES]
    ct, st = ct_ref[...], st_ref[...]
    nq = _rms(cq, qg_ref[...]).astype(BF16)
    nkv = _rms(ckv, kg_ref[...]).astype(BF16)
    qn_ref[...] = (_dot(nq, wqn_ref[...]) * scale).astype(BF16)
    cth = jnp.concatenate([ct] * heads, axis=1)
    sth = jnp.concatenate([st] * heads, axis=1)
    qr_ref[...] = ((_dot(nq, wqr_ref[...]) * cth + _dot(nq, wqs_ref[...]) * sth) * scale).astype(BF16)
    kn_ref[...] = _dot(nkv, wkn_ref[...]).astype(BF16)
    v_ref[...] = _dot(nkv, wv_ref[...]).astype(BF16)
    kr_ref[...] = (kr * ct + krs * st).astype(BF16)


def _flash_kernel(qn_ref, qr_ref, kn_ref, kr_ref, v_ref, o_ref, m_ref, l_ref, acc_ref, *, tq, tk):
    i, j = pl.program_id(2), pl.program_id(3)

    @pl.when(j == 0)
    def _():
        m_ref[...] = jnp.full_like(m_ref, NEG_BIG)
        l_ref[...] = jnp.zeros_like(l_ref)
        acc_ref[...] = jnp.zeros_like(acc_ref)

    def step(masked):
        q = jnp.concatenate([qn_ref[...], qr_ref[...]], axis=1)
        k = jnp.concatenate([kn_ref[...], kr_ref[...]], axis=1)
        s = _dot_t(q, k)
        if masked:
            qpos = i * tq + lax.broadcasted_iota(jnp.int32, (tq, tk), 0)
            kpos = j * tk + lax.broadcasted_iota(jnp.int32, (tq, tk), 1)
            s = jnp.where(kpos <= qpos, s, -jnp.inf)
        m_prev = m_ref[...]
        m_new = jnp.maximum(m_prev, jnp.max(s, axis=1, keepdims=True))
        alpha = jnp.exp2(m_prev - m_new)
        p = jnp.exp2(s - jnp.concatenate([m_new] * (tk // LANES), axis=1))
        l_ref[...] = alpha * l_ref[...] + jnp.sum(p, axis=1, keepdims=True)
        acc_ref[...] = alpha * acc_ref[...] + _dot(p.astype(BF16), v_ref[...])
        m_ref[...] = m_new

    below = j * tk + tk - 1 <= i * tq
    touches = j * tk <= i * tq + tq - 1

    @pl.when(below)
    def _():
        step(False)

    @pl.when(jnp.logical_and(touches, jnp.logical_not(below)))
    def _():
        step(True)

    @pl.when(j == pl.num_programs(3) - 1)
    def _():
        o_ref[...] = (acc_ref[...] / l_ref[...]).astype(o_ref.dtype)


def _rope_tables(positions):
    half = MLA_ROPE // 2
    inv = ROPE_THETA ** (-jnp.arange(0, MLA_ROPE, 2, dtype=F32) / MLA_ROPE)
    ang = positions.reshape(-1).astype(F32)[:, None] * inv
    cos, sin = jnp.cos(ang), jnp.sin(ang)
    zero = jnp.zeros((ang.shape[0], LANES - 2 * half), F32)
    return jnp.concatenate([cos, cos, zero], 1), jnp.concatenate([-sin, sin, zero], 1)


def _swap_halves(w):
    half = w.shape[-1] // 2
    return jnp.concatenate([w[..., half:], w[..., :half]], -1)


def _mla(lat, ct, st, q_norm, w_uq, kv_norm, w_ukv, batch, seq):
    t = lat.shape[0]
    heads = w_uq.shape[1] // (MLA_NOPE + MLA_ROPE)
    hw = heads * LANES
    wq = w_uq.reshape(MLA_Q_RANK, heads, MLA_NOPE + MLA_ROPE)
    wqn = wq[:, :, :MLA_NOPE].reshape(MLA_Q_RANK, hw).astype(BF16)
    rope_pad = ((0, 0), (0, 0), (0, LANES - MLA_ROPE))
    wqr = jnp.pad(wq[:, :, MLA_NOPE:], rope_pad).reshape(MLA_Q_RANK, hw).astype(BF16)
    wqs = jnp.pad(_swap_halves(wq[:, :, MLA_NOPE:]), rope_pad).reshape(MLA_Q_RANK, hw).astype(BF16)
    wkv = w_ukv.reshape(MLA_KV_RANK, heads, MLA_NOPE + MLA_V)
    wkn = wkv[:, :, :MLA_NOPE].reshape(MLA_KV_RANK, hw).astype(BF16)
    wv = wkv[:, :, MLA_NOPE:].reshape(MLA_KV_RANK, hw).astype(BF16)

    tm = _tile(t, 512, 8)
    row = lambda w: pl.BlockSpec((tm, w), lambda i: (i, 0))
    full = lambda a: pl.BlockSpec(a.shape, lambda i: (0, 0))
    qg, kg = q_norm.reshape(1, -1), kv_norm.reshape(1, -1)
    qn, qr, kn, kr, v = pl.pallas_call(
        functools.partial(_mla_prep_kernel, heads=heads, scale=(MLA_NOPE + MLA_ROPE) ** -0.5 * LOG2E),
        grid=(t // tm,),
        in_specs=[row(lat.shape[1]), row(LANES), row(LANES), full(qg), full(kg),
                  full(wqn), full(wqr), full(wqs), full(wkn), full(wv)],
        out_specs=[row(hw), row(hw), row(hw), row(LANES), row(hw)],
        out_shape=[jax.ShapeDtypeStruct((t, hw), BF16)] * 3 + [jax.ShapeDtypeStruct((t, LANES), BF16),
                                                               jax.ShapeDtypeStruct((t, hw), BF16)],
        compiler_params=_cparams("parallel"),
        name="mla_prep",
    )(lat, ct, st, qg, kg, wqn, wqr, wqs, wkn, wv)

    tq = tk = _tile(seq, 1024, 8)
    nq, nk = seq // tq, seq // tk
    r3 = lambda a: a.reshape(batch, seq, a.shape[1])
    last = lambda i, j: jnp.minimum(j, ((i + 1) * tq - 1) // tk)
    q_spec = pl.BlockSpec((None, tq, LANES), lambda b, h, i, j: (b, i, h))
    k_spec = pl.BlockSpec((None, tk, LANES), lambda b, h, i, j: (b, last(i, j), h))
    kr_spec = pl.BlockSpec((None, tk, LANES), lambda b, h, i, j: (b, last(i, j), 0))
    out = pl.pallas_call(
        functools.partial(_flash_kernel, tq=tq, tk=tk),
        grid=(batch, heads, nq, nk),
        in_specs=[q_spec, q_spec, k_spec, kr_spec, k_spec],
        out_specs=q_spec,
        out_shape=jax.ShapeDtypeStruct((batch, seq, hw), BF16),
        scratch_shapes=[pltpu.VMEM((tq, LANES), F32), pltpu.VMEM((tq, LANES), F32), pltpu.VMEM((tq, LANES), F32)],
        compiler_params=_cparams("parallel", "parallel", "parallel", "arbitrary"),
        name="mla_flash",
    )(r3(qn), r3(qr), r3(kn), r3(kr), r3(v))
    return out.reshape(t, hw)


DSA_TQ = 128
DSA_TK = 512
INT_MIN = -2 ** 31


def _dsa_kernel(iq_ref, q_ref, k_ref, v_ref, ik_ref, iw_ref, o_ref, key_ref, m_ref, l_ref, acc_ref,
                *, tq, tk, n_sel, heads, pos_bits):
    i = pl.program_id(1)
    n_kt = (i * tq + tq - 1) // tk + 1
    reps = tk // LANES
    wide = lambda a: jnp.concatenate([a] * reps, axis=1)
    qpos = i * tq + lax.broadcasted_iota(jnp.int32, (tq, tk), 0)
    lane_pos = lax.broadcasted_iota(jnp.int32, (tq, tk), 1)

    iq = iq_ref[...]
    iw = iw_ref[...].astype(F32) * (IDX_HEADS ** -0.5 * IDX_DIM ** -0.5)
    iw_b = [jnp.broadcast_to(iw[:, h:h + 1], (tq, LANES)) for h in range(IDX_HEADS)]

    def score_tile(j, carry):
        ikj = ik_ref[pl.ds(pl.multiple_of(j * tk, tk), tk), :]
        sc = jnp.zeros((tq, tk), F32)
        for h in range(IDX_HEADS):
            logit = _dot_t(iq[:, h * LANES:(h + 1) * LANES], ikj)
            sc = sc + wide(iw_b[h]) * jnp.maximum(logit, 0.0)
        sc = jnp.where(j * tk + lane_pos <= qpos, sc + 0.0, -jnp.inf)
        bits = lax.bitcast_convert_type(sc, jnp.int32)
        key_ref[j] = jnp.where(bits >= 0, bits, bits ^ 0x7FFFFFFF)
        return carry

    lax.fori_loop(0, n_kt, score_tile, 0)

    def count(pred):
        def body(j, acc):
            c = pred(key_ref[j], j * tk + lane_pos).astype(jnp.int32)
            for rep in range(reps):
                acc = acc + c[:, rep * LANES:(rep + 1) * LANES]
            return acc
        acc = lax.fori_loop(0, n_kt, body, jnp.zeros((tq, LANES), jnp.int32))
        return jnp.broadcast_to(jnp.sum(acc, axis=1, keepdims=True), (tq, LANES))

    def thr_bit(bit_i, thr):
        cand = thr ^ jnp.left_shift(jnp.int32(1), 31 - bit_i)
        cnt = count(lambda key, pos: key >= wide(cand))
        return jnp.where(cnt >= n_sel, cand, thr)

    thr = lax.fori_loop(0, 32, thr_bit, jnp.full((tq, LANES), INT_MIN, jnp.int32))
    cnt_gt = count(lambda key, pos: key > wide(thr))
    cnt_ge = count(lambda key, pos: key >= wide(thr))
    need = n_sel - cnt_gt

    def tie_bit(bit_i, cut):
        cand = cut | jnp.left_shift(jnp.int32(1), pos_bits - 1 - bit_i)
        cnt = count(lambda key, pos: jnp.logical_and(key == wide(thr), pos < wide(cand)))
        return jnp.where(cnt < need, cand, cut)

    surplus = jnp.max(cnt_ge - cnt_gt - need) > 0
    cut = lax.cond(surplus,
                   lambda: lax.fori_loop(0, pos_bits, tie_bit, jnp.zeros((tq, LANES), jnp.int32)),
                   lambda: jnp.full((tq, LANES), 2 ** 31 - 1, jnp.int32))

    q = q_ref[...]
    q_all = jnp.concatenate([q[:, h * LANES:(h + 1) * LANES] for h in range(heads)], axis=0)
    m_ref[...] = jnp.full_like(m_ref, NEG_BIG)
    l_ref[...] = jnp.zeros_like(l_ref)
    acc_ref[...] = jnp.zeros_like(acc_ref)

    def attend(j, carry):
        rows = pl.ds(pl.multiple_of(j * tk, tk), tk)
        s = _dot_t(q_all, k_ref[rows, :])
        key, pos = key_ref[j], j * tk + lane_pos
        keep = jnp.where(key > wide(thr), 0.0,
                         jnp.where(jnp.logical_and(key == wide(thr), pos <= wide(cut)), 0.0, -jnp.inf))
        keep = jnp.where(pos <= qpos, keep, -jnp.inf)
        s = s + jnp.concatenate([keep] * heads, axis=0)
        m_prev = m_ref[...]
        m_new = jnp.maximum(m_prev, jnp.max(s, axis=1, keepdims=True))
        alpha = jnp.exp2(m_prev - m_new)
        p = jnp.exp2(s - jnp.concatenate([m_new] * reps, axis=1))
        l_ref[...] = alpha * l_ref[...] + jnp.sum(p, axis=1, keepdims=True)
        acc_ref[...] = alpha * acc_ref[...] + _dot(p.astype(BF16), v_ref[rows, :])
        m_ref[...] = m_new
        return carry

    lax.fori_loop(0, n_kt, attend, 0)
    out = acc_ref[...] / l_ref[...]
    o_ref[...] = jnp.concatenate([out[h * tq:(h + 1) * tq] for h in range(heads)], axis=1).astype(o_ref.dtype)


def _dsa(pb, batch, seq, width):
    t, wb = pb.shape
    heads = width // DSA_DH
    iqw = IDX_HEADS * LANES
    tq, tk = _tile(seq, DSA_TQ, 8), _tile(seq, DSA_TK)
    n_sel = min(IDX_TOPK_MAX, seq // 4)
    c0 = (iqw + width) // LANES
    pb3 = pb.reshape(batch, seq, wb)
    qblk = lambda w, idx: pl.BlockSpec((None, tq, w), lambda b, i: (b, i, idx))
    seqblk = lambda idx: pl.BlockSpec((None, seq, LANES), lambda b, i: (b, 0, idx))
    out = pl.pallas_call(
        functools.partial(_dsa_kernel, tq=tq, tk=tk, n_sel=n_sel, heads=heads,
                          pos_bits=max(1, (seq - 1).bit_length())),
        grid=(batch, seq // tq),
        in_specs=[qblk(iqw, 0), qblk(width, iqw // width), seqblk(c0), seqblk(c0 + 1), seqblk(c0 + 2),
                  qblk(LANES, c0 + 3)],
        out_specs=qblk(width, 0),
        out_shape=jax.ShapeDtypeStruct((batch, seq, width), BF16),
        scratch_shapes=[pltpu.VMEM((seq // tk, tq, tk), jnp.int32), pltpu.VMEM((heads * tq, LANES), F32),
                        pltpu.VMEM((heads * tq, LANES), F32), pltpu.VMEM((heads * tq, LANES), F32)],
        compiler_params=_cparams("parallel", "arbitrary"),
        name="dsa",
    )(pb3, pb3, pb3, pb3, pb3, pb3)
    return out.reshape(t, width)


def _gla_kernel(q_ref, k_ref, v_ref, a_ref, r_ref, wg_ref, bg_ref, ng_ref, o_ref, state_ref):
    c_len, sub = GLA_CHUNK, GLA_SUB

    @pl.when(pl.program_id(2) == 0)
    def _():
        state_ref[...] = jnp.zeros_like(state_ref)

    z_all = _dot(a_ref[...].astype(BF16), wg_ref[...]) + bg_ref[...]
    g_all = jax.nn.log_sigmoid(z_all) / GLA_TAU
    ri = lax.broadcasted_iota(jnp.int32, (c_len, c_len), 0)
    ci = lax.broadcasted_iota(jnp.int32, (c_len, c_len), 1)
    b_all = _masked_sums((ri >= ci).astype(BF16), g_all)
    ones = jnp.ones((GLA_DK, LANES), BF16)
    trow = lax.broadcasted_iota(jnp.int32, (sub, GLA_DK), 0)
    lane = lax.broadcasted_iota(jnp.int32, (sub, LANES), 1)
    for hh in range(state_ref.shape[0]):
        ksl = slice(hh * GLA_DK, (hh + 1) * GLA_DK)
        vsl = slice(hh * GLA_DV, (hh + 1) * GLA_DV)
        o = _gla_head(q_ref[:, ksl] * GLA_DK ** -0.5, k_ref[:, ksl], v_ref[:, vsl].astype(BF16), b_all[:, ksl],
                      state_ref.at[hh], ones, trow, lane)
        r = r_ref[:, vsl]
        o_ref[:, vsl] = (_rms(o, ng_ref[:, vsl]) * (r * jax.nn.sigmoid(r))).astype(o_ref.dtype)


def _gla_head(q, k, vb, b, state_ref, ones, trow, lane):
    c_len, sub = GLA_CHUNK, GLA_SUB
    state_t = state_ref[...]
    o_inter = _dot_t((q * jnp.exp(b)).astype(BF16), state_t.astype(BF16))
    outs = []
    for blk in range(c_len // sub):
        lo = blk * sub
        qi, bi, ki = q[lo:lo + sub], b[lo:lo + sub], k[lo:lo + sub]
        rows = [qi * ki[s:s + 1] * jnp.exp(jnp.where(trow >= s, bi - bi[s:s + 1], -jnp.inf)) for s in range(sub)]
        prod = jnp.concatenate(rows, axis=0)
        p_hi = prod.astype(BF16)
        p_lo = (prod - p_hi.astype(F32)).astype(BF16)
        sums = _dot(jnp.concatenate([p_hi, p_lo], axis=0), ones)
        sums = sums[:sub * sub] + sums[sub * sub:]
        attn = jnp.zeros((sub, LANES), F32)
        for s in range(sub):
            attn = attn + jnp.where(lane == s, sums[s * sub:(s + 1) * sub], 0.0)
        o_blk = _dot(attn[:, :sub].astype(BF16), vb[lo:lo + sub])
        if blk:
            qa = qi * jnp.exp(bi - bi[0:1])
            ka = k[:lo] * jnp.exp(bi[0:1] - b[:lo])
            o_blk = o_blk + _dot(_dot_t(qa.astype(BF16), ka.astype(BF16)).astype(BF16), vb[:lo])
        outs.append(o_blk)
    o = o_inter + jnp.concatenate(outs, axis=0)

    b_last = b[c_len - 1:c_len]
    k_dec = (k * jnp.exp(b_last - b)).astype(BF16)
    state_ref[...] = state_t * jnp.exp(b_last) + lax.dot_general(
        vb, k_dec, (((0,), (0,)), ((), ())), preferred_element_type=F32)
    return o


GLA_HEADS_PER_STEP = 4


def _gla(pc, w_gate2, b_gate, norm_g, batch, seq):
    t = pc.shape[0]
    heads = norm_g.shape[0] // GLA_DV
    hps = math.gcd(heads, GLA_HEADS_PER_STEP)
    groups = heads // hps
    wg = jnp.pad(w_gate2, ((0, LANES - GLA_GATE_RANK), (0, 0))).astype(BF16)
    nc = seq // GLA_CHUNK
    blk = lambda w, off: pl.BlockSpec((None, GLA_CHUNK, hps * w), lambda b, g, c: (b, c, off + g))
    par = lambda rows, w: pl.BlockSpec((rows, hps * w), lambda b, g, c: (0, g))
    pc3 = pc.reshape(batch, seq, pc.shape[1])
    out = pl.pallas_call(
        _gla_kernel,
        grid=(batch, groups, nc),
        in_specs=[blk(GLA_DK, 4 * groups), blk(GLA_DK, 5 * groups), blk(GLA_DV, 0),
                  pl.BlockSpec((None, GLA_CHUNK, LANES), lambda b, g, c: (b, c, 6 * heads)),
                  blk(GLA_DV, groups), par(LANES, GLA_DK), par(1, GLA_DK), par(1, GLA_DV)],
        out_specs=blk(GLA_DV, 0),
        out_shape=jax.ShapeDtypeStruct((batch, seq, heads * GLA_DV), BF16),
        scratch_shapes=[pltpu.VMEM((hps, GLA_DV, GLA_DK), F32)],
        compiler_params=_cparams("parallel", "parallel", "arbitrary"),
        name="gla",
    )(pc3, pc3, pc3, pc3, pc3, wg, b_gate.reshape(1, -1), norm_g.reshape(1, -1))
    return out.reshape(t, heads * GLA_DV)


_DIMS = {'nn': (((1,), (0,)), ((), ())), 'nt': (((1,), (1,)), ((), ())), 'tn': (((0,), (0,)), ((), ()))}


def _mm(a, b, form):
    return lax.dot_general(a.astype(BF16), b.astype(BF16), _DIMS[form], preferred_element_type=F32)


def _seg_sum(x, width):
    gi = lax.broadcasted_iota(jnp.int32, (LANES, LANES), 0) // width
    gj = lax.broadcasted_iota(jnp.int32, (LANES, LANES), 1) // width
    ones = (gi == gj).astype(F32)
    return jnp.concatenate([_dot_f32(x[:, c:c + LANES], ones) for c in range(0, x.shape[1], LANES)], axis=1)


def _rwkv_prep_kernel(*refs, width, seq, has_vres):
    if has_vres:
        (p_ref, pp_ref, prm_ref, ml_ref, w2_ref, a2_ref, g2_ref, vf_ref, v0_ref, v1_ref, v2_ref,
         r_ref, lw_ref, k_ref, v_ref, kk_ref, bb_ref, bonus_ref, g_ref) = refs
    else:
        (p_ref, pp_ref, prm_ref, ml_ref, w2_ref, a2_ref, g2_ref,
         r_ref, lw_ref, k_ref, v_ref, kk_ref, bb_ref, bonus_ref, g_ref) = refs
    w = width
    p, prm, ml = p_ref[...], prm_ref[...], ml_ref[...]
    tm = p.shape[0]
    above = jnp.where((pl.program_id(0) * tm) % seq == 0, 0.0, pp_ref[7:8, :])
    first = lax.broadcasted_iota(jnp.int32, p.shape, 0) == 0
    pp = jnp.where(first, jnp.broadcast_to(above, p.shape), pltpu.roll(p, 1, axis=0))
    lerp = lambda lo, hi, mu: p[:, lo:hi] + (pp[:, lo:hi] - p[:, lo:hi]) * mu
    r = lerp(0, w, prm[0:1])
    k = lerp(w, 2 * w, prm[1:2])
    v = lerp(2 * w, 3 * w, prm[2:3])
    x_wa = lerp(3 * w, 3 * w + LANES, ml[0:1])
    x_g = lerp(3 * w + LANES, 3 * w + 2 * LANES, ml[1:2])
    w_log = -jax.nn.softplus(-(prm[3:4] + _dot(jnp.tanh(x_wa).astype(BF16), w2_ref[...]))) - 0.5
    lw_ref[...] = -jnp.exp(w_log)
    a = jax.nn.sigmoid(prm[4:5] + _dot(x_wa.astype(BF16), a2_ref[...]))
    g_ref[...] = _dot(jax.nn.sigmoid(x_g).astype(BF16), g2_ref[...])
    if has_vres:
        mix = jax.nn.sigmoid(v0_ref[...] + _dot(_dot(v.astype(BF16), v1_ref[...]).astype(BF16), v2_ref[...]))
        v = v + (vf_ref[...] - v) * mix
    kk = k * prm[5:6]
    kk = kk * lax.rsqrt(jnp.maximum(_seg_sum(kk * kk, RWKV_HS), 1e-24))
    k2 = k * (1.0 + (a - 1.0) * prm[6:7])
    r_ref[...] = r
    k_ref[...] = k2
    v_ref[...] = v
    kk_ref[...] = kk
    bb_ref[...] = kk * a
    bonus_ref[...] = _seg_sum(r * k2 * prm[7:8], RWKV_HS) * v


RWKV_HEADS_PER_STEP = 8


def _rwkv_chunk_kernel(r_ref, lw_ref, k_ref, v_ref, kk_ref, bb_ref, y_ref, s_ref, *, n_chunk):
    c_len, hs = RWKV_CHUNK, RWKV_HS
    rows_n = n_chunk * c_len

    @pl.when(pl.program_id(2) == 0)
    def _():
        s_ref[...] = jnp.zeros_like(s_ref)

    ri = lax.broadcasted_iota(jnp.int32, (rows_n, rows_n), 0)
    ci = lax.broadcasted_iota(jnp.int32, (rows_n, rows_n), 1)
    same = (ri // c_len) == (ci // c_len)
    incl = jnp.logical_and(same, ci <= ri)
    strict = jnp.logical_and(same, ci < ri)
    eye = (ri == ci).astype(F32)
    e_r = lax.broadcasted_iota(jnp.int32, (hs, hs), 0)
    e_c = lax.broadcasted_iota(jnp.int32, (hs, hs), 1)
    sums = _masked_sums(jnp.concatenate([incl, same], axis=0).astype(BF16), lw_ref[...])
    cum_all, cum_c_all = sums[:rows_n], sums[rows_n:]
    heads = range(s_ref.shape[0])
    hd = []
    for hh in heads:
        sl = slice(hh * hs, (hh + 1) * hs)
        r, lw, k, v, kk, bb = (ref[...][:, sl] for ref in (r_ref, lw_ref, k_ref, v_ref, kk_ref, bb_ref))
        cum, cum_c = cum_all[:, sl], cum_c_all[:, sl]
        g_inv, g_end = jnp.exp(-cum), jnp.exp(cum_c - cum)
        kap, rt = kk * jnp.exp(cum - lw), r * jnp.exp(cum)
        bet, kt = bb * g_inv, k * g_inv
        gram = _mm(jnp.concatenate([kap, rt], axis=0), jnp.concatenate([bet, kt], axis=0), 'nt')
        hd.append(dict(v=v, kap=kap, rt=rt, bet_c=bb * g_end, kt_c=k * g_end, cum_c=cum_c,
                       n_m=jnp.where(strict, gram[:rows_n, :rows_n], 0.0),
                       a_kk=jnp.where(strict, gram[:rows_n, rows_n:], 0.0),
                       a_rb=jnp.where(incl, gram[rows_n:, :rows_n], 0.0),
                       a_rk=jnp.where(incl, gram[rows_n:, rows_n:], 0.0)))
    t_inv = [eye - h['n_m'] for h in hd]
    pw = [-h['n_m'] for h in hd]
    for _ in range(int(math.log2(c_len)) - 1):
        pw = [_mm(p, p, 'nn') for p in pw]
        t_inv = [t + _mm(t, p, 'nn') for t, p in zip(t_inv, pw)]
    av = [_mm(jnp.concatenate([h['a_kk'], h['a_rk']], axis=0), h['v'], 'nn') for h in hd]
    z = [_mm(t, jnp.concatenate([h['kap'], a[:rows_n]], axis=1), 'nn') for t, h, a in zip(t_inv, hd, av)]
    az = [_mm(h['a_rb'], zz, 'nn') for h, zz in zip(hd, z)]
    r_p = [h['rt'] - a[:, :hs] for h, a in zip(hd, az)]
    y0 = [a[rows_n:] - b[:, hs:] for a, b in zip(av, az)]
    s = [s_ref[hh] for hh in heads]
    outs = [[] for _ in heads]
    for c in range(n_chunk):
        rows = slice(c * c_len, (c + 1) * c_len)
        for hh in heads:
            h = hd[hh]
            outs[hh].append(_mm(r_p[hh][rows], s[hh], 'nt') + y0[hh][rows])
            zb = _mm(z[hh][rows], h['bet_c'][rows], 'tn')
            g_chunk = jnp.exp(h['cum_c'][c * c_len:c * c_len + 1])
            m = jnp.where(e_r == e_c, jnp.broadcast_to(g_chunk, (hs, hs)), 0.0) - zb[:hs]
            s[hh] = _mm(s[hh], m, 'nn') + _mm(h['v'][rows], h['kt_c'][rows], 'tn') - zb[hs:]
    for hh in heads:
        s_ref[hh] = s[hh]
    y_ref[...] = jnp.concatenate([jnp.concatenate(o, axis=0) for o in outs], axis=1)


def _rwkv_post_kernel(y_ref, bonus_ref, g_ref, lg_ref, lb_ref, o_ref):
    y = y_ref[...]
    mu = _seg_sum(y, RWKV_HS) * (1.0 / RWKV_HS)
    yc = y - mu
    var = _seg_sum(yc * yc, RWKV_HS) * (1.0 / RWKV_HS)
    yn = yc * lax.rsqrt(var + RWKV_GN_EPS) * lg_ref[...] + lb_ref[...]
    o_ref[...] = ((yn + bonus_ref[...]) * g_ref[...]).astype(o_ref.dtype)


def _rwkv(pd, mu_rkv, mu_lora, w0, w2, a0, a2, g2, k_k, k_a, r_k, lnx_g, lnx_b, v_first, vres, batch, seq):
    t, wd = pd.shape
    w = w0.shape[0]
    prm = jnp.stack([mu_rkv[0], mu_rkv[1], mu_rkv[2], w0, a0, k_k, k_a, r_k.reshape(-1)])
    zl = jnp.zeros((RWKV_LORA,), F32)
    ml = jnp.stack([jnp.concatenate([mu_lora[0], mu_lora[1]]), jnp.concatenate([mu_lora[2], zl])])
    zw = jnp.zeros((LANES - RWKV_LORA, w), F32)
    w2p = jnp.concatenate([w2, zw]).astype(BF16)
    a2p = jnp.concatenate([zw, a2]).astype(BF16)
    g2p = jnp.concatenate([g2, zw]).astype(BF16)
    has_vres = vres is not None
    tm = _tile(seq, 256, 8)
    row = lambda width: pl.BlockSpec((tm, width), lambda i: (i, 0))
    full = lambda a: pl.BlockSpec(a.shape, lambda i: (0, 0))
    above = pl.BlockSpec((8, wd), lambda i: (jnp.maximum(i * (tm // 8) - 1, 0), 0))
    args = [pd, pd, prm, ml, w2p, a2p, g2p]
    specs = [row(wd), above, full(prm), full(ml), full(w2p), full(a2p), full(g2p)]
    if has_vres:
        v0, v1, v2 = vres
        v1p = _pad_cols(v1, LANES).astype(BF16)
        v2p = jnp.pad(v2, ((0, LANES - v2.shape[0]), (0, 0))).astype(BF16)
        v0r = v0.reshape(1, w)
        args += [v_first, v0r, v1p, v2p]
        specs += [row(w), full(v0r), full(v1p), full(v2p)]
    r, lw, k2, v, kk, bb, bonus, g = pl.pallas_call(
        functools.partial(_rwkv_prep_kernel, width=w, seq=seq, has_vres=has_vres),
        grid=(t // tm,),
        in_specs=specs,
        out_specs=[row(w)] * 8,
        out_shape=[jax.ShapeDtypeStruct((t, w), F32)] * 8,
        compiler_params=_cparams("parallel"),
        name="rwkv_prep",
    )(*args)

    n_chunk = 4 if seq % (4 * RWKV_CHUNK) == 0 else 1
    rows_n = n_chunk * RWKV_CHUNK
    bw = math.gcd(w, RWKV_HEADS_PER_STEP * RWKV_HS)
    blk = pl.BlockSpec((None, rows_n, bw), lambda b, h, c: (b, c, h))
    r3 = lambda a: a.reshape(batch, seq, w)
    y = pl.pallas_call(
        functools.partial(_rwkv_chunk_kernel, n_chunk=n_chunk),
        grid=(batch, w // bw, seq // rows_n),
        in_specs=[blk] * 6,
        out_specs=blk,
        out_shape=jax.ShapeDtypeStruct((batch, seq, w), F32),
        scratch_shapes=[pltpu.VMEM((bw // RWKV_HS, RWKV_HS, RWKV_HS), F32)],
        compiler_params=_cparams("parallel", "parallel", "arbitrary"),
        name="rwkv_chunk",
    )(r3(r), r3(lw), r3(k2), r3(v), r3(kk), r3(bb))

    lg, lb = lnx_g.reshape(1, w), lnx_b.reshape(1, w)
    out = pl.pallas_call(
        _rwkv_post_kernel,
        grid=(t // tm,),
        in_specs=[row(w), row(w), row(w), full(lg), full(lb)],
        out_specs=row(w),
        out_shape=jax.ShapeDtypeStruct((t, w), BF16),
        compiler_params=_cparams("parallel"),
        name="rwkv_post",
    )(y.reshape(t, w), bonus, g, lg, lb)
    return out, (v_first if has_vres else v)


N_BRANCH = 4


def _merge_kernel(*refs):
    x_ref = refs[0]
    y_refs = refs[1:1 + N_BRANCH]
    wg_refs = refs[1 + N_BRANCH:1 + 2 * N_BRANCH]
    wb_refs = refs[1 + 2 * N_BRANCH:1 + 3 * N_BRANCH]
    o_ref = refs[-1]
    x = x_ref[...]
    acc = None
    for y_ref, wg_ref, wb_ref in zip(y_refs, wg_refs, wb_refs):
        term = jax.nn.sigmoid(_dot(x, wg_ref[...])) * _dot(y_ref[...], wb_ref[...])
        acc = term if acc is None else acc + term
    o_ref[...] = acc.astype(o_ref.dtype)


def _merge(xb, ys, w_gate, w_branch):
    t, d = xb.shape
    w = ys[0].shape[1]
    tm, tn = _tile(t, 512, 8), _tile(d, 256)
    nj = d // tn
    gate_spec = lambda i: pl.BlockSpec((d, tn), lambda r, j: (0, i * nj + j))
    br_spec = lambda i: pl.BlockSpec((None, w, tn), lambda r, j: (i, 0, j))
    return pl.pallas_call(
        _merge_kernel,
        grid=(t // tm, nj),
        in_specs=[pl.BlockSpec((tm, d), lambda r, j: (r, 0))] + [pl.BlockSpec((tm, w), lambda r, j: (r, 0))] * N_BRANCH
        + [gate_spec(i) for i in range(N_BRANCH)] + [br_spec(i) for i in range(N_BRANCH)],
        out_specs=pl.BlockSpec((tm, tn), lambda r, j: (r, j)),
        out_shape=jax.ShapeDtypeStruct((t, d), BF16),
        compiler_params=_cparams("parallel", "parallel"),
        name="merge",
    )(xb, *ys, *([w_gate] * N_BRANCH), *([w_branch] * N_BRANCH))


def _router_kernel(x_ref, w_ref, b_ref, comb_ref, ids_ref, wts_ref, rank_ref, cnt_ref, run_ref, *, n_exp):
    tm = x_ref.shape[0]

    @pl.when(pl.program_id(0) == 0)
    def _():
        run_ref[...] = jnp.zeros_like(run_ref)

    scores = jax.nn.sigmoid(_dot(x_ref[...], w_ref[...]))
    lane = lax.broadcasted_iota(jnp.int32, (tm, LANES), 1)
    work = jnp.where(lane < n_exp, scores + b_ref[...], -jnp.inf)
    chosen = jnp.zeros((tm, LANES), F32)
    sel = jnp.zeros((tm, LANES), F32)
    ids = jnp.zeros((tm, LANES), jnp.int32)
    wts = jnp.zeros((tm, LANES), F32)
    for it in range(TOP_K):
        best = jnp.max(work, axis=1, keepdims=True)
        first = jnp.min(jnp.where(work == best, lane, LANES), axis=1, keepdims=True)
        hit = lane == first
        chosen = jnp.where(hit, scores, chosen)
        sel = jnp.where(hit, 1.0, sel)
        ids = jnp.where(lane == it, first, ids)
        wts = jnp.where(lane == it, jnp.sum(jnp.where(hit, scores, 0.0), axis=1, keepdims=True), wts)
        work = jnp.where(hit, -jnp.inf, work)
    norm = ROUTED_SCALE / jnp.sum(chosen, axis=1, keepdims=True)
    comb_ref[...] = jnp.where(lane == n_exp, 1.0, chosen * norm)
    ids_ref[...] = ids
    wts_ref[...] = wts * norm
    ri = lax.broadcasted_iota(jnp.int32, (tm, tm), 0)
    ci = lax.broadcasted_iota(jnp.int32, (tm, tm), 1)
    before = _dot((ci < ri).astype(BF16), sel.astype(BF16))
    run = run_ref[0:1, :]
    rank_ref[...] = (before + run).astype(jnp.int32)
    run = run + jnp.sum(sel, axis=0, keepdims=True)
    run_ref[...] = jnp.broadcast_to(run, run_ref.shape)
    cnt_ref[...] = jnp.broadcast_to(run, cnt_ref.shape).astype(jnp.int32)


def _router(xb, router_w, router_bias):
    t, d = xb.shape
    n_exp = router_w.shape[1]
    assert n_exp < LANES
    tm = _tile(t, 512, 8)
    wr = _pad_cols(router_w, LANES).astype(BF16)
    br = _pad_cols(router_bias.reshape(1, n_exp), LANES)
    tok = pl.BlockSpec((tm, LANES), lambda i: (i, 0))
    tok_shape = lambda dt: jax.ShapeDtypeStruct((t, LANES), dt)
    return pl.pallas_call(
        functools.partial(_router_kernel, n_exp=n_exp),
        grid=(t // tm,),
        in_specs=[pl.BlockSpec((tm, d), lambda i: (i, 0)), pl.BlockSpec((d, LANES), lambda i: (0, 0)),
                  pl.BlockSpec((1, LANES), lambda i: (0, 0))],
        out_specs=[tok, tok, tok, tok, pl.BlockSpec((8, LANES), lambda i: (0, 0))],
        out_shape=[tok_shape(F32), tok_shape(jnp.int32), tok_shape(F32), tok_shape(jnp.int32),
                   jax.ShapeDtypeStruct((8, LANES), jnp.int32)],
        scratch_shapes=[pltpu.VMEM((8, LANES), F32)],
        compiler_params=_cparams("arbitrary"),
        name="router",
    )(xb, wr, br)


MOE_COLS = 1024


MOE_GROUP = 2


def _moe_kernel(x_ref, c_ref, wgu_ref, wd_ref, o_ref, *, hid, group, lane0):
    g = pl.program_id(1)
    tm, d = o_ref.shape

    @pl.when(g == 0)
    def _():
        o_ref[...] = jnp.zeros_like(o_ref)

    x = x_ref[...]
    comb = c_ref[...]
    lane = lax.broadcasted_iota(jnp.int32, (tm, LANES), 1)
    hs = []
    for u in range(group):
        c = jnp.sum(jnp.where(lane == lane0 + g * group + u, comb, 0.0), axis=1, keepdims=True)
        gu = _dot(x, wgu_ref[u])
        gate = gu[:, :hid]
        hs.append((gate * jax.nn.sigmoid(gate) * gu[:, hid:] * c).astype(BF16))
    h = jnp.concatenate(hs, axis=1)
    cols = _tile(d, MOE_COLS)
    for lo in range(0, d, cols):
        o_ref[:, lo:lo + cols] += _dot(h, wd_ref[:, lo:lo + cols])


def _experts(xb, comb, w_gate, w_up, w_down, group, lane0):
    t, d = xb.shape
    n_e, _, hid = w_gate.shape
    assert n_e % group == 0
    wgu = jnp.concatenate([w_gate, w_up], axis=2).astype(BF16)
    wd = w_down.astype(BF16).reshape(n_e // group, group * hid, d)
    tm = _tile(t, 512, 8)
    once = pl.Buffered(1)
    tok = lambda width: pl.BlockSpec((tm, width), lambda i, e: (i, 0), pipeline_mode=once)
    return pl.pallas_call(
        functools.partial(_moe_kernel, hid=hid, group=group, lane0=lane0),
        grid=(t // tm, n_e // group),
        in_specs=[tok(d), tok(LANES), pl.BlockSpec((group, d, 2 * hid), lambda i, e: (e, 0, 0)),
                  pl.BlockSpec((None, group * hid, d), lambda i, e: (e, 0, 0))],
        out_specs=tok(d),
        out_shape=jax.ShapeDtypeStruct((t, d), F32),
        compiler_params=_cparams("parallel", "arbitrary"),
        name="moe",
    )(xb, comb, wgu, wd)


MOE_ROWS = 512
SC_WINDOW = 16


def _sc_mesh():
    return plsc.VectorSubcoreMesh(core_axis_name="core", subcore_axis_name="subcore")


def _sc_windows(n_items):
    mesh = _sc_mesh()
    workers = mesh.num_cores * mesh.num_subcores
    n_win = n_items // LANES
    assert n_items % LANES == 0
    return mesh, n_win, -(-n_win // workers)


def _sc_worker(mesh):
    return lax.axis_index("core") * mesh.num_subcores + lax.axis_index("subcore")


def _sc_scatter_rows(x, pos, n_rows):
    t, d = x.shape
    k = pos.shape[0]
    mesh, n_win, per = _sc_windows(t)

    @pl.kernel(out_type=jax.ShapeDtypeStruct((n_rows, d), x.dtype), mesh=mesh,
               scratch_types=[pltpu.VMEM((k, LANES), jnp.int32), pltpu.VMEM((SC_WINDOW, d), x.dtype)])
    def scatter(x_hbm, pos_hbm, o_hbm, idx_vmem, buf):
        first = _sc_worker(mesh) * per

        @pl.loop(0, per)
        def _(step):
            win = first + step

            @pl.when(win < n_win)
            def _():
                base = win * LANES
                pltpu.sync_copy(pos_hbm.at[:, pl.ds(base, LANES)], idx_vmem)
                for j in range(LANES // SC_WINDOW):
                    pltpu.sync_copy(x_hbm.at[pl.ds(base + j * SC_WINDOW, SC_WINDOW)], buf)
                    for kk in range(k):
                        pltpu.sync_copy(buf, o_hbm.at[idx_vmem[kk, pl.ds(j * SC_WINDOW, SC_WINDOW)]])

    return scatter(x, pos)


def _sc_gather_rows(y, idx):
    m = idx.shape[1]
    d = y.shape[1]
    mesh, n_win, per = _sc_windows(m)

    @pl.kernel(out_type=jax.ShapeDtypeStruct((m, d), y.dtype), mesh=mesh,
               scratch_types=[pltpu.VMEM((1, LANES), jnp.int32), pltpu.VMEM((SC_WINDOW, d), y.dtype)])
    def gather(y_hbm, i_hbm, o_hbm, idx_vmem, buf):
        first = _sc_worker(mesh) * per

        @pl.loop(0, per)
        def _(step):
            win = first + step

            @pl.when(win < n_win)
            def _():
                base = win * LANES
                pltpu.sync_copy(i_hbm.at[:, pl.ds(base, LANES)], idx_vmem)
                for j in range(LANES // SC_WINDOW):
                    pltpu.sync_copy(y_hbm.at[idx_vmem[0, pl.ds(j * SC_WINDOW, SC_WINDOW)]], buf)
                    pltpu.sync_copy(buf, o_hbm.at[pl.ds(base + j * SC_WINDOW, SC_WINDOW)])

    return gather(y, idx)


def _grouped_kernel(te_ref, nu_ref, x_ref, wg_ref, wu_ref, wd_ref, o_ref, wgu_bf, wd_bf, *, hid):
    i = pl.program_id(0)

    @pl.when(jnp.logical_or(i == 0, te_ref[i] != te_ref[jnp.maximum(i - 1, 0)]))
    def _():
        wgu_bf[:, :hid] = wg_ref[...].astype(BF16)
        wgu_bf[:, hid:] = wu_ref[...].astype(BF16)
        wd_bf[...] = wd_ref[...].astype(BF16)

    @pl.when(i < nu_ref[0])
    def _():
        lo, hi = _unpack_pairs(x_ref[...])
        x = jnp.concatenate([lo.astype(BF16), hi.astype(BF16)], axis=1)
        gu = _dot(x, wgu_bf[...])
        gate = gu[:, :hid]
        h = (gate * jax.nn.sigmoid(gate) * gu[:, hid:]).astype(BF16)
        o_ref[...] = _pack_pairs(_dot(h, wd_bf[...]))


def _grouped_swiglu(xs, tile_expert, n_used, w_gate, w_up, w_down):
    n, half = xs.shape
    d = 2 * half
    hid = w_down.shape[1]
    once = pl.Buffered(1)
    grid_spec = pltpu.PrefetchScalarGridSpec(
        num_scalar_prefetch=2,
        grid=(n // MOE_ROWS,),
        in_specs=[pl.BlockSpec((MOE_ROWS, half), lambda i, te, nu: (i, 0)),
                  pl.BlockSpec((None, d, hid), lambda i, te, nu: (te[i], 0, 0), pipeline_mode=once),
                  pl.BlockSpec((None, d, hid), lambda i, te, nu: (te[i], 0, 0), pipeline_mode=once),
                  pl.BlockSpec((None, hid, d), lambda i, te, nu: (te[i], 0, 0), pipeline_mode=once)],
        out_specs=pl.BlockSpec((MOE_ROWS, half), lambda i, te, nu: (i, 0)),
        scratch_shapes=[pltpu.VMEM((d, 2 * hid), BF16), pltpu.VMEM((hid, d), BF16)],
    )
    return pl.pallas_call(
        functools.partial(_grouped_kernel, hid=hid),
        grid_spec=grid_spec,
        out_shape=jax.ShapeDtypeStruct((n, half), jnp.int32),
        compiler_params=_cparams("arbitrary"),
        name="moe_grouped",
    )(tile_expert, n_used, xs, w_gate, w_up, w_down)


def _combine_ln_kernel(x_ref, yg_ref, w_ref, fs_ref, g_ref, b_ref, o_ref, ob_ref, *, alpha):
    w = w_ref[...]
    acc_lo, acc_hi = None, None
    for k in range(TOP_K):
        lo, hi = _unpack_pairs(yg_ref[k])
        wk = w[:, k:k + 1]
        acc_lo = wk * lo if k == 0 else acc_lo + wk * lo
        acc_hi = wk * hi if k == 0 else acc_hi + wk * hi
    acc = alpha * x_ref[...] + fs_ref[...] + jnp.concatenate([acc_lo, acc_hi], axis=1)
    mu = jnp.mean(acc, axis=-1, keepdims=True)
    xc = acc - mu
    var = jnp.mean(xc * xc, axis=-1, keepdims=True)
    y = xc * lax.rsqrt(var + LN_EPS) * g_ref[...] + b_ref[...]
    o_ref[...] = y
    ob_ref[...] = y.astype(BF16)


def _combine_ln(x, yg, wts, f_shared, g, b, alpha):
    t, d = x.shape
    tm = _tile(t, 64, 8)
    row = lambda width: pl.BlockSpec((tm, width), lambda i: (i, 0))
    vec = pl.BlockSpec((1, d), lambda i: (0, 0))
    return pl.pallas_call(
        functools.partial(_combine_ln_kernel, alpha=alpha),
        grid=(t // tm,),
        in_specs=[row(d), pl.BlockSpec((TOP_K, tm, d // 2), lambda i: (0, i, 0)), row(LANES), row(d), vec, vec],
        out_specs=[row(d), row(d)],
        out_shape=[jax.ShapeDtypeStruct((t, d), F32), jax.ShapeDtypeStruct((t, d), BF16)],
        compiler_params=_cparams("parallel"),
        name="moe_combine_ln",
    )(x, yg, wts, f_shared, g.reshape(1, d), b.reshape(1, d))


def _routed_ln(xf, xpk, ids, wts, rank, counts, w_gate, w_up, w_down, f_shared, g, b, alpha):
    t, d = xf.shape
    n_exp = w_gate.shape[0]
    n_rows = t * TOP_K + n_exp * MOE_ROWS
    n_tiles = n_rows // MOE_ROWS
    cnt = counts[0, :n_exp]
    padded = (cnt + MOE_ROWS - 1) // MOE_ROWS * MOE_ROWS
    ends = jnp.cumsum(padded)
    starts = ends - padded
    tile_start = jnp.arange(n_tiles, dtype=jnp.int32) * MOE_ROWS
    tile_expert = jnp.minimum(jnp.sum((ends[None, :] <= tile_start[:, None]).astype(jnp.int32), axis=1), n_exp - 1)
    n_used = (ends[-1:] // MOE_ROWS).astype(jnp.int32)
    top = ids[:, :TOP_K]
    pos_t = (starts[top] + jnp.take_along_axis(rank, top, axis=1)).astype(jnp.int32).T
    xs = _sc_scatter_rows(xpk, pos_t, n_rows)
    ys = _grouped_swiglu(xs, tile_expert, n_used, w_gate, w_up, w_down)
    yg = _sc_gather_rows(ys, pos_t.reshape(1, TOP_K * t)).reshape(TOP_K, t, d // 2)
    return _combine_ln(xf, yg, wts, f_shared, g, b, alpha)


PROJ_ALIGN = 512


def _split_w_in(w, d):
    wdt = d // N_BRANCH
    gla_heads = wdt // GLA_DV
    names = (('gate', N_BRANCH * d), ('mla_cq', MLA_Q_RANK), ('mla_ckv', MLA_KV_RANK), ('mla_kr', MLA_ROPE),
             ('dsa_q', wdt), ('dsa_k', DSA_DH), ('dsa_v', DSA_DH),
             ('idx_q', IDX_HEADS * IDX_DIM), ('idx_k', IDX_DIM), ('idx_w', IDX_HEADS),
             ('gla_q', gla_heads * GLA_DK), ('gla_k', gla_heads * GLA_DK), ('gla_v', wdt),
             ('gla_a', GLA_GATE_RANK), ('gla_r', wdt),
             ('rwkv_r', wdt), ('rwkv_k', wdt), ('rwkv_v', wdt),
             ('rwkv_w', RWKV_LORA), ('rwkv_a', RWKV_LORA), ('rwkv_g', RWKV_LORA))
    parts, off = {}, 0
    for name, width in names:
        parts[name] = w[:, off:off + width]
        off += width
    assert off == w.shape[1], (off, w.shape)
    return parts


def _group(cols, dtype=BF16):
    wcat = jnp.concatenate(cols, axis=1)
    return _pad_cols(wcat, -(-wcat.shape[1] // PROJ_ALIGN) * PROJ_ALIGN).astype(dtype)


def kernel(x, positions, ln_in_g, ln_in_b, w_in, w_branch, w_out, mla_q_norm, mla_w_uq, mla_kv_norm, mla_w_ukv,
           gla_w_gate2, gla_b_gate, gla_norm_g, rwkv_mu_rkv, rwkv_mu_lora, rwkv_w0, rwkv_w2, rwkv_a0, rwkv_a2,
           rwkv_g2, rwkv_k_k, rwkv_k_a, rwkv_r_k, rwkv_lnx_g, rwkv_lnx_b, rwkv_v0, rwkv_v1, rwkv_v2,
           ln_mix_g, ln_mix_b, router_w, router_bias, exp_w_gate, exp_w_up, exp_w_down,
           sh_w_gate, sh_w_up, sh_w_down, ln_ffn_g, ln_ffn_b):
    batch, seq, d = x.shape
    depth = w_in.shape[0]
    t, wdt = batch * seq, d // N_BRANCH
    alpha = (2 * depth) ** 0.25
    ct, st = _rope_tables(positions)
    xf, xb = _layer_norm(x.reshape(t, d), (), ln_in_g, ln_in_b)
    v_first = None
    for l in range(depth):
        p = _split_w_in(w_in[l], d)
        pad128 = lambda w: _pad_cols(w, LANES)
        idx_q = jnp.pad(p['idx_q'].reshape(d, IDX_HEADS, IDX_DIM),
                        ((0, 0), (0, 0), (0, LANES - IDX_DIM))).reshape(d, IDX_HEADS * LANES)
        w_lat = _group([p['mla_cq'], p['mla_ckv'], pad128(p['mla_kr']), pad128(_swap_halves(p['mla_kr']))])
        dsa_q = p['dsa_q'] * (DSA_DH ** -0.5 * LOG2E)
        w_dsa = _group([idx_q, dsa_q, p['dsa_k'], p['dsa_v'], pad128(p['idx_k']), pad128(p['idx_w'])])
        w_gla = _group([p['gla_v'], p['gla_r'], p['gla_q'], p['gla_k'], pad128(p['gla_a'])])
        w_rwkv = _group([p['rwkv_r'], p['rwkv_k'], p['rwkv_v'], p['rwkv_w'], p['rwkv_a'], pad128(p['rwkv_g'])])
        lat = _matmul(xb, w_lat, F32)
        pb = _matmul(xb, w_dsa, BF16)
        pc = _matmul(xb, w_gla, F32)
        pd = _matmul(xb, w_rwkv, F32)

        y_mla = _mla(lat, ct, st, mla_q_norm[l], mla_w_uq[l], mla_kv_norm[l], mla_w_ukv[l], batch, seq)
        y_dsa = _dsa(pb, batch, seq, wdt)
        y_gla = _gla(pc, gla_w_gate2[l], gla_b_gate[l], gla_norm_g[l], batch, seq)
        vres = None if l == 0 else (rwkv_v0[l - 1], rwkv_v1[l - 1], rwkv_v2[l - 1])
        y_rwkv, v_first = _rwkv(pd, rwkv_mu_rkv[l], rwkv_mu_lora[l], rwkv_w0[l], rwkv_w2[l], rwkv_a0[l], rwkv_a2[l],
                                rwkv_g2[l], rwkv_k_k[l], rwkv_k_a[l], rwkv_r_k[l], rwkv_lnx_g[l], rwkv_lnx_b[l],
                                v_first, vres, batch, seq)

        merged = _merge(xb, (y_mla, y_dsa, y_gla, y_rwkv), p['gate'].astype(BF16), w_branch[l].astype(BF16))
        xf, xb, xpk = _layer_norm(xf, (_matmul(merged, w_out[l].astype(BF16), F32),), ln_mix_g[l], ln_mix_b[l], alpha,
                                  packed=True)

        comb, ids, wts, rank, counts = _router(xb, router_w[l], router_bias[l])
        n_exp = exp_w_gate.shape[1]
        f_shared = _experts(xb, comb, sh_w_gate[l][None], sh_w_up[l][None], sh_w_down[l][None], 1, n_exp)
        xf, xb = _routed_ln(xf, xpk, ids, wts, rank, counts, exp_w_gate[l], exp_w_up[l], exp_w_down[l],
                            f_shared, ln_ffn_g[l], ln_ffn_b[l], alpha)
    return xf.reshape(batch, seq, d)
```

```python
import functools
import math

import jax
import jax.numpy as jnp
from jax import lax
from jax.experimental import pallas as pl
from jax.experimental.pallas import tpu as pltpu
from jax.experimental.pallas import tpu_sc as plsc

F32 = jnp.float32
BF16 = jnp.bfloat16

LANES = 128

MLA_NOPE, MLA_ROPE, MLA_V = 128, 64, 128
MLA_Q_RANK, MLA_KV_RANK = 768, 256
ROPE_THETA = 10000.0
DSA_DH = 128
IDX_HEADS, IDX_DIM, IDX_TOPK_MAX = 16, 64, 256
GLA_DV, GLA_DK, GLA_GATE_RANK, GLA_TAU, GLA_CHUNK = 256, 128, 16, 16.0, 64
GLA_SUB = 16
RWKV_HS, RWKV_LORA, RWKV_GN_EPS = 64, 64, 64e-5
RWKV_CHUNK = 64
TOP_K, ROUTED_SCALE = 8, 2.5
LN_EPS, RMS_EPS = 1e-5, 1e-6
NEG_BIG = -1e30
LOG2E = math.log2(math.e)

VMEM_LIMIT = 56 * 1024 * 1024


def _cparams(*sem):
    return pltpu.CompilerParams(dimension_semantics=sem, vmem_limit_bytes=VMEM_LIMIT)


def _tile(n, pref, unit=LANES):
    if n <= pref:
        return n
    t = (pref // unit) * unit
    while t > unit and n % t:
        t -= unit
    assert n % t == 0, (n, pref, unit)
    return t


def _dot(a, b):
    return jnp.dot(a, b, preferred_element_type=F32)


def _dot_t(a, b):
    return lax.dot_general(a, b, (((1,), (1,)), ((), ())), preferred_element_type=F32)


def _dot_f32(a, b):
    return jnp.dot(a, b, preferred_element_type=F32, precision=lax.Precision.HIGHEST)


def _masked_sums(mask, x):
    w = x.shape[1]
    hi = x.astype(BF16)
    rest = x - hi.astype(F32)
    mid = rest.astype(BF16)
    lo = (rest - mid.astype(F32)).astype(BF16)
    s = _dot(mask, jnp.concatenate([hi, mid, lo], axis=1))
    return s[:, :w] + s[:, w:2 * w] + s[:, 2 * w:]


def _mm_kernel(a_ref, b_ref, o_ref):
    o_ref[...] = _dot(a_ref[...], b_ref[...]).astype(o_ref.dtype)


def _matmul(a, b, out_dtype, tm=1024, tn=512):
    m, k = a.shape
    n = b.shape[1]
    tm, tn = _tile(m, tm, 8), _tile(n, tn)
    return pl.pallas_call(
        _mm_kernel,
        grid=(m // tm, n // tn),
        in_specs=[pl.BlockSpec((tm, k), lambda i, j: (i, 0)), pl.BlockSpec((k, tn), lambda i, j: (0, j))],
        out_specs=pl.BlockSpec((tm, tn), lambda i, j: (i, j)),
        out_shape=jax.ShapeDtypeStruct((m, n), out_dtype),
        compiler_params=_cparams("parallel", "parallel"),
        name="matmul",
    )(a, b)


def _pack_pairs(v):
    half = v.shape[1] // 2
    bits = lax.bitcast_convert_type(v.astype(BF16).astype(F32), jnp.int32)
    return (bits[:, half:] & jnp.int32(-65536)) | lax.shift_right_logical(bits[:, :half], 16)


def _unpack_pairs(w):
    return (lax.bitcast_convert_type(lax.shift_left(w, 16), F32),
            lax.bitcast_convert_type(w & jnp.int32(-65536), F32))


def _ln_kernel(*refs, alpha, n_res, packed):
    x_ref, res_refs = refs[0], refs[1:1 + n_res]
    g_ref, b_ref, o_ref, ob_ref = refs[1 + n_res:5 + n_res]
    x = x_ref[...]
    if n_res:
        x = alpha * x
        for f_ref in res_refs:
            x = x + f_ref[...]
    mu = jnp.mean(x, axis=-1, keepdims=True)
    xc = x - mu
    var = jnp.mean(xc * xc, axis=-1, keepdims=True)
    y = xc * lax.rsqrt(var + LN_EPS) * g_ref[...] + b_ref[...]
    o_ref[...] = y
    ob_ref[...] = y.astype(BF16)
    if packed:
        refs[-1][...] = _pack_pairs(y)


def _layer_norm(x, res, g, b, alpha=1.0, packed=False):
    t, d = x.shape
    tm = _tile(t, 128, 8)
    row = pl.BlockSpec((tm, d), lambda i: (i, 0))
    vec = pl.BlockSpec((1, d), lambda i: (0, 0))
    args = (x,) + tuple(res)
    extra_spec = [pl.BlockSpec((tm, d // 2), lambda i: (i, 0))] if packed else []
    extra_shape = [jax.ShapeDtypeStruct((t, d // 2), jnp.int32)] if packed else []
    return pl.pallas_call(
        functools.partial(_ln_kernel, alpha=alpha, n_res=len(res), packed=packed),
        grid=(t // tm,),
        in_specs=[row] * len(args) + [vec, vec],
        out_specs=[row, row] + extra_spec,
        out_shape=[jax.ShapeDtypeStruct((t, d), F32), jax.ShapeDtypeStruct((t, d), BF16)] + extra_shape,
        compiler_params=_cparams("parallel"),
        name="layer_norm",
    )(*args, g.reshape(1, d), b.reshape(1, d))


def _pad_cols(w, width):
    return jnp.pad(w, ((0, 0), (0, width - w.shape[1])))


def _rms(x, g):
    return x * lax.rsqrt(jnp.mean(x * x, axis=-1, keepdims=True) + RMS_EPS) * g


def _mla_prep_kernel(lat_ref, ct_ref, st_ref, qg_ref, kg_ref, wqn_ref, wqr_ref, wqs_ref, wkn_ref, wv_ref,
                     qn_ref, qr_ref, kn_ref, kr_ref, v_ref, *, heads, scale):
    lat = lat_ref[...]
    cq = lat[:, :MLA_Q_RANK]
    ckv = lat[:, MLA_Q_RANK:MLA_Q_RANK + MLA_KV_RANK]
    kr = lat[:, MLA_Q_RANK + MLA_KV_RANK:MLA_Q_RANK + MLA_KV_RANK + LANES]
    krs = lat[:, MLA_Q_RANK + MLA_KV_RANK + LANES:MLA_Q_RANK + MLA_KV_RANK + 2 * LANES]
    ct, st = ct_ref[...], st_ref[...]
    nq = _rms(cq, qg_ref[...]).astype(BF16)
    nkv = _rms(ckv, kg_ref[...]).astype(BF16)
    qn_ref[...] = (_dot(nq, wqn_ref[...]) * scale).astype(BF16)
    cth = jnp.concatenate([ct] * heads, axis=1)
    sth = jnp.concatenate([st] * heads, axis=1)
    qr_ref[...] = ((_dot(nq, wqr_ref[...]) * cth + _dot(nq, wqs_ref[...]) * sth) * scale).astype(BF16)
    kn_ref[...] = _dot(nkv, wkn_ref[...]).astype(BF16)
    v_ref[...] = _dot(nkv, wv_ref[...]).astype(BF16)
    kr_ref[...] = (kr * ct + krs * st).astype(BF16)


def _flash_kernel(qn_ref, qr_ref, kn_ref, kr_ref, v_ref, o_ref, m_ref, l_ref, acc_ref, *, tq, tk):
    i, j = pl.program_id(2), pl.program_id(3)

    @pl.when(j == 0)
    def _():
        m_ref[...] = jnp.full_like(m_ref, NEG_BIG)
        l_ref[...] = jnp.zeros_like(l_ref)
        acc_ref[...] = jnp.zeros_like(acc_ref)

    def step(masked):
        q = jnp.concatenate([qn_ref[...], qr_ref[...]], axis=1)
        k = jnp.concatenate([kn_ref[...], kr_ref[...]], axis=1)
        s = _dot_t(q, k)
        if masked:
            qpos = i * tq + lax.broadcasted_iota(jnp.int32, (tq, tk), 0)
            kpos = j * tk + lax.broadcasted_iota(jnp.int32, (tq, tk), 1)
            s = jnp.where(kpos <= qpos, s, -jnp.inf)
        m_prev = m_ref[...]
        m_new = jnp.maximum(m_prev, jnp.max(s, axis=1, keepdims=True))
        alpha = jnp.exp2(m_prev - m_new)
        p = jnp.exp2(s - jnp.concatenate([m_new] * (tk // LANES), axis=1))
        l_ref[...] = alpha * l_ref[...] + jnp.sum(p, axis=1, keepdims=True)
        acc_ref[...] = alpha * acc_ref[...] + _dot(p.astype(BF16), v_ref[...])
        m_ref[...] = m_new

    below = j * tk + tk - 1 <= i * tq
    touches = j * tk <= i * tq + tq - 1

    @pl.when(below)
    def _():
        step(False)

    @pl.when(jnp.logical_and(touches, jnp.logical_not(below)))
    def _():
        step(True)

    @pl.when(j == pl.num_programs(3) - 1)
    def _():
        o_ref[...] = (acc_ref[...] / l_ref[...]).astype(o_ref.dtype)


def _rope_tables(positions):
    half = MLA_ROPE // 2
    inv = ROPE_THETA ** (-jnp.arange(0, MLA_ROPE, 2, dtype=F32) / MLA_ROPE)
    ang = positions.reshape(-1).astype(F32)[:, None] * inv
    cos, sin = jnp.cos(ang), jnp.sin(ang)
    zero = jnp.zeros((ang.shape[0], LANES - 2 * half), F32)
    return jnp.concatenate([cos, cos, zero], 1), jnp.concatenate([-sin, sin, zero], 1)


def _swap_halves(w):
    half = w.shape[-1] // 2
    return jnp.concatenate([w[..., half:], w[..., :half]], -1)


def _mla(lat, ct, st, q_norm, w_uq, kv_norm, w_ukv, batch, seq):
    t = lat.shape[0]
    heads = w_uq.shape[1] // (MLA_NOPE + MLA_ROPE)
    hw = heads * LANES
    wq = w_uq.reshape(MLA_Q_RANK, heads, MLA_NOPE + MLA_ROPE)
    wqn = wq[:, :, :MLA_NOPE].reshape(MLA_Q_RANK, hw).astype(BF16)
    rope_pad = ((0, 0), (0, 0), (0, LANES - MLA_ROPE))
    wqr = jnp.pad(wq[:, :, MLA_NOPE:], rope_pad).reshape(MLA_Q_RANK, hw).astype(BF16)
    wqs = jnp.pad(_swap_halves(wq[:, :, MLA_NOPE:]), rope_pad).reshape(MLA_Q_RANK, hw).astype(BF16)
    wkv = w_ukv.reshape(MLA_KV_RANK, heads, MLA_NOPE + MLA_V)
    wkn = wkv[:, :, :MLA_NOPE].reshape(MLA_KV_RANK, hw).astype(BF16)
    wv = wkv[:, :, MLA_NOPE:].reshape(MLA_KV_RANK, hw).astype(BF16)

    tm = _tile(t, 512, 8)
    row = lambda w: pl.BlockSpec((tm, w), lambda i: (i, 0))
    full = lambda a: pl.BlockSpec(a.shape, lambda i: (0, 0))
    qg, kg = q_norm.reshape(1, -1), kv_norm.reshape(1, -1)
    qn, qr, kn, kr, v = pl.pallas_call(
        functools.partial(_mla_prep_kernel, heads=heads, scale=(MLA_NOPE + MLA_ROPE) ** -0.5 * LOG2E),
        grid=(t // tm,),
        in_specs=[row(lat.shape[1]), row(LANES), row(LANES), full(qg), full(kg),
                  full(wqn), full(wqr), full(wqs), full(wkn), full(wv)],
        out_specs=[row(hw), row(hw), row(hw), row(LANES), row(hw)],
        out_shape=[jax.ShapeDtypeStruct((t, hw), BF16)] * 3 + [jax.ShapeDtypeStruct((t, LANES), BF16),
                                                               jax.ShapeDtypeStruct((t, hw), BF16)],
        compiler_params=_cparams("parallel"),
        name="mla_prep",
    )(lat, ct, st, qg, kg, wqn, wqr, wqs, wkn, wv)

    tq = tk = _tile(seq, 1024, 8)
    nq, nk = seq // tq, seq // tk
    r3 = lambda a: a.reshape(batch, seq, a.shape[1])
    last = lambda i, j: jnp.minimum(j, ((i + 1) * tq - 1) // tk)
    q_spec = pl.BlockSpec((None, tq, LANES), lambda b, h, i, j: (b, i, h))
    k_spec = pl.BlockSpec((None, tk, LANES), lambda b, h, i, j: (b, last(i, j), h))
    kr_spec = pl.BlockSpec((None, tk, LANES), lambda b, h, i, j: (b, last(i, j), 0))
    out = pl.pallas_call(
        functools.partial(_flash_kernel, tq=tq, tk=tk),
        grid=(batch, heads, nq, nk),
        in_specs=[q_spec, q_spec, k_spec, kr_spec, k_spec],
        out_specs=q_spec,
        out_shape=jax.ShapeDtypeStruct((batch, seq, hw), BF16),
        scratch_shapes=[pltpu.VMEM((tq, LANES), F32), pltpu.VMEM((tq, LANES), F32), pltpu.VMEM((tq, LANES), F32)],
        compiler_params=_cparams("parallel", "parallel", "parallel", "arbitrary"),
        name="mla_flash",
    )(r3(qn), r3(qr), r3(kn), r3(kr), r3(v))
    return out.reshape(t, hw)


DSA_TQ = 256
DSA_TK = 512
INT_MIN = -2 ** 31


def _dsa_kernel(iq_ref, q_ref, k_ref, v_ref, ik_ref, iw_ref, o_ref, key_ref, m_ref, l_ref, acc_ref,
                *, tq, tk, n_sel, heads, pos_bits):
    i = pl.program_id(1)
    n_kt = (i * tq + tq - 1) // tk + 1
    reps = tk // LANES
    wide = lambda a: jnp.concatenate([a] * reps, axis=1)
    qpos = i * tq + lax.broadcasted_iota(jnp.int32, (tq, tk), 0)
    lane_pos = lax.broadcasted_iota(jnp.int32, (tq, tk), 1)

    iq = iq_ref[...]
    iw = iw_ref[...].astype(F32) * (IDX_HEADS ** -0.5 * IDX_DIM ** -0.5)
    iw_b = [jnp.broadcast_to(iw[:, h:h + 1], (tq, LANES)) for h in range(IDX_HEADS)]

    def score_tile(j, carry):
        ikj = ik_ref[pl.ds(pl.multiple_of(j * tk, tk), tk), :]
        sc = jnp.zeros((tq, tk), F32)
        for h in range(IDX_HEADS):
            logit = _dot_t(iq[:, h * LANES:(h + 1) * LANES], ikj)
            sc = sc + wide(iw_b[h]) * jnp.maximum(logit, 0.0)
        sc = jnp.where(j * tk + lane_pos <= qpos, sc + 0.0, -jnp.inf)
        bits = lax.bitcast_convert_type(sc, jnp.int32)
        key_ref[j] = jnp.where(bits >= 0, bits, bits ^ 0x7FFFFFFF)
        return carry

    lax.fori_loop(0, n_kt, score_tile, 0)

    def count(pred):
        def body(j, acc):
            c = pred(key_ref[j], j * tk + lane_pos).astype(jnp.int32)
            for rep in range(reps):
                acc = acc + c[:, rep * LANES:(rep + 1) * LANES]
            return acc
        acc = lax.fori_loop(0, n_kt, body, jnp.zeros((tq, LANES), jnp.int32))
        return jnp.broadcast_to(jnp.sum(acc, axis=1, keepdims=True), (tq, LANES))

    def thr_bit(bit_i, thr):
        cand = thr ^ jnp.left_shift(jnp.int32(1), 31 - bit_i)
        cnt = count(lambda key, pos: key >= wide(cand))
        return jnp.where(cnt >= n_sel, cand, thr)

    thr = lax.fori_loop(0, 32, thr_bit, jnp.full((tq, LANES), INT_MIN, jnp.int32))
    cnt_gt = count(lambda key, pos: key > wide(thr))
    cnt_ge = count(lambda key, pos: key >= wide(thr))
    need = n_sel - cnt_gt

    def tie_bit(bit_i, cut):
        cand = cut | jnp.left_shift(jnp.int32(1), pos_bits - 1 - bit_i)
        cnt = count(lambda key, pos: jnp.logical_and(key == wide(thr), pos < wide(cand)))
        return jnp.where(cnt < need, cand, cut)

    surplus = jnp.max(cnt_ge - cnt_gt - need) > 0
    cut = lax.cond(surplus,
                   lambda: lax.fori_loop(0, pos_bits, tie_bit, jnp.zeros((tq, LANES), jnp.int32)),
                   lambda: jnp.full((tq, LANES), 2 ** 31 - 1, jnp.int32))

    q = q_ref[...]
    q_all = jnp.concatenate([q[:, h * LANES:(h + 1) * LANES] for h in range(heads)], axis=0)
    m_ref[...] = jnp.full_like(m_ref, NEG_BIG)
    l_ref[...] = jnp.zeros_like(l_ref)
    acc_ref[...] = jnp.zeros_like(acc_ref)

    def attend(j, carry):
        rows = pl.ds(pl.multiple_of(j * tk, tk), tk)
        s = _dot_t(q_all, k_ref[rows, :])
        key, pos = key_ref[j], j * tk + lane_pos
        keep = jnp.where(key > wide(thr), 0.0,
                         jnp.where(jnp.logical_and(key == wide(thr), pos <= wide(cut)), 0.0, -jnp.inf))
        keep = jnp.where(pos <= qpos, keep, -jnp.inf)
        s = s + jnp.concatenate([keep] * heads, axis=0)
        m_prev = m_ref[...]
        m_new = jnp.maximum(m_prev, jnp.max(s, axis=1, keepdims=True))
        alpha = jnp.exp2(m_prev - m_new)
        p = jnp.exp2(s - jnp.concatenate([m_new] * reps, axis=1))
        l_ref[...] = alpha * l_ref[...] + jnp.sum(p, axis=1, keepdims=True)
        acc_ref[...] = alpha * acc_ref[...] + _dot(p.astype(BF16), v_ref[rows, :])
        m_ref[...] = m_new
        return carry

    lax.fori_loop(0, n_kt, attend, 0)
    out = acc_ref[...] / l_ref[...]
    o_ref[...] = jnp.concatenate([out[h * tq:(h + 1) * tq] for h in range(heads)], axis=1).astype(o_ref.dtype)


def _dsa(pb, batch, seq, width):
    t, wb = pb.shape
    heads = width // DSA_DH
    iqw = IDX_HEADS * LANES
    tq, tk = _tile(seq, DSA_TQ, 8), _tile(seq, DSA_TK)
    n_sel = min(IDX_TOPK_MAX, seq // 4)
    c0 = (iqw + width) // LANES
    pb3 = pb.reshape(batch, seq, wb)
    qblk = lambda w, idx: pl.BlockSpec((None, tq, w), lambda b, i: (b, i, idx))
    seqblk = lambda idx: pl.BlockSpec((None, seq, LANES), lambda b, i: (b, 0, idx))
    out = pl.pallas_call(
        functools.partial(_dsa_kernel, tq=tq, tk=tk, n_sel=n_sel, heads=heads,
                          pos_bits=max(1, (seq - 1).bit_length())),
        grid=(batch, seq // tq),
        in_specs=[qblk(iqw, 0), qblk(width, iqw // width), seqblk(c0), seqblk(c0 + 1), seqblk(c0 + 2),
                  qblk(LANES, c0 + 3)],
        out_specs=qblk(width, 0),
        out_shape=jax.ShapeDtypeStruct((batch, seq, width), BF16),
        scratch_shapes=[pltpu.VMEM((seq // tk, tq, tk), jnp.int32), pltpu.VMEM((heads * tq, LANES), F32),
                        pltpu.VMEM((heads * tq, LANES), F32), pltpu.VMEM((heads * tq, LANES), F32)],
        compiler_params=_cparams("parallel", "arbitrary"),
        name="dsa",
    )(pb3, pb3, pb3, pb3, pb3, pb3)
    return out.reshape(t, width)


def _gla_kernel(q_ref, k_ref, v_ref, a_ref, r_ref, wg_ref, bg_ref, ng_ref, o_ref, state_ref):
    c_len, sub = GLA_CHUNK, GLA_SUB

    @pl.when(pl.program_id(2) == 0)
    def _():
        state_ref[...] = jnp.zeros_like(state_ref)

    z_all = _dot(a_ref[...].astype(BF16), wg_ref[...]) + bg_ref[...]
    g_all = jax.nn.log_sigmoid(z_all) / GLA_TAU
    ri = lax.broadcasted_iota(jnp.int32, (c_len, c_len), 0)
    ci = lax.broadcasted_iota(jnp.int32, (c_len, c_len), 1)
    b_all = _masked_sums((ri >= ci).astype(BF16), g_all)
    ones = jnp.ones((GLA_DK, LANES), BF16)
    trow = lax.broadcasted_iota(jnp.int32, (sub, GLA_DK), 0)
    lane = lax.broadcasted_iota(jnp.int32, (sub, LANES), 1)
    for hh in range(state_ref.shape[0]):
        ksl = slice(hh * GLA_DK, (hh + 1) * GLA_DK)
        vsl = slice(hh * GLA_DV, (hh + 1) * GLA_DV)
        o = _gla_head(q_ref[:, ksl] * GLA_DK ** -0.5, k_ref[:, ksl], v_ref[:, vsl].astype(BF16), b_all[:, ksl],
                      state_ref.at[hh], ones, trow, lane)
        r = r_ref[:, vsl]
        o_ref[:, vsl] = (_rms(o, ng_ref[:, vsl]) * (r * jax.nn.sigmoid(r))).astype(o_ref.dtype)


def _gla_head(q, k, vb, b, state_ref, ones, trow, lane):
    c_len, sub = GLA_CHUNK, GLA_SUB
    state_t = state_ref[...]
    o_inter = _dot_t((q * jnp.exp(b)).astype(BF16), state_t.astype(BF16))
    outs = []
    for blk in range(c_len // sub):
        lo = blk * sub
        qi, bi, ki = q[lo:lo + sub], b[lo:lo + sub], k[lo:lo + sub]
        rows = [qi * ki[s:s + 1] * jnp.exp(jnp.where(trow >= s, bi - bi[s:s + 1], -jnp.inf)) for s in range(sub)]
        prod = jnp.concatenate(rows, axis=0)
        p_hi = prod.astype(BF16)
        p_lo = (prod - p_hi.astype(F32)).astype(BF16)
        sums = _dot(jnp.concatenate([p_hi, p_lo], axis=0), ones)
        sums = sums[:sub * sub] + sums[sub * sub:]
        attn = jnp.zeros((sub, LANES), F32)
        for s in range(sub):
            attn = attn + jnp.where(lane == s, sums[s * sub:(s + 1) * sub], 0.0)
        o_blk = _dot(attn[:, :sub].astype(BF16), vb[lo:lo + sub])
        if blk:
            qa = qi * jnp.exp(bi - bi[0:1])
            ka = k[:lo] * jnp.exp(bi[0:1] - b[:lo])
            o_blk = o_blk + _dot(_dot_t(qa.astype(BF16), ka.astype(BF16)).astype(BF16), vb[:lo])
        outs.append(o_blk)
    o = o_inter + jnp.concatenate(outs, axis=0)

    b_last = b[c_len - 1:c_len]
    k_dec = (k * jnp.exp(b_last - b)).astype(BF16)
    state_ref[...] = state_t * jnp.exp(b_last) + lax.dot_general(
        vb, k_dec, (((0,), (0,)), ((), ())), preferred_element_type=F32)
    return o


GLA_HEADS_PER_STEP = 4


def _gla(pc, w_gate2, b_gate, norm_g, batch, seq):
    t = pc.shape[0]
    heads = norm_g.shape[0] // GLA_DV
    hps = math.gcd(heads, GLA_HEADS_PER_STEP)
    groups = heads // hps
    wg = jnp.pad(w_gate2, ((0, LANES - GLA_GATE_RANK), (0, 0))).astype(BF16)
    nc = seq // GLA_CHUNK
    blk = lambda w, off: pl.BlockSpec((None, GLA_CHUNK, hps * w), lambda b, g, c: (b, c, off + g))
    par = lambda rows, w: pl.BlockSpec((rows, hps * w), lambda b, g, c: (0, g))
    pc3 = pc.reshape(batch, seq, pc.shape[1])
    out = pl.pallas_call(
        _gla_kernel,
        grid=(batch, groups, nc),
        in_specs=[blk(GLA_DK, 4 * groups), blk(GLA_DK, 5 * groups), blk(GLA_DV, 0),
                  pl.BlockSpec((None, GLA_CHUNK, LANES), lambda b, g, c: (b, c, 6 * heads)),
                  blk(GLA_DV, groups), par(LANES, GLA_DK), par(1, GLA_DK), par(1, GLA_DV)],
        out_specs=blk(GLA_DV, 0),
        out_shape=jax.ShapeDtypeStruct((batch, seq, heads * GLA_DV), BF16),
        scratch_shapes=[pltpu.VMEM((hps, GLA_DV, GLA_DK), F32)],
        compiler_params=_cparams("parallel", "parallel", "arbitrary"),
        name="gla",
    )(pc3, pc3, pc3, pc3, pc3, wg, b_gate.reshape(1, -1), norm_g.reshape(1, -1))
    return out.reshape(t, heads * GLA_DV)


_DIMS = {'nn': (((1,), (0,)), ((), ())), 'nt': (((1,), (1,)), ((), ())), 'tn': (((0,), (0,)), ((), ()))}


def _mm(a, b, form):
    return lax.dot_general(a.astype(BF16), b.astype(BF16), _DIMS[form], preferred_element_type=F32)


def _seg_sum(x, width):
    gi = lax.broadcasted_iota(jnp.int32, (LANES, LANES), 0) // width
    gj = lax.broadcasted_iota(jnp.int32, (LANES, LANES), 1) // width
    ones = (gi == gj).astype(F32)
    return jnp.concatenate([_dot_f32(x[:, c:c + LANES], ones) for c in range(0, x.shape[1], LANES)], axis=1)


def _rwkv_prep_kernel(*refs, width, seq, has_vres):
    if has_vres:
        (p_ref, pp_ref, prm_ref, ml_ref, w2_ref, a2_ref, g2_ref, vf_ref, v0_ref, v1_ref, v2_ref,
         r_ref, lw_ref, k_ref, v_ref, kk_ref, bb_ref, bonus_ref, g_ref) = refs
    else:
        (p_ref, pp_ref, prm_ref, ml_ref, w2_ref, a2_ref, g2_ref,
         r_ref, lw_ref, k_ref, v_ref, kk_ref, bb_ref, bonus_ref, g_ref) = refs
    w = width
    p, prm, ml = p_ref[...], prm_ref[...], ml_ref[...]
    tm = p.shape[0]
    above = jnp.where((pl.program_id(0) * tm) % seq == 0, 0.0, pp_ref[7:8, :])
    first = lax.broadcasted_iota(jnp.int32, p.shape, 0) == 0
    pp = jnp.where(first, jnp.broadcast_to(above, p.shape), pltpu.roll(p, 1, axis=0))
    lerp = lambda lo, hi, mu: p[:, lo:hi] + (pp[:, lo:hi] - p[:, lo:hi]) * mu
    r = lerp(0, w, prm[0:1])
    k = lerp(w, 2 * w, prm[1:2])
    v = lerp(2 * w, 3 * w, prm[2:3])
    x_wa = lerp(3 * w, 3 * w + LANES, ml[0:1])
    x_g = lerp(3 * w + LANES, 3 * w + 2 * LANES, ml[1:2])
    w_log = -jax.nn.softplus(-(prm[3:4] + _dot(jnp.tanh(x_wa).astype(BF16), w2_ref[...]))) - 0.5
    lw_ref[...] = -jnp.exp(w_log)
    a = jax.nn.sigmoid(prm[4:5] + _dot(x_wa.astype(BF16), a2_ref[...]))
    g_ref[...] = _dot(jax.nn.sigmoid(x_g).astype(BF16), g2_ref[...])
    if has_vres:
        mix = jax.nn.sigmoid(v0_ref[...] + _dot(_dot(v.astype(BF16), v1_ref[...]).astype(BF16), v2_ref[...]))
        v = v + (vf_ref[...] - v) * mix
    kk = k * prm[5:6]
    kk = kk * lax.rsqrt(jnp.maximum(_seg_sum(kk * kk, RWKV_HS), 1e-24))
    k2 = k * (1.0 + (a - 1.0) * prm[6:7])
    r_ref[...] = r
    k_ref[...] = k2
    v_ref[...] = v
    kk_ref[...] = kk
    bb_ref[...] = kk * a
    bonus_ref[...] = _seg_sum(r * k2 * prm[7:8], RWKV_HS) * v


RWKV_HEADS_PER_STEP = 8


def _rwkv_chunk_kernel(r_ref, lw_ref, k_ref, v_ref, kk_ref, bb_ref, y_ref, s_ref, *, n_chunk):
    c_len, hs = RWKV_CHUNK, RWKV_HS
    rows_n = n_chunk * c_len

    @pl.when(pl.program_id(2) == 0)
    def _():
        s_ref[...] = jnp.zeros_like(s_ref)

    ri = lax.broadcasted_iota(jnp.int32, (rows_n, rows_n), 0)
    ci = lax.broadcasted_iota(jnp.int32, (rows_n, rows_n), 1)
    same = (ri // c_len) == (ci // c_len)
    incl = jnp.logical_and(same, ci <= ri)
    strict = jnp.logical_and(same, ci < ri)
    eye = (ri == ci).astype(F32)
    e_r = lax.broadcasted_iota(jnp.int32, (hs, hs), 0)
    e_c = lax.broadcasted_iota(jnp.int32, (hs, hs), 1)
    sums = _masked_sums(jnp.concatenate([incl, same], axis=0).astype(BF16), lw_ref[...])
    cum_all, cum_c_all = sums[:rows_n], sums[rows_n:]
    heads = range(s_ref.shape[0])
    hd = []
    for hh in heads:
        sl = slice(hh * hs, (hh + 1) * hs)
        r, lw, k, v, kk, bb = (ref[...][:, sl] for ref in (r_ref, lw_ref, k_ref, v_ref, kk_ref, bb_ref))
        cum, cum_c = cum_all[:, sl], cum_c_all[:, sl]
        g_inv, g_end = jnp.exp(-cum), jnp.exp(cum_c - cum)
        kap, rt = kk * jnp.exp(cum - lw), r * jnp.exp(cum)
        bet, kt = bb * g_inv, k * g_inv
        gram = _mm(jnp.concatenate([kap, rt], axis=0), jnp.concatenate([bet, kt], axis=0), 'nt')
        hd.append(dict(v=v, kap=kap, rt=rt, bet_c=bb * g_end, kt_c=k * g_end, cum_c=cum_c,
                       n_m=jnp.where(strict, gram[:rows_n, :rows_n], 0.0),
                       a_kk=jnp.where(strict, gram[:rows_n, rows_n:], 0.0),
                       a_rb=jnp.where(incl, gram[rows_n:, :rows_n], 0.0),
                       a_rk=jnp.where(incl, gram[rows_n:, rows_n:], 0.0)))
    t_inv = [eye - h['n_m'] for h in hd]
    pw = [-h['n_m'] for h in hd]
    for _ in range(int(math.log2(c_len)) - 1):
        pw = [_mm(p, p, 'nn') for p in pw]
        t_inv = [t + _mm(t, p, 'nn') for t, p in zip(t_inv, pw)]
    av = [_mm(jnp.concatenate([h['a_kk'], h['a_rk']], axis=0), h['v'], 'nn') for h in hd]
    z = [_mm(t, jnp.concatenate([h['kap'], a[:rows_n]], axis=1), 'nn') for t, h, a in zip(t_inv, hd, av)]
    az = [_mm(h['a_rb'], zz, 'nn') for h, zz in zip(hd, z)]
    r_p = [h['rt'] - a[:, :hs] for h, a in zip(hd, az)]
    y0 = [a[rows_n:] - b[:, hs:] for a, b in zip(av, az)]
    s = [s_ref[hh] for hh in heads]
    outs = [[] for _ in heads]
    for c in range(n_chunk):
        rows = slice(c * c_len, (c + 1) * c_len)
        for hh in heads:
            h = hd[hh]
            outs[hh].append(_mm(r_p[hh][rows], s[hh], 'nt') + y0[hh][rows])
            zb = _mm(z[hh][rows], h['bet_c'][rows], 'tn')
            g_chunk = jnp.exp(h['cum_c'][c * c_len:c * c_len + 1])
            m = jnp.where(e_r == e_c, jnp.broadcast_to(g_chunk, (hs, hs)), 0.0) - zb[:hs]
            s[hh] = _mm(s[hh], m, 'nn') + _mm(h['v'][rows], h['kt_c'][rows], 'tn') - zb[hs:]
    for hh in heads:
        s_ref[hh] = s[hh]
    y_ref[...] = jnp.concatenate([jnp.concatenate(o, axis=0) for o in outs], axis=1)


def _rwkv_post_kernel(y_ref, bonus_ref, g_ref, lg_ref, lb_ref, o_ref):
    y = y_ref[...]
    mu = _seg_sum(y, RWKV_HS) * (1.0 / RWKV_HS)
    yc = y - mu
    var = _seg_sum(yc * yc, RWKV_HS) * (1.0 / RWKV_HS)
    yn = yc * lax.rsqrt(var + RWKV_GN_EPS) * lg_ref[...] + lb_ref[...]
    o_ref[...] = ((yn + bonus_ref[...]) * g_ref[...]).astype(o_ref.dtype)


def _rwkv(pd, mu_rkv, mu_lora, w0, w2, a0, a2, g2, k_k, k_a, r_k, lnx_g, lnx_b, v_first, vres, batch, seq):
    t, wd = pd.shape
    w = w0.shape[0]
    prm = jnp.stack([mu_rkv[0], mu_rkv[1], mu_rkv[2], w0, a0, k_k, k_a, r_k.reshape(-1)])
    zl = jnp.zeros((RWKV_LORA,), F32)
    ml = jnp.stack([jnp.concatenate([mu_lora[0], mu_lora[1]]), jnp.concatenate([mu_lora[2], zl])])
    zw = jnp.zeros((LANES - RWKV_LORA, w), F32)
    w2p = jnp.concatenate([w2, zw]).astype(BF16)
    a2p = jnp.concatenate([zw, a2]).astype(BF16)
    g2p = jnp.concatenate([g2, zw]).astype(BF16)
    has_vres = vres is not None
    tm = _tile(seq, 256, 8)
    row = lambda width: pl.BlockSpec((tm, width), lambda i: (i, 0))
    full = lambda a: pl.BlockSpec(a.shape, lambda i: (0, 0))
    above = pl.BlockSpec((8, wd), lambda i: (jnp.maximum(i * (tm // 8) - 1, 0), 0))
    args = [pd, pd, prm, ml, w2p, a2p, g2p]
    specs = [row(wd), above, full(prm), full(ml), full(w2p), full(a2p), full(g2p)]
    if has_vres:
        v0, v1, v2 = vres
        v1p = _pad_cols(v1, LANES).astype(BF16)
        v2p = jnp.pad(v2, ((0, LANES - v2.shape[0]), (0, 0))).astype(BF16)
        v0r = v0.reshape(1, w)
        args += [v_first, v0r, v1p, v2p]
        specs += [row(w), full(v0r), full(v1p), full(v2p)]
    r, lw, k2, v, kk, bb, bonus, g = pl.pallas_call(
        functools.partial(_rwkv_prep_kernel, width=w, seq=seq, has_vres=has_vres),
        grid=(t // tm,),
        in_specs=specs,
        out_specs=[row(w)] * 8,
        out_shape=[jax.ShapeDtypeStruct((t, w), F32)] * 8,
        compiler_params=_cparams("parallel"),
        name="rwkv_prep",
    )(*args)

    n_chunk = 4 if seq % (4 * RWKV_CHUNK) == 0 else 1
    rows_n = n_chunk * RWKV_CHUNK
    bw = math.gcd(w, RWKV_HEADS_PER_STEP * RWKV_HS)
    blk = pl.BlockSpec((None, rows_n, bw), lambda b, h, c: (b, c, h))
    r3 = lambda a: a.reshape(batch, seq, w)
    y = pl.pallas_call(
        functools.partial(_rwkv_chunk_kernel, n_chunk=n_chunk),
        grid=(batch, w // bw, seq // rows_n),
        in_specs=[blk] * 6,
        out_specs=blk,
        out_shape=jax.ShapeDtypeStruct((batch, seq, w), F32),
        scratch_shapes=[pltpu.VMEM((bw // RWKV_HS, RWKV_HS, RWKV_HS), F32)],
        compiler_params=_cparams("parallel", "parallel", "arbitrary"),
        name="rwkv_chunk",
    )(r3(r), r3(lw), r3(k2), r3(v), r3(kk), r3(bb))

    lg, lb = lnx_g.reshape(1, w), lnx_b.reshape(1, w)
    out = pl.pallas_call(
        _rwkv_post_kernel,
        grid=(t // tm,),
        in_specs=[row(w), row(w), row(w), full(lg), full(lb)],
        out_specs=row(w),
        out_shape=jax.ShapeDtypeStruct((t, w), BF16),
        compiler_params=_cparams("parallel"),
        name="rwkv_post",
    )(y.reshape(t, w), bonus, g, lg, lb)
    return out, (v_first if has_vres else v)


N_BRANCH = 4


def _merge_kernel(*refs):
    x_ref = refs[0]
    y_refs = refs[1:1 + N_BRANCH]
    wg_refs = refs[1 + N_BRANCH:1 + 2 * N_BRANCH]
    wb_refs = refs[1 + 2 * N_BRANCH:1 + 3 * N_BRANCH]
    o_ref = refs[-1]
    x = x_ref[...]
    acc = None
    for y_ref, wg_ref, wb_ref in zip(y_refs, wg_refs, wb_refs):
        term = jax.nn.sigmoid(_dot(x, wg_ref[...])) * _dot(y_ref[...], wb_ref[...])
        acc = term if acc is None else acc + term
    o_ref[...] = acc.astype(o_ref.dtype)


def _merge(xb, ys, w_gate, w_branch):
    t, d = xb.shape
    w = ys[0].shape[1]
    tm, tn = _tile(t, 512, 8), _tile(d, 256)
    nj = d // tn
    gate_spec = lambda i: pl.BlockSpec((d, tn), lambda r, j: (0, i * nj + j))
    br_spec = lambda i: pl.BlockSpec((None, w, tn), lambda r, j: (i, 0, j))
    return pl.pallas_call(
        _merge_kernel,
        grid=(t // tm, nj),
        in_specs=[pl.BlockSpec((tm, d), lambda r, j: (r, 0))] + [pl.BlockSpec((tm, w), lambda r, j: (r, 0))] * N_BRANCH
        + [gate_spec(i) for i in range(N_BRANCH)] + [br_spec(i) for i in range(N_BRANCH)],
        out_specs=pl.BlockSpec((tm, tn), lambda r, j: (r, j)),
        out_shape=jax.ShapeDtypeStruct((t, d), BF16),
        compiler_params=_cparams("parallel", "parallel"),
        name="merge",
    )(xb, *ys, *([w_gate] * N_BRANCH), *([w_branch] * N_BRANCH))


def _router_kernel(x_ref, w_ref, b_ref, comb_ref, ids_ref, wts_ref, rank_ref, cnt_ref, run_ref, *, n_exp):
    tm = x_ref.shape[0]

    @pl.when(pl.program_id(0) == 0)
    def _():
        run_ref[...] = jnp.zeros_like(run_ref)

    scores = jax.nn.sigmoid(_dot(x_ref[...], w_ref[...]))
    lane = lax.broadcasted_iota(jnp.int32, (tm, LANES), 1)
    work = jnp.where(lane < n_exp, scores + b_ref[...], -jnp.inf)
    chosen = jnp.zeros((tm, LANES), F32)
    sel = jnp.zeros((tm, LANES), F32)
    ids = jnp.zeros((tm, LANES), jnp.int32)
    wts = jnp.zeros((tm, LANES), F32)
    for it in range(TOP_K):
        best = jnp.max(work, axis=1, keepdims=True)
        first = jnp.min(jnp.where(work == best, lane, LANES), axis=1, keepdims=True)
        hit = lane == first
        chosen = jnp.where(hit, scores, chosen)
        sel = jnp.where(hit, 1.0, sel)
        ids = jnp.where(lane == it, first, ids)
        wts = jnp.where(lane == it, jnp.sum(jnp.where(hit, scores, 0.0), axis=1, keepdims=True), wts)
        work = jnp.where(hit, -jnp.inf, work)
    norm = ROUTED_SCALE / jnp.sum(chosen, axis=1, keepdims=True)
    comb_ref[...] = jnp.where(lane == n_exp, 1.0, chosen * norm)
    ids_ref[...] = ids
    wts_ref[...] = wts * norm
    ri = lax.broadcasted_iota(jnp.int32, (tm, tm), 0)
    ci = lax.broadcasted_iota(jnp.int32, (tm, tm), 1)
    before = _dot((ci < ri).astype(BF16), sel.astype(BF16))
    run = run_ref[0:1, :]
    rank_ref[...] = (before + run).astype(jnp.int32)
    run = run + jnp.sum(sel, axis=0, keepdims=True)
    run_ref[...] = jnp.broadcast_to(run, run_ref.shape)
    cnt_ref[...] = jnp.broadcast_to(run, cnt_ref.shape).astype(jnp.int32)


def _router(xb, router_w, router_bias):
    t, d = xb.shape
    n_exp = router_w.shape[1]
    assert n_exp < LANES
    tm = _tile(t, 512, 8)
    wr = _pad_cols(router_w, LANES).astype(BF16)
    br = _pad_cols(router_bias.reshape(1, n_exp), LANES)
    tok = pl.BlockSpec((tm, LANES), lambda i: (i, 0))
    tok_shape = lambda dt: jax.ShapeDtypeStruct((t, LANES), dt)
    return pl.pallas_call(
        functools.partial(_router_kernel, n_exp=n_exp),
        grid=(t // tm,),
        in_specs=[pl.BlockSpec((tm, d), lambda i: (i, 0)), pl.BlockSpec((d, LANES), lambda i: (0, 0)),
                  pl.BlockSpec((1, LANES), lambda i: (0, 0))],
        out_specs=[tok, tok, tok, tok, pl.BlockSpec((8, LANES), lambda i: (0, 0))],
        out_shape=[tok_shape(F32), tok_shape(jnp.int32), tok_shape(F32), tok_shape(jnp.int32),
                   jax.ShapeDtypeStruct((8, LANES), jnp.int32)],
        scratch_shapes=[pltpu.VMEM((8, LANES), F32)],
        compiler_params=_cparams("arbitrary"),
        name="router",
    )(xb, wr, br)


MOE_COLS = 1024


MOE_GROUP = 2


def _moe_kernel(x_ref, c_ref, wgu_ref, wd_ref, o_ref, *, hid, group, lane0):
    g = pl.program_id(1)
    tm, d = o_ref.shape

    @pl.when(g == 0)
    def _():
        o_ref[...] = jnp.zeros_like(o_ref)

    x = x_ref[...]
    comb = c_ref[...]
    lane = lax.broadcasted_iota(jnp.int32, (tm, LANES), 1)
    hs = []
    for u in range(group):
        c = jnp.sum(jnp.where(lane == lane0 + g * group + u, comb, 0.0), axis=1, keepdims=True)
        gu = _dot(x, wgu_ref[u])
        gate = gu[:, :hid]
        hs.append((gate * jax.nn.sigmoid(gate) * gu[:, hid:] * c).astype(BF16))
    h = jnp.concatenate(hs, axis=1)
    cols = _tile(d, MOE_COLS)
    for lo in range(0, d, cols):
        o_ref[:, lo:lo + cols] += _dot(h, wd_ref[:, lo:lo + cols])


def _experts(xb, comb, w_gate, w_up, w_down, group, lane0):
    t, d = xb.shape
    n_e, _, hid = w_gate.shape
    assert n_e % group == 0
    wgu = jnp.concatenate([w_gate, w_up], axis=2).astype(BF16)
    wd = w_down.astype(BF16).reshape(n_e // group, group * hid, d)
    tm = _tile(t, 512, 8)
    mode = dict(pipeline_mode=pl.Buffered(1)) if n_e > group else {}
    tok = lambda width: pl.BlockSpec((tm, width), lambda i, e: (i, 0), **mode)
    return pl.pallas_call(
        functools.partial(_moe_kernel, hid=hid, group=group, lane0=lane0),
        grid=(t // tm, n_e // group),
        in_specs=[tok(d), tok(LANES), pl.BlockSpec((group, d, 2 * hid), lambda i, e: (e, 0, 0)),
                  pl.BlockSpec((None, group * hid, d), lambda i, e: (e, 0, 0))],
        out_specs=tok(d),
        out_shape=jax.ShapeDtypeStruct((t, d), F32),
        compiler_params=_cparams("parallel", "arbitrary"),
        name="moe",
    )(xb, comb, wgu, wd)


MOE_ROWS = 512
SC_WINDOW = 16


def _sc_mesh():
    return plsc.VectorSubcoreMesh(core_axis_name="core", subcore_axis_name="subcore")


def _sc_windows(n_items):
    mesh = _sc_mesh()
    workers = mesh.num_cores * mesh.num_subcores
    n_win = n_items // LANES
    assert n_items % LANES == 0
    return mesh, n_win, -(-n_win // workers)


def _sc_worker(mesh):
    return lax.axis_index("core") * mesh.num_subcores + lax.axis_index("subcore")


def _sc_scatter_rows(x, pos, n_rows):
    t, d = x.shape
    k = pos.shape[0]
    mesh, n_win, per = _sc_windows(t)

    @pl.kernel(out_type=jax.ShapeDtypeStruct((n_rows, d), x.dtype), mesh=mesh,
               scratch_types=[pltpu.VMEM((k, LANES), jnp.int32), pltpu.VMEM((SC_WINDOW, d), x.dtype)])
    def scatter(x_hbm, pos_hbm, o_hbm, idx_vmem, buf):
        first = _sc_worker(mesh) * per

        @pl.loop(0, per)
        def _(step):
            win = first + step

            @pl.when(win < n_win)
            def _():
                base = win * LANES
                pltpu.sync_copy(pos_hbm.at[:, pl.ds(base, LANES)], idx_vmem)
                for j in range(LANES // SC_WINDOW):
                    pltpu.sync_copy(x_hbm.at[pl.ds(base + j * SC_WINDOW, SC_WINDOW)], buf)
                    for kk in range(k):
                        pltpu.sync_copy(buf, o_hbm.at[idx_vmem[kk, pl.ds(j * SC_WINDOW, SC_WINDOW)]])

    return scatter(x, pos)


def _sc_gather_rows(y, idx):
    m = idx.shape[1]
    d = y.shape[1]
    mesh, n_win, per = _sc_windows(m)

    @pl.kernel(out_type=jax.ShapeDtypeStruct((m, d), y.dtype), mesh=mesh,
               scratch_types=[pltpu.VMEM((1, LANES), jnp.int32), pltpu.VMEM((SC_WINDOW, d), y.dtype)])
    def gather(y_hbm, i_hbm, o_hbm, idx_vmem, buf):
        first = _sc_worker(mesh) * per

        @pl.loop(0, per)
        def _(step):
            win = first + step

            @pl.when(win < n_win)
            def _():
                base = win * LANES
                pltpu.sync_copy(i_hbm.at[:, pl.ds(base, LANES)], idx_vmem)
                for j in range(LANES // SC_WINDOW):
                    pltpu.sync_copy(y_hbm.at[idx_vmem[0, pl.ds(j * SC_WINDOW, SC_WINDOW)]], buf)
                    pltpu.sync_copy(buf, o_hbm.at[pl.ds(base + j * SC_WINDOW, SC_WINDOW)])

    return gather(y, idx)


def _grouped_kernel(te_ref, nu_ref, x_ref, wg_ref, wu_ref, wd_ref, o_ref, wgu_bf, wd_bf, *, hid):
    i = pl.program_id(0)

    @pl.when(jnp.logical_or(i == 0, te_ref[i] != te_ref[jnp.maximum(i - 1, 0)]))
    def _():
        wgu_bf[:, :hid] = wg_ref[...].astype(BF16)
        wgu_bf[:, hid:] = wu_ref[...].astype(BF16)
        wd_bf[...] = wd_ref[...].astype(BF16)

    @pl.when(i < nu_ref[0])
    def _():
        lo, hi = _unpack_pairs(x_ref[...])
        x = jnp.concatenate([lo.astype(BF16), hi.astype(BF16)], axis=1)
        gu = _dot(x, wgu_bf[...])
        gate = gu[:, :hid]
        h = (gate * jax.nn.sigmoid(gate) * gu[:, hid:]).astype(BF16)
        o_ref[...] = _pack_pairs(_dot(h, wd_bf[...]))


def _grouped_swiglu(xs, tile_expert, n_used, w_gate, w_up, w_down, layer):
    n, half = xs.shape
    d = 2 * half
    hid = w_down.shape[2]
    once = pl.Buffered(1)
    grid_spec = pltpu.PrefetchScalarGridSpec(
        num_scalar_prefetch=2,
        grid=(n // MOE_ROWS,),
        in_specs=[pl.BlockSpec((MOE_ROWS, half), lambda i, te, nu: (i, 0)),
                  pl.BlockSpec((None, None, d, hid), lambda i, te, nu: (layer, te[i], 0, 0), pipeline_mode=once),
                  pl.BlockSpec((None, None, d, hid), lambda i, te, nu: (layer, te[i], 0, 0), pipeline_mode=once),
                  pl.BlockSpec((None, None, hid, d), lambda i, te, nu: (layer, te[i], 0, 0))],
        out_specs=pl.BlockSpec((MOE_ROWS, half), lambda i, te, nu: (i, 0)),
        scratch_shapes=[pltpu.VMEM((d, 2 * hid), BF16), pltpu.VMEM((hid, d), BF16)],
    )
    return pl.pallas_call(
        functools.partial(_grouped_kernel, hid=hid),
        grid_spec=grid_spec,
        out_shape=jax.ShapeDtypeStruct((n, half), jnp.int32),
        compiler_params=_cparams("arbitrary"),
        name="moe_grouped",
    )(tile_expert, n_used, xs, w_gate, w_up, w_down)


def _combine_ln_kernel(x_ref, yg_ref, w_ref, fs_ref, g_ref, b_ref, o_ref, ob_ref, *, alpha):
    w = w_ref[...]
    acc_lo, acc_hi = None, None
    for k in range(TOP_K):
        lo, hi = _unpack_pairs(yg_ref[k])
        wk = w[:, k:k + 1]
        acc_lo = wk * lo if k == 0 else acc_lo + wk * lo
        acc_hi = wk * hi if k == 0 else acc_hi + wk * hi
    acc = alpha * x_ref[...] + fs_ref[...] + jnp.concatenate([acc_lo, acc_hi], axis=1)
    mu = jnp.mean(acc, axis=-1, keepdims=True)
    xc = acc - mu
    var = jnp.mean(xc * xc, axis=-1, keepdims=True)
    y = xc * lax.rsqrt(var + LN_EPS) * g_ref[...] + b_ref[...]
    o_ref[...] = y
    ob_ref[...] = y.astype(BF16)


def _combine_ln(x, yg, wts, f_shared, g, b, alpha):
    t, d = x.shape
    tm = _tile(t, 64, 8)
    row = lambda width: pl.BlockSpec((tm, width), lambda i: (i, 0))
    vec = pl.BlockSpec((1, d), lambda i: (0, 0))
    return pl.pallas_call(
        functools.partial(_combine_ln_kernel, alpha=alpha),
        grid=(t // tm,),
        in_specs=[row(d), pl.BlockSpec((TOP_K, tm, d // 2), lambda i: (0, i, 0)), row(LANES), row(d), vec, vec],
        out_specs=[row(d), row(d)],
        out_shape=[jax.ShapeDtypeStruct((t, d), F32), jax.ShapeDtypeStruct((t, d), BF16)],
        compiler_params=_cparams("parallel"),
        name="moe_combine_ln",
    )(x, yg, wts, f_shared, g.reshape(1, d), b.reshape(1, d))


def _routed_ln(xf, xpk, ids, wts, rank, counts, w_gate, w_up, w_down, layer, f_shared, g, b, alpha):
    t, d = xf.shape
    n_exp = w_gate.shape[1]
    n_rows = t * TOP_K + n_exp * MOE_ROWS
    n_tiles = n_rows // MOE_ROWS
    cnt = counts[0, :n_exp]
    padded = (cnt + MOE_ROWS - 1) // MOE_ROWS * MOE_ROWS
    ends = jnp.cumsum(padded)
    starts = ends - padded
    tile_start = jnp.arange(n_tiles, dtype=jnp.int32) * MOE_ROWS
    tile_expert = jnp.minimum(jnp.sum((ends[None, :] <= tile_start[:, None]).astype(jnp.int32), axis=1), n_exp - 1)
    n_used = (ends[-1:] // MOE_ROWS).astype(jnp.int32)
    top = ids[:, :TOP_K]
    pos_t = (starts[top] + jnp.take_along_axis(rank, top, axis=1)).astype(jnp.int32).T
    xs = _sc_scatter_rows(xpk, pos_t, n_rows)
    ys = _grouped_swiglu(xs, tile_expert, n_used, w_gate, w_up, w_down, layer)
    yg = _sc_gather_rows(ys, pos_t.reshape(1, TOP_K * t)).reshape(TOP_K, t, d // 2)
    return _combine_ln(xf, yg, wts, f_shared, g, b, alpha)


PROJ_ALIGN = 512


def _split_w_in(w, d):
    wdt = d // N_BRANCH
    gla_heads = wdt // GLA_DV
    names = (('gate', N_BRANCH * d), ('mla_cq', MLA_Q_RANK), ('mla_ckv', MLA_KV_RANK), ('mla_kr', MLA_ROPE),
             ('dsa_q', wdt), ('dsa_k', DSA_DH), ('dsa_v', DSA_DH),
             ('idx_q', IDX_HEADS * IDX_DIM), ('idx_k', IDX_DIM), ('idx_w', IDX_HEADS),
             ('gla_q', gla_heads * GLA_DK), ('gla_k', gla_heads * GLA_DK), ('gla_v', wdt),
             ('gla_a', GLA_GATE_RANK), ('gla_r', wdt),
             ('rwkv_r', wdt), ('rwkv_k', wdt), ('rwkv_v', wdt),
             ('rwkv_w', RWKV_LORA), ('rwkv_a', RWKV_LORA), ('rwkv_g', RWKV_LORA))
    parts, off = {}, 0
    for name, width in names:
        parts[name] = w[:, off:off + width]
        off += width
    assert off == w.shape[1], (off, w.shape)
    return parts


def _group(cols, dtype=BF16):
    wcat = jnp.concatenate(cols, axis=1)
    return _pad_cols(wcat, -(-wcat.shape[1] // PROJ_ALIGN) * PROJ_ALIGN).astype(dtype)


def kernel(x, positions, ln_in_g, ln_in_b, w_in, w_branch, w_out, mla_q_norm, mla_w_uq, mla_kv_norm, mla_w_ukv,
           gla_w_gate2, gla_b_gate, gla_norm_g, rwkv_mu_rkv, rwkv_mu_lora, rwkv_w0, rwkv_w2, rwkv_a0, rwkv_a2,
           rwkv_g2, rwkv_k_k, rwkv_k_a, rwkv_r_k, rwkv_lnx_g, rwkv_lnx_b, rwkv_v0, rwkv_v1, rwkv_v2,
           ln_mix_g, ln_mix_b, router_w, router_bias, exp_w_gate, exp_w_up, exp_w_down,
           sh_w_gate, sh_w_up, sh_w_down, ln_ffn_g, ln_ffn_b):
    batch, seq, d = x.shape
    depth = w_in.shape[0]
    t, wdt = batch * seq, d // N_BRANCH
    alpha = (2 * depth) ** 0.25
    ct, st = _rope_tables(positions)
    xf, xb = _layer_norm(x.reshape(t, d), (), ln_in_g, ln_in_b)
    v_first = None
    for l in range(depth):
        p = _split_w_in(w_in[l], d)
        pad128 = lambda w: _pad_cols(w, LANES)
        idx_q = jnp.pad(p['idx_q'].reshape(d, IDX_HEADS, IDX_DIM),
                        ((0, 0), (0, 0), (0, LANES - IDX_DIM))).reshape(d, IDX_HEADS * LANES)
        w_lat = _group([p['mla_cq'], p['mla_ckv'], pad128(p['mla_kr']), pad128(_swap_halves(p['mla_kr']))])
        dsa_q = p['dsa_q'] * (DSA_DH ** -0.5 * LOG2E)
        w_dsa = _group([idx_q, dsa_q, p['dsa_k'], p['dsa_v'], pad128(p['idx_k']), pad128(p['idx_w'])])
        w_gla = _group([p['gla_v'], p['gla_r'], p['gla_q'], p['gla_k'], pad128(p['gla_a'])])
        w_rwkv = _group([p['rwkv_r'], p['rwkv_k'], p['rwkv_v'], p['rwkv_w'], p['rwkv_a'], pad128(p['rwkv_g'])])
        lat = _matmul(xb, w_lat, F32)
        pb = _matmul(xb, w_dsa, BF16)
        pc = _matmul(xb, w_gla, F32)
        pd = _matmul(xb, w_rwkv, F32)

        y_mla = _mla(lat, ct, st, mla_q_norm[l], mla_w_uq[l], mla_kv_norm[l], mla_w_ukv[l], batch, seq)
        y_dsa = _dsa(pb, batch, seq, wdt)
        y_gla = _gla(pc, gla_w_gate2[l], gla_b_gate[l], gla_norm_g[l], batch, seq)
        vres = None if l == 0 else (rwkv_v0[l - 1], rwkv_v1[l - 1], rwkv_v2[l - 1])
        y_rwkv, v_first = _rwkv(pd, rwkv_mu_rkv[l], rwkv_mu_lora[l], rwkv_w0[l], rwkv_w2[l], rwkv_a0[l], rwkv_a2[l],
                                rwkv_g2[l], rwkv_k_k[l], rwkv_k_a[l], rwkv_r_k[l], rwkv_lnx_g[l], rwkv_lnx_b[l],
                                v_first, vres, batch, seq)

        merged = _merge(xb, (y_mla, y_dsa, y_gla, y_rwkv), p['gate'].astype(BF16), w_branch[l].astype(BF16))
        xf, xb, xpk = _layer_norm(xf, (_matmul(merged, w_out[l].astype(BF16), F32),), ln_mix_g[l], ln_mix_b[l], alpha,
                                  packed=True)

        comb, ids, wts, rank, counts = _router(xb, router_w[l], router_bias[l])
        n_exp = exp_w_gate.shape[1]
        f_shared = _experts(xb, comb, sh_w_gate[l][None], sh_w_up[l][None], sh_w_down[l][None], 1, n_exp)
        xf, xb = _routed_ln(xf, xpk, ids, wts, rank, counts, exp_w_gate, exp_w_up, exp_w_down, l,
                            f_shared, ln_ffn_g[l], ln_ffn_b[l], alpha)
    return xf.reshape(batch, seq, d)
```

```python
import functools
import math

import jax
import jax.numpy as jnp
from jax import lax
from jax.experimental import pallas as pl
from jax.experimental.pallas import tpu as pltpu
from jax.experimental.pallas import tpu_sc as plsc

F32 = jnp.float32
BF16 = jnp.bfloat16

LANES = 128

MLA_NOPE, MLA_ROPE, MLA_V = 128, 64, 128
MLA_Q_RANK, MLA_KV_RANK = 768, 256
ROPE_THETA = 10000.0
DSA_DH = 128
IDX_HEADS, IDX_DIM, IDX_TOPK_MAX = 16, 64, 256
GLA_DV, GLA_DK, GLA_GATE_RANK, GLA_TAU, GLA_CHUNK = 256, 128, 16, 16.0, 64
GLA_SUB = 16
RWKV_HS, RWKV_LORA, RWKV_GN_EPS = 64, 64, 64e-5
RWKV_CHUNK = 64
TOP_K, ROUTED_SCALE = 8, 2.5
LN_EPS, RMS_EPS = 1e-5, 1e-6
NEG_BIG = -1e30
LOG2E = math.log2(math.e)

VMEM_LIMIT = 56 * 1024 * 1024


def _cparams(*sem):
    return pltpu.CompilerParams(dimension_semantics=sem, vmem_limit_bytes=VMEM_LIMIT)


def _tile(n, pref, unit=LANES):
    if n <= pref:
        return n
    t = (pref // unit) * unit
    while t > unit and n % t:
        t -= unit
    assert n % t == 0, (n, pref, unit)
    return t


def _dot(a, b):
    return jnp.dot(a, b, preferred_element_type=F32)


def _dot_t(a, b):
    return lax.dot_general(a, b, (((1,), (1,)), ((), ())), preferred_element_type=F32)


def _dot_f32(a, b):
    return jnp.dot(a, b, preferred_element_type=F32, precision=lax.Precision.HIGHEST)


def _masked_sums(mask, x):
    w = x.shape[1]
    hi = x.astype(BF16)
    rest = x - hi.astype(F32)
    mid = rest.astype(BF16)
    lo = (rest - mid.astype(F32)).astype(BF16)
    s = _dot(mask, jnp.concatenate([hi, mid, lo], axis=1))
    return s[:, :w] + s[:, w:2 * w] + s[:, 2 * w:]


def _mm_kernel(a_ref, b_ref, o_ref):
    o_ref[...] = _dot(a_ref[...], b_ref[...]).astype(o_ref.dtype)


def _matmul(a, b, out_dtype, tm=1024, tn=512):
    m, k = a.shape
    n = b.shape[1]
    tm, tn = _tile(m, tm, 8), _tile(n, tn)
    return pl.pallas_call(
        _mm_kernel,
        grid=(m // tm, n // tn),
        in_specs=[pl.BlockSpec((tm, k), lambda i, j: (i, 0)), pl.BlockSpec((k, tn), lambda i, j: (0, j))],
        out_specs=pl.BlockSpec((tm, tn), lambda i, j: (i, j)),
        out_shape=jax.ShapeDtypeStruct((m, n), out_dtype),
        compiler_params=_cparams("parallel", "parallel"),
        name="matmul",
    )(a, b)


def _pack_pairs(v):
    half = v.shape[1] // 2
    bits = lax.bitcast_convert_type(v.astype(BF16).astype(F32), jnp.int32)
    return (bits[:, half:] & jnp.int32(-65536)) | lax.shift_right_logical(bits[:, :half], 16)


def _unpack_pairs(w):
    return (lax.bitcast_convert_type(lax.shift_left(w, 16), F32),
            lax.bitcast_convert_type(w & jnp.int32(-65536), F32))


def _ln_kernel(*refs, alpha, n_res, packed):
    x_ref, res_refs = refs[0], refs[1:1 + n_res]
    g_ref, b_ref, o_ref, ob_ref = refs[1 + n_res:5 + n_res]
    x = x_ref[...]
    if n_res:
        x = alpha * x
        for f_ref in res_refs:
            x = x + f_ref[...]
    mu = jnp.mean(x, axis=-1, keepdims=True)
    xc = x - mu
    var = jnp.mean(xc * xc, axis=-1, keepdims=True)
    y = xc * lax.rsqrt(var + LN_EPS) * g_ref[...] + b_ref[...]
    o_ref[...] = y
    ob_ref[...] = y.astype(BF16)
    if packed:
        refs[-1][...] = _pack_pairs(y)


def _layer_norm(x, res, g, b, alpha=1.0, packed=False):
    t, d = x.shape
    tm = _tile(t, 128, 8)
    row = pl.BlockSpec((tm, d), lambda i: (i, 0))
    vec = pl.BlockSpec((1, d), lambda i: (0, 0))
    args = (x,) + tuple(res)
    extra_spec = [pl.BlockSpec((tm, d // 2), lambda i: (i, 0))] if packed else []
    extra_shape = [jax.ShapeDtypeStruct((t, d // 2), jnp.int32)] if packed else []
    return pl.pallas_call(
        functools.partial(_ln_kernel, alpha=alpha, n_res=len(res), packed=packed),
        grid=(t // tm,),
        in_specs=[row] * len(args) + [vec, vec],
        out_specs=[row, row] + extra_spec,
        out_shape=[jax.ShapeDtypeStruct((t, d), F32), jax.ShapeDtypeStruct((t, d), BF16)] + extra_shape,
        compiler_params=_cparams("parallel"),
        name="layer_norm",
    )(*args, g.reshape(1, d), b.reshape(1, d))


def _pad_cols(w, width):
    return jnp.pad(w, ((0, 0), (0, width - w.shape[1])))


def _rms(x, g):
    return x * lax.rsqrt(jnp.mean(x * x, axis=-1, keepdims=True) + RMS_EPS) * g


def _mla_prep_kernel(lat_ref, ct_ref, st_ref, qg_ref, kg_ref, wqn_ref, wqr_ref, wqs_ref, wkn_ref, wv_ref,
                     qn_ref, qr_ref, kn_ref, kr_ref, v_ref, *, heads, scale):
    lat = lat_ref[...]
    cq = lat[:, :MLA_Q_RANK]
    ckv = lat[:, MLA_Q_RANK:MLA_Q_RANK + MLA_KV_RANK]
    kr = lat[:, MLA_Q_RANK + MLA_KV_RANK:MLA_Q_RANK + MLA_KV_RANK + LANES]
    krs = lat[:, MLA_Q_RANK + MLA_KV_RANK + LANES:MLA_Q_RANK + MLA_KV_RANK + 2 * LANES]
    ct, st = ct_ref[...], st_ref[...]
    nq = _rms(cq, qg_ref[...]).astype(BF16)
    nkv = _rms(ckv, kg_ref[...]).astype(BF16)
    qn_ref[...] = (_dot(nq, wqn_ref[...]) * scale).astype(BF16)
    cth = jnp.concatenate([ct] * heads, axis=1)
    sth = jnp.concatenate([st] * heads, axis=1)
    qr_ref[...] = ((_dot(nq, wqr_ref[...]) * cth + _dot(nq, wqs_ref[...]) * sth) * scale).astype(BF16)
    kn_ref[...] = _dot(nkv, wkn_ref[...]).astype(BF16)
    v_ref[...] = _dot(nkv, wv_ref[...]).astype(BF16)
    kr_ref[...] = (kr * ct + krs * st).astype(BF16)


def _flash_kernel(qn_ref, qr_ref, kn_ref, kr_ref, v_ref, o_ref, m_ref, l_ref, acc_ref, *, tq, tk):
    i, j = pl.program_id(2), pl.program_id(3)

    @pl.when(j == 0)
    def _():
        m_ref[...] = jnp.full_like(m_ref, NEG_BIG)
        l_ref[...] = jnp.zeros_like(l_ref)
        acc_ref[...] = jnp.zeros_like(acc_ref)

    def step(masked):
        q = jnp.concatenate([qn_ref[...], qr_ref[...]], axis=1)
        k = jnp.concatenate([kn_ref[...], kr_ref[...]], axis=1)
        s = _dot_t(q, k)
        if masked:
            qpos = i * tq + lax.broadcasted_iota(jnp.int32, (tq, tk), 0)
            kpos = j * tk + lax.broadcasted_iota(jnp.int32, (tq, tk), 1)
            s = jnp.where(kpos <= qpos, s, -jnp.inf)
        m_prev = m_ref[...]
        m_new = jnp.maximum(m_prev, jnp.max(s, axis=1, keepdims=True))
        alpha = jnp.exp2(m_prev - m_new)
        p = jnp.exp2(s - jnp.concatenate([m_new] * (tk // LANES), axis=1))
        l_ref[...] = alpha * l_ref[...] + jnp.sum(p, axis=1, keepdims=True)
        acc_ref[...] = alpha * acc_ref[...] + _dot(p.astype(BF16), v_ref[...])
        m_ref[...] = m_new

    below = j * tk + tk - 1 <= i * tq
    touches = j * tk <= i * tq + tq - 1

    @pl.when(below)
    def _():
        step(False)

    @pl.when(jnp.logical_and(touches, jnp.logical_not(below)))
    def _():
        step(True)

    @pl.when(j == pl.num_programs(3) - 1)
    def _():
        o_ref[...] = (acc_ref[...] / l_ref[...]).astype(o_ref.dtype)


def _rope_tables(positions):
    half = MLA_ROPE // 2
    inv = ROPE_THETA ** (-jnp.arange(0, MLA_ROPE, 2, dtype=F32) / MLA_ROPE)
    ang = positions.reshape(-1).astype(F32)[:, None] * inv
    cos, sin = jnp.cos(ang), jnp.sin(ang)
    zero = jnp.zeros((ang.shape[0], LANES - 2 * half), F32)
    return jnp.concatenate([cos, cos, zero], 1), jnp.concatenate([-sin, sin, zero], 1)


def _swap_halves(w):
    half = w.shape[-1] // 2
    return jnp.concatenate([w[..., half:], w[..., :half]], -1)


def _mla(lat, ct, st, q_norm, w_uq, kv_norm, w_ukv, batch, seq):
    t = lat.shape[0]
    heads = w_uq.shape[1] // (MLA_NOPE + MLA_ROPE)
    hw = heads * LANES
    wq = w_uq.reshape(MLA_Q_RANK, heads, MLA_NOPE + MLA_ROPE)
    wqn = wq[:, :, :MLA_NOPE].reshape(MLA_Q_RANK, hw).astype(BF16)
    rope_pad = ((0, 0), (0, 0), (0, LANES - MLA_ROPE))
    wqr = jnp.pad(wq[:, :, MLA_NOPE:], rope_pad).reshape(MLA_Q_RANK, hw).astype(BF16)
    wqs = jnp.pad(_swap_halves(wq[:, :, MLA_NOPE:]), rope_pad).reshape(MLA_Q_RANK, hw).astype(BF16)
    wkv = w_ukv.reshape(MLA_KV_RANK, heads, MLA_NOPE + MLA_V)
    wkn = wkv[:, :, :MLA_NOPE].reshape(MLA_KV_RANK, hw).astype(BF16)
    wv = wkv[:, :, MLA_NOPE:].reshape(MLA_KV_RANK, hw).astype(BF16)

    tm = _tile(t, 512, 8)
    row = lambda w: pl.BlockSpec((tm, w), lambda i: (i, 0))
    full = lambda a: pl.BlockSpec(a.shape, lambda i: (0, 0))
    qg, kg = q_norm.reshape(1, -1), kv_norm.reshape(1, -1)
    qn, qr, kn, kr, v = pl.pallas_call(
        functools.partial(_mla_prep_kernel, heads=heads, scale=(MLA_NOPE + MLA_ROPE) ** -0.5 * LOG2E),
        grid=(t // tm,),
        in_specs=[row(lat.shape[1]), row(LANES), row(LANES), full(qg), full(kg),
                  full(wqn), full(wqr), full(wqs), full(wkn), full(wv)],
        out_specs=[row(hw), row(hw), row(hw), row(LANES), row(hw)],
        out_shape=[jax.ShapeDtypeStruct((t, hw), BF16)] * 3 + [jax.ShapeDtypeStruct((t, LANES), BF16),
                                                               jax.ShapeDtypeStruct((t, hw), BF16)],
        compiler_params=_cparams("parallel"),
        name="mla_prep",
    )(lat, ct, st, qg, kg, wqn, wqr, wqs, wkn, wv)

    tq = tk = _tile(seq, 1024, 8)
    nq, nk = seq // tq, seq // tk
    r3 = lambda a: a.reshape(batch, seq, a.shape[1])
    last = lambda i, j: jnp.minimum(j, ((i + 1) * tq - 1) // tk)
    q_spec = pl.BlockSpec((None, tq, LANES), lambda b, h, i, j: (b, i, h))
    k_spec = pl.BlockSpec((None, tk, LANES), lambda b, h, i, j: (b, last(i, j), h))
    kr_spec = pl.BlockSpec((None, tk, LANES), lambda b, h, i, j: (b, last(i, j), 0))
    out = pl.pallas_call(
        functools.partial(_flash_kernel, tq=tq, tk=tk),
        grid=(batch, heads, nq, nk),
        in_specs=[q_spec, q_spec, k_spec, kr_spec, k_spec],
        out_specs=q_spec,
        out_shape=jax.ShapeDtypeStruct((batch, seq, hw), BF16),
        scratch_shapes=[pltpu.VMEM((tq, LANES), F32), pltpu.VMEM((tq, LANES), F32), pltpu.VMEM((tq, LANES), F32)],
        compiler_params=_cparams("parallel", "parallel", "parallel", "arbitrary"),
        name="mla_flash",
    )(r3(qn), r3(qr), r3(kn), r3(kr), r3(v))
    return out.reshape(t, hw)


DSA_TQ = 256
DSA_TK = 512
INT_MIN = -2 ** 31


def _dsa_kernel(iq_ref, q_ref, k_ref, v_ref, ik_ref, iw_ref, o_ref, key_ref, m_ref, l_ref, acc_ref,
                *, tq, tk, n_sel, heads, pos_bits):
    i = pl.program_id(1)
    n_kt = (i * tq + tq - 1) // tk + 1
    reps = tk // LANES
    wide = lambda a: jnp.concatenate([a] * reps, axis=1)
    qpos = i * tq + lax.broadcasted_iota(jnp.int32, (tq, tk), 0)
    lane_pos = lax.broadcasted_iota(jnp.int32, (tq, tk), 1)

    iq = iq_ref[...]
    iw = iw_ref[...].astype(F32) * (IDX_HEADS ** -0.5 * IDX_DIM ** -0.5)
    iw_b = [jnp.broadcast_to(iw[:, h:h + 1], (tq, LANES)) for h in range(IDX_HEADS)]

    def score_tile(j, carry):
        ikj = ik_ref[pl.ds(pl.multiple_of(j * tk, tk), tk), :]
        sc = jnp.zeros((tq, tk), F32)
        for h in range(IDX_HEADS):
            logit = _dot_t(iq[:, h * LANES:(h + 1) * LANES], ikj)
            sc = sc + wide(iw_b[h]) * jnp.maximum(logit, 0.0)
        sc = jnp.where(j * tk + lane_pos <= qpos, sc + 0.0, -jnp.inf)
        bits = lax.bitcast_convert_type(sc, jnp.int32)
        key_ref[j] = jnp.where(bits >= 0, bits, bits ^ 0x7FFFFFFF)
        return carry

    lax.fori_loop(0, n_kt, score_tile, 0)

    def count(pred):
        def body(j, acc):
            c = pred(key_ref[j], j * tk + lane_pos).astype(jnp.int32)
            for rep in range(reps):
                acc = acc + c[:, rep * LANES:(rep + 1) * LANES]
            return acc
        acc = lax.fori_loop(0, n_kt, body, jnp.zeros((tq, LANES), jnp.int32))
        return jnp.broadcast_to(jnp.sum(acc, axis=1, keepdims=True), (tq, LANES))

    def thr_bit(bit_i, thr):
        cand = thr ^ jnp.left_shift(jnp.int32(1), 31 - bit_i)
        cnt = count(lambda key, pos: key >= wide(cand))
        return jnp.where(cnt >= n_sel, cand, thr)

    thr = lax.fori_loop(0, 32, thr_bit, jnp.full((tq, LANES), INT_MIN, jnp.int32))
    cnt_gt = count(lambda key, pos: key > wide(thr))
    cnt_ge = count(lambda key, pos: key >= wide(thr))
    need = n_sel - cnt_gt

    def tie_bit(bit_i, cut):
        cand = cut | jnp.left_shift(jnp.int32(1), pos_bits - 1 - bit_i)
        cnt = count(lambda key, pos: jnp.logical_and(key == wide(thr), pos < wide(cand)))
        return jnp.where(cnt < need, cand, cut)

    surplus = jnp.max(cnt_ge - cnt_gt - need) > 0
    cut = lax.cond(surplus,
                   lambda: lax.fori_loop(0, pos_bits, tie_bit, jnp.zeros((tq, LANES), jnp.int32)),
                   lambda: jnp.full((tq, LANES), 2 ** 31 - 1, jnp.int32))

    q = q_ref[...]
    q_all = jnp.concatenate([q[:, h * LANES:(h + 1) * LANES] for h in range(heads)], axis=0)
    m_ref[...] = jnp.full_like(m_ref, NEG_BIG)
    l_ref[...] = jnp.zeros_like(l_ref)
    acc_ref[...] = jnp.zeros_like(acc_ref)

    def attend(j, carry):
        rows = pl.ds(pl.multiple_of(j * tk, tk), tk)
        s = _dot_t(q_all, k_ref[rows, :])
        key, pos = key_ref[j], j * tk + lane_pos
        keep = jnp.where(key > wide(thr), 0.0,
                         jnp.where(jnp.logical_and(key == wide(thr), pos <= wide(cut)), 0.0, -jnp.inf))
        keep = jnp.where(pos <= qpos, keep, -jnp.inf)
        s = s + jnp.concatenate([keep] * heads, axis=0)
        m_prev = m_ref[...]
        m_new = jnp.maximum(m_prev, jnp.max(s, axis=1, keepdims=True))
        alpha = jnp.exp2(m_prev - m_new)
        p = jnp.exp2(s - jnp.concatenate([m_new] * reps, axis=1))
        l_ref[...] = alpha * l_ref[...] + jnp.sum(p, axis=1, keepdims=True)
        acc_ref[...] = alpha * acc_ref[...] + _dot(p.astype(BF16), v_ref[rows, :])
        m_ref[...] = m_new
        return carry

    lax.fori_loop(0, n_kt, attend, 0)
    out = acc_ref[...] / l_ref[...]
    o_ref[...] = jnp.concatenate([out[h * tq:(h + 1) * tq] for h in range(heads)], axis=1).astype(o_ref.dtype)


def _dsa(pb, batch, seq, width):
    t, wb = pb.shape
    heads = width // DSA_DH
    iqw = IDX_HEADS * LANES
    tq, tk = _tile(seq, DSA_TQ, 8), _tile(seq, DSA_TK)
    n_sel = min(IDX_TOPK_MAX, seq // 4)
    c0 = (iqw + width) // LANES
    pb3 = pb.reshape(batch, seq, wb)
    qblk = lambda w, idx: pl.BlockSpec((None, tq, w), lambda b, i: (b, i, idx))
    seqblk = lambda idx: pl.BlockSpec((None, seq, LANES), lambda b, i: (b, 0, idx))
    out = pl.pallas_call(
        functools.partial(_dsa_kernel, tq=tq, tk=tk, n_sel=n_sel, heads=heads,
                          pos_bits=max(1, (seq - 1).bit_length())),
        grid=(batch, seq // tq),
        in_specs=[qblk(iqw, 0), qblk(width, iqw // width), seqblk(c0), seqblk(c0 + 1), seqblk(c0 + 2),
                  qblk(LANES, c0 + 3)],
        out_specs=qblk(width, 0),
        out_shape=jax.ShapeDtypeStruct((batch, seq, width), BF16),
        scratch_shapes=[pltpu.VMEM((seq // tk, tq, tk), jnp.int32), pltpu.VMEM((heads * tq, LANES), F32),
                        pltpu.VMEM((heads * tq, LANES), F32), pltpu.VMEM((heads * tq, LANES), F32)],
        compiler_params=_cparams("parallel", "arbitrary"),
        name="dsa",
    )(pb3, pb3, pb3, pb3, pb3, pb3)
    return out.reshape(t, width)


def _gla_kernel(q_ref, k_ref, v_ref, a_ref, r_ref, wg_ref, bg_ref, ng_ref, o_ref, state_ref):
    c_len, sub = GLA_CHUNK, GLA_SUB

    @pl.when(pl.program_id(2) == 0)
    def _():
        state_ref[...] = jnp.zeros_like(state_ref)

    z_all = _dot(a_ref[...].astype(BF16), wg_ref[...]) + bg_ref[...]
    g_all = jax.nn.log_sigmoid(z_all) / GLA_TAU
    ri = lax.broadcasted_iota(jnp.int32, (c_len, c_len), 0)
    ci = lax.broadcasted_iota(jnp.int32, (c_len, c_len), 1)
    b_all = _masked_sums((ri >= ci).astype(BF16), g_all)
    ones = jnp.ones((GLA_DK, LANES), BF16)
    trow = lax.broadcasted_iota(jnp.int32, (sub, GLA_DK), 0)
    lane = lax.broadcasted_iota(jnp.int32, (sub, LANES), 1)
    for hh in range(state_ref.shape[0]):
        ksl = slice(hh * GLA_DK, (hh + 1) * GLA_DK)
        vsl = slice(hh * GLA_DV, (hh + 1) * GLA_DV)
        o = _gla_head(q_ref[:, ksl] * GLA_DK ** -0.5, k_ref[:, ksl], v_ref[:, vsl].astype(BF16), b_all[:, ksl],
                      state_ref.at[hh], ones, trow, lane)
        r = r_ref[:, vsl]
        o_ref[:, vsl] = (_rms(o, ng_ref[:, vsl]) * (r * jax.nn.sigmoid(r))).astype(o_ref.dtype)


def _gla_head(q, k, vb, b, state_ref, ones, trow, lane):
    c_len, sub = GLA_CHUNK, GLA_SUB
    state_t = state_ref[...]
    o_inter = _dot_t((q * jnp.exp(b)).astype(BF16), state_t.astype(BF16))
    outs = []
    for blk in range(c_len // sub):
        lo = blk * sub
        qi, bi, ki = q[lo:lo + sub], b[lo:lo + sub], k[lo:lo + sub]
        rows = [qi * ki[s:s + 1] * jnp.exp(jnp.where(trow >= s, bi - bi[s:s + 1], -jnp.inf)) for s in range(sub)]
        prod = jnp.concatenate(rows, axis=0)
        p_hi = prod.astype(BF16)
        p_lo = (prod - p_hi.astype(F32)).astype(BF16)
        sums = _dot(jnp.concatenate([p_hi, p_lo], axis=0), ones)
        sums = sums[:sub * sub] + sums[sub * sub:]
        attn = jnp.zeros((sub, LANES), F32)
        for s in range(sub):
            attn = attn + jnp.where(lane == s, sums[s * sub:(s + 1) * sub], 0.0)
        o_blk = _dot(attn[:, :sub].astype(BF16), vb[lo:lo + sub])
        if blk:
            qa = qi * jnp.exp(bi - bi[0:1])
            ka = k[:lo] * jnp.exp(bi[0:1] - b[:lo])
            o_blk = o_blk + _dot(_dot_t(qa.astype(BF16), ka.astype(BF16)).astype(BF16), vb[:lo])
        outs.append(o_blk)
    o = o_inter + jnp.concatenate(outs, axis=0)

    b_last = b[c_len - 1:c_len]
    k_dec = (k * jnp.exp(b_last - b)).astype(BF16)
    state_ref[...] = state_t * jnp.exp(b_last) + lax.dot_general(
        vb, k_dec, (((0,), (0,)), ((), ())), preferred_element_type=F32)
    return o


GLA_HEADS_PER_STEP = 4


def _gla(pc, w_gate2, b_gate, norm_g, batch, seq):
    t = pc.shape[0]
    heads = norm_g.shape[0] // GLA_DV
    hps = math.gcd(heads, GLA_HEADS_PER_STEP)
    groups = heads // hps
    wg = jnp.pad(w_gate2, ((0, LANES - GLA_GATE_RANK), (0, 0))).astype(BF16)
    nc = seq // GLA_CHUNK
    blk = lambda w, off: pl.BlockSpec((None, GLA_CHUNK, hps * w), lambda b, g, c: (b, c, off + g))
    par = lambda rows, w: pl.BlockSpec((rows, hps * w), lambda b, g, c: (0, g))
    pc3 = pc.reshape(batch, seq, pc.shape[1])
    out = pl.pallas_call(
        _gla_kernel,
        grid=(batch, groups, nc),
        in_specs=[blk(GLA_DK, 4 * groups), blk(GLA_DK, 5 * groups), blk(GLA_DV, 0),
                  pl.BlockSpec((None, GLA_CHUNK, LANES), lambda b, g, c: (b, c, 6 * heads)),
                  blk(GLA_DV, groups), par(LANES, GLA_DK), par(1, GLA_DK), par(1, GLA_DV)],
        out_specs=blk(GLA_DV, 0),
        out_shape=jax.ShapeDtypeStruct((batch, seq, heads * GLA_DV), BF16),
        scratch_shapes=[pltpu.VMEM((hps, GLA_DV, GLA_DK), F32)],
        compiler_params=_cparams("parallel", "parallel", "arbitrary"),
        name="gla",
    )(pc3, pc3, pc3, pc3, pc3, wg, b_gate.reshape(1, -1), norm_g.reshape(1, -1))
    return out.reshape(t, heads * GLA_DV)


_DIMS = {'nn': (((1,), (0,)), ((), ())), 'nt': (((1,), (1,)), ((), ())), 'tn': (((0,), (0,)), ((), ()))}


def _mm(a, b, form):
    return lax.dot_general(a.astype(BF16), b.astype(BF16), _DIMS[form], preferred_element_type=F32)


def _seg_sum(x, width):
    gi = lax.broadcasted_iota(jnp.int32, (LANES, LANES), 0) // width
    gj = lax.broadcasted_iota(jnp.int32, (LANES, LANES), 1) // width
    ones = (gi == gj).astype(F32)
    return jnp.concatenate([_dot_f32(x[:, c:c + LANES], ones) for c in range(0, x.shape[1], LANES)], axis=1)


def _rwkv_prep_kernel(*refs, width, seq, has_vres):
    if has_vres:
        (p_ref, pp_ref, prm_ref, ml_ref, w2_ref, a2_ref, g2_ref, vf_ref, v0_ref, v1_ref, v2_ref,
         r_ref, lw_ref, k_ref, v_ref, kk_ref, bb_ref, bonus_ref, g_ref) = refs
    else:
        (p_ref, pp_ref, prm_ref, ml_ref, w2_ref, a2_ref, g2_ref,
         r_ref, lw_ref, k_ref, v_ref, kk_ref, bb_ref, bonus_ref, g_ref) = refs
    w = width
    p, prm, ml = p_ref[...], prm_ref[...], ml_ref[...]
    tm = p.shape[0]
    above = jnp.where((pl.program_id(0) * tm) % seq == 0, 0.0, pp_ref[7:8, :])
    first = lax.broadcasted_iota(jnp.int32, p.shape, 0) == 0
    pp = jnp.where(first, jnp.broadcast_to(above, p.shape), pltpu.roll(p, 1, axis=0))
    lerp = lambda lo, hi, mu: p[:, lo:hi] + (pp[:, lo:hi] - p[:, lo:hi]) * mu
    r = lerp(0, w, prm[0:1])
    k = lerp(w, 2 * w, prm[1:2])
    v = lerp(2 * w, 3 * w, prm[2:3])
    x_wa = lerp(3 * w, 3 * w + LANES, ml[0:1])
    x_g = lerp(3 * w + LANES, 3 * w + 2 * LANES, ml[1:2])
    w_log = -jax.nn.softplus(-(prm[3:4] + _dot(jnp.tanh(x_wa).astype(BF16), w2_ref[...]))) - 0.5
    lw_ref[...] = -jnp.exp(w_log)
    a = jax.nn.sigmoid(prm[4:5] + _dot(x_wa.astype(BF16), a2_ref[...]))
    g_ref[...] = _dot(jax.nn.sigmoid(x_g).astype(BF16), g2_ref[...])
    if has_vres:
        mix = jax.nn.sigmoid(v0_ref[...] + _dot(_dot(v.astype(BF16), v1_ref[...]).astype(BF16), v2_ref[...]))
        v = v + (vf_ref[...] - v) * mix
    kk = k * prm[5:6]
    kk = kk * lax.rsqrt(jnp.maximum(_seg_sum(kk * kk, RWKV_HS), 1e-24))
    k2 = k * (1.0 + (a - 1.0) * prm[6:7])
    r_ref[...] = r
    k_ref[...] = k2
    v_ref[...] = v
    kk_ref[...] = kk
    bb_ref[...] = kk * a
    bonus_ref[...] = _seg_sum(r * k2 * prm[7:8], RWKV_HS) * v


RWKV_HEADS_PER_STEP = 8


def _rwkv_chunk_kernel(r_ref, lw_ref, k_ref, v_ref, kk_ref, bb_ref, y_ref, s_ref, *, n_chunk):
    c_len, hs = RWKV_CHUNK, RWKV_HS
    rows_n = n_chunk * c_len

    @pl.when(pl.program_id(2) == 0)
    def _():
        s_ref[...] = jnp.zeros_like(s_ref)

    ri = lax.broadcasted_iota(jnp.int32, (rows_n, rows_n), 0)
    ci = lax.broadcasted_iota(jnp.int32, (rows_n, rows_n), 1)
    same = (ri // c_len) == (ci // c_len)
    incl = jnp.logical_and(same, ci <= ri)
    strict = jnp.logical_and(same, ci < ri)
    eye = (ri == ci).astype(F32)
    e_r = lax.broadcasted_iota(jnp.int32, (hs, hs), 0)
    e_c = lax.broadcasted_iota(jnp.int32, (hs, hs), 1)
    sums = _masked_sums(jnp.concatenate([incl, same], axis=0).astype(BF16), lw_ref[...])
    cum_all, cum_c_all = sums[:rows_n], sums[rows_n:]
    heads = range(s_ref.shape[0])
    hd = []
    for hh in heads:
        sl = slice(hh * hs, (hh + 1) * hs)
        r, lw, k, v, kk, bb = (ref[...][:, sl] for ref in (r_ref, lw_ref, k_ref, v_ref, kk_ref, bb_ref))
        cum, cum_c = cum_all[:, sl], cum_c_all[:, sl]
        g_inv, g_end = jnp.exp(-cum), jnp.exp(cum_c - cum)
        kap, rt = kk * jnp.exp(cum - lw), r * jnp.exp(cum)
        bet, kt = bb * g_inv, k * g_inv
        gram = _mm(jnp.concatenate([kap, rt], axis=0), jnp.concatenate([bet, kt], axis=0), 'nt')
        hd.append(dict(v=v, kap=kap, rt=rt, bet_c=bb * g_end, kt_c=k * g_end, cum_c=cum_c,
                       n_m=jnp.where(strict, gram[:rows_n, :rows_n], 0.0),
                       a_kk=jnp.where(strict, gram[:rows_n, rows_n:], 0.0),
                       a_rb=jnp.where(incl, gram[rows_n:, :rows_n], 0.0),
                       a_rk=jnp.where(incl, gram[rows_n:, rows_n:], 0.0)))
    t_inv = [eye - h['n_m'] for h in hd]
    pw = [-h['n_m'] for h in hd]
    for _ in range(int(math.log2(c_len)) - 1):
        pw = [_mm(p, p, 'nn') for p in pw]
        t_inv = [t + _mm(t, p, 'nn') for t, p in zip(t_inv, pw)]
    av = [_mm(jnp.concatenate([h['a_kk'], h['a_rk']], axis=0), h['v'], 'nn') for h in hd]
    z = [_mm(t, jnp.concatenate([h['kap'], a[:rows_n]], axis=1), 'nn') for t, h, a in zip(t_inv, hd, av)]
    az = [_mm(h['a_rb'], zz, 'nn') for h, zz in zip(hd, z)]
    r_p = [h['rt'] - a[:, :hs] for h, a in zip(hd, az)]
    y0 = [a[rows_n:] - b[:, hs:] for a, b in zip(av, az)]
    s = [s_ref[hh] for hh in heads]
    outs = [[] for _ in heads]
    for c in range(n_chunk):
        rows = slice(c * c_len, (c + 1) * c_len)
        for hh in heads:
            h = hd[hh]
            outs[hh].append(_mm(r_p[hh][rows], s[hh], 'nt') + y0[hh][rows])
            zb = _mm(z[hh][rows], h['bet_c'][rows], 'tn')
            g_chunk = jnp.exp(h['cum_c'][c * c_len:c * c_len + 1])
            m = jnp.where(e_r == e_c, jnp.broadcast_to(g_chunk, (hs, hs)), 0.0) - zb[:hs]
            s[hh] = _mm(s[hh], m, 'nn') + _mm(h['v'][rows], h['kt_c'][rows], 'tn') - zb[hs:]
    for hh in heads:
        s_ref[hh] = s[hh]
    y_ref[...] = jnp.concatenate([jnp.concatenate(o, axis=0) for o in outs], axis=1)


def _rwkv_post_kernel(y_ref, bonus_ref, g_ref, lg_ref, lb_ref, o_ref):
    y = y_ref[...]
    mu = _seg_sum(y, RWKV_HS) * (1.0 / RWKV_HS)
    yc = y - mu
    var = _seg_sum(yc * yc, RWKV_HS) * (1.0 / RWKV_HS)
    yn = yc * lax.rsqrt(var + RWKV_GN_EPS) * lg_ref[...] + lb_ref[...]
    o_ref[...] = ((yn + bonus_ref[...]) * g_ref[...]).astype(o_ref.dtype)


def _rwkv(pd, mu_rkv, mu_lora, w0, w2, a0, a2, g2, k_k, k_a, r_k, lnx_g, lnx_b, v_first, vres, batch, seq):
    t, wd = pd.shape
    w = w0.shape[0]
    prm = jnp.stack([mu_rkv[0], mu_rkv[1], mu_rkv[2], w0, a0, k_k, k_a, r_k.reshape(-1)])
    zl = jnp.zeros((RWKV_LORA,), F32)
    ml = jnp.stack([jnp.concatenate([mu_lora[0], mu_lora[1]]), jnp.concatenate([mu_lora[2], zl])])
    zw = jnp.zeros((LANES - RWKV_LORA, w), F32)
    w2p = jnp.concatenate([w2, zw]).astype(BF16)
    a2p = jnp.concatenate([zw, a2]).astype(BF16)
    g2p = jnp.concatenate([g2, zw]).astype(BF16)
    has_vres = vres is not None
    tm = _tile(seq, 256, 8)
    row = lambda width: pl.BlockSpec((tm, width), lambda i: (i, 0))
    full = lambda a: pl.BlockSpec(a.shape, lambda i: (0, 0))
    above = pl.BlockSpec((8, wd), lambda i: (jnp.maximum(i * (tm // 8) - 1, 0), 0))
    args = [pd, pd, prm, ml, w2p, a2p, g2p]
    specs = [row(wd), above, full(prm), full(ml), full(w2p), full(a2p), full(g2p)]
    if has_vres:
        v0, v1, v2 = vres
        v1p = _pad_cols(v1, LANES).astype(BF16)
        v2p = jnp.pad(v2, ((0, LANES - v2.shape[0]), (0, 0))).astype(BF16)
        v0r = v0.reshape(1, w)
        args += [v_first, v0r, v1p, v2p]
        specs += [row(w), full(v0r), full(v1p), full(v2p)]
    r, lw, k2, v, kk, bb, bonus, g = pl.pallas_call(
        functools.partial(_rwkv_prep_kernel, width=w, seq=seq, has_vres=has_vres),
        grid=(t // tm,),
        in_specs=specs,
        out_specs=[row(w)] * 8,
        out_shape=[jax.ShapeDtypeStruct((t, w), F32)] * 8,
        compiler_params=_cparams("parallel"),
        name="rwkv_prep",
    )(*args)

    n_chunk = 4 if seq % (4 * RWKV_CHUNK) == 0 else 1
    rows_n = n_chunk * RWKV_CHUNK
    bw = math.gcd(w, RWKV_HEADS_PER_STEP * RWKV_HS)
    blk = pl.BlockSpec((None, rows_n, bw), lambda b, h, c: (b, c, h))
    r3 = lambda a: a.reshape(batch, seq, w)
    y = pl.pallas_call(
        functools.partial(_rwkv_chunk_kernel, n_chunk=n_chunk),
        grid=(batch, w // bw, seq // rows_n),
        in_specs=[blk] * 6,
        out_specs=blk,
        out_shape=jax.ShapeDtypeStruct((batch, seq, w), F32),
        scratch_shapes=[pltpu.VMEM((bw // RWKV_HS, RWKV_HS, RWKV_HS), F32)],
        compiler_params=_cparams("parallel", "parallel", "arbitrary"),
        name="rwkv_chunk",
    )(r3(r), r3(lw), r3(k2), r3(v), r3(kk), r3(bb))

    lg, lb = lnx_g.reshape(1, w), lnx_b.reshape(1, w)
    out = pl.pallas_call(
        _rwkv_post_kernel,
        grid=(t // tm,),
        in_specs=[row(w), row(w), row(w), full(lg), full(lb)],
        out_specs=row(w),
        out_shape=jax.ShapeDtypeStruct((t, w), BF16),
        compiler_params=_cparams("parallel"),
        name="rwkv_post",
    )(y.reshape(t, w), bonus, g, lg, lb)
    return out, (v_first if has_vres else v)


N_BRANCH = 4


def _merge_kernel(*refs):
    x_ref = refs[0]
    y_refs = refs[1:1 + N_BRANCH]
    wg_refs = refs[1 + N_BRANCH:1 + 2 * N_BRANCH]
    wb_refs = refs[1 + 2 * N_BRANCH:1 + 3 * N_BRANCH]
    o_ref = refs[-1]
    x = x_ref[...]
    acc = None
    for y_ref, wg_ref, wb_ref in zip(y_refs, wg_refs, wb_refs):
        term = jax.nn.sigmoid(_dot(x, wg_ref[...])) * _dot(y_ref[...], wb_ref[...])
        acc = term if acc is None else acc + term
    o_ref[...] = acc.astype(o_ref.dtype)


def _merge(xb, ys, w_gate, w_branch):
    t, d = xb.shape
    w = ys[0].shape[1]
    tm, tn = _tile(t, 512, 8), _tile(d, 256)
    nj = d // tn
    gate_spec = lambda i: pl.BlockSpec((d, tn), lambda r, j: (0, i * nj + j))
    br_spec = lambda i: pl.BlockSpec((None, w, tn), lambda r, j: (i, 0, j))
    return pl.pallas_call(
        _merge_kernel,
        grid=(t // tm, nj),
        in_specs=[pl.BlockSpec((tm, d), lambda r, j: (r, 0))] + [pl.BlockSpec((tm, w), lambda r, j: (r, 0))] * N_BRANCH
        + [gate_spec(i) for i in range(N_BRANCH)] + [br_spec(i) for i in range(N_BRANCH)],
        out_specs=pl.BlockSpec((tm, tn), lambda r, j: (r, j)),
        out_shape=jax.ShapeDtypeStruct((t, d), BF16),
        compiler_params=_cparams("parallel", "parallel"),
        name="merge",
    )(xb, *ys, *([w_gate] * N_BRANCH), *([w_branch] * N_BRANCH))


def _router_kernel(x_ref, w_ref, b_ref, ids_ref, wts_ref, rank_ref, cnt_ref, run_ref, *, n_exp):
    tm = x_ref.shape[0]

    @pl.when(pl.program_id(0) == 0)
    def _():
        run_ref[...] = jnp.zeros_like(run_ref)

    scores = jax.nn.sigmoid(_dot(x_ref[...], w_ref[...]))
    lane = lax.broadcasted_iota(jnp.int32, (tm, LANES), 1)
    work = jnp.where(lane < n_exp, scores + b_ref[...], -jnp.inf)
    chosen = jnp.zeros((tm, LANES), F32)
    sel = jnp.zeros((tm, LANES), F32)
    ids = jnp.zeros((tm, LANES), jnp.int32)
    wts = jnp.zeros((tm, LANES), F32)
    for it in range(TOP_K):
        best = jnp.max(work, axis=1, keepdims=True)
        first = jnp.min(jnp.where(work == best, lane, LANES), axis=1, keepdims=True)
        hit = lane == first
        chosen = jnp.where(hit, scores, chosen)
        sel = jnp.where(hit, 1.0, sel)
        ids = jnp.where(lane == it, first, ids)
        wts = jnp.where(lane == it, jnp.sum(jnp.where(hit, scores, 0.0), axis=1, keepdims=True), wts)
        work = jnp.where(hit, -jnp.inf, work)
    ids_ref[...] = ids
    wts_ref[...] = wts * (ROUTED_SCALE / jnp.sum(chosen, axis=1, keepdims=True))
    ri = lax.broadcasted_iota(jnp.int32, (tm, tm), 0)
    ci = lax.broadcasted_iota(jnp.int32, (tm, tm), 1)
    before = _dot((ci < ri).astype(BF16), sel.astype(BF16))
    run = run_ref[0:1, :]
    rank_ref[...] = (before + run).astype(jnp.int32)
    run = run + jnp.sum(sel, axis=0, keepdims=True)
    run_ref[...] = jnp.broadcast_to(run, run_ref.shape)
    cnt_ref[...] = jnp.broadcast_to(run, cnt_ref.shape).astype(jnp.int32)


def _router(xb, router_w, router_bias):
    t, d = xb.shape
    n_exp = router_w.shape[1]
    assert n_exp < LANES
    tm = _tile(t, 512, 8)
    wr = _pad_cols(router_w, LANES).astype(BF16)
    br = _pad_cols(router_bias.reshape(1, n_exp), LANES)
    tok = pl.BlockSpec((tm, LANES), lambda i: (i, 0))
    tok_shape = lambda dt: jax.ShapeDtypeStruct((t, LANES), dt)
    return pl.pallas_call(
        functools.partial(_router_kernel, n_exp=n_exp),
        grid=(t // tm,),
        in_specs=[pl.BlockSpec((tm, d), lambda i: (i, 0)), pl.BlockSpec((d, LANES), lambda i: (0, 0)),
                  pl.BlockSpec((1, LANES), lambda i: (0, 0))],
        out_specs=[tok, tok, tok, pl.BlockSpec((8, LANES), lambda i: (0, 0))],
        out_shape=[tok_shape(jnp.int32), tok_shape(F32), tok_shape(jnp.int32),
                   jax.ShapeDtypeStruct((8, LANES), jnp.int32)],
        scratch_shapes=[pltpu.VMEM((8, LANES), F32)],
        compiler_params=_cparams("arbitrary"),
        name="router",
    )(xb, wr, br)


def _shared_kernel(x_ref, wgu_ref, wd_ref, o_ref, *, hid):
    gu = _dot(x_ref[...], wgu_ref[...])
    gate = gu[:, :hid]
    h = (gate * jax.nn.sigmoid(gate) * gu[:, hid:]).astype(BF16)
    o_ref[...] = _dot(h, wd_ref[...])


def _shared_expert(xb, w_gate, w_up, w_down):
    t, d = xb.shape
    hid = w_gate.shape[1]
    wgu = jnp.concatenate([w_gate, w_up], axis=1).astype(BF16)
    wd = w_down.astype(BF16)
    tm = _tile(t, 512, 8)
    full = lambda a: pl.BlockSpec(a.shape, lambda i: (0, 0))
    return pl.pallas_call(
        functools.partial(_shared_kernel, hid=hid),
        grid=(t // tm,),
        in_specs=[pl.BlockSpec((tm, d), lambda i: (i, 0)), full(wgu), full(wd)],
        out_specs=pl.BlockSpec((tm, d), lambda i: (i, 0)),
        out_shape=jax.ShapeDtypeStruct((t, d), F32),
        compiler_params=_cparams("parallel"),
        name="shared_expert",
    )(xb, wgu, wd)


MOE_ROWS = 512
SC_WINDOW = 16


def _sc_mesh():
    return plsc.VectorSubcoreMesh(core_axis_name="core", subcore_axis_name="subcore")


def _sc_windows(n_items):
    mesh = _sc_mesh()
    workers = mesh.num_cores * mesh.num_subcores
    n_win = n_items // LANES
    assert n_items % LANES == 0
    return mesh, n_win, -(-n_win // workers)


def _sc_worker(mesh):
    return lax.axis_index("core") * mesh.num_subcores + lax.axis_index("subcore")


def _sc_scatter_rows(x, pos, n_rows):
    t, d = x.shape
    k = pos.shape[0]
    mesh, n_win, per = _sc_windows(t)

    n_sub = LANES // SC_WINDOW

    @pl.kernel(out_type=jax.ShapeDtypeStruct((n_rows, d), x.dtype), mesh=mesh,
               scratch_types=[pltpu.VMEM((k, LANES), jnp.int32), pltpu.VMEM((2, SC_WINDOW, d), x.dtype),
                              pltpu.SemaphoreType.DMA, pltpu.SemaphoreType.DMA((2,))])
    def scatter(x_hbm, pos_hbm, o_hbm, idx_vmem, buf, sem_in, sem_out):
        first = _sc_worker(mesh) * per

        @pl.loop(0, per)
        def _(step):
            win = first + step

            @pl.when(win < n_win)
            def _():
                base = win * LANES
                pltpu.sync_copy(pos_hbm.at[:, pl.ds(base, LANES)], idx_vmem)
                load = lambda j: pltpu.make_async_copy(
                    x_hbm.at[pl.ds(base + j * SC_WINDOW, SC_WINDOW)], buf.at[j % 2], sem_in)
                store = lambda j, kk: pltpu.make_async_copy(
                    buf.at[j % 2], o_hbm.at[idx_vmem[kk, pl.ds(j * SC_WINDOW, SC_WINDOW)]], sem_out.at[j % 2])
                load(0).start()
                for j in range(n_sub):
                    load(j).wait()
                    if j >= 1:
                        for kk in range(k):
                            store(j - 1, kk).wait()
                    if j + 1 < n_sub:
                        load(j + 1).start()
                    for kk in range(k):
                        store(j, kk).start()
                for kk in range(k):
                    store(n_sub - 1, kk).wait()

    return scatter(x, pos)


def _sc_gather_rows(y, idx):
    m = idx.shape[1]
    d = y.shape[1]
    mesh, n_win, per = _sc_windows(m)

    n_sub = LANES // SC_WINDOW

    @pl.kernel(out_type=jax.ShapeDtypeStruct((m, d), y.dtype), mesh=mesh,
               scratch_types=[pltpu.VMEM((1, LANES), jnp.int32), pltpu.VMEM((2, SC_WINDOW, d), y.dtype),
                              pltpu.SemaphoreType.DMA, pltpu.SemaphoreType.DMA((2,))])
    def gather(y_hbm, i_hbm, o_hbm, idx_vmem, buf, sem_in, sem_out):
        first = _sc_worker(mesh) * per

        @pl.loop(0, per)
        def _(step):
            win = first + step

            @pl.when(win < n_win)
            def _():
                base = win * LANES
                pltpu.sync_copy(i_hbm.at[:, pl.ds(base, LANES)], idx_vmem)
                load = lambda j: pltpu.make_async_copy(
                    y_hbm.at[idx_vmem[0, pl.ds(j * SC_WINDOW, SC_WINDOW)]], buf.at[j % 2], sem_in)
                store = lambda j: pltpu.make_async_copy(
                    buf.at[j % 2], o_hbm.at[pl.ds(base + j * SC_WINDOW, SC_WINDOW)], sem_out.at[j % 2])
                load(0).start()
                for j in range(n_sub):
                    load(j).wait()
                    if j >= 1:
                        store(j - 1).wait()
                    if j + 1 < n_sub:
                        load(j + 1).start()
                    store(j).start()
                store(n_sub - 1).wait()

    return gather(y, idx)


def _grouped_kernel(te_ref, nu_ref, x_ref, wg_ref, wu_ref, wd_ref, o_ref, wgu_bf, wd_bf, *, hid):
    i = pl.program_id(0)

    @pl.when(jnp.logical_or(i == 0, te_ref[i] != te_ref[jnp.maximum(i - 1, 0)]))
    def _():
        wgu_bf[:, :hid] = wg_ref[...].astype(BF16)
        wgu_bf[:, hid:] = wu_ref[...].astype(BF16)
        wd_bf[...] = wd_ref[...].astype(BF16)

    @pl.when(i < nu_ref[0])
    def _():
        lo, hi = _unpack_pairs(x_ref[...])
        x = jnp.concatenate([lo.astype(BF16), hi.astype(BF16)], axis=1)
        gu = _dot(x, wgu_bf[...])
        gate = gu[:, :hid]
        h = (gate * jax.nn.sigmoid(gate) * gu[:, hid:]).astype(BF16)
        o_ref[...] = _pack_pairs(_dot(h, wd_bf[...]))


def _grouped_swiglu(xs, tile_expert, n_used, w_gate, w_up, w_down, layer):
    n, half = xs.shape
    d = 2 * half
    hid = w_down.shape[2]
    once = pl.Buffered(1)
    grid_spec = pltpu.PrefetchScalarGridSpec(
        num_scalar_prefetch=2,
        grid=(n // MOE_ROWS,),
        in_specs=[pl.BlockSpec((MOE_ROWS, half), lambda i, te, nu: (i, 0)),
                  pl.BlockSpec((None, None, d, hid), lambda i, te, nu: (layer, te[i], 0, 0), pipeline_mode=once),
                  pl.BlockSpec((None, None, d, hid), lambda i, te, nu: (layer, te[i], 0, 0), pipeline_mode=once),
                  pl.BlockSpec((None, None, hid, d), lambda i, te, nu: (layer, te[i], 0, 0))],
        out_specs=pl.BlockSpec((MOE_ROWS, half), lambda i, te, nu: (i, 0)),
        scratch_shapes=[pltpu.VMEM((d, 2 * hid), BF16), pltpu.VMEM((hid, d), BF16)],
    )
    return pl.pallas_call(
        functools.partial(_grouped_kernel, hid=hid),
        grid_spec=grid_spec,
        out_shape=jax.ShapeDtypeStruct((n, half), jnp.int32),
        compiler_params=_cparams("arbitrary"),
        name="moe_grouped",
    )(tile_expert, n_used, xs, w_gate, w_up, w_down)


def _combine_ln_kernel(x_ref, yg_ref, w_ref, fs_ref, g_ref, b_ref, o_ref, ob_ref, *, alpha):
    w = w_ref[...]
    acc_lo, acc_hi = None, None
    for k in range(TOP_K):
        lo, hi = _unpack_pairs(yg_ref[k])
        wk = w[:, k:k + 1]
        acc_lo = wk * lo if k == 0 else acc_lo + wk * lo
        acc_hi = wk * hi if k == 0 else acc_hi + wk * hi
    acc = alpha * x_ref[...] + fs_ref[...] + jnp.concatenate([acc_lo, acc_hi], axis=1)
    mu = jnp.mean(acc, axis=-1, keepdims=True)
    xc = acc - mu
    var = jnp.mean(xc * xc, axis=-1, keepdims=True)
    y = xc * lax.rsqrt(var + LN_EPS) * g_ref[...] + b_ref[...]
    o_ref[...] = y
    ob_ref[...] = y.astype(BF16)


def _combine_ln(x, yg, wts, f_shared, g, b, alpha):
    t, d = x.shape
    tm = _tile(t, 64, 8)
    row = lambda width: pl.BlockSpec((tm, width), lambda i: (i, 0))
    vec = pl.BlockSpec((1, d), lambda i: (0, 0))
    return pl.pallas_call(
        functools.partial(_combine_ln_kernel, alpha=alpha),
        grid=(t // tm,),
        in_specs=[row(d), pl.BlockSpec((TOP_K, tm, d // 2), lambda i: (0, i, 0)), row(LANES), row(d), vec, vec],
        out_specs=[row(d), row(d)],
        out_shape=[jax.ShapeDtypeStruct((t, d), F32), jax.ShapeDtypeStruct((t, d), BF16)],
        compiler_params=_cparams("parallel"),
        name="moe_combine_ln",
    )(x, yg, wts, f_shared, g.reshape(1, d), b.reshape(1, d))


def _routed_ln(xf, xpk, ids, wts, rank, counts, w_gate, w_up, w_down, layer, f_shared, g, b, alpha):
    t, d = xf.shape
    n_exp = w_gate.shape[1]
    n_rows = t * TOP_K + n_exp * MOE_ROWS
    n_tiles = n_rows // MOE_ROWS
    cnt = counts[0, :n_exp]
    padded = (cnt + MOE_ROWS - 1) // MOE_ROWS * MOE_ROWS
    ends = jnp.cumsum(padded)
    starts = ends - padded
    tile_start = jnp.arange(n_tiles, dtype=jnp.int32) * MOE_ROWS
    tile_expert = jnp.minimum(jnp.sum((ends[None, :] <= tile_start[:, None]).astype(jnp.int32), axis=1), n_exp - 1)
    n_used = (ends[-1:] // MOE_ROWS).astype(jnp.int32)
    top = ids[:, :TOP_K]
    pos_t = (starts[top] + jnp.take_along_axis(rank, top, axis=1)).astype(jnp.int32).T
    xs = _sc_scatter_rows(xpk, pos_t, n_rows)
    ys = _grouped_swiglu(xs, tile_expert, n_used, w_gate, w_up, w_down, layer)
    yg = _sc_gather_rows(ys, pos_t.reshape(1, TOP_K * t)).reshape(TOP_K, t, d // 2)
    return _combine_ln(xf, yg, wts, f_shared, g, b, alpha)


PROJ_ALIGN = 512


def _split_w_in(w, d):
    wdt = d // N_BRANCH
    gla_heads = wdt // GLA_DV
    names = (('gate', N_BRANCH * d), ('mla_cq', MLA_Q_RANK), ('mla_ckv', MLA_KV_RANK), ('mla_kr', MLA_ROPE),
             ('dsa_q', wdt), ('dsa_k', DSA_DH), ('dsa_v', DSA_DH),
             ('idx_q', IDX_HEADS * IDX_DIM), ('idx_k', IDX_DIM), ('idx_w', IDX_HEADS),
             ('gla_q', gla_heads * GLA_DK), ('gla_k', gla_heads * GLA_DK), ('gla_v', wdt),
             ('gla_a', GLA_GATE_RANK), ('gla_r', wdt),
             ('rwkv_r', wdt), ('rwkv_k', wdt), ('rwkv_v', wdt),
             ('rwkv_w', RWKV_LORA), ('rwkv_a', RWKV_LORA), ('rwkv_g', RWKV_LORA))
    parts, off = {}, 0
    for name, width in names:
        parts[name] = w[:, off:off + width]
        off += width
    assert off == w.shape[1], (off, w.shape)
    return parts


def _group(cols, dtype=BF16):
    wcat = jnp.concatenate(cols, axis=1)
    return _pad_cols(wcat, -(-wcat.shape[1] // PROJ_ALIGN) * PROJ_ALIGN).astype(dtype)


def kernel(x, positions, ln_in_g, ln_in_b, w_in, w_branch, w_out, mla_q_norm, mla_w_uq, mla_kv_norm, mla_w_ukv,
           gla_w_gate2, gla_b_gate, gla_norm_g, rwkv_mu_rkv, rwkv_mu_lora, rwkv_w0, rwkv_w2, rwkv_a0, rwkv_a2,
           rwkv_g2, rwkv_k_k, rwkv_k_a, rwkv_r_k, rwkv_lnx_g, rwkv_lnx_b, rwkv_v0, rwkv_v1, rwkv_v2,
           ln_mix_g, ln_mix_b, router_w, router_bias, exp_w_gate, exp_w_up, exp_w_down,
           sh_w_gate, sh_w_up, sh_w_down, ln_ffn_g, ln_ffn_b):
    batch, seq, d = x.shape
    depth = w_in.shape[0]
    t, wdt = batch * seq, d // N_BRANCH
    alpha = (2 * depth) ** 0.25
    ct, st = _rope_tables(positions)
    xf, xb = _layer_norm(x.reshape(t, d), (), ln_in_g, ln_in_b)
    v_first = None
    for l in range(depth):
        p = _split_w_in(w_in[l], d)
        pad128 = lambda w: _pad_cols(w, LANES)
        idx_q = jnp.pad(p['idx_q'].reshape(d, IDX_HEADS, IDX_DIM),
                        ((0, 0), (0, 0), (0, LANES - IDX_DIM))).reshape(d, IDX_HEADS * LANES)
        w_lat = _group([p['mla_cq'], p['mla_ckv'], pad128(p['mla_kr']), pad128(_swap_halves(p['mla_kr']))])
        dsa_q = p['dsa_q'] * (DSA_DH ** -0.5 * LOG2E)
        w_dsa = _group([idx_q, dsa_q, p['dsa_k'], p['dsa_v'], pad128(p['idx_k']), pad128(p['idx_w'])])
        w_gla = _group([p['gla_v'], p['gla_r'], p['gla_q'], p['gla_k'], pad128(p['gla_a'])])
        w_rwkv = _group([p['rwkv_r'], p['rwkv_k'], p['rwkv_v'], p['rwkv_w'], p['rwkv_a'], pad128(p['rwkv_g'])])
        lat = _matmul(xb, w_lat, F32)
        pb = _matmul(xb, w_dsa, BF16)
        pc = _matmul(xb, w_gla, F32)
        pd = _matmul(xb, w_rwkv, F32)

        y_mla = _mla(lat, ct, st, mla_q_norm[l], mla_w_uq[l], mla_kv_norm[l], mla_w_ukv[l], batch, seq)
        y_dsa = _dsa(pb, batch, seq, wdt)
        y_gla = _gla(pc, gla_w_gate2[l], gla_b_gate[l], gla_norm_g[l], batch, seq)
        vres = None if l == 0 else (rwkv_v0[l - 1], rwkv_v1[l - 1], rwkv_v2[l - 1])
        y_rwkv, v_first = _rwkv(pd, rwkv_mu_rkv[l], rwkv_mu_lora[l], rwkv_w0[l], rwkv_w2[l], rwkv_a0[l], rwkv_a2[l],
                                rwkv_g2[l], rwkv_k_k[l], rwkv_k_a[l], rwkv_r_k[l], rwkv_lnx_g[l], rwkv_lnx_b[l],
                                v_first, vres, batch, seq)

        merged = _merge(xb, (y_mla, y_dsa, y_gla, y_rwkv), p['gate'].astype(BF16), w_branch[l].astype(BF16))
        xf, xb, xpk = _layer_norm(xf, (_matmul(merged, w_out[l].astype(BF16), F32),), ln_mix_g[l], ln_mix_b[l], alpha,
                                  packed=True)

        ids, wts, rank, counts = _router(xb, router_w[l], router_bias[l])
        f_shared = _shared_expert(xb, sh_w_gate[l], sh_w_up[l], sh_w_down[l])
        xf, xb = _routed_ln(xf, xpk, ids, wts, rank, counts, exp_w_gate, exp_w_up, exp_w_down, l,
                            f_shared, ln_ffn_g[l], ln_ffn_b[l], alpha)
    return xf.reshape(batch, seq, d)
```

```python
import functools
import math

import jax
import jax.numpy as jnp
from jax import lax
from jax.experimental import pallas as pl
from jax.experimental.pallas import tpu as pltpu
from jax.experimental.pallas import tpu_sc as plsc

F32 = jnp.float32
BF16 = jnp.bfloat16

LANES = 128

MLA_NOPE, MLA_ROPE, MLA_V = 128, 64, 128
MLA_Q_RANK, MLA_KV_RANK = 768, 256
ROPE_THETA = 10000.0
DSA_DH = 128
IDX_HEADS, IDX_DIM, IDX_TOPK_MAX = 16, 64, 256
GLA_DV, GLA_DK, GLA_GATE_RANK, GLA_TAU, GLA_CHUNK = 256, 128, 16, 16.0, 64
GLA_SUB = 16
RWKV_HS, RWKV_LORA, RWKV_GN_EPS = 64, 64, 64e-5
RWKV_CHUNK = 64
TOP_K, ROUTED_SCALE = 8, 2.5
LN_EPS, RMS_EPS = 1e-5, 1e-6
NEG_BIG = -1e30
LOG2E = math.log2(math.e)

VMEM_LIMIT = 56 * 1024 * 1024


def _cparams(*sem):
    return pltpu.CompilerParams(dimension_semantics=sem, vmem_limit_bytes=VMEM_LIMIT)


def _tile(n, pref, unit=LANES):
    if n <= pref:
        return n
    t = (pref // unit) * unit
    while t > unit and n % t:
        t -= unit
    assert n % t == 0, (n, pref, unit)
    return t


def _dot(a, b):
    return jnp.dot(a, b, preferred_element_type=F32)


def _dot_t(a, b):
    return lax.dot_general(a, b, (((1,), (1,)), ((), ())), preferred_element_type=F32)


def _dot_f32(a, b):
    return jnp.dot(a, b, preferred_element_type=F32, precision=lax.Precision.HIGHEST)


def _masked_sums(mask, x):
    w = x.shape[1]
    hi = x.astype(BF16)
    rest = x - hi.astype(F32)
    mid = rest.astype(BF16)
    lo = (rest - mid.astype(F32)).astype(BF16)
    s = _dot(mask, jnp.concatenate([hi, mid, lo], axis=1))
    return s[:, :w] + s[:, w:2 * w] + s[:, 2 * w:]


def _mm_kernel(a_ref, b_ref, o_ref):
    o_ref[...] = _dot(a_ref[...], b_ref[...]).astype(o_ref.dtype)


def _matmul(a, b, out_dtype, tm=1024, tn=512):
    m, k = a.shape
    n = b.shape[1]
    tm, tn = _tile(m, tm, 8), _tile(n, tn)
    return pl.pallas_call(
        _mm_kernel,
        grid=(m // tm, n // tn),
        in_specs=[pl.BlockSpec((tm, k), lambda i, j: (i, 0)), pl.BlockSpec((k, tn), lambda i, j: (0, j))],
        out_specs=pl.BlockSpec((tm, tn), lambda i, j: (i, j)),
        out_shape=jax.ShapeDtypeStruct((m, n), out_dtype),
        compiler_params=_cparams("parallel", "parallel"),
        name="matmul",
    )(a, b)


def _pack_pairs(v):
    half = v.shape[1] // 2
    bits = lax.bitcast_convert_type(v.astype(BF16).astype(F32), jnp.int32)
    return (bits[:, half:] & jnp.int32(-65536)) | lax.shift_right_logical(bits[:, :half], 16)


def _unpack_pairs(w):
    return (lax.bitcast_convert_type(lax.shift_left(w, 16), F32),
            lax.bitcast_convert_type(w & jnp.int32(-65536), F32))


def _ln_kernel(*refs, alpha, n_res, packed):
    x_ref, res_refs = refs[0], refs[1:1 + n_res]
    g_ref, b_ref, o_ref, ob_ref = refs[1 + n_res:5 + n_res]
    x = x_ref[...]
    if n_res:
        x = alpha * x
        for f_ref in res_refs:
            x = x + f_ref[...]
    mu = jnp.mean(x, axis=-1, keepdims=True)
    xc = x - mu
    var = jnp.mean(xc * xc, axis=-1, keepdims=True)
    y = xc * lax.rsqrt(var + LN_EPS) * g_ref[...] + b_ref[...]
    o_ref[...] = y
    ob_ref[...] = y.astype(BF16)
    if packed:
        refs[-1][...] = _pack_pairs(y)


def _layer_norm(x, res, g, b, alpha=1.0, packed=False):
    t, d = x.shape
    tm = _tile(t, 128, 8)
    row = pl.BlockSpec((tm, d), lambda i: (i, 0))
    vec = pl.BlockSpec((1, d), lambda i: (0, 0))
    args = (x,) + tuple(res)
    extra_spec = [pl.BlockSpec((tm, d // 2), lambda i: (i, 0))] if packed else []
    extra_shape = [jax.ShapeDtypeStruct((t, d // 2), jnp.int32)] if packed else []
    return pl.pallas_call(
        functools.partial(_ln_kernel, alpha=alpha, n_res=len(res), packed=packed),
        grid=(t // tm,),
        in_specs=[row] * len(args) + [vec, vec],
        out_specs=[row, row] + extra_spec,
        out_shape=[jax.ShapeDtypeStruct((t, d), F32), jax.ShapeDtypeStruct((t, d), BF16)] + extra_shape,
        compiler_params=_cparams("parallel"),
        name="layer_norm",
    )(*args, g.reshape(1, d), b.reshape(1, d))


def _pad_cols(w, width):
    return jnp.pad(w, ((0, 0), (0, width - w.shape[1])))


def _rms(x, g):
    return x * lax.rsqrt(jnp.mean(x * x, axis=-1, keepdims=True) + RMS_EPS) * g


def _mla_prep_kernel(lat_ref, ct_ref, st_ref, qg_ref, kg_ref, wqn_ref, wqr_ref, wqs_ref, wkn_ref, wv_ref,
                     qn_ref, qr_ref, kn_ref, kr_ref, v_ref, *, heads, scale):
    lat = lat_ref[...]
    cq = lat[:, :MLA_Q_RANK]
    ckv = lat[:, MLA_Q_RANK:MLA_Q_RANK + MLA_KV_RANK]
    kr = lat[:, MLA_Q_RANK + MLA_KV_RANK:MLA_Q_RANK + MLA_KV_RANK + LANES]
    krs = lat[:, MLA_Q_RANK + MLA_KV_RANK + LANES:MLA_Q_RANK + MLA_KV_RANK + 2 * LANES]
    ct, st = ct_ref[...], st_ref[...]
    nq = _rms(cq, qg_ref[...]).astype(BF16)
    nkv = _rms(ckv, kg_ref[...]).astype(BF16)
    qn_ref[...] = (_dot(nq, wqn_ref[...]) * scale).astype(BF16)
    cth = jnp.concatenate([ct] * heads, axis=1)
    sth = jnp.concatenate([st] * heads, axis=1)
    qr_ref[...] = ((_dot(nq, wqr_ref[...]) * cth + _dot(nq, wqs_ref[...]) * sth) * scale).astype(BF16)
    kn_ref[...] = _dot(nkv, wkn_ref[...]).astype(BF16)
    v_ref[...] = _dot(nkv, wv_ref[...]).astype(BF16)
    kr_ref[...] = (kr * ct + krs * st).astype(BF16)


def _flash_kernel(qn_ref, qr_ref, kn_ref, kr_ref, v_ref, o_ref, m_ref, l_ref, acc_ref, *, tq, tk):
    i, j = pl.program_id(2), pl.program_id(3)

    @pl.when(j == 0)
    def _():
        m_ref[...] = jnp.full_like(m_ref, NEG_BIG)
        l_ref[...] = jnp.zeros_like(l_ref)
        acc_ref[...] = jnp.zeros_like(acc_ref)

    def step(masked):
        q = jnp.concatenate([qn_ref[...], qr_ref[...]], axis=1)
        k = jnp.concatenate([kn_ref[...], kr_ref[...]], axis=1)
        s = _dot_t(q, k)
        if masked:
            qpos = i * tq + lax.broadcasted_iota(jnp.int32, (tq, tk), 0)
            kpos = j * tk + lax.broadcasted_iota(jnp.int32, (tq, tk), 1)
            s = jnp.where(kpos <= qpos, s, -jnp.inf)
        m_prev = m_ref[...]
        m_new = jnp.maximum(m_prev, jnp.max(s, axis=1, keepdims=True))
        alpha = jnp.exp2(m_prev - m_new)
        p = jnp.exp2(s - jnp.concatenate([m_new] * (tk // LANES), axis=1))
        l_ref[...] = alpha * l_ref[...] + jnp.sum(p, axis=1, keepdims=True)
        acc_ref[...] = alpha * acc_ref[...] + _dot(p.astype(BF16), v_ref[...])
        m_ref[...] = m_new

    below = j * tk + tk - 1 <= i * tq
    touches = j * tk <= i * tq + tq - 1

    @pl.when(below)
    def _():
        step(False)

    @pl.when(jnp.logical_and(touches, jnp.logical_not(below)))
    def _():
        step(True)

    @pl.when(j == pl.num_programs(3) - 1)
    def _():
        o_ref[...] = (acc_ref[...] / l_ref[...]).astype(o_ref.dtype)


def _rope_tables(positions):
    half = MLA_ROPE // 2
    inv = ROPE_THETA ** (-jnp.arange(0, MLA_ROPE, 2, dtype=F32) / MLA_ROPE)
    ang = positions.reshape(-1).astype(F32)[:, None] * inv
    cos, sin = jnp.cos(ang), jnp.sin(ang)
    zero = jnp.zeros((ang.shape[0], LANES - 2 * half), F32)
    return jnp.concatenate([cos, cos, zero], 1), jnp.concatenate([-sin, sin, zero], 1)


def _swap_halves(w):
    half = w.shape[-1] // 2
    return jnp.concatenate([w[..., half:], w[..., :half]], -1)


def _mla(lat, ct, st, q_norm, w_uq, kv_norm, w_ukv, batch, seq):
    t = lat.shape[0]
    heads = w_uq.shape[1] // (MLA_NOPE + MLA_ROPE)
    hw = heads * LANES
    wq = w_uq.reshape(MLA_Q_RANK, heads, MLA_NOPE + MLA_ROPE)
    wqn = wq[:, :, :MLA_NOPE].reshape(MLA_Q_RANK, hw).astype(BF16)
    rope_pad = ((0, 0), (0, 0), (0, LANES - MLA_ROPE))
    wqr = jnp.pad(wq[:, :, MLA_NOPE:], rope_pad).reshape(MLA_Q_RANK, hw).astype(BF16)
    wqs = jnp.pad(_swap_halves(wq[:, :, MLA_NOPE:]), rope_pad).reshape(MLA_Q_RANK, hw).astype(BF16)
    wkv = w_ukv.reshape(MLA_KV_RANK, heads, MLA_NOPE + MLA_V)
    wkn = wkv[:, :, :MLA_NOPE].reshape(MLA_KV_RANK, hw).astype(BF16)
    wv = wkv[:, :, MLA_NOPE:].reshape(MLA_KV_RANK, hw).astype(BF16)

    tm = _tile(t, 512, 8)
    row = lambda w: pl.BlockSpec((tm, w), lambda i: (i, 0))
    full = lambda a: pl.BlockSpec(a.shape, lambda i: (0, 0))
    qg, kg = q_norm.reshape(1, -1), kv_norm.reshape(1, -1)
    qn, qr, kn, kr, v = pl.pallas_call(
        functools.partial(_mla_prep_kernel, heads=heads, scale=(MLA_NOPE + MLA_ROPE) ** -0.5 * LOG2E),
        grid=(t // tm,),
        in_specs=[row(lat.shape[1]), row(LANES), row(LANES), full(qg), full(kg),
                  full(wqn), full(wqr), full(wqs), full(wkn), full(wv)],
        out_specs=[row(hw), row(hw), row(hw), row(LANES), row(hw)],
        out_shape=[jax.ShapeDtypeStruct((t, hw), BF16)] * 3 + [jax.ShapeDtypeStruct((t, LANES), BF16),
                                                               jax.ShapeDtypeStruct((t, hw), BF16)],
        compiler_params=_cparams("parallel"),
        name="mla_prep",
    )(lat, ct, st, qg, kg, wqn, wqr, wqs, wkn, wv)

    tq = tk = _tile(seq, 1024, 8)
    nq, nk = seq // tq, seq // tk
    r3 = lambda a: a.reshape(batch, seq, a.shape[1])
    last = lambda i, j: jnp.minimum(j, ((i + 1) * tq - 1) // tk)
    q_spec = pl.BlockSpec((None, tq, LANES), lambda b, h, i, j: (b, i, h))
    k_spec = pl.BlockSpec((None, tk, LANES), lambda b, h, i, j: (b, last(i, j), h))
    kr_spec = pl.BlockSpec((None, tk, LANES), lambda b, h, i, j: (b, last(i, j), 0))
    out = pl.pallas_call(
        functools.partial(_flash_kernel, tq=tq, tk=tk),
        grid=(batch, heads, nq, nk),
        in_specs=[q_spec, q_spec, k_spec, kr_spec, k_spec],
        out_specs=q_spec,
        out_shape=jax.ShapeDtypeStruct((batch, seq, hw), BF16),
        scratch_shapes=[pltpu.VMEM((tq, LANES), F32), pltpu.VMEM((tq, LANES), F32), pltpu.VMEM((tq, LANES), F32)],
        compiler_params=_cparams("parallel", "parallel", "parallel", "arbitrary"),
        name="mla_flash",
    )(r3(qn), r3(qr), r3(kn), r3(kr), r3(v))
    return out.reshape(t, hw)


DSA_TQ = 256
DSA_TK = 512
INT_MIN = -2 ** 31


def _dsa_kernel(iq_ref, q_ref, k_ref, v_ref, ik_ref, iw_ref, o_ref, key_ref, m_ref, l_ref, acc_ref,
                *, tq, tk, n_sel, heads, pos_bits):
    i = pl.program_id(1)
    n_kt = (i * tq + tq - 1) // tk + 1
    reps = tk // LANES
    wide = lambda a: jnp.concatenate([a] * reps, axis=1)
    qpos = i * tq + lax.broadcasted_iota(jnp.int32, (tq, tk), 0)
    lane_pos = lax.broadcasted_iota(jnp.int32, (tq, tk), 1)

    iq = iq_ref[...]
    iw = iw_ref[...].astype(F32) * (IDX_HEADS ** -0.5 * IDX_DIM ** -0.5)
    iw_b = [jnp.broadcast_to(iw[:, h:h + 1], (tq, LANES)) for h in range(IDX_HEADS)]

    def score_tile(j, carry):
        ikj = ik_ref[pl.ds(pl.multiple_of(j * tk, tk), tk), :]
        sc = jnp.zeros((tq, tk), F32)
        for h in range(IDX_HEADS):
            logit = _dot_t(iq[:, h * LANES:(h + 1) * LANES], ikj)
            sc = sc + wide(iw_b[h]) * jnp.maximum(logit, 0.0)
        sc = jnp.where(j * tk + lane_pos <= qpos, sc + 0.0, -jnp.inf)
        bits = lax.bitcast_convert_type(sc, jnp.int32)
        key_ref[j] = jnp.where(bits >= 0, bits, bits ^ 0x7FFFFFFF)
        return carry

    lax.fori_loop(0, n_kt, score_tile, 0)

    def count(pred):
        def body(j, acc):
            c = pred(key_ref[j], j * tk + lane_pos).astype(jnp.int32)
            for rep in range(reps):
                acc = acc + c[:, rep * LANES:(rep + 1) * LANES]
            return acc
        acc = lax.fori_loop(0, n_kt, body, jnp.zeros((tq, LANES), jnp.int32))
        return jnp.broadcast_to(jnp.sum(acc, axis=1, keepdims=True), (tq, LANES))

    def thr_bit(bit_i, thr):
        cand = thr ^ jnp.left_shift(jnp.int32(1), 31 - bit_i)
        cnt = count(lambda key, pos: key >= wide(cand))
        return jnp.where(cnt >= n_sel, cand, thr)

    thr = lax.fori_loop(0, 32, thr_bit, jnp.full((tq, LANES), INT_MIN, jnp.int32))
    cnt_gt = count(lambda key, pos: key > wide(thr))
    cnt_ge = count(lambda key, pos: key >= wide(thr))
    need = n_sel - cnt_gt

    def tie_bit(bit_i, cut):
        cand = cut | jnp.left_shift(jnp.int32(1), pos_bits - 1 - bit_i)
        cnt = count(lambda key, pos: jnp.logical_and(key == wide(thr), pos < wide(cand)))
        return jnp.where(cnt < need, cand, cut)

    surplus = jnp.max(cnt_ge - cnt_gt - need) > 0
    cut = lax.cond(surplus,
                   lambda: lax.fori_loop(0, pos_bits, tie_bit, jnp.zeros((tq, LANES), jnp.int32)),
                   lambda: jnp.full((tq, LANES), 2 ** 31 - 1, jnp.int32))

    q = q_ref[...]
    q_all = jnp.concatenate([q[:, h * LANES:(h + 1) * LANES] for h in range(heads)], axis=0)
    m_ref[...] = jnp.full_like(m_ref, NEG_BIG)
    l_ref[...] = jnp.zeros_like(l_ref)
    acc_ref[...] = jnp.zeros_like(acc_ref)

    def attend(j, carry):
        rows = pl.ds(pl.multiple_of(j * tk, tk), tk)
        s = _dot_t(q_all, k_ref[rows, :])
        key, pos = key_ref[j], j * tk + lane_pos
        keep = jnp.where(key > wide(thr), 0.0,
                         jnp.where(jnp.logical_and(key == wide(thr), pos <= wide(cut)), 0.0, -jnp.inf))
        keep = jnp.where(pos <= qpos, keep, -jnp.inf)
        s = s + jnp.concatenate([keep] * heads, axis=0)
        m_prev = m_ref[...]
        m_new = jnp.maximum(m_prev, jnp.max(s, axis=1, keepdims=True))
        alpha = jnp.exp2(m_prev - m_new)
        p = jnp.exp2(s - jnp.concatenate([m_new] * reps, axis=1))
        l_ref[...] = alpha * l_ref[...] + jnp.sum(p, axis=1, keepdims=True)
        acc_ref[...] = alpha * acc_ref[...] + _dot(p.astype(BF16), v_ref[rows, :])
        m_ref[...] = m_new
        return carry

    lax.fori_loop(0, n_kt, attend, 0)
    out = acc_ref[...] / l_ref[...]
    o_ref[...] = jnp.concatenate([out[h * tq:(h + 1) * tq] for h in range(heads)], axis=1).astype(o_ref.dtype)


def _dsa(pb, batch, seq, width):
    t, wb = pb.shape
    heads = width // DSA_DH
    iqw = IDX_HEADS * LANES
    tq, tk = _tile(seq, DSA_TQ, 8), _tile(seq, DSA_TK)
    n_sel = min(IDX_TOPK_MAX, seq // 4)
    c0 = (iqw + width) // LANES
    pb3 = pb.reshape(batch, seq, wb)
    qblk = lambda w, idx: pl.BlockSpec((None, tq, w), lambda b, i: (b, i, idx))
    seqblk = lambda idx: pl.BlockSpec((None, seq, LANES), lambda b, i: (b, 0, idx))
    out = pl.pallas_call(
        functools.partial(_dsa_kernel, tq=tq, tk=tk, n_sel=n_sel, heads=heads,
                          pos_bits=max(1, (seq - 1).bit_length())),
        grid=(batch, seq // tq),
        in_specs=[qblk(iqw, 0), qblk(width, iqw // width), seqblk(c0), seqblk(c0 + 1), seqblk(c0 + 2),
                  qblk(LANES, c0 + 3)],
        out_specs=qblk(width, 0),
        out_shape=jax.ShapeDtypeStruct((batch, seq, width), BF16),
        scratch_shapes=[pltpu.VMEM((seq // tk, tq, tk), jnp.int32), pltpu.VMEM((heads * tq, LANES), F32),
                        pltpu.VMEM((heads * tq, LANES), F32), pltpu.VMEM((heads * tq, LANES), F32)],
        compiler_params=_cparams("parallel", "arbitrary"),
        name="dsa",
    )(pb3, pb3, pb3, pb3, pb3, pb3)
    return out.reshape(t, width)


def _gla_kernel(q_ref, k_ref, v_ref, a_ref, r_ref, wg_ref, bg_ref, ng_ref, o_ref, state_ref):
    c_len, sub = GLA_CHUNK, GLA_SUB

    @pl.when(pl.program_id(2) == 0)
    def _():
        state_ref[...] = jnp.zeros_like(state_ref)

    z_all = _dot(a_ref[...].astype(BF16), wg_ref[...]) + bg_ref[...]
    g_all = jax.nn.log_sigmoid(z_all) / GLA_TAU
    ri = lax.broadcasted_iota(jnp.int32, (c_len, c_len), 0)
    ci = lax.broadcasted_iota(jnp.int32, (c_len, c_len), 1)
    b_all = _masked_sums((ri >= ci).astype(BF16), g_all)
    ones = jnp.ones((GLA_DK, LANES), BF16)
    trow = lax.broadcasted_iota(jnp.int32, (sub, GLA_DK), 0)
    lane = lax.broadcasted_iota(jnp.int32, (sub, LANES), 1)
    for hh in range(state_ref.shape[0]):
        ksl = slice(hh * GLA_DK, (hh + 1) * GLA_DK)
        vsl = slice(hh * GLA_DV, (hh + 1) * GLA_DV)
        o = _gla_head(q_ref[:, ksl] * GLA_DK ** -0.5, k_ref[:, ksl], v_ref[:, vsl].astype(BF16), b_all[:, ksl],
                      state_ref.at[hh], ones, trow, lane)
        r = r_ref[:, vsl]
        o_ref[:, vsl] = (_rms(o, ng_ref[:, vsl]) * (r * jax.nn.sigmoid(r))).astype(o_ref.dtype)


def _gla_head(q, k, vb, b, state_ref, ones, trow, lane):
    c_len, sub = GLA_CHUNK, GLA_SUB
    state_t = state_ref[...]
    o_inter = _dot_t((q * jnp.exp(b)).astype(BF16), state_t.astype(BF16))
    outs = []
    for blk in range(c_len // sub):
        lo = blk * sub
        qi, bi, ki = q[lo:lo + sub], b[lo:lo + sub], k[lo:lo + sub]
        rows = [qi * ki[s:s + 1] * jnp.exp(jnp.where(trow >= s, bi - bi[s:s + 1], -jnp.inf)) for s in range(sub)]
        prod = jnp.concatenate(rows, axis=0)
        p_hi = prod.astype(BF16)
        p_lo = (prod - p_hi.astype(F32)).astype(BF16)
        sums = _dot(jnp.concatenate([p_hi, p_lo], axis=0), ones)
        sums = sums[:sub * sub] + sums[sub * sub:]
        attn = jnp.zeros((sub, LANES), F32)
        for s in range(sub):
            attn = attn + jnp.where(lane == s, sums[s * sub:(s + 1) * sub], 0.0)
        o_blk = _dot(attn[:, :sub].astype(BF16), vb[lo:lo + sub])
        if blk:
            qa = qi * jnp.exp(bi - bi[0:1])
            ka = k[:lo] * jnp.exp(bi[0:1] - b[:lo])
            o_blk = o_blk + _dot(_dot_t(qa.astype(BF16), ka.astype(BF16)).astype(BF16), vb[:lo])
        outs.append(o_blk)
    o = o_inter + jnp.concatenate(outs, axis=0)

    b_last = b[c_len - 1:c_len]
    k_dec = (k * jnp.exp(b_last - b)).astype(BF16)
    state_ref[...] = state_t * jnp.exp(b_last) + lax.dot_general(
        vb, k_dec, (((0,), (0,)), ((), ())), preferred_element_type=F32)
    return o


GLA_HEADS_PER_STEP = 4


def _gla(pc, w_gate2, b_gate, norm_g, batch, seq):
    t = pc.shape[0]
    heads = norm_g.shape[0] // GLA_DV
    hps = math.gcd(heads, GLA_HEADS_PER_STEP)
    groups = heads // hps
    wg = jnp.pad(w_gate2, ((0, LANES - GLA_GATE_RANK), (0, 0))).astype(BF16)
    nc = seq // GLA_CHUNK
    blk = lambda w, off: pl.BlockSpec((None, GLA_CHUNK, hps * w), lambda b, g, c: (b, c, off + g))
    par = lambda rows, w: pl.BlockSpec((rows, hps * w), lambda b, g, c: (0, g))
    pc3 = pc.reshape(batch, seq, pc.shape[1])
    out = pl.pallas_call(
        _gla_kernel,
        grid=(batch, groups, nc),
        in_specs=[blk(GLA_DK, 4 * groups), blk(GLA_DK, 5 * groups), blk(GLA_DV, 0),
                  pl.BlockSpec((None, GLA_CHUNK, LANES), lambda b, g, c: (b, c, 6 * heads)),
                  blk(GLA_DV, groups), par(LANES, GLA_DK), par(1, GLA_DK), par(1, GLA_DV)],
        out_specs=blk(GLA_DV, 0),
        out_shape=jax.ShapeDtypeStruct((batch, seq, heads * GLA_DV), BF16),
        scratch_shapes=[pltpu.VMEM((hps, GLA_DV, GLA_DK), F32)],
        compiler_params=_cparams("parallel", "parallel", "arbitrary"),
        name="gla",
    )(pc3, pc3, pc3, pc3, pc3, wg, b_gate.reshape(1, -1), norm_g.reshape(1, -1))
    return out.reshape(t, heads * GLA_DV)


_DIMS = {'nn': (((1,), (0,)), ((), ())), 'nt': (((1,), (1,)), ((), ())), 'tn': (((0,), (0,)), ((), ()))}


def _mm(a, b, form):
    return lax.dot_general(a.astype(BF16), b.astype(BF16), _DIMS[form], preferred_element_type=F32)


def _seg_sum(x, width):
    gi = lax.broadcasted_iota(jnp.int32, (LANES, LANES), 0) // width
    gj = lax.broadcasted_iota(jnp.int32, (LANES, LANES), 1) // width
    ones = (gi == gj).astype(F32)
    return jnp.concatenate([_dot_f32(x[:, c:c + LANES], ones) for c in range(0, x.shape[1], LANES)], axis=1)


def _rwkv_prep_kernel(*refs, width, seq, has_vres):
    if has_vres:
        (p_ref, pp_ref, prm_ref, ml_ref, w2_ref, a2_ref, g2_ref, vf_ref, v0_ref, v1_ref, v2_ref,
         r_ref, lw_ref, k_ref, v_ref, kk_ref, bb_ref, bonus_ref, g_ref) = refs
    else:
        (p_ref, pp_ref, prm_ref, ml_ref, w2_ref, a2_ref, g2_ref,
         r_ref, lw_ref, k_ref, v_ref, kk_ref, bb_ref, bonus_ref, g_ref) = refs
    w = width
    p, prm, ml = p_ref[...], prm_ref[...], ml_ref[...]
    tm = p.shape[0]
    above = jnp.where((pl.program_id(0) * tm) % seq == 0, 0.0, pp_ref[7:8, :])
    first = lax.broadcasted_iota(jnp.int32, p.shape, 0) == 0
    pp = jnp.where(first, jnp.broadcast_to(above, p.shape), pltpu.roll(p, 1, axis=0))
    lerp = lambda lo, hi, mu: p[:, lo:hi] + (pp[:, lo:hi] - p[:, lo:hi]) * mu
    r = lerp(0, w, prm[0:1])
    k = lerp(w, 2 * w, prm[1:2])
    v = lerp(2 * w, 3 * w, prm[2:3])
    x_wa = lerp(3 * w, 3 * w + LANES, ml[0:1])
    x_g = lerp(3 * w + LANES, 3 * w + 2 * LANES, ml[1:2])
    w_log = -jax.nn.softplus(-(prm[3:4] + _dot(jnp.tanh(x_wa).astype(BF16), w2_ref[...]))) - 0.5
    lw_ref[...] = -jnp.exp(w_log)
    a = jax.nn.sigmoid(prm[4:5] + _dot(x_wa.astype(BF16), a2_ref[...]))
    g_ref[...] = _dot(jax.nn.sigmoid(x_g).astype(BF16), g2_ref[...])
    if has_vres:
        mix = jax.nn.sigmoid(v0_ref[...] + _dot(_dot(v.astype(BF16), v1_ref[...]).astype(BF16), v2_ref[...]))
        v = v + (vf_ref[...] - v) * mix
    kk = k * prm[5:6]
    kk = kk * lax.rsqrt(jnp.maximum(_seg_sum(kk * kk, RWKV_HS), 1e-24))
    k2 = k * (1.0 + (a - 1.0) * prm[6:7])
    r_ref[...] = r
    k_ref[...] = k2
    v_ref[...] = v
    kk_ref[...] = kk
    bb_ref[...] = kk * a
    bonus_ref[...] = _seg_sum(r * k2 * prm[7:8], RWKV_HS) * v


RWKV_HEADS_PER_STEP = 8


def _rwkv_chunk_kernel(r_ref, lw_ref, k_ref, v_ref, kk_ref, bb_ref, y_ref, s_ref, *, n_chunk):
    c_len, hs = RWKV_CHUNK, RWKV_HS
    rows_n = n_chunk * c_len

    @pl.when(pl.program_id(2) == 0)
    def _():
        s_ref[...] = jnp.zeros_like(s_ref)

    ri = lax.broadcasted_iota(jnp.int32, (rows_n, rows_n), 0)
    ci = lax.broadcasted_iota(jnp.int32, (rows_n, rows_n), 1)
    same = (ri // c_len) == (ci // c_len)
    incl = jnp.logical_and(same, ci <= ri)
    strict = jnp.logical_and(same, ci < ri)
    eye = (ri == ci).astype(F32)
    e_r = lax.broadcasted_iota(jnp.int32, (hs, hs), 0)
    e_c = lax.broadcasted_iota(jnp.int32, (hs, hs), 1)
    sums = _masked_sums(jnp.concatenate([incl, same], axis=0).astype(BF16), lw_ref[...])
    cum_all, cum_c_all = sums[:rows_n], sums[rows_n:]
    heads = range(s_ref.shape[0])
    hd = []
    for hh in heads:
        sl = slice(hh * hs, (hh + 1) * hs)
        r, lw, k, v, kk, bb = (ref[...][:, sl] for ref in (r_ref, lw_ref, k_ref, v_ref, kk_ref, bb_ref))
        cum, cum_c = cum_all[:, sl], cum_c_all[:, sl]
        g_inv, g_end = jnp.exp(-cum), jnp.exp(cum_c - cum)
        kap, rt = kk * jnp.exp(cum - lw), r * jnp.exp(cum)
        bet, kt = bb * g_inv, k * g_inv
        gram = _mm(jnp.concatenate([kap, rt], axis=0), jnp.concatenate([bet, kt], axis=0), 'nt')
        hd.append(dict(v=v, kap=kap, rt=rt, bet_c=bb * g_end, kt_c=k * g_end, cum_c=cum_c,
                       n_m=jnp.where(strict, gram[:rows_n, :rows_n], 0.0),
                       a_kk=jnp.where(strict, gram[:rows_n, rows_n:], 0.0),
                       a_rb=jnp.where(incl, gram[rows_n:, :rows_n], 0.0),
                       a_rk=jnp.where(incl, gram[rows_n:, rows_n:], 0.0)))
    t_inv = [eye - h['n_m'] for h in hd]
    pw = [-h['n_m'] for h in hd]
    for _ in range(int(math.log2(c_len)) - 1):
        pw = [_mm(p, p, 'nn') for p in pw]
        t_inv = [t + _mm(t, p, 'nn') for t, p in zip(t_inv, pw)]
    av = [_mm(jnp.concatenate([h['a_kk'], h['a_rk']], axis=0), h['v'], 'nn') for h in hd]
    z = [_mm(t, jnp.concatenate([h['kap'], a[:rows_n]], axis=1), 'nn') for t, h, a in zip(t_inv, hd, av)]
    az = [_mm(h['a_rb'], zz, 'nn') for h, zz in zip(hd, z)]
    r_p = [h['rt'] - a[:, :hs] for h, a in zip(hd, az)]
    y0 = [a[rows_n:] - b[:, hs:] for a, b in zip(av, az)]
    s = [s_ref[hh] for hh in heads]
    outs = [[] for _ in heads]
    for c in range(n_chunk):
        rows = slice(c * c_len, (c + 1) * c_len)
        for hh in heads:
            h = hd[hh]
            outs[hh].append(_mm(r_p[hh][rows], s[hh], 'nt') + y0[hh][rows])
            zb = _mm(z[hh][rows], h['bet_c'][rows], 'tn')
            g_chunk = jnp.exp(h['cum_c'][c * c_len:c * c_len + 1])
            m = jnp.where(e_r == e_c, jnp.broadcast_to(g_chunk, (hs, hs)), 0.0) - zb[:hs]
            s[hh] = _mm(s[hh], m, 'nn') + _mm(h['v'][rows], h['kt_c'][rows], 'tn') - zb[hs:]
    for hh in heads:
        s_ref[hh] = s[hh]
    y_ref[...] = jnp.concatenate([jnp.concatenate(o, axis=0) for o in outs], axis=1)


def _rwkv_post_kernel(y_ref, bonus_ref, g_ref, lg_ref, lb_ref, o_ref):
    y = y_ref[...]
    mu = _seg_sum(y, RWKV_HS) * (1.0 / RWKV_HS)
    yc = y - mu
    var = _seg_sum(yc * yc, RWKV_HS) * (1.0 / RWKV_HS)
    yn = yc * lax.rsqrt(var + RWKV_GN_EPS) * lg_ref[...] + lb_ref[...]
    o_ref[...] = ((yn + bonus_ref[...]) * g_ref[...]).astype(o_ref.dtype)


def _rwkv(pd, mu_rkv, mu_lora, w0, w2, a0, a2, g2, k_k, k_a, r_k, lnx_g, lnx_b, v_first, vres, batch, seq):
    t, wd = pd.shape
    w = w0.shape[0]
    prm = jnp.stack([mu_rkv[0], mu_rkv[1], mu_rkv[2], w0, a0, k_k, k_a, r_k.reshape(-1)])
    zl = jnp.zeros((RWKV_LORA,), F32)
    ml = jnp.stack([jnp.concatenate([mu_lora[0], mu_lora[1]]), jnp.concatenate([mu_lora[2], zl])])
    zw = jnp.zeros((LANES - RWKV_LORA, w), F32)
    w2p = jnp.concatenate([w2, zw]).astype(BF16)
    a2p = jnp.concatenate([zw, a2]).astype(BF16)
    g2p = jnp.concatenate([g2, zw]).astype(BF16)
    has_vres = vres is not None
    tm = _tile(seq, 256, 8)
    row = lambda width: pl.BlockSpec((tm, width), lambda i: (i, 0))
    full = lambda a: pl.BlockSpec(a.shape, lambda i: (0, 0))
    above = pl.BlockSpec((8, wd), lambda i: (jnp.maximum(i * (tm // 8) - 1, 0), 0))
    args = [pd, pd, prm, ml, w2p, a2p, g2p]
    specs = [row(wd), above, full(prm), full(ml), full(w2p), full(a2p), full(g2p)]
    if has_vres:
        v0, v1, v2 = vres
        v1p = _pad_cols(v1, LANES).astype(BF16)
        v2p = jnp.pad(v2, ((0, LANES - v2.shape[0]), (0, 0))).astype(BF16)
        v0r = v0.reshape(1, w)
        args += [v_first, v0r, v1p, v2p]
        specs += [row(w), full(v0r), full(v1p), full(v2p)]
    r, lw, k2, v, kk, bb, bonus, g = pl.pallas_call(
        functools.partial(_rwkv_prep_kernel, width=w, seq=seq, has_vres=has_vres),
        grid=(t // tm,),
        in_specs=specs,
        out_specs=[row(w)] * 8,
        out_shape=[jax.ShapeDtypeStruct((t, w), F32)] * 8,
        compiler_params=_cparams("parallel"),
        name="rwkv_prep",
    )(*args)

    n_chunk = 4 if seq % (4 * RWKV_CHUNK) == 0 else 1
    rows_n = n_chunk * RWKV_CHUNK
    bw = math.gcd(w, RWKV_HEADS_PER_STEP * RWKV_HS)
    blk = pl.BlockSpec((None, rows_n, bw), lambda b, h, c: (b, c, h))
    r3 = lambda a: a.reshape(batch, seq, w)
    y = pl.pallas_call(
        functools.partial(_rwkv_chunk_kernel, n_chunk=n_chunk),
        grid=(batch, w // bw, seq // rows_n),
        in_specs=[blk] * 6,
        out_specs=blk,
        out_shape=jax.ShapeDtypeStruct((batch, seq, w), F32),
        scratch_shapes=[pltpu.VMEM((bw // RWKV_HS, RWKV_HS, RWKV_HS), F32)],
        compiler_params=_cparams("parallel", "parallel", "arbitrary"),
        name="rwkv_chunk",
    )(r3(r), r3(lw), r3(k2), r3(v), r3(kk), r3(bb))

    lg, lb = lnx_g.reshape(1, w), lnx_b.reshape(1, w)
    out = pl.pallas_call(
        _rwkv_post_kernel,
        grid=(t // tm,),
        in_specs=[row(w), row(w), row(w), full(lg), full(lb)],
        out_specs=row(w),
        out_shape=jax.ShapeDtypeStruct((t, w), BF16),
        compiler_params=_cparams("parallel"),
        name="rwkv_post",
    )(y.reshape(t, w), bonus, g, lg, lb)
    return out, (v_first if has_vres else v)


N_BRANCH = 4


def _merge_kernel(*refs):
    x_ref = refs[0]
    y_refs = refs[1:1 + N_BRANCH]
    wg_refs = refs[1 + N_BRANCH:1 + 2 * N_BRANCH]
    wb_refs = refs[1 + 2 * N_BRANCH:1 + 3 * N_BRANCH]
    o_ref = refs[-1]
    x = x_ref[...]
    acc = None
    for y_ref, wg_ref, wb_ref in zip(y_refs, wg_refs, wb_refs):
        term = jax.nn.sigmoid(_dot(x, wg_ref[...])) * _dot(y_ref[...], wb_ref[...])
        acc = term if acc is None else acc + term
    o_ref[...] = acc.astype(o_ref.dtype)


def _merge(xb, ys, w_gate, w_branch):
    t, d = xb.shape
    w = ys[0].shape[1]
    tm, tn = _tile(t, 512, 8), _tile(d, 256)
    nj = d // tn
    gate_spec = lambda i: pl.BlockSpec((d, tn), lambda r, j: (0, i * nj + j))
    br_spec = lambda i: pl.BlockSpec((None, w, tn), lambda r, j: (i, 0, j))
    return pl.pallas_call(
        _merge_kernel,
        grid=(t // tm, nj),
        in_specs=[pl.BlockSpec((tm, d), lambda r, j: (r, 0))] + [pl.BlockSpec((tm, w), lambda r, j: (r, 0))] * N_BRANCH
        + [gate_spec(i) for i in range(N_BRANCH)] + [br_spec(i) for i in range(N_BRANCH)],
        out_specs=pl.BlockSpec((tm, tn), lambda r, j: (r, j)),
        out_shape=jax.ShapeDtypeStruct((t, d), BF16),
        compiler_params=_cparams("parallel", "parallel"),
        name="merge",
    )(xb, *ys, *([w_gate] * N_BRANCH), *([w_branch] * N_BRANCH))


def _router_kernel(x_ref, w_ref, b_ref, ids_ref, wts_ref, rank_ref, cnt_ref, run_ref, *, n_exp):
    tm = x_ref.shape[0]

    @pl.when(pl.program_id(0) == 0)
    def _():
        run_ref[...] = jnp.zeros_like(run_ref)

    scores = jax.nn.sigmoid(_dot(x_ref[...], w_ref[...]))
    lane = lax.broadcasted_iota(jnp.int32, (tm, LANES), 1)
    work = jnp.where(lane < n_exp, scores + b_ref[...], -jnp.inf)
    chosen = jnp.zeros((tm, LANES), F32)
    sel = jnp.zeros((tm, LANES), F32)
    ids = jnp.zeros((tm, LANES), jnp.int32)
    wts = jnp.zeros((tm, LANES), F32)
    for it in range(TOP_K):
        best = jnp.max(work, axis=1, keepdims=True)
        first = jnp.min(jnp.where(work == best, lane, LANES), axis=1, keepdims=True)
        hit = lane == first
        chosen = jnp.where(hit, scores, chosen)
        sel = jnp.where(hit, 1.0, sel)
        ids = jnp.where(lane == it, first, ids)
        wts = jnp.where(lane == it, jnp.sum(jnp.where(hit, scores, 0.0), axis=1, keepdims=True), wts)
        work = jnp.where(hit, -jnp.inf, work)
    ids_ref[...] = ids
    wts_ref[...] = wts * (ROUTED_SCALE / jnp.sum(chosen, axis=1, keepdims=True))
    ri = lax.broadcasted_iota(jnp.int32, (tm, tm), 0)
    ci = lax.broadcasted_iota(jnp.int32, (tm, tm), 1)
    before = _dot((ci < ri).astype(BF16), sel.astype(BF16))
    run = run_ref[0:1, :]
    rank_ref[...] = (before + run).astype(jnp.int32)
    run = run + jnp.sum(sel, axis=0, keepdims=True)
    run_ref[...] = jnp.broadcast_to(run, run_ref.shape)
    cnt_ref[...] = jnp.broadcast_to(run, cnt_ref.shape).astype(jnp.int32)


def _router(xb, router_w, router_bias):
    t, d = xb.shape
    n_exp = router_w.shape[1]
    assert n_exp < LANES
    tm = _tile(t, 512, 8)
    wr = _pad_cols(router_w, LANES).astype(BF16)
    br = _pad_cols(router_bias.reshape(1, n_exp), LANES)
    tok = pl.BlockSpec((tm, LANES), lambda i: (i, 0))
    tok_shape = lambda dt: jax.ShapeDtypeStruct((t, LANES), dt)
    return pl.pallas_call(
        functools.partial(_router_kernel, n_exp=n_exp),
        grid=(t // tm,),
        in_specs=[pl.BlockSpec((tm, d), lambda i: (i, 0)), pl.BlockSpec((d, LANES), lambda i: (0, 0)),
                  pl.BlockSpec((1, LANES), lambda i: (0, 0))],
        out_specs=[tok, tok, tok, pl.BlockSpec((8, LANES), lambda i: (0, 0))],
        out_shape=[tok_shape(jnp.int32), tok_shape(F32), tok_shape(jnp.int32),
                   jax.ShapeDtypeStruct((8, LANES), jnp.int32)],
        scratch_shapes=[pltpu.VMEM((8, LANES), F32)],
        compiler_params=_cparams("arbitrary"),
        name="router",
    )(xb, wr, br)


def _shared_kernel(x_ref, wgu_ref, wd_ref, o_ref, *, hid):
    gu = _dot(x_ref[...], wgu_ref[...])
    gate = gu[:, :hid]
    h = (gate * jax.nn.sigmoid(gate) * gu[:, hid:]).astype(BF16)
    o_ref[...] = _dot(h, wd_ref[...])


def _shared_expert(xb, w_gate, w_up, w_down):
    t, d = xb.shape
    hid = w_gate.shape[1]
    wgu = jnp.concatenate([w_gate, w_up], axis=1).astype(BF16)
    wd = w_down.astype(BF16)
    tm = _tile(t, 512, 8)
    full = lambda a: pl.BlockSpec(a.shape, lambda i: (0, 0))
    return pl.pallas_call(
        functools.partial(_shared_kernel, hid=hid),
        grid=(t // tm,),
        in_specs=[pl.BlockSpec((tm, d), lambda i: (i, 0)), full(wgu), full(wd)],
        out_specs=pl.BlockSpec((tm, d), lambda i: (i, 0)),
        out_shape=jax.ShapeDtypeStruct((t, d), F32),
        compiler_params=_cparams("parallel"),
        name="shared_expert",
    )(xb, wgu, wd)


MOE_ROWS = 512
SC_WINDOW = 16


def _sc_mesh():
    return plsc.VectorSubcoreMesh(core_axis_name="core", subcore_axis_name="subcore")


def _sc_windows(n_items):
    mesh = _sc_mesh()
    workers = mesh.num_cores * mesh.num_subcores
    n_win = n_items // LANES
    assert n_items % LANES == 0
    return mesh, n_win, -(-n_win // workers)


def _sc_worker(mesh):
    return lax.axis_index("core") * mesh.num_subcores + lax.axis_index("subcore")


def _sc_scatter_rows(x, pos, n_rows):
    t, d = x.shape
    k = pos.shape[0]
    mesh, n_win, per = _sc_windows(t)

    n_sub = LANES // SC_WINDOW

    @pl.kernel(out_type=jax.ShapeDtypeStruct((n_rows, d), x.dtype), mesh=mesh,
               scratch_types=[pltpu.VMEM((k, LANES), jnp.int32), pltpu.VMEM((2, SC_WINDOW, d), x.dtype),
                              pltpu.SemaphoreType.DMA, pltpu.SemaphoreType.DMA((2,))])
    def scatter(x_hbm, pos_hbm, o_hbm, idx_vmem, buf, sem_in, sem_out):
        first = _sc_worker(mesh) * per

        @pl.loop(0, per)
        def _(step):
            win = first + step

            @pl.when(win < n_win)
            def _():
                base = win * LANES
                pltpu.sync_copy(pos_hbm.at[:, pl.ds(base, LANES)], idx_vmem)
                load = lambda j: pltpu.make_async_copy(
                    x_hbm.at[pl.ds(base + j * SC_WINDOW, SC_WINDOW)], buf.at[j % 2], sem_in)
                store = lambda j, kk: pltpu.make_async_copy(
                    buf.at[j % 2], o_hbm.at[idx_vmem[kk, pl.ds(j * SC_WINDOW, SC_WINDOW)]], sem_out.at[j % 2])
                load(0).start()
                for j in range(n_sub):
                    load(j).wait()
                    if j >= 1:
                        for kk in range(k):
                            store(j - 1, kk).wait()
                    if j + 1 < n_sub:
                        load(j + 1).start()
                    for kk in range(k):
                        store(j, kk).start()
                for kk in range(k):
                    store(n_sub - 1, kk).wait()

    return scatter(x, pos)


def _sc_gather_rows(y, idx):
    m = idx.shape[1]
    d = y.shape[1]
    mesh, n_win, per = _sc_windows(m)

    n_sub = LANES // SC_WINDOW

    @pl.kernel(out_type=jax.ShapeDtypeStruct((m, d), y.dtype), mesh=mesh,
               scratch_types=[pltpu.VMEM((1, LANES), jnp.int32), pltpu.VMEM((2, SC_WINDOW, d), y.dtype),
                              pltpu.SemaphoreType.DMA, pltpu.SemaphoreType.DMA((2,))])
    def gather(y_hbm, i_hbm, o_hbm, idx_vmem, buf, sem_in, sem_out):
        first = _sc_worker(mesh) * per

        @pl.loop(0, per)
        def _(step):
            win = first + step

            @pl.when(win < n_win)
            def _():
                base = win * LANES
                pltpu.sync_copy(i_hbm.at[:, pl.ds(base, LANES)], idx_vmem)
                load = lambda j: pltpu.make_async_copy(
                    y_hbm.at[idx_vmem[0, pl.ds(j * SC_WINDOW, SC_WINDOW)]], buf.at[j % 2], sem_in)
                store = lambda j: pltpu.make_async_copy(
                    buf.at[j % 2], o_hbm.at[pl.ds(base + j * SC_WINDOW, SC_WINDOW)], sem_out.at[j % 2])
                load(0).start()
                for j in range(n_sub):
                    load(j).wait()
                    if j >= 1:
                        store(j - 1).wait()
                    if j + 1 < n_sub:
                        load(j + 1).start()
                    store(j).start()
                store(n_sub - 1).wait()

    return gather(y, idx)


def _grouped_kernel(te_ref, nu_ref, x_ref, wg_ref, wu_ref, wd_ref, o_ref, wgu_bf, wd_bf, *, hid):
    i = pl.program_id(0)

    @pl.when(jnp.logical_or(i == 0, te_ref[i] != te_ref[jnp.maximum(i - 1, 0)]))
    def _():
        wgu_bf[:, :hid] = wg_ref[...].astype(BF16)
        wgu_bf[:, hid:] = wu_ref[...].astype(BF16)
        wd_bf[...] = wd_ref[...].astype(BF16)

    @pl.when(i < nu_ref[0])
    def _():
        lo, hi = _unpack_pairs(x_ref[...])
        x = jnp.concatenate([lo.astype(BF16), hi.astype(BF16)], axis=1)
        gu = _dot(x, wgu_bf[...])
        gate = gu[:, :hid]
        h = (gate * jax.nn.sigmoid(gate) * gu[:, hid:]).astype(BF16)
        o_ref[...] = _pack_pairs(_dot(h, wd_bf[...]))


def _grouped_swiglu(xs, tile_expert, n_used, w_gate, w_up, w_down, layer):
    n, half = xs.shape
    d = 2 * half
    hid = w_down.shape[2]
    once = pl.Buffered(1)
    grid_spec = pltpu.PrefetchScalarGridSpec(
        num_scalar_prefetch=2,
        grid=(n // MOE_ROWS,),
        in_specs=[pl.BlockSpec((MOE_ROWS, half), lambda i, te, nu: (i, 0)),
                  pl.BlockSpec((None, None, d, hid), lambda i, te, nu: (layer, te[i], 0, 0), pipeline_mode=once),
                  pl.BlockSpec((None, None, d, hid), lambda i, te, nu: (layer, te[i], 0, 0), pipeline_mode=once),
                  pl.BlockSpec((None, None, hid, d), lambda i, te, nu: (layer, te[i], 0, 0))],
        out_specs=pl.BlockSpec((MOE_ROWS, half), lambda i, te, nu: (i, 0)),
        scratch_shapes=[pltpu.VMEM((d, 2 * hid), BF16), pltpu.VMEM((hid, d), BF16)],
    )
    return pl.pallas_call(
        functools.partial(_grouped_kernel, hid=hid),
        grid_spec=grid_spec,
        out_shape=jax.ShapeDtypeStruct((n, half), jnp.int32),
        compiler_params=_cparams("arbitrary"),
        name="moe_grouped",
    )(tile_expert, n_used, xs, w_gate, w_up, w_down)


def _combine_ln_kernel(x_ref, yg_ref, w_ref, fs_ref, g_ref, b_ref, o_ref, ob_ref, *, alpha):
    w = w_ref[...]
    acc_lo, acc_hi = None, None
    for k in range(TOP_K):
        lo, hi = _unpack_pairs(yg_ref[k])
        wk = w[:, k:k + 1]
        acc_lo = wk * lo if k == 0 else acc_lo + wk * lo
        acc_hi = wk * hi if k == 0 else acc_hi + wk * hi
    acc = alpha * x_ref[...] + fs_ref[...] + jnp.concatenate([acc_lo, acc_hi], axis=1)
    mu = jnp.mean(acc, axis=-1, keepdims=True)
    xc = acc - mu
    var = jnp.mean(xc * xc, axis=-1, keepdims=True)
    y = xc * lax.rsqrt(var + LN_EPS) * g_ref[...] + b_ref[...]
    o_ref[...] = y
    ob_ref[...] = y.astype(BF16)


def _combine_ln(x, yg, wts, f_shared, g, b, alpha):
    t, d = x.shape
    tm = _tile(t, 64, 8)
    row = lambda width: pl.BlockSpec((tm, width), lambda i: (i, 0))
    vec = pl.BlockSpec((1, d), lambda i: (0, 0))
    return pl.pallas_call(
        functools.partial(_combine_ln_kernel, alpha=alpha),
        grid=(t // tm,),
        in_specs=[row(d), pl.BlockSpec((TOP_K, tm, d // 2), lambda i: (0, i, 0)), row(LANES), row(d), vec, vec],
        out_specs=[row(d), row(d)],
        out_shape=[jax.ShapeDtypeStruct((t, d), F32), jax.ShapeDtypeStruct((t, d), BF16)],
        compiler_params=_cparams("parallel"),
        name="moe_combine_ln",
    )(x, yg, wts, f_shared, g.reshape(1, d), b.reshape(1, d))


def _routed_ln(xf, xpk, ids, wts, rank, counts, w_gate, w_up, w_down, layer, f_shared, g, b, alpha):
    t, d = xf.shape
    n_exp = w_gate.shape[1]
    n_rows = t * TOP_K + n_exp * MOE_ROWS
    n_tiles = n_rows // MOE_ROWS
    cnt = counts[0, :n_exp]
    padded = (cnt + MOE_ROWS - 1) // MOE_ROWS * MOE_ROWS
    ends = jnp.cumsum(padded)
    starts = ends - padded
    tile_start = jnp.arange(n_tiles, dtype=jnp.int32) * MOE_ROWS
    tile_expert = jnp.minimum(jnp.sum((ends[None, :] <= tile_start[:, None]).astype(jnp.int32), axis=1), n_exp - 1)
    n_used = (ends[-1:] // MOE_ROWS).astype(jnp.int32)
    top = ids[:, :TOP_K]
    pos_t = (starts[top] + jnp.take_along_axis(rank, top, axis=1)).astype(jnp.int32).T
    xs = _sc_scatter_rows(xpk, pos_t, n_rows)
    ys = _grouped_swiglu(xs, tile_expert, n_used, w_gate, w_up, w_down, layer)
    yg = _sc_gather_rows(ys, pos_t.reshape(1, TOP_K * t)).reshape(TOP_K, t, d // 2)
    return _combine_ln(xf, yg, wts, f_shared, g, b, alpha)


PROJ_ALIGN = 512


def _w_in_plan(d, n_cols):
    wdt = d // N_BRANCH
    gla_heads = wdt // GLA_DV
    names = (('gate', N_BRANCH * d), ('mla_cq', MLA_Q_RANK), ('mla_ckv', MLA_KV_RANK), ('mla_kr', MLA_ROPE),
             ('dsa_q', wdt), ('dsa_k', DSA_DH), ('dsa_v', DSA_DH),
             ('idx_q', IDX_HEADS * IDX_DIM), ('idx_k', IDX_DIM), ('idx_w', IDX_HEADS),
             ('gla_q', gla_heads * GLA_DK), ('gla_k', gla_heads * GLA_DK), ('gla_v', wdt),
             ('gla_a', GLA_GATE_RANK), ('gla_r', wdt),
             ('rwkv_r', wdt), ('rwkv_k', wdt), ('rwkv_v', wdt),
             ('rwkv_w', RWKV_LORA), ('rwkv_a', RWKV_LORA), ('rwkv_g', RWKV_LORA))
    src, off = {}, 0
    for name, width in names:
        src[name] = (off, width)
        off += width
    assert off == n_cols, (off, n_cols)

    def layout(items):
        copies, dst = [], 0
        for name, slot, scale in items:
            if name is not None:
                copies.append((dst, src[name][0], src[name][1], scale))
            dst += slot
        return copies, -(-dst // PROJ_ALIGN) * PROJ_ALIGN

    half = MLA_ROPE // 2
    kr0 = src['mla_kr'][0]
    lat, lat_w = layout([('mla_cq', MLA_Q_RANK, None), ('mla_ckv', MLA_KV_RANK, None), ('mla_kr', LANES, None),
                         (None, LANES, None)])
    swap_dst = MLA_Q_RANK + MLA_KV_RANK + LANES
    lat += [(swap_dst, kr0 + half, half, None), (swap_dst + half, kr0, half, None)]
    iq0 = src['idx_q'][0]
    dsa, dsa_w = layout([(None, IDX_HEADS * LANES, None), ('dsa_q', wdt, DSA_DH ** -0.5 * LOG2E),
                         ('dsa_k', DSA_DH, None), ('dsa_v', DSA_DH, None), ('idx_k', LANES, None),
                         ('idx_w', LANES, None)])
    dsa += [(h * LANES, iq0 + h * IDX_DIM, IDX_DIM, None) for h in range(IDX_HEADS)]
    gla, gla_w = layout([('gla_v', wdt, None), ('gla_r', wdt, None), ('gla_q', gla_heads * GLA_DK, None),
                         ('gla_k', gla_heads * GLA_DK, None), ('gla_a', LANES, None)])
    rwkv, rwkv_w = layout([('rwkv_r', wdt, None), ('rwkv_k', wdt, None), ('rwkv_v', wdt, None),
                           ('rwkv_w', RWKV_LORA, None), ('rwkv_a', RWKV_LORA, None), ('rwkv_g', LANES, None)])
    return (lat, dsa, gla, rwkv), (lat_w, dsa_w, gla_w, rwkv_w)


def _regroup_kernel(w_ref, gate_ref, *out_refs, plans, gate_w):
    gate_ref[...] = w_ref[:, :gate_w].astype(BF16)
    for o_ref, plan in zip(out_refs, plans):
        o_ref[...] = jnp.zeros_like(o_ref)
        for dst, src, width, scale in plan:
            piece = w_ref[:, src:src + width]
            if scale is not None:
                piece = piece * scale
            o_ref[:, dst:dst + width] = piece.astype(BF16)


def _regroup_w_in(w_in, layer, d):
    n_cols = w_in.shape[2]
    plans, widths = _w_in_plan(d, n_cols)
    gate_w = N_BRANCH * d
    tr = _tile(d, 128, 8)
    out = lambda width: pl.BlockSpec((tr, width), lambda i: (i, 0))
    return pl.pallas_call(
        functools.partial(_regroup_kernel, plans=plans, gate_w=gate_w),
        grid=(d // tr,),
        in_specs=[pl.BlockSpec((None, tr, n_cols), lambda i: (layer, i, 0))],
        out_specs=[out(gate_w)] + [out(wd) for wd in widths],
        out_shape=[jax.ShapeDtypeStruct((d, gate_w), BF16)] + [jax.ShapeDtypeStruct((d, wd), BF16) for wd in widths],
        compiler_params=_cparams("parallel"),
        name="regroup_w_in",
    )(w_in)


def kernel(x, positions, ln_in_g, ln_in_b, w_in, w_branch, w_out, mla_q_norm, mla_w_uq, mla_kv_norm, mla_w_ukv,
           gla_w_gate2, gla_b_gate, gla_norm_g, rwkv_mu_rkv, rwkv_mu_lora, rwkv_w0, rwkv_w2, rwkv_a0, rwkv_a2,
           rwkv_g2, rwkv_k_k, rwkv_k_a, rwkv_r_k, rwkv_lnx_g, rwkv_lnx_b, rwkv_v0, rwkv_v1, rwkv_v2,
           ln_mix_g, ln_mix_b, router_w, router_bias, exp_w_gate, exp_w_up, exp_w_down,
           sh_w_gate, sh_w_up, sh_w_down, ln_ffn_g, ln_ffn_b):
    batch, seq, d = x.shape
    depth = w_in.shape[0]
    t, wdt = batch * seq, d // N_BRANCH
    alpha = (2 * depth) ** 0.25
    ct, st = _rope_tables(positions)
    xf, xb = _layer_norm(x.reshape(t, d), (), ln_in_g, ln_in_b)
    v_first = None
    for l in range(depth):
        w_gate, w_lat, w_dsa, w_gla, w_rwkv = _regroup_w_in(w_in, l, d)
        lat = _matmul(xb, w_lat, F32)
        pb = _matmul(xb, w_dsa, BF16)
        pc = _matmul(xb, w_gla, F32)
        pd = _matmul(xb, w_rwkv, F32)

        y_mla = _mla(lat, ct, st, mla_q_norm[l], mla_w_uq[l], mla_kv_norm[l], mla_w_ukv[l], batch, seq)
        y_dsa = _dsa(pb, batch, seq, wdt)
        y_gla = _gla(pc, gla_w_gate2[l], gla_b_gate[l], gla_norm_g[l], batch, seq)
        vres = None if l == 0 else (rwkv_v0[l - 1], rwkv_v1[l - 1], rwkv_v2[l - 1])
        y_rwkv, v_first = _rwkv(pd, rwkv_mu_rkv[l], rwkv_mu_lora[l], rwkv_w0[l], rwkv_w2[l], rwkv_a0[l], rwkv_a2[l],
                                rwkv_g2[l], rwkv_k_k[l], rwkv_k_a[l], rwkv_r_k[l], rwkv_lnx_g[l], rwkv_lnx_b[l],
                                v_first, vres, batch, seq)

        merged = _merge(xb, (y_mla, y_dsa, y_gla, y_rwkv), w_gate, w_branch[l].astype(BF16))
        xf, xb, xpk = _layer_norm(xf, (_matmul(merged, w_out[l].astype(BF16), F32),), ln_mix_g[l], ln_mix_b[l], alpha,
                                  packed=True)

        ids, wts, rank, counts = _router(xb, router_w[l], router_bias[l])
        f_shared = _shared_expert(xb, sh_w_gate[l], sh_w_up[l], sh_w_down[l])
        xf, xb = _routed_ln(xf, xpk, ids, wts, rank, counts, exp_w_gate, exp_w_up, exp_w_down, l,
                            f_shared, ln_ffn_g[l], ln_ffn_b[l], alpha)
    return xf.reshape(batch, seq, d)
```

```python
import functools
import math

import jax
import jax.numpy as jnp
from jax import lax
from jax.experimental import pallas as pl
from jax.experimental.pallas import tpu as pltpu
from jax.experimental.pallas import tpu_sc as plsc

F32 = jnp.float32
BF16 = jnp.bfloat16

LANES = 128

MLA_NOPE, MLA_ROPE, MLA_V = 128, 64, 128
MLA_Q_RANK, MLA_KV_RANK = 768, 256
ROPE_THETA = 10000.0
DSA_DH = 128
IDX_HEADS, IDX_DIM, IDX_TOPK_MAX = 16, 64, 256
GLA_DV, GLA_DK, GLA_GATE_RANK, GLA_TAU, GLA_CHUNK = 256, 128, 16, 16.0, 64
GLA_SUB = 16
RWKV_HS, RWKV_LORA, RWKV_GN_EPS = 64, 64, 64e-5
RWKV_CHUNK = 64
TOP_K, ROUTED_SCALE = 8, 2.5
LN_EPS, RMS_EPS = 1e-5, 1e-6
NEG_BIG = -1e30
LOG2E = math.log2(math.e)

VMEM_LIMIT = 56 * 1024 * 1024


def _cparams(*sem):
    return pltpu.CompilerParams(dimension_semantics=sem, vmem_limit_bytes=VMEM_LIMIT)


def _tile(n, pref, unit=LANES):
    if n <= pref:
        return n
    t = (pref // unit) * unit
    while t > unit and n % t:
        t -= unit
    assert n % t == 0, (n, pref, unit)
    return t


def _dot(a, b):
    return jnp.dot(a, b, preferred_element_type=F32)


def _dot_t(a, b):
    return lax.dot_general(a, b, (((1,), (1,)), ((), ())), preferred_element_type=F32)


def _dot_f32(a, b):
    return jnp.dot(a, b, preferred_element_type=F32, precision=lax.Precision.HIGHEST)


def _masked_sums(mask, x):
    w = x.shape[1]
    hi = x.astype(BF16)
    rest = x - hi.astype(F32)
    mid = rest.astype(BF16)
    lo = (rest - mid.astype(F32)).astype(BF16)
    s = _dot(mask, jnp.concatenate([hi, mid, lo], axis=1))
    return s[:, :w] + s[:, w:2 * w] + s[:, 2 * w:]


def _mm_kernel(a_ref, b_ref, o_ref):
    o_ref[...] = _dot(a_ref[...], b_ref[...]).astype(o_ref.dtype)


def _matmul(a, b, out_dtype, tm=1024, tn=512):
    m, k = a.shape
    n = b.shape[1]
    tm, tn = _tile(m, tm, 8), _tile(n, tn)
    return pl.pallas_call(
        _mm_kernel,
        grid=(m // tm, n // tn),
        in_specs=[pl.BlockSpec((tm, k), lambda i, j: (i, 0)), pl.BlockSpec((k, tn), lambda i, j: (0, j))],
        out_specs=pl.BlockSpec((tm, tn), lambda i, j: (i, j)),
        out_shape=jax.ShapeDtypeStruct((m, n), out_dtype),
        compiler_params=_cparams("parallel", "parallel"),
        name="matmul",
    )(a, b)


def _pack_pairs(v):
    half = v.shape[1] // 2
    bits = lax.bitcast_convert_type(v.astype(BF16).astype(F32), jnp.int32)
    return (bits[:, half:] & jnp.int32(-65536)) | lax.shift_right_logical(bits[:, :half], 16)


def _unpack_pairs(w):
    return (lax.bitcast_convert_type(lax.shift_left(w, 16), F32),
            lax.bitcast_convert_type(w & jnp.int32(-65536), F32))


def _ln_kernel(*refs, alpha, n_res, packed):
    x_ref, res_refs = refs[0], refs[1:1 + n_res]
    g_ref, b_ref, o_ref, ob_ref = refs[1 + n_res:5 + n_res]
    x = x_ref[...]
    if n_res:
        x = alpha * x
        for f_ref in res_refs:
            x = x + f_ref[...]
    mu = jnp.mean(x, axis=-1, keepdims=True)
    xc = x - mu
    var = jnp.mean(xc * xc, axis=-1, keepdims=True)
    y = xc * lax.rsqrt(var + LN_EPS) * g_ref[...] + b_ref[...]
    o_ref[...] = y
    ob_ref[...] = y.astype(BF16)
    if packed:
        refs[-1][...] = _pack_pairs(y)


def _layer_norm(x, res, g, b, alpha=1.0, packed=False):
    t, d = x.shape
    tm = _tile(t, 128, 8)
    row = pl.BlockSpec((tm, d), lambda i: (i, 0))
    vec = pl.BlockSpec((1, d), lambda i: (0, 0))
    args = (x,) + tuple(res)
    extra_spec = [pl.BlockSpec((tm, d // 2), lambda i: (i, 0))] if packed else []
    extra_shape = [jax.ShapeDtypeStruct((t, d // 2), jnp.int32)] if packed else []
    return pl.pallas_call(
        functools.partial(_ln_kernel, alpha=alpha, n_res=len(res), packed=packed),
        grid=(t // tm,),
        in_specs=[row] * len(args) + [vec, vec],
        out_specs=[row, row] + extra_spec,
        out_shape=[jax.ShapeDtypeStruct((t, d), F32), jax.ShapeDtypeStruct((t, d), BF16)] + extra_shape,
        compiler_params=_cparams("parallel"),
        name="layer_norm",
    )(*args, g.reshape(1, d), b.reshape(1, d))


def _pad_cols(w, width):
    return jnp.pad(w, ((0, 0), (0, width - w.shape[1])))


def _rms(x, g):
    return x * lax.rsqrt(jnp.mean(x * x, axis=-1, keepdims=True) + RMS_EPS) * g


def _mla_prep_kernel(lat_ref, ct_ref, st_ref, qg_ref, kg_ref, wqn_ref, wqr_ref, wqs_ref, wkn_ref, wv_ref,
                     qn_ref, qr_ref, kn_ref, kr_ref, v_ref, *, heads, scale):
    lat = lat_ref[...]
    cq = lat[:, :MLA_Q_RANK]
    ckv = lat[:, MLA_Q_RANK:MLA_Q_RANK + MLA_KV_RANK]
    kr = lat[:, MLA_Q_RANK + MLA_KV_RANK:MLA_Q_RANK + MLA_KV_RANK + LANES]
    krs = lat[:, MLA_Q_RANK + MLA_KV_RANK + LANES:MLA_Q_RANK + MLA_KV_RANK + 2 * LANES]
    ct, st = ct_ref[...], st_ref[...]
    nq = _rms(cq, qg_ref[...]).astype(BF16)
    nkv = _rms(ckv, kg_ref[...]).astype(BF16)
    qn_ref[...] = (_dot(nq, wqn_ref[...]) * scale).astype(BF16)
    cth = jnp.concatenate([ct] * heads, axis=1)
    sth = jnp.concatenate([st] * heads, axis=1)
    qr_ref[...] = ((_dot(nq, wqr_ref[...]) * cth + _dot(nq, wqs_ref[...]) * sth) * scale).astype(BF16)
    kn_ref[...] = _dot(nkv, wkn_ref[...]).astype(BF16)
    v_ref[...] = _dot(nkv, wv_ref[...]).astype(BF16)
    kr_ref[...] = (kr * ct + krs * st).astype(BF16)


def _flash_kernel(qn_ref, qr_ref, kn_ref, kr_ref, v_ref, o_ref, m_ref, l_ref, acc_ref, *, tq, tk):
    i, j = pl.program_id(2), pl.program_id(3)

    @pl.when(j == 0)
    def _():
        m_ref[...] = jnp.full_like(m_ref, NEG_BIG)
        l_ref[...] = jnp.zeros_like(l_ref)
        acc_ref[...] = jnp.zeros_like(acc_ref)

    def step(masked):
        q = jnp.concatenate([qn_ref[...], qr_ref[...]], axis=1)
        k = jnp.concatenate([kn_ref[...], kr_ref[...]], axis=1)
        s = _dot_t(q, k)
        if masked:
            qpos = i * tq + lax.broadcasted_iota(jnp.int32, (tq, tk), 0)
            kpos = j * tk + lax.broadcasted_iota(jnp.int32, (tq, tk), 1)
            s = jnp.where(kpos <= qpos, s, -jnp.inf)
        m_prev = m_ref[...]
        m_new = jnp.maximum(m_prev, jnp.max(s, axis=1, keepdims=True))
        alpha = jnp.exp2(m_prev - m_new)
        p = jnp.exp2(s - jnp.concatenate([m_new] * (tk // LANES), axis=1))
        l_ref[...] = alpha * l_ref[...] + jnp.sum(p, axis=1, keepdims=True)
        acc_ref[...] = alpha * acc_ref[...] + _dot(p.astype(BF16), v_ref[...])
        m_ref[...] = m_new

    below = j * tk + tk - 1 <= i * tq
    touches = j * tk <= i * tq + tq - 1

    @pl.when(below)
    def _():
        step(False)

    @pl.when(jnp.logical_and(touches, jnp.logical_not(below)))
    def _():
        step(True)

    @pl.when(j == pl.num_programs(3) - 1)
    def _():
        o_ref[...] = (acc_ref[...] / l_ref[...]).astype(o_ref.dtype)


def _rope_tables(positions):
    half = MLA_ROPE // 2
    inv = ROPE_THETA ** (-jnp.arange(0, MLA_ROPE, 2, dtype=F32) / MLA_ROPE)
    ang = positions.reshape(-1).astype(F32)[:, None] * inv
    cos, sin = jnp.cos(ang), jnp.sin(ang)
    zero = jnp.zeros((ang.shape[0], LANES - 2 * half), F32)
    return jnp.concatenate([cos, cos, zero], 1), jnp.concatenate([-sin, sin, zero], 1)


def _swap_halves(w):
    half = w.shape[-1] // 2
    return jnp.concatenate([w[..., half:], w[..., :half]], -1)


def _mla(lat, ct, st, q_norm, w_uq, kv_norm, w_ukv, batch, seq):
    t = lat.shape[0]
    heads = w_uq.shape[1] // (MLA_NOPE + MLA_ROPE)
    hw = heads * LANES
    wq = w_uq.reshape(MLA_Q_RANK, heads, MLA_NOPE + MLA_ROPE)
    wqn = wq[:, :, :MLA_NOPE].reshape(MLA_Q_RANK, hw).astype(BF16)
    rope_pad = ((0, 0), (0, 0), (0, LANES - MLA_ROPE))
    wqr = jnp.pad(wq[:, :, MLA_NOPE:], rope_pad).reshape(MLA_Q_RANK, hw).astype(BF16)
    wqs = jnp.pad(_swap_halves(wq[:, :, MLA_NOPE:]), rope_pad).reshape(MLA_Q_RANK, hw).astype(BF16)
    wkv = w_ukv.reshape(MLA_KV_RANK, heads, MLA_NOPE + MLA_V)
    wkn = wkv[:, :, :MLA_NOPE].reshape(MLA_KV_RANK, hw).astype(BF16)
    wv = wkv[:, :, MLA_NOPE:].reshape(MLA_KV_RANK, hw).astype(BF16)

    tm = _tile(t, 512, 8)
    row = lambda w: pl.BlockSpec((tm, w), lambda i: (i, 0))
    full = lambda a: pl.BlockSpec(a.shape, lambda i: (0, 0))
    qg, kg = q_norm.reshape(1, -1), kv_norm.reshape(1, -1)
    qn, qr, kn, kr, v = pl.pallas_call(
        functools.partial(_mla_prep_kernel, heads=heads, scale=(MLA_NOPE + MLA_ROPE) ** -0.5 * LOG2E),
        grid=(t // tm,),
        in_specs=[row(lat.shape[1]), row(LANES), row(LANES), full(qg), full(kg),
                  full(wqn), full(wqr), full(wqs), full(wkn), full(wv)],
        out_specs=[row(hw), row(hw), row(hw), row(LANES), row(hw)],
        out_shape=[jax.ShapeDtypeStruct((t, hw), BF16)] * 3 + [jax.ShapeDtypeStruct((t, LANES), BF16),
                                                               jax.ShapeDtypeStruct((t, hw), BF16)],
        compiler_params=_cparams("parallel"),
        name="mla_prep",
    )(lat, ct, st, qg, kg, wqn, wqr, wqs, wkn, wv)

    tq = tk = _tile(seq, 1024, 8)
    nq, nk = seq // tq, seq // tk
    r3 = lambda a: a.reshape(batch, seq, a.shape[1])
    last = lambda i, j: jnp.minimum(j, ((i + 1) * tq - 1) // tk)
    q_spec = pl.BlockSpec((None, tq, LANES), lambda b, h, i, j: (b, i, h))
    k_spec = pl.BlockSpec((None, tk, LANES), lambda b, h, i, j: (b, last(i, j), h))
    kr_spec = pl.BlockSpec((None, tk, LANES), lambda b, h, i, j: (b, last(i, j), 0))
    out = pl.pallas_call(
        functools.partial(_flash_kernel, tq=tq, tk=tk),
        grid=(batch, heads, nq, nk),
        in_specs=[q_spec, q_spec, k_spec, kr_spec, k_spec],
        out_specs=q_spec,
        out_shape=jax.ShapeDtypeStruct((batch, seq, hw), BF16),
        scratch_shapes=[pltpu.VMEM((tq, LANES), F32), pltpu.VMEM((tq, LANES), F32), pltpu.VMEM((tq, LANES), F32)],
        compiler_params=_cparams("parallel", "parallel", "parallel", "arbitrary"),
        name="mla_flash",
    )(r3(qn), r3(qr), r3(kn), r3(kr), r3(v))
    return out.reshape(t, hw)


DSA_TQ = 256
DSA_TK = 1024
INT_MIN = -2 ** 31


def _dsa_kernel(iq_ref, q_ref, k_ref, v_ref, ik_ref, iw_ref, o_ref, key_ref, m_ref, l_ref, acc_ref,
                *, tq, tk, n_sel, heads, pos_bits):
    i = pl.program_id(1)
    n_kt = (i * tq + tq - 1) // tk + 1
    reps = tk // LANES
    wide = lambda a: jnp.concatenate([a] * reps, axis=1)
    qpos = i * tq + lax.broadcasted_iota(jnp.int32, (tq, tk), 0)
    lane_pos = lax.broadcasted_iota(jnp.int32, (tq, tk), 1)

    iq = iq_ref[...]
    iw = iw_ref[...].astype(F32) * (IDX_HEADS ** -0.5 * IDX_DIM ** -0.5)
    iw_b = [jnp.broadcast_to(iw[:, h:h + 1], (tq, LANES)) for h in range(IDX_HEADS)]

    def score_tile(j, carry):
        ikj = ik_ref[pl.ds(pl.multiple_of(j * tk, tk), tk), :]
        sc = jnp.zeros((tq, tk), F32)
        for h in range(IDX_HEADS):
            logit = _dot_t(iq[:, h * LANES:(h + 1) * LANES], ikj)
            sc = sc + wide(iw_b[h]) * jnp.maximum(logit, 0.0)
        sc = jnp.where(j * tk + lane_pos <= qpos, sc + 0.0, -jnp.inf)
        bits = lax.bitcast_convert_type(sc, jnp.int32)
        key_ref[j] = jnp.where(bits >= 0, bits, bits ^ 0x7FFFFFFF)
        return carry

    lax.fori_loop(0, n_kt, score_tile, 0)

    def count(pred):
        def body(j, acc):
            c = pred(key_ref[j], j * tk + lane_pos).astype(jnp.int32)
            for rep in range(reps):
                acc = acc + c[:, rep * LANES:(rep + 1) * LANES]
            return acc
        acc = lax.fori_loop(0, n_kt, body, jnp.zeros((tq, LANES), jnp.int32))
        return jnp.broadcast_to(jnp.sum(acc, axis=1, keepdims=True), (tq, LANES))

    def thr_bit(bit_i, thr):
        cand = thr ^ jnp.left_shift(jnp.int32(1), 31 - bit_i)
        cnt = count(lambda key, pos: key >= wide(cand))
        return jnp.where(cnt >= n_sel, cand, thr)

    thr = lax.fori_loop(0, 32, thr_bit, jnp.full((tq, LANES), INT_MIN, jnp.int32))
    cnt_gt = count(lambda key, pos: key > wide(thr))
    cnt_ge = count(lambda key, pos: key >= wide(thr))
    need = n_sel - cnt_gt

    def tie_bit(bit_i, cut):
        cand = cut | jnp.left_shift(jnp.int32(1), pos_bits - 1 - bit_i)
        cnt = count(lambda key, pos: jnp.logical_and(key == wide(thr), pos < wide(cand)))
        return jnp.where(cnt < need, cand, cut)

    surplus = jnp.max(cnt_ge - cnt_gt - need) > 0
    cut = lax.cond(surplus,
                   lambda: lax.fori_loop(0, pos_bits, tie_bit, jnp.zeros((tq, LANES), jnp.int32)),
                   lambda: jnp.full((tq, LANES), 2 ** 31 - 1, jnp.int32))

    q = q_ref[...]
    q_all = jnp.concatenate([q[:, h * LANES:(h + 1) * LANES] for h in range(heads)], axis=0)
    m_ref[...] = jnp.full_like(m_ref, NEG_BIG)
    l_ref[...] = jnp.zeros_like(l_ref)
    acc_ref[...] = jnp.zeros_like(acc_ref)

    def attend(j, carry):
        rows = pl.ds(pl.multiple_of(j * tk, tk), tk)
        s = _dot_t(q_all, k_ref[rows, :])
        key, pos = key_ref[j], j * tk + lane_pos
        keep = jnp.where(key > wide(thr), 0.0,
                         jnp.where(jnp.logical_and(key == wide(thr), pos <= wide(cut)), 0.0, -jnp.inf))
        keep = jnp.where(pos <= qpos, keep, -jnp.inf)
        s = s + jnp.concatenate([keep] * heads, axis=0)
        m_prev = m_ref[...]
        m_new = jnp.maximum(m_prev, jnp.max(s, axis=1, keepdims=True))
        alpha = jnp.exp2(m_prev - m_new)
        p = jnp.exp2(s - jnp.concatenate([m_new] * reps, axis=1))
        l_ref[...] = alpha * l_ref[...] + jnp.sum(p, axis=1, keepdims=True)
        acc_ref[...] = alpha * acc_ref[...] + _dot(p.astype(BF16), v_ref[rows, :])
        m_ref[...] = m_new
        return carry

    lax.fori_loop(0, n_kt, attend, 0)
    out = acc_ref[...] / l_ref[...]
    o_ref[...] = jnp.concatenate([out[h * tq:(h + 1) * tq] for h in range(heads)], axis=1).astype(o_ref.dtype)


def _dsa(pb, batch, seq, width):
    t, wb = pb.shape
    heads = width // DSA_DH
    iqw = IDX_HEADS * LANES
    tq, tk = _tile(seq, DSA_TQ, 8), _tile(seq, DSA_TK)
    n_sel = min(IDX_TOPK_MAX, seq // 4)
    c0 = (iqw + width) // LANES
    pb3 = pb.reshape(batch, seq, wb)
    qblk = lambda w, idx: pl.BlockSpec((None, tq, w), lambda b, i: (b, i, idx))
    seqblk = lambda idx: pl.BlockSpec((None, seq, LANES), lambda b, i: (b, 0, idx))
    out = pl.pallas_call(
        functools.partial(_dsa_kernel, tq=tq, tk=tk, n_sel=n_sel, heads=heads,
                          pos_bits=max(1, (seq - 1).bit_length())),
        grid=(batch, seq // tq),
        in_specs=[qblk(iqw, 0), qblk(width, iqw // width), seqblk(c0), seqblk(c0 + 1), seqblk(c0 + 2),
                  qblk(LANES, c0 + 3)],
        out_specs=qblk(width, 0),
        out_shape=jax.ShapeDtypeStruct((batch, seq, width), BF16),
        scratch_shapes=[pltpu.VMEM((seq // tk, tq, tk), jnp.int32), pltpu.VMEM((heads * tq, LANES), F32),
                        pltpu.VMEM((heads * tq, LANES), F32), pltpu.VMEM((heads * tq, LANES), F32)],
        compiler_params=_cparams("parallel", "arbitrary"),
        name="dsa",
    )(pb3, pb3, pb3, pb3, pb3, pb3)
    return out.reshape(t, width)


def _gla_kernel(q_ref, k_ref, v_ref, a_ref, r_ref, wg_ref, bg_ref, ng_ref, o_ref, state_ref):
    c_len, sub = GLA_CHUNK, GLA_SUB

    @pl.when(pl.program_id(2) == 0)
    def _():
        state_ref[...] = jnp.zeros_like(state_ref)

    z_all = _dot(a_ref[...].astype(BF16), wg_ref[...]) + bg_ref[...]
    g_all = jax.nn.log_sigmoid(z_all) / GLA_TAU
    ri = lax.broadcasted_iota(jnp.int32, (c_len, c_len), 0)
    ci = lax.broadcasted_iota(jnp.int32, (c_len, c_len), 1)
    b_all = _masked_sums((ri >= ci).astype(BF16), g_all)
    ones = jnp.ones((GLA_DK, LANES), BF16)
    trow = lax.broadcasted_iota(jnp.int32, (sub, GLA_DK), 0)
    lane = lax.broadcasted_iota(jnp.int32, (sub, LANES), 1)
    for hh in range(state_ref.shape[0]):
        ksl = slice(hh * GLA_DK, (hh + 1) * GLA_DK)
        vsl = slice(hh * GLA_DV, (hh + 1) * GLA_DV)
        o = _gla_head(q_ref[:, ksl] * GLA_DK ** -0.5, k_ref[:, ksl], v_ref[:, vsl].astype(BF16), b_all[:, ksl],
                      state_ref.at[hh], ones, trow, lane)
        r = r_ref[:, vsl]
        o_ref[:, vsl] = (_rms(o, ng_ref[:, vsl]) * (r * jax.nn.sigmoid(r))).astype(o_ref.dtype)


def _gla_head(q, k, vb, b, state_ref, ones, trow, lane):
    c_len, sub = GLA_CHUNK, GLA_SUB
    state_t = state_ref[...]
    o_inter = _dot_t((q * jnp.exp(b)).astype(BF16), state_t.astype(BF16))
    outs = []
    for blk in range(c_len // sub):
        lo = blk * sub
        qi, bi, ki = q[lo:lo + sub], b[lo:lo + sub], k[lo:lo + sub]
        rows = [qi * ki[s:s + 1] * jnp.exp(jnp.where(trow >= s, bi - bi[s:s + 1], -jnp.inf)) for s in range(sub)]
        prod = jnp.concatenate(rows, axis=0)
        p_hi = prod.astype(BF16)
        p_lo = (prod - p_hi.astype(F32)).astype(BF16)
        sums = _dot(jnp.concatenate([p_hi, p_lo], axis=0), ones)
        sums = sums[:sub * sub] + sums[sub * sub:]
        attn = jnp.zeros((sub, LANES), F32)
        for s in range(sub):
            attn = attn + jnp.where(lane == s, sums[s * sub:(s + 1) * sub], 0.0)
        o_blk = _dot(attn[:, :sub].astype(BF16), vb[lo:lo + sub])
        if blk:
            qa = qi * jnp.exp(bi - bi[0:1])
            ka = k[:lo] * jnp.exp(bi[0:1] - b[:lo])
            o_blk = o_blk + _dot(_dot_t(qa.astype(BF16), ka.astype(BF16)).astype(BF16), vb[:lo])
        outs.append(o_blk)
    o = o_inter + jnp.concatenate(outs, axis=0)

    b_last = b[c_len - 1:c_len]
    k_dec = (k * jnp.exp(b_last - b)).astype(BF16)
    state_ref[...] = state_t * jnp.exp(b_last) + lax.dot_general(
        vb, k_dec, (((0,), (0,)), ((), ())), preferred_element_type=F32)
    return o


GLA_HEADS_PER_STEP = 4


def _gla(pc, w_gate2, b_gate, norm_g, batch, seq):
    t = pc.shape[0]
    heads = norm_g.shape[0] // GLA_DV
    hps = math.gcd(heads, GLA_HEADS_PER_STEP)
    groups = heads // hps
    wg = jnp.pad(w_gate2, ((0, LANES - GLA_GATE_RANK), (0, 0))).astype(BF16)
    nc = seq // GLA_CHUNK
    blk = lambda w, off: pl.BlockSpec((None, GLA_CHUNK, hps * w), lambda b, g, c: (b, c, off + g))
    par = lambda rows, w: pl.BlockSpec((rows, hps * w), lambda b, g, c: (0, g))
    pc3 = pc.reshape(batch, seq, pc.shape[1])
    out = pl.pallas_call(
        _gla_kernel,
        grid=(batch, groups, nc),
        in_specs=[blk(GLA_DK, 4 * groups), blk(GLA_DK, 5 * groups), blk(GLA_DV, 0),
                  pl.BlockSpec((None, GLA_CHUNK, LANES), lambda b, g, c: (b, c, 6 * heads)),
                  blk(GLA_DV, groups), par(LANES, GLA_DK), par(1, GLA_DK), par(1, GLA_DV)],
        out_specs=blk(GLA_DV, 0),
        out_shape=jax.ShapeDtypeStruct((batch, seq, heads * GLA_DV), BF16),
        scratch_shapes=[pltpu.VMEM((hps, GLA_DV, GLA_DK), F32)],
        compiler_params=_cparams("parallel", "parallel", "arbitrary"),
        name="gla",
    )(pc3, pc3, pc3, pc3, pc3, wg, b_gate.reshape(1, -1), norm_g.reshape(1, -1))
    return out.reshape(t, heads * GLA_DV)


_DIMS = {'nn': (((1,), (0,)), ((), ())), 'nt': (((1,), (1,)), ((), ())), 'tn': (((0,), (0,)), ((), ()))}


def _mm(a, b, form):
    return lax.dot_general(a.astype(BF16), b.astype(BF16), _DIMS[form], preferred_element_type=F32)


def _seg_sum(x, width):
    gi = lax.broadcasted_iota(jnp.int32, (LANES, LANES), 0) // width
    gj = lax.broadcasted_iota(jnp.int32, (LANES, LANES), 1) // width
    ones = (gi == gj).astype(F32)
    return jnp.concatenate([_dot_f32(x[:, c:c + LANES], ones) for c in range(0, x.shape[1], LANES)], axis=1)


def _rwkv_prep_kernel(*refs, width, seq, has_vres):
    if has_vres:
        (p_ref, pp_ref, prm_ref, ml_ref, w2_ref, a2_ref, g2_ref, vf_ref, v0_ref, v1_ref, v2_ref,
         r_ref, lw_ref, k_ref, v_ref, kk_ref, bb_ref, bonus_ref, g_ref) = refs
    else:
        (p_ref, pp_ref, prm_ref, ml_ref, w2_ref, a2_ref, g2_ref,
         r_ref, lw_ref, k_ref, v_ref, kk_ref, bb_ref, bonus_ref, g_ref) = refs
    w = width
    p, prm, ml = p_ref[...], prm_ref[...], ml_ref[...]
    tm = p.shape[0]
    above = jnp.where((pl.program_id(0) * tm) % seq == 0, 0.0, pp_ref[7:8, :])
    first = lax.broadcasted_iota(jnp.int32, p.shape, 0) == 0
    pp = jnp.where(first, jnp.broadcast_to(above, p.shape), pltpu.roll(p, 1, axis=0))
    lerp = lambda lo, hi, mu: p[:, lo:hi] + (pp[:, lo:hi] - p[:, lo:hi]) * mu
    r = lerp(0, w, prm[0:1])
    k = lerp(w, 2 * w, prm[1:2])
    v = lerp(2 * w, 3 * w, prm[2:3])
    x_wa = lerp(3 * w, 3 * w + LANES, ml[0:1])
    x_g = lerp(3 * w + LANES, 3 * w + 2 * LANES, ml[1:2])
    w_log = -jax.nn.softplus(-(prm[3:4] + _dot(jnp.tanh(x_wa).astype(BF16), w2_ref[...]))) - 0.5
    lw_ref[...] = -jnp.exp(w_log)
    a = jax.nn.sigmoid(prm[4:5] + _dot(x_wa.astype(BF16), a2_ref[...]))
    g_ref[...] = _dot(jax.nn.sigmoid(x_g).astype(BF16), g2_ref[...])
    if has_vres:
        mix = jax.nn.sigmoid(v0_ref[...] + _dot(_dot(v.astype(BF16), v1_ref[...]).astype(BF16), v2_ref[...]))
        v = v + (vf_ref[...] - v) * mix
    kk = k * prm[5:6]
    kk = kk * lax.rsqrt(jnp.maximum(_seg_sum(kk * kk, RWKV_HS), 1e-24))
    k2 = k * (1.0 + (a - 1.0) * prm[6:7])
    r_ref[...] = r
    k_ref[...] = k2
    v_ref[...] = v
    kk_ref[...] = kk
    bb_ref[...] = kk * a
    bonus_ref[...] = _seg_sum(r * k2 * prm[7:8], RWKV_HS) * v


RWKV_HEADS_PER_STEP = 8


def _rwkv_chunk_kernel(r_ref, lw_ref, k_ref, v_ref, kk_ref, bb_ref, y_ref, s_ref, *, n_chunk):
    c_len, hs = RWKV_CHUNK, RWKV_HS
    rows_n = n_chunk * c_len

    @pl.when(pl.program_id(2) == 0)
    def _():
        s_ref[...] = jnp.zeros_like(s_ref)

    ri = lax.broadcasted_iota(jnp.int32, (rows_n, rows_n), 0)
    ci = lax.broadcasted_iota(jnp.int32, (rows_n, rows_n), 1)
    same = (ri // c_len) == (ci // c_len)
    incl = jnp.logical_and(same, ci <= ri)
    strict = jnp.logical_and(same, ci < ri)
    eye = (ri == ci).astype(F32)
    e_r = lax.broadcasted_iota(jnp.int32, (hs, hs), 0)
    e_c = lax.broadcasted_iota(jnp.int32, (hs, hs), 1)
    sums = _masked_sums(jnp.concatenate([incl, same], axis=0).astype(BF16), lw_ref[...])
    cum_all, cum_c_all = sums[:rows_n], sums[rows_n:]
    heads = range(s_ref.shape[0])
    hd = []
    for hh in heads:
        sl = slice(hh * hs, (hh + 1) * hs)
        r, lw, k, v, kk, bb = (ref[...][:, sl] for ref in (r_ref, lw_ref, k_ref, v_ref, kk_ref, bb_ref))
        cum, cum_c = cum_all[:, sl], cum_c_all[:, sl]
        g_inv, g_end = jnp.exp(-cum), jnp.exp(cum_c - cum)
        kap, rt = kk * jnp.exp(cum - lw), r * jnp.exp(cum)
        bet, kt = bb * g_inv, k * g_inv
        gram = _mm(jnp.concatenate([kap, rt], axis=0), jnp.concatenate([bet, kt], axis=0), 'nt')
        hd.append(dict(v=v, kap=kap, rt=rt, bet_c=bb * g_end, kt_c=k * g_end, cum_c=cum_c,
                       n_m=jnp.where(strict, gram[:rows_n, :rows_n], 0.0),
                       a_kk=jnp.where(strict, gram[:rows_n, rows_n:], 0.0),
                       a_rb=jnp.where(incl, gram[rows_n:, :rows_n], 0.0),
                       a_rk=jnp.where(incl, gram[rows_n:, rows_n:], 0.0)))
    t_inv = [eye - h['n_m'] for h in hd]
    pw = [-h['n_m'] for h in hd]
    for _ in range(int(math.log2(c_len)) - 1):
        pw = [_mm(p, p, 'nn') for p in pw]
        t_inv = [t + _mm(t, p, 'nn') for t, p in zip(t_inv, pw)]
    av = [_mm(jnp.concatenate([h['a_kk'], h['a_rk']], axis=0), h['v'], 'nn') for h in hd]
    z = [_mm(t, jnp.concatenate([h['kap'], a[:rows_n]], axis=1), 'nn') for t, h, a in zip(t_inv, hd, av)]
    az = [_mm(h['a_rb'], zz, 'nn') for h, zz in zip(hd, z)]
    r_p = [h['rt'] - a[:, :hs] for h, a in zip(hd, az)]
    y0 = [a[rows_n:] - b[:, hs:] for a, b in zip(av, az)]
    s = [s_ref[hh] for hh in heads]
    outs = [[] for _ in heads]
    for c in range(n_chunk):
        rows = slice(c * c_len, (c + 1) * c_len)
        for hh in heads:
            h = hd[hh]
            outs[hh].append(_mm(r_p[hh][rows], s[hh], 'nt') + y0[hh][rows])
            zb = _mm(z[hh][rows], h['bet_c'][rows], 'tn')
            g_chunk = jnp.exp(h['cum_c'][c * c_len:c * c_len + 1])
            m = jnp.where(e_r == e_c, jnp.broadcast_to(g_chunk, (hs, hs)), 0.0) - zb[:hs]
            s[hh] = _mm(s[hh], m, 'nn') + _mm(h['v'][rows], h['kt_c'][rows], 'tn') - zb[hs:]
    for hh in heads:
        s_ref[hh] = s[hh]
    y_ref[...] = jnp.concatenate([jnp.concatenate(o, axis=0) for o in outs], axis=1)


def _rwkv_post_kernel(y_ref, bonus_ref, g_ref, lg_ref, lb_ref, o_ref):
    y = y_ref[...]
    mu = _seg_sum(y, RWKV_HS) * (1.0 / RWKV_HS)
    yc = y - mu
    var = _seg_sum(yc * yc, RWKV_HS) * (1.0 / RWKV_HS)
    yn = yc * lax.rsqrt(var + RWKV_GN_EPS) * lg_ref[...] + lb_ref[...]
    o_ref[...] = ((yn + bonus_ref[...]) * g_ref[...]).astype(o_ref.dtype)


def _rwkv(pd, mu_rkv, mu_lora, w0, w2, a0, a2, g2, k_k, k_a, r_k, lnx_g, lnx_b, v_first, vres, batch, seq):
    t, wd = pd.shape
    w = w0.shape[0]
    prm = jnp.stack([mu_rkv[0], mu_rkv[1], mu_rkv[2], w0, a0, k_k, k_a, r_k.reshape(-1)])
    zl = jnp.zeros((RWKV_LORA,), F32)
    ml = jnp.stack([jnp.concatenate([mu_lora[0], mu_lora[1]]), jnp.concatenate([mu_lora[2], zl])])
    zw = jnp.zeros((LANES - RWKV_LORA, w), F32)
    w2p = jnp.concatenate([w2, zw]).astype(BF16)
    a2p = jnp.concatenate([zw, a2]).astype(BF16)
    g2p = jnp.concatenate([g2, zw]).astype(BF16)
    has_vres = vres is not None
    tm = _tile(seq, 256, 8)
    row = lambda width: pl.BlockSpec((tm, width), lambda i: (i, 0))
    full = lambda a: pl.BlockSpec(a.shape, lambda i: (0, 0))
    above = pl.BlockSpec((8, wd), lambda i: (jnp.maximum(i * (tm // 8) - 1, 0), 0))
    args = [pd, pd, prm, ml, w2p, a2p, g2p]
    specs = [row(wd), above, full(prm), full(ml), full(w2p), full(a2p), full(g2p)]
    if has_vres:
        v0, v1, v2 = vres
        v1p = _pad_cols(v1, LANES).astype(BF16)
        v2p = jnp.pad(v2, ((0, LANES - v2.shape[0]), (0, 0))).astype(BF16)
        v0r = v0.reshape(1, w)
        args += [v_first, v0r, v1p, v2p]
        specs += [row(w), full(v0r), full(v1p), full(v2p)]
    r, lw, k2, v, kk, bb, bonus, g = pl.pallas_call(
        functools.partial(_rwkv_prep_kernel, width=w, seq=seq, has_vres=has_vres),
        grid=(t // tm,),
        in_specs=specs,
        out_specs=[row(w)] * 8,
        out_shape=[jax.ShapeDtypeStruct((t, w), F32)] * 8,
        compiler_params=_cparams("parallel"),
        name="rwkv_prep",
    )(*args)

    n_chunk = 4 if seq % (4 * RWKV_CHUNK) == 0 else 1
    rows_n = n_chunk * RWKV_CHUNK
    bw = math.gcd(w, RWKV_HEADS_PER_STEP * RWKV_HS)
    blk = pl.BlockSpec((None, rows_n, bw), lambda b, h, c: (b, c, h))
    r3 = lambda a: a.reshape(batch, seq, w)
    y = pl.pallas_call(
        functools.partial(_rwkv_chunk_kernel, n_chunk=n_chunk),
        grid=(batch, w // bw, seq // rows_n),
        in_specs=[blk] * 6,
        out_specs=blk,
        out_shape=jax.ShapeDtypeStruct((batch, seq, w), F32),
        scratch_shapes=[pltpu.VMEM((bw // RWKV_HS, RWKV_HS, RWKV_HS), F32)],
        compiler_params=_cparams("parallel", "parallel", "arbitrary"),
        name="rwkv_chunk",
    )(r3(r), r3(lw), r3(k2), r3(v), r3(kk), r3(bb))

    lg, lb = lnx_g.reshape(1, w), lnx_b.reshape(1, w)
    out = pl.pallas_call(
        _rwkv_post_kernel,
        grid=(t // tm,),
        in_specs=[row(w), row(w), row(w), full(lg), full(lb)],
        out_specs=row(w),
        out_shape=jax.ShapeDtypeStruct((t, w), BF16),
        compiler_params=_cparams("parallel"),
        name="rwkv_post",
    )(y.reshape(t, w), bonus, g, lg, lb)
    return out, (v_first if has_vres else v)


N_BRANCH = 4


def _merge_kernel(*refs):
    x_ref = refs[0]
    y_refs = refs[1:1 + N_BRANCH]
    wg_refs = refs[1 + N_BRANCH:1 + 2 * N_BRANCH]
    wb_refs = refs[1 + 2 * N_BRANCH:1 + 3 * N_BRANCH]
    o_ref = refs[-1]
    x = x_ref[...]
    acc = None
    for y_ref, wg_ref, wb_ref in zip(y_refs, wg_refs, wb_refs):
        term = jax.nn.sigmoid(_dot(x, wg_ref[...])) * _dot(y_ref[...], wb_ref[...])
        acc = term if acc is None else acc + term
    o_ref[...] = acc.astype(o_ref.dtype)


def _merge(xb, ys, w_gate, w_branch):
    t, d = xb.shape
    w = ys[0].shape[1]
    tm, tn = _tile(t, 512, 8), _tile(d, 256)
    nj = d // tn
    gate_spec = lambda i: pl.BlockSpec((d, tn), lambda r, j: (0, i * nj + j))
    br_spec = lambda i: pl.BlockSpec((None, w, tn), lambda r, j: (i, 0, j))
    return pl.pallas_call(
        _merge_kernel,
        grid=(t // tm, nj),
        in_specs=[pl.BlockSpec((tm, d), lambda r, j: (r, 0))] + [pl.BlockSpec((tm, w), lambda r, j: (r, 0))] * N_BRANCH
        + [gate_spec(i) for i in range(N_BRANCH)] + [br_spec(i) for i in range(N_BRANCH)],
        out_specs=pl.BlockSpec((tm, tn), lambda r, j: (r, j)),
        out_shape=jax.ShapeDtypeStruct((t, d), BF16),
        compiler_params=_cparams("parallel", "parallel"),
        name="merge",
    )(xb, *ys, *([w_gate] * N_BRANCH), *([w_branch] * N_BRANCH))


def _router_kernel(x_ref, w_ref, b_ref, ids_ref, wts_ref, rank_ref, cnt_ref, run_ref, *, n_exp):
    tm = x_ref.shape[0]

    @pl.when(pl.program_id(0) == 0)
    def _():
        run_ref[...] = jnp.zeros_like(run_ref)

    scores = jax.nn.sigmoid(_dot(x_ref[...], w_ref[...]))
    lane = lax.broadcasted_iota(jnp.int32, (tm, LANES), 1)
    work = jnp.where(lane < n_exp, scores + b_ref[...], -jnp.inf)
    chosen = jnp.zeros((tm, LANES), F32)
    sel = jnp.zeros((tm, LANES), F32)
    ids = jnp.zeros((tm, LANES), jnp.int32)
    wts = jnp.zeros((tm, LANES), F32)
    for it in range(TOP_K):
        best = jnp.max(work, axis=1, keepdims=True)
        first = jnp.min(jnp.where(work == best, lane, LANES), axis=1, keepdims=True)
        hit = lane == first
        chosen = jnp.where(hit, scores, chosen)
        sel = jnp.where(hit, 1.0, sel)
        ids = jnp.where(lane == it, first, ids)
        wts = jnp.where(lane == it, jnp.sum(jnp.where(hit, scores, 0.0), axis=1, keepdims=True), wts)
        work = jnp.where(hit, -jnp.inf, work)
    ids_ref[...] = ids
    wts_ref[...] = wts * (ROUTED_SCALE / jnp.sum(chosen, axis=1, keepdims=True))
    ri = lax.broadcasted_iota(jnp.int32, (tm, tm), 0)
    ci = lax.broadcasted_iota(jnp.int32, (tm, tm), 1)
    before = _dot((ci < ri).astype(BF16), sel.astype(BF16))
    run = run_ref[0:1, :]
    rank_ref[...] = (before + run).astype(jnp.int32)
    run = run + jnp.sum(sel, axis=0, keepdims=True)
    run_ref[...] = jnp.broadcast_to(run, run_ref.shape)
    cnt_ref[...] = jnp.broadcast_to(run, cnt_ref.shape).astype(jnp.int32)


def _router(xb, router_w, router_bias):
    t, d = xb.shape
    n_exp = router_w.shape[1]
    assert n_exp < LANES
    tm = _tile(t, 512, 8)
    wr = _pad_cols(router_w, LANES).astype(BF16)
    br = _pad_cols(router_bias.reshape(1, n_exp), LANES)
    tok = pl.BlockSpec((tm, LANES), lambda i: (i, 0))
    tok_shape = lambda dt: jax.ShapeDtypeStruct((t, LANES), dt)
    return pl.pallas_call(
        functools.partial(_router_kernel, n_exp=n_exp),
        grid=(t // tm,),
        in_specs=[pl.BlockSpec((tm, d), lambda i: (i, 0)), pl.BlockSpec((d, LANES), lambda i: (0, 0)),
                  pl.BlockSpec((1, LANES), lambda i: (0, 0))],
        out_specs=[tok, tok, tok, pl.BlockSpec((8, LANES), lambda i: (0, 0))],
        out_shape=[tok_shape(jnp.int32), tok_shape(F32), tok_shape(jnp.int32),
                   jax.ShapeDtypeStruct((8, LANES), jnp.int32)],
        scratch_shapes=[pltpu.VMEM((8, LANES), F32)],
        compiler_params=_cparams("arbitrary"),
        name="router",
    )(xb, wr, br)


def _shared_kernel(x_ref, wgu_ref, wd_ref, o_ref, *, hid):
    gu = _dot(x_ref[...], wgu_ref[...])
    gate = gu[:, :hid]
    h = (gate * jax.nn.sigmoid(gate) * gu[:, hid:]).astype(BF16)
    o_ref[...] = _dot(h, wd_ref[...])


def _shared_expert(xb, w_gate, w_up, w_down):
    t, d = xb.shape
    hid = w_gate.shape[1]
    wgu = jnp.concatenate([w_gate, w_up], axis=1).astype(BF16)
    wd = w_down.astype(BF16)
    tm = _tile(t, 512, 8)
    full = lambda a: pl.BlockSpec(a.shape, lambda i: (0, 0))
    return pl.pallas_call(
        functools.partial(_shared_kernel, hid=hid),
        grid=(t // tm,),
        in_specs=[pl.BlockSpec((tm, d), lambda i: (i, 0)), full(wgu), full(wd)],
        out_specs=pl.BlockSpec((tm, d), lambda i: (i, 0)),
        out_shape=jax.ShapeDtypeStruct((t, d), F32),
        compiler_params=_cparams("parallel"),
        name="shared_expert",
    )(xb, wgu, wd)


MOE_ROWS = 512
SC_WINDOW = 16


def _sc_mesh():
    return plsc.VectorSubcoreMesh(core_axis_name="core", subcore_axis_name="subcore")


def _sc_windows(n_items):
    mesh = _sc_mesh()
    workers = mesh.num_cores * mesh.num_subcores
    n_win = n_items // LANES
    assert n_items % LANES == 0
    return mesh, n_win, -(-n_win // workers)


def _sc_worker(mesh):
    return lax.axis_index("core") * mesh.num_subcores + lax.axis_index("subcore")


def _sc_scatter_rows(x, pos, n_rows):
    t, d = x.shape
    k = pos.shape[0]
    mesh, n_win, per = _sc_windows(t)

    n_sub = LANES // SC_WINDOW

    @pl.kernel(out_type=jax.ShapeDtypeStruct((n_rows, d), x.dtype), mesh=mesh,
               scratch_types=[pltpu.VMEM((k, LANES), jnp.int32), pltpu.VMEM((2, SC_WINDOW, d), x.dtype),
                              pltpu.SemaphoreType.DMA, pltpu.SemaphoreType.DMA((2,))])
    def scatter(x_hbm, pos_hbm, o_hbm, idx_vmem, buf, sem_in, sem_out):
        first = _sc_worker(mesh) * per

        @pl.loop(0, per)
        def _(step):
            win = first + step

            @pl.when(win < n_win)
            def _():
                base = win * LANES
                pltpu.sync_copy(pos_hbm.at[:, pl.ds(base, LANES)], idx_vmem)
                load = lambda j: pltpu.make_async_copy(
                    x_hbm.at[pl.ds(base + j * SC_WINDOW, SC_WINDOW)], buf.at[j % 2], sem_in)
                store = lambda j, kk: pltpu.make_async_copy(
                    buf.at[j % 2], o_hbm.at[idx_vmem[kk, pl.ds(j * SC_WINDOW, SC_WINDOW)]], sem_out.at[j % 2])
                load(0).start()
                for j in range(n_sub):
                    load(j).wait()
                    if j >= 1:
                        for kk in range(k):
                            store(j - 1, kk).wait()
                    if j + 1 < n_sub:
                        load(j + 1).start()
                    for kk in range(k):
                        store(j, kk).start()
                for kk in range(k):
                    store(n_sub - 1, kk).wait()

    return scatter(x, pos)


def _sc_gather_rows(y, idx):
    m = idx.shape[1]
    d = y.shape[1]
    mesh, n_win, per = _sc_windows(m)

    n_sub = LANES // SC_WINDOW

    @pl.kernel(out_type=jax.ShapeDtypeStruct((m, d), y.dtype), mesh=mesh,
               scratch_types=[pltpu.VMEM((1, LANES), jnp.int32), pltpu.VMEM((2, SC_WINDOW, d), y.dtype),
                              pltpu.SemaphoreType.DMA, pltpu.SemaphoreType.DMA((2,))])
    def gather(y_hbm, i_hbm, o_hbm, idx_vmem, buf, sem_in, sem_out):
        first = _sc_worker(mesh) * per

        @pl.loop(0, per)
        def _(step):
            win = first + step

            @pl.when(win < n_win)
            def _():
                base = win * LANES
                pltpu.sync_copy(i_hbm.at[:, pl.ds(base, LANES)], idx_vmem)
                load = lambda j: pltpu.make_async_copy(
                    y_hbm.at[idx_vmem[0, pl.ds(j * SC_WINDOW, SC_WINDOW)]], buf.at[j % 2], sem_in)
                store = lambda j: pltpu.make_async_copy(
                    buf.at[j % 2], o_hbm.at[pl.ds(base + j * SC_WINDOW, SC_WINDOW)], sem_out.at[j % 2])
                load(0).start()
                for j in range(n_sub):
                    load(j).wait()
                    if j >= 1:
                        store(j - 1).wait()
                    if j + 1 < n_sub:
                        load(j + 1).start()
                    store(j).start()
                store(n_sub - 1).wait()

    return gather(y, idx)


def _grouped_kernel(te_ref, nu_ref, x_ref, wg_ref, wu_ref, wd_ref, o_ref, wgu_bf, wd_bf, *, hid):
    i = pl.program_id(0)

    @pl.when(jnp.logical_or(i == 0, te_ref[i] != te_ref[jnp.maximum(i - 1, 0)]))
    def _():
        wgu_bf[:, :hid] = wg_ref[...].astype(BF16)
        wgu_bf[:, hid:] = wu_ref[...].astype(BF16)
        wd_bf[...] = wd_ref[...].astype(BF16)

    @pl.when(i < nu_ref[0])
    def _():
        lo, hi = _unpack_pairs(x_ref[...])
        x = jnp.concatenate([lo.astype(BF16), hi.astype(BF16)], axis=1)
        gu = _dot(x, wgu_bf[...])
        gate = gu[:, :hid]
        h = (gate * jax.nn.sigmoid(gate) * gu[:, hid:]).astype(BF16)
        o_ref[...] = _pack_pairs(_dot(h, wd_bf[...]))


def _grouped_swiglu(xs, tile_expert, n_used, w_gate, w_up, w_down, layer):
    n, half = xs.shape
    d = 2 * half
    hid = w_down.shape[2]
    once = pl.Buffered(1)
    grid_spec = pltpu.PrefetchScalarGridSpec(
        num_scalar_prefetch=2,
        grid=(n // MOE_ROWS,),
        in_specs=[pl.BlockSpec((MOE_ROWS, half), lambda i, te, nu: (i, 0)),
                  pl.BlockSpec((None, None, d, hid), lambda i, te, nu: (layer, te[i], 0, 0), pipeline_mode=once),
                  pl.BlockSpec((None, None, d, hid), lambda i, te, nu: (layer, te[i], 0, 0), pipeline_mode=once),
                  pl.BlockSpec((None, None, hid, d), lambda i, te, nu: (layer, te[i], 0, 0))],
        out_specs=pl.BlockSpec((MOE_ROWS, half), lambda i, te, nu: (i, 0)),
        scratch_shapes=[pltpu.VMEM((d, 2 * hid), BF16), pltpu.VMEM((hid, d), BF16)],
    )
    return pl.pallas_call(
        functools.partial(_grouped_kernel, hid=hid),
        grid_spec=grid_spec,
        out_shape=jax.ShapeDtypeStruct((n, half), jnp.int32),
        compiler_params=_cparams("arbitrary"),
        name="moe_grouped",
    )(tile_expert, n_used, xs, w_gate, w_up, w_down)


def _combine_ln_kernel(x_ref, yg_ref, w_ref, fs_ref, g_ref, b_ref, o_ref, ob_ref, *, alpha):
    w = w_ref[...]
    acc_lo, acc_hi = None, None
    for k in range(TOP_K):
        lo, hi = _unpack_pairs(yg_ref[k])
        wk = w[:, k:k + 1]
        acc_lo = wk * lo if k == 0 else acc_lo + wk * lo
        acc_hi = wk * hi if k == 0 else acc_hi + wk * hi
    acc = alpha * x_ref[...] + fs_ref[...] + jnp.concatenate([acc_lo, acc_hi], axis=1)
    mu = jnp.mean(acc, axis=-1, keepdims=True)
    xc = acc - mu
    var = jnp.mean(xc * xc, axis=-1, keepdims=True)
    y = xc * lax.rsqrt(var + LN_EPS) * g_ref[...] + b_ref[...]
    o_ref[...] = y
    ob_ref[...] = y.astype(BF16)


def _combine_ln(x, yg, wts, f_shared, g, b, alpha):
    t, d = x.shape
    tm = _tile(t, 64, 8)
    row = lambda width: pl.BlockSpec((tm, width), lambda i: (i, 0))
    vec = pl.BlockSpec((1, d), lambda i: (0, 0))
    return pl.pallas_call(
        functools.partial(_combine_ln_kernel, alpha=alpha),
        grid=(t // tm,),
        in_specs=[row(d), pl.BlockSpec((TOP_K, tm, d // 2), lambda i: (0, i, 0)), row(LANES), row(d), vec, vec],
        out_specs=[row(d), row(d)],
        out_shape=[jax.ShapeDtypeStruct((t, d), F32), jax.ShapeDtypeStruct((t, d), BF16)],
        compiler_params=_cparams("parallel"),
        name="moe_combine_ln",
    )(x, yg, wts, f_shared, g.reshape(1, d), b.reshape(1, d))


def _routed_ln(xf, xpk, ids, wts, rank, counts, w_gate, w_up, w_down, layer, f_shared, g, b, alpha):
    t, d = xf.shape
    n_exp = w_gate.shape[1]
    n_rows = t * TOP_K + n_exp * MOE_ROWS
    n_tiles = n_rows // MOE_ROWS
    cnt = counts[0, :n_exp]
    padded = (cnt + MOE_ROWS - 1) // MOE_ROWS * MOE_ROWS
    ends = jnp.cumsum(padded)
    starts = ends - padded
    tile_start = jnp.arange(n_tiles, dtype=jnp.int32) * MOE_ROWS
    tile_expert = jnp.minimum(jnp.sum((ends[None, :] <= tile_start[:, None]).astype(jnp.int32), axis=1), n_exp - 1)
    n_used = (ends[-1:] // MOE_ROWS).astype(jnp.int32)
    top = ids[:, :TOP_K]
    pos_t = (starts[top] + jnp.take_along_axis(rank, top, axis=1)).astype(jnp.int32).T
    xs = _sc_scatter_rows(xpk, pos_t, n_rows)
    ys = _grouped_swiglu(xs, tile_expert, n_used, w_gate, w_up, w_down, layer)
    yg = _sc_gather_rows(ys, pos_t.reshape(1, TOP_K * t)).reshape(TOP_K, t, d // 2)
    return _combine_ln(xf, yg, wts, f_shared, g, b, alpha)


PROJ_ALIGN = 512


def _w_in_plan(d, n_cols):
    wdt = d // N_BRANCH
    gla_heads = wdt // GLA_DV
    names = (('gate', N_BRANCH * d), ('mla_cq', MLA_Q_RANK), ('mla_ckv', MLA_KV_RANK), ('mla_kr', MLA_ROPE),
             ('dsa_q', wdt), ('dsa_k', DSA_DH), ('dsa_v', DSA_DH),
             ('idx_q', IDX_HEADS * IDX_DIM), ('idx_k', IDX_DIM), ('idx_w', IDX_HEADS),
             ('gla_q', gla_heads * GLA_DK), ('gla_k', gla_heads * GLA_DK), ('gla_v', wdt),
             ('gla_a', GLA_GATE_RANK), ('gla_r', wdt),
             ('rwkv_r', wdt), ('rwkv_k', wdt), ('rwkv_v', wdt),
             ('rwkv_w', RWKV_LORA), ('rwkv_a', RWKV_LORA), ('rwkv_g', RWKV_LORA))
    src, off = {}, 0
    for name, width in names:
        src[name] = (off, width)
        off += width
    assert off == n_cols, (off, n_cols)

    def layout(items):
        copies, dst = [], 0
        for name, slot, scale in items:
            if name is not None:
                copies.append((dst, src[name][0], src[name][1], scale))
            dst += slot
        return copies, -(-dst // PROJ_ALIGN) * PROJ_ALIGN

    half = MLA_ROPE // 2
    kr0 = src['mla_kr'][0]
    lat, lat_w = layout([('mla_cq', MLA_Q_RANK, None), ('mla_ckv', MLA_KV_RANK, None), ('mla_kr', LANES, None),
                         (None, LANES, None)])
    swap_dst = MLA_Q_RANK + MLA_KV_RANK + LANES
    lat += [(swap_dst, kr0 + half, half, None), (swap_dst + half, kr0, half, None)]
    iq0 = src['idx_q'][0]
    dsa, dsa_w = layout([(None, IDX_HEADS * LANES, None), ('dsa_q', wdt, DSA_DH ** -0.5 * LOG2E),
                         ('dsa_k', DSA_DH, None), ('dsa_v', DSA_DH, None), ('idx_k', LANES, None),
                         ('idx_w', LANES, None)])
    dsa += [(h * LANES, iq0 + h * IDX_DIM, IDX_DIM, None) for h in range(IDX_HEADS)]
    gla, gla_w = layout([('gla_v', wdt, None), ('gla_r', wdt, None), ('gla_q', gla_heads * GLA_DK, None),
                         ('gla_k', gla_heads * GLA_DK, None), ('gla_a', LANES, None)])
    rwkv, rwkv_w = layout([('rwkv_r', wdt, None), ('rwkv_k', wdt, None), ('rwkv_v', wdt, None),
                           ('rwkv_w', RWKV_LORA, None), ('rwkv_a', RWKV_LORA, None), ('rwkv_g', LANES, None)])
    return (lat, dsa, gla, rwkv), (lat_w, dsa_w, gla_w, rwkv_w)


def _regroup_kernel(w_ref, gate_ref, *out_refs, plans, gate_w):
    gate_ref[...] = w_ref[:, :gate_w].astype(BF16)
    for o_ref, plan in zip(out_refs, plans):
        o_ref[...] = jnp.zeros_like(o_ref)
        for dst, src, width, scale in plan:
            piece = w_ref[:, src:src + width]
            if scale is not None:
                piece = piece * scale
            o_ref[:, dst:dst + width] = piece.astype(BF16)


def _regroup_w_in(w_in, layer, d):
    n_cols = w_in.shape[2]
    plans, widths = _w_in_plan(d, n_cols)
    gate_w = N_BRANCH * d
    tr = _tile(d, 128, 8)
    out = lambda width: pl.BlockSpec((tr, width), lambda i: (i, 0))
    return pl.pallas_call(
        functools.partial(_regroup_kernel, plans=plans, gate_w=gate_w),
        grid=(d // tr,),
        in_specs=[pl.BlockSpec((None, tr, n_cols), lambda i: (layer, i, 0))],
        out_specs=[out(gate_w)] + [out(wd) for wd in widths],
        out_shape=[jax.ShapeDtypeStruct((d, gate_w), BF16)] + [jax.ShapeDtypeStruct((d, wd), BF16) for wd in widths],
        compiler_params=_cparams("parallel"),
        name="regroup_w_in",
    )(w_in)


def kernel(x, positions, ln_in_g, ln_in_b, w_in, w_branch, w_out, mla_q_norm, mla_w_uq, mla_kv_norm, mla_w_ukv,
           gla_w_gate2, gla_b_gate, gla_norm_g, rwkv_mu_rkv, rwkv_mu_lora, rwkv_w0, rwkv_w2, rwkv_a0, rwkv_a2,
           rwkv_g2, rwkv_k_k, rwkv_k_a, rwkv_r_k, rwkv_lnx_g, rwkv_lnx_b, rwkv_v0, rwkv_v1, rwkv_v2,
           ln_mix_g, ln_mix_b, router_w, router_bias, exp_w_gate, exp_w_up, exp_w_down,
           sh_w_gate, sh_w_up, sh_w_down, ln_ffn_g, ln_ffn_b):
    batch, seq, d = x.shape
    depth = w_in.shape[0]
    t, wdt = batch * seq, d // N_BRANCH
    alpha = (2 * depth) ** 0.25
    ct, st = _rope_tables(positions)
    xf, xb = _layer_norm(x.reshape(t, d), (), ln_in_g, ln_in_b)
    v_first = None
    for l in range(depth):
        w_gate, w_lat, w_dsa, w_gla, w_rwkv = _regroup_w_in(w_in, l, d)
        lat = _matmul(xb, w_lat, F32)
        pb = _matmul(xb, w_dsa, BF16)
        pc = _matmul(xb, w_gla, F32)
        pd = _matmul(xb, w_rwkv, F32)

        y_mla = _mla(lat, ct, st, mla_q_norm[l], mla_w_uq[l], mla_kv_norm[l], mla_w_ukv[l], batch, seq)
        y_dsa = _dsa(pb, batch, seq, wdt)
        y_gla = _gla(pc, gla_w_gate2[l], gla_b_gate[l], gla_norm_g[l], batch, seq)
        vres = None if l == 0 else (rwkv_v0[l - 1], rwkv_v1[l - 1], rwkv_v2[l - 1])
        y_rwkv, v_first = _rwkv(pd, rwkv_mu_rkv[l], rwkv_mu_lora[l], rwkv_w0[l], rwkv_w2[l], rwkv_a0[l], rwkv_a2[l],
                                rwkv_g2[l], rwkv_k_k[l], rwkv_k_a[l], rwkv_r_k[l], rwkv_lnx_g[l], rwkv_lnx_b[l],
                                v_first, vres, batch, seq)

        merged = _merge(xb, (y_mla, y_dsa, y_gla, y_rwkv), w_gate, w_branch[l].astype(BF16))
        xf, xb, xpk = _layer_norm(xf, (_matmul(merged, w_out[l].astype(BF16), F32),), ln_mix_g[l], ln_mix_b[l], alpha,
                                  packed=True)

        ids, wts, rank, counts = _router(xb, router_w[l], router_bias[l])
        f_shared = _shared_expert(xb, sh_w_gate[l], sh_w_up[l], sh_w_down[l])
        xf, xb = _routed_ln(xf, xpk, ids, wts, rank, counts, exp_w_gate, exp_w_up, exp_w_down, l,
                            f_shared, ln_ffn_g[l], ln_ffn_b[l], alpha)
    return xf.reshape(batch, seq, d)
```

```python
import functools
import math

import jax
import jax.numpy as jnp
from jax import lax
from jax.experimental import pallas as pl
from jax.experimental.pallas import tpu as pltpu
from jax.experimental.pallas import tpu_sc as plsc

F32 = jnp.float32
BF16 = jnp.bfloat16

LANES = 128

MLA_NOPE, MLA_ROPE, MLA_V = 128, 64, 128
MLA_Q_RANK, MLA_KV_RANK = 768, 256
ROPE_THETA = 10000.0
DSA_DH = 128
IDX_HEADS, IDX_DIM, IDX_TOPK_MAX = 16, 64, 256
GLA_DV, GLA_DK, GLA_GATE_RANK, GLA_TAU, GLA_CHUNK = 256, 128, 16, 16.0, 64
GLA_SUB = 16
RWKV_HS, RWKV_LORA, RWKV_GN_EPS = 64, 64, 64e-5
RWKV_CHUNK = 64
TOP_K, ROUTED_SCALE = 8, 2.5
LN_EPS, RMS_EPS = 1e-5, 1e-6
NEG_BIG = -1e30
LOG2E = math.log2(math.e)

VMEM_LIMIT = 56 * 1024 * 1024


def _cparams(*sem):
    return pltpu.CompilerParams(dimension_semantics=sem, vmem_limit_bytes=VMEM_LIMIT)


def _tile(n, pref, unit=LANES):
    if n <= pref:
        return n
    t = (pref // unit) * unit
    while t > unit and n % t:
        t -= unit
    assert n % t == 0, (n, pref, unit)
    return t


def _dot(a, b):
    return jnp.dot(a, b, preferred_element_type=F32)


def _dot_t(a, b):
    return lax.dot_general(a, b, (((1,), (1,)), ((), ())), preferred_element_type=F32)


def _dot_f32(a, b):
    return jnp.dot(a, b, preferred_element_type=F32, precision=lax.Precision.HIGHEST)


def _masked_sums(mask, x):
    w = x.shape[1]
    hi = x.astype(BF16)
    rest = x - hi.astype(F32)
    mid = rest.astype(BF16)
    lo = (rest - mid.astype(F32)).astype(BF16)
    s = _dot(mask, jnp.concatenate([hi, mid, lo], axis=1))
    return s[:, :w] + s[:, w:2 * w] + s[:, 2 * w:]


def _mm_kernel(a_ref, b_ref, o_ref):
    o_ref[...] = _dot(a_ref[...], b_ref[...]).astype(o_ref.dtype)


def _matmul(a, b, out_dtype, tm=1024, tn=512):
    m, k = a.shape
    n = b.shape[1]
    tm, tn = _tile(m, tm, 8), _tile(n, tn)
    return pl.pallas_call(
        _mm_kernel,
        grid=(m // tm, n // tn),
        in_specs=[pl.BlockSpec((tm, k), lambda i, j: (i, 0)), pl.BlockSpec((k, tn), lambda i, j: (0, j))],
        out_specs=pl.BlockSpec((tm, tn), lambda i, j: (i, j)),
        out_shape=jax.ShapeDtypeStruct((m, n), out_dtype),
        compiler_params=_cparams("parallel", "parallel"),
        name="matmul",
    )(a, b)


def _pack_pairs(v):
    half = v.shape[1] // 2
    bits = lax.bitcast_convert_type(v.astype(BF16).astype(F32), jnp.int32)
    return (bits[:, half:] & jnp.int32(-65536)) | lax.shift_right_logical(bits[:, :half], 16)


def _unpack_pairs(w):
    return (lax.bitcast_convert_type(lax.shift_left(w, 16), F32),
            lax.bitcast_convert_type(w & jnp.int32(-65536), F32))


def _ln_kernel(*refs, alpha, n_res, packed):
    x_ref, res_refs = refs[0], refs[1:1 + n_res]
    g_ref, b_ref, o_ref, ob_ref = refs[1 + n_res:5 + n_res]
    x = x_ref[...]
    if n_res:
        x = alpha * x
        for f_ref in res_refs:
            x = x + f_ref[...]
    mu = jnp.mean(x, axis=-1, keepdims=True)
    xc = x - mu
    var = jnp.mean(xc * xc, axis=-1, keepdims=True)
    y = xc * lax.rsqrt(var + LN_EPS) * g_ref[...] + b_ref[...]
    o_ref[...] = y
    ob_ref[...] = y.astype(BF16)
    if packed:
        refs[-1][...] = _pack_pairs(y)


def _layer_norm(x, res, g, b, alpha=1.0, packed=False):
    t, d = x.shape
    tm = _tile(t, 128, 8)
    row = pl.BlockSpec((tm, d), lambda i: (i, 0))
    vec = pl.BlockSpec((1, d), lambda i: (0, 0))
    args = (x,) + tuple(res)
    extra_spec = [pl.BlockSpec((tm, d // 2), lambda i: (i, 0))] if packed else []
    extra_shape = [jax.ShapeDtypeStruct((t, d // 2), jnp.int32)] if packed else []
    return pl.pallas_call(
        functools.partial(_ln_kernel, alpha=alpha, n_res=len(res), packed=packed),
        grid=(t // tm,),
        in_specs=[row] * len(args) + [vec, vec],
        out_specs=[row, row] + extra_spec,
        out_shape=[jax.ShapeDtypeStruct((t, d), F32), jax.ShapeDtypeStruct((t, d), BF16)] + extra_shape,
        compiler_params=_cparams("parallel"),
        name="layer_norm",
    )(*args, g.reshape(1, d), b.reshape(1, d))


def _pad_cols(w, width):
    return jnp.pad(w, ((0, 0), (0, width - w.shape[1])))


def _rms(x, g):
    return x * lax.rsqrt(jnp.mean(x * x, axis=-1, keepdims=True) + RMS_EPS) * g


def _mla_prep_kernel(lat_ref, ct_ref, st_ref, qg_ref, kg_ref, wqn_ref, wqr_ref, wqs_ref, wkn_ref, wv_ref,
                     qn_ref, qr_ref, kn_ref, kr_ref, v_ref, *, heads, scale):
    lat = lat_ref[...]
    cq = lat[:, :MLA_Q_RANK]
    ckv = lat[:, MLA_Q_RANK:MLA_Q_RANK + MLA_KV_RANK]
    kr = lat[:, MLA_Q_RANK + MLA_KV_RANK:MLA_Q_RANK + MLA_KV_RANK + LANES]
    krs = lat[:, MLA_Q_RANK + MLA_KV_RANK + LANES:MLA_Q_RANK + MLA_KV_RANK + 2 * LANES]
    ct, st = ct_ref[...], st_ref[...]
    nq = _rms(cq, qg_ref[...]).astype(BF16)
    nkv = _rms(ckv, kg_ref[...]).astype(BF16)
    qn_ref[...] = (_dot(nq, wqn_ref[...]) * scale).astype(BF16)
    cth = jnp.concatenate([ct] * heads, axis=1)
    sth = jnp.concatenate([st] * heads, axis=1)
    qr_ref[...] = ((_dot(nq, wqr_ref[...]) * cth + _dot(nq, wqs_ref[...]) * sth) * scale).astype(BF16)
    kn_ref[...] = _dot(nkv, wkn_ref[...]).astype(BF16)
    v_ref[...] = _dot(nkv, wv_ref[...]).astype(BF16)
    kr_ref[...] = (kr * ct + krs * st).astype(BF16)


def _flash_kernel(qn_ref, qr_ref, kn_ref, kr_ref, v_ref, o_ref, m_ref, l_ref, acc_ref, *, tq, tk):
    i, j = pl.program_id(2), pl.program_id(3)

    @pl.when(j == 0)
    def _():
        m_ref[...] = jnp.full_like(m_ref, NEG_BIG)
        l_ref[...] = jnp.zeros_like(l_ref)
        acc_ref[...] = jnp.zeros_like(acc_ref)

    def step(masked):
        q = jnp.concatenate([qn_ref[...], qr_ref[...]], axis=1)
        k = jnp.concatenate([kn_ref[...], kr_ref[...]], axis=1)
        s = _dot_t(q, k)
        if masked:
            qpos = i * tq + lax.broadcasted_iota(jnp.int32, (tq, tk), 0)
            kpos = j * tk + lax.broadcasted_iota(jnp.int32, (tq, tk), 1)
            s = jnp.where(kpos <= qpos, s, -jnp.inf)
        m_prev = m_ref[...]
        m_new = jnp.maximum(m_prev, jnp.max(s, axis=1, keepdims=True))
        alpha = jnp.exp2(m_prev - m_new)
        p = jnp.exp2(s - jnp.concatenate([m_new] * (tk // LANES), axis=1))
        l_ref[...] = alpha * l_ref[...] + jnp.sum(p, axis=1, keepdims=True)
        acc_ref[...] = alpha * acc_ref[...] + _dot(p.astype(BF16), v_ref[...])
        m_ref[...] = m_new

    below = j * tk + tk - 1 <= i * tq
    touches = j * tk <= i * tq + tq - 1

    @pl.when(below)
    def _():
        step(False)

    @pl.when(jnp.logical_and(touches, jnp.logical_not(below)))
    def _():
        step(True)

    @pl.when(j == pl.num_programs(3) - 1)
    def _():
        o_ref[...] = (acc_ref[...] / l_ref[...]).astype(o_ref.dtype)


def _rope_tables(positions):
    half = MLA_ROPE // 2
    inv = ROPE_THETA ** (-jnp.arange(0, MLA_ROPE, 2, dtype=F32) / MLA_ROPE)
    ang = positions.reshape(-1).astype(F32)[:, None] * inv
    cos, sin = jnp.cos(ang), jnp.sin(ang)
    zero = jnp.zeros((ang.shape[0], LANES - 2 * half), F32)
    return jnp.concatenate([cos, cos, zero], 1), jnp.concatenate([-sin, sin, zero], 1)


def _swap_halves(w):
    half = w.shape[-1] // 2
    return jnp.concatenate([w[..., half:], w[..., :half]], -1)


def _mla(lat, ct, st, q_norm, w_uq, kv_norm, w_ukv, batch, seq):
    t = lat.shape[0]
    heads = w_uq.shape[1] // (MLA_NOPE + MLA_ROPE)
    hw = heads * LANES
    wq = w_uq.reshape(MLA_Q_RANK, heads, MLA_NOPE + MLA_ROPE)
    wqn = wq[:, :, :MLA_NOPE].reshape(MLA_Q_RANK, hw).astype(BF16)
    rope_pad = ((0, 0), (0, 0), (0, LANES - MLA_ROPE))
    wqr = jnp.pad(wq[:, :, MLA_NOPE:], rope_pad).reshape(MLA_Q_RANK, hw).astype(BF16)
    wqs = jnp.pad(_swap_halves(wq[:, :, MLA_NOPE:]), rope_pad).reshape(MLA_Q_RANK, hw).astype(BF16)
    wkv = w_ukv.reshape(MLA_KV_RANK, heads, MLA_NOPE + MLA_V)
    wkn = wkv[:, :, :MLA_NOPE].reshape(MLA_KV_RANK, hw).astype(BF16)
    wv = wkv[:, :, MLA_NOPE:].reshape(MLA_KV_RANK, hw).astype(BF16)

    tm = _tile(t, 512, 8)
    row = lambda w: pl.BlockSpec((tm, w), lambda i: (i, 0))
    full = lambda a: pl.BlockSpec(a.shape, lambda i: (0, 0))
    qg, kg = q_norm.reshape(1, -1), kv_norm.reshape(1, -1)
    qn, qr, kn, kr, v = pl.pallas_call(
        functools.partial(_mla_prep_kernel, heads=heads, scale=(MLA_NOPE + MLA_ROPE) ** -0.5 * LOG2E),
        grid=(t // tm,),
        in_specs=[row(lat.shape[1]), row(LANES), row(LANES), full(qg), full(kg),
                  full(wqn), full(wqr), full(wqs), full(wkn), full(wv)],
        out_specs=[row(hw), row(hw), row(hw), row(LANES), row(hw)],
        out_shape=[jax.ShapeDtypeStruct((t, hw), BF16)] * 3 + [jax.ShapeDtypeStruct((t, LANES), BF16),
                                                               jax.ShapeDtypeStruct((t, hw), BF16)],
        compiler_params=_cparams("parallel"),
        name="mla_prep",
    )(lat, ct, st, qg, kg, wqn, wqr, wqs, wkn, wv)

    tq = tk = _tile(seq, 1024, 8)
    nq, nk = seq // tq, seq // tk
    r3 = lambda a: a.reshape(batch, seq, a.shape[1])
    last = lambda i, j: jnp.minimum(j, ((i + 1) * tq - 1) // tk)
    q_spec = pl.BlockSpec((None, tq, LANES), lambda b, h, i, j: (b, i, h))
    k_spec = pl.BlockSpec((None, tk, LANES), lambda b, h, i, j: (b, last(i, j), h))
    kr_spec = pl.BlockSpec((None, tk, LANES), lambda b, h, i, j: (b, last(i, j), 0))
    out = pl.pallas_call(
        functools.partial(_flash_kernel, tq=tq, tk=tk),
        grid=(batch, heads, nq, nk),
        in_specs=[q_spec, q_spec, k_spec, kr_spec, k_spec],
        out_specs=q_spec,
        out_shape=jax.ShapeDtypeStruct((batch, seq, hw), BF16),
        scratch_shapes=[pltpu.VMEM((tq, LANES), F32), pltpu.VMEM((tq, LANES), F32), pltpu.VMEM((tq, LANES), F32)],
        compiler_params=_cparams("parallel", "parallel", "parallel", "arbitrary"),
        name="mla_flash",
    )(r3(qn), r3(qr), r3(kn), r3(kr), r3(v))
    return out.reshape(t, hw)


DSA_TQ = 256
DSA_TK = 1024
INT_MIN = -2 ** 31


def _dsa_kernel(iq_ref, q_ref, k_ref, v_ref, ik_ref, iw_ref, o_ref, key_ref, m_ref, l_ref, acc_ref,
                *, tq, tk, n_sel, heads, pos_bits):
    i = pl.program_id(1)
    n_kt = (i * tq + tq - 1) // tk + 1
    reps = tk // LANES
    wide = lambda a: jnp.concatenate([a] * reps, axis=1)
    qpos = i * tq + lax.broadcasted_iota(jnp.int32, (tq, tk), 0)
    lane_pos = lax.broadcasted_iota(jnp.int32, (tq, tk), 1)

    iq = iq_ref[...]
    iw = iw_ref[...].astype(F32) * (IDX_HEADS ** -0.5 * IDX_DIM ** -0.5)
    iw_b = [jnp.broadcast_to(iw[:, h:h + 1], (tq, LANES)) for h in range(IDX_HEADS)]

    def score_tile(j, carry):
        ikj = ik_ref[pl.ds(pl.multiple_of(j * tk, tk), tk), :]
        sc = jnp.zeros((tq, tk), F32)
        for h in range(IDX_HEADS):
            logit = _dot_t(iq[:, h * LANES:(h + 1) * LANES], ikj)
            sc = sc + wide(iw_b[h]) * jnp.maximum(logit, 0.0)
        sc = jnp.where(j * tk + lane_pos <= qpos, sc + 0.0, -jnp.inf)
        bits = lax.bitcast_convert_type(sc, jnp.int32)
        key_ref[j] = jnp.where(bits >= 0, bits, bits ^ 0x7FFFFFFF)
        return carry

    lax.fori_loop(0, n_kt, score_tile, 0)

    def count(pred):
        def body(j, acc):
            c = pred(key_ref[j], j * tk + lane_pos).astype(jnp.int32)
            for rep in range(reps):
                acc = acc + c[:, rep * LANES:(rep + 1) * LANES]
            return acc
        acc = lax.fori_loop(0, n_kt, body, jnp.zeros((tq, LANES), jnp.int32))
        return jnp.broadcast_to(jnp.sum(acc, axis=1, keepdims=True), (tq, LANES))

    def thr_bit(bit_i, thr):
        cand = thr ^ jnp.left_shift(jnp.int32(1), 31 - bit_i)
        cnt = count(lambda key, pos: key >= wide(cand))
        return jnp.where(cnt >= n_sel, cand, thr)

    thr = lax.fori_loop(0, 32, thr_bit, jnp.full((tq, LANES), INT_MIN, jnp.int32))
    cnt_gt = count(lambda key, pos: key > wide(thr))
    cnt_ge = count(lambda key, pos: key >= wide(thr))
    need = n_sel - cnt_gt

    def tie_bit(bit_i, cut):
        cand = cut | jnp.left_shift(jnp.int32(1), pos_bits - 1 - bit_i)
        cnt = count(lambda key, pos: jnp.logical_and(key == wide(thr), pos < wide(cand)))
        return jnp.where(cnt < need, cand, cut)

    surplus = jnp.max(cnt_ge - cnt_gt - need) > 0
    cut = lax.cond(surplus,
                   lambda: lax.fori_loop(0, pos_bits, tie_bit, jnp.zeros((tq, LANES), jnp.int32)),
                   lambda: jnp.full((tq, LANES), 2 ** 31 - 1, jnp.int32))

    q = q_ref[...]
    q_all = jnp.concatenate([q[:, h * LANES:(h + 1) * LANES] for h in range(heads)], axis=0)
    m_ref[...] = jnp.full_like(m_ref, NEG_BIG)
    l_ref[...] = jnp.zeros_like(l_ref)
    acc_ref[...] = jnp.zeros_like(acc_ref)

    def attend(j, carry):
        rows = pl.ds(pl.multiple_of(j * tk, tk), tk)
        s = _dot_t(q_all, k_ref[rows, :])
        key, pos = key_ref[j], j * tk + lane_pos
        keep = jnp.where(key > wide(thr), 0.0,
                         jnp.where(jnp.logical_and(key == wide(thr), pos <= wide(cut)), 0.0, -jnp.inf))
        keep = jnp.where(pos <= qpos, keep, -jnp.inf)
        s = s + jnp.concatenate([keep] * heads, axis=0)
        m_prev = m_ref[...]
        m_new = jnp.maximum(m_prev, jnp.max(s, axis=1, keepdims=True))
        alpha = jnp.exp2(m_prev - m_new)
        p = jnp.exp2(s - jnp.concatenate([m_new] * reps, axis=1))
        l_ref[...] = alpha * l_ref[...] + jnp.sum(p, axis=1, keepdims=True)
        acc_ref[...] = alpha * acc_ref[...] + _dot(p.astype(BF16), v_ref[rows, :])
        m_ref[...] = m_new
        return carry

    lax.fori_loop(0, n_kt, attend, 0)
    out = acc_ref[...] / l_ref[...]
    o_ref[...] = jnp.concatenate([out[h * tq:(h + 1) * tq] for h in range(heads)], axis=1).astype(o_ref.dtype)


def _dsa(pb, batch, seq, width):
    t, wb = pb.shape
    heads = width // DSA_DH
    iqw = IDX_HEADS * LANES
    tq, tk = _tile(seq, DSA_TQ, 8), _tile(seq, DSA_TK)
    n_sel = min(IDX_TOPK_MAX, seq // 4)
    c0 = (iqw + width) // LANES
    pb3 = pb.reshape(batch, seq, wb)
    qblk = lambda w, idx: pl.BlockSpec((None, tq, w), lambda b, i: (b, i, idx))
    seqblk = lambda idx: pl.BlockSpec((None, seq, LANES), lambda b, i: (b, 0, idx))
    out = pl.pallas_call(
        functools.partial(_dsa_kernel, tq=tq, tk=tk, n_sel=n_sel, heads=heads,
                          pos_bits=max(1, (seq - 1).bit_length())),
        grid=(batch, seq // tq),
        in_specs=[qblk(iqw, 0), qblk(width, iqw // width), seqblk(c0), seqblk(c0 + 1), seqblk(c0 + 2),
                  qblk(LANES, c0 + 3)],
        out_specs=qblk(width, 0),
        out_shape=jax.ShapeDtypeStruct((batch, seq, width), BF16),
        scratch_shapes=[pltpu.VMEM((seq // tk, tq, tk), jnp.int32), pltpu.VMEM((heads * tq, LANES), F32),
                        pltpu.VMEM((heads * tq, LANES), F32), pltpu.VMEM((heads * tq, LANES), F32)],
        compiler_params=_cparams("parallel", "arbitrary"),
        name="dsa",
    )(pb3, pb3, pb3, pb3, pb3, pb3)
    return out.reshape(t, width)


def _gla_kernel(q_ref, k_ref, v_ref, a_ref, r_ref, wg_ref, bg_ref, ng_ref, o_ref, state_ref):
    c_len, sub = GLA_CHUNK, GLA_SUB

    @pl.when(pl.program_id(2) == 0)
    def _():
        state_ref[...] = jnp.zeros_like(state_ref)

    z_all = _dot(a_ref[...].astype(BF16), wg_ref[...]) + bg_ref[...]
    g_all = jax.nn.log_sigmoid(z_all) / GLA_TAU
    ri = lax.broadcasted_iota(jnp.int32, (c_len, c_len), 0)
    ci = lax.broadcasted_iota(jnp.int32, (c_len, c_len), 1)
    b_all = _masked_sums((ri >= ci).astype(BF16), g_all)
    ones = jnp.ones((GLA_DK, LANES), BF16)
    trow = lax.broadcasted_iota(jnp.int32, (sub, GLA_DK), 0)
    lane = lax.broadcasted_iota(jnp.int32, (sub, LANES), 1)
    for hh in range(state_ref.shape[0]):
        ksl = slice(hh * GLA_DK, (hh + 1) * GLA_DK)
        vsl = slice(hh * GLA_DV, (hh + 1) * GLA_DV)
        o = _gla_head(q_ref[:, ksl] * GLA_DK ** -0.5, k_ref[:, ksl], v_ref[:, vsl].astype(BF16), b_all[:, ksl],
                      state_ref.at[hh], ones, trow, lane)
        r = r_ref[:, vsl]
        o_ref[:, vsl] = (_rms(o, ng_ref[:, vsl]) * (r * jax.nn.sigmoid(r))).astype(o_ref.dtype)


def _gla_head(q, k, vb, b, state_ref, ones, trow, lane):
    c_len, sub = GLA_CHUNK, GLA_SUB
    state_t = state_ref[...]
    o_inter = _dot_t((q * jnp.exp(b)).astype(BF16), state_t.astype(BF16))
    outs = []
    for blk in range(c_len // sub):
        lo = blk * sub
        qi, bi, ki = q[lo:lo + sub], b[lo:lo + sub], k[lo:lo + sub]
        rows = [qi * ki[s:s + 1] * jnp.exp(jnp.where(trow >= s, bi - bi[s:s + 1], -jnp.inf)) for s in range(sub)]
        prod = jnp.concatenate(rows, axis=0)
        p_hi = prod.astype(BF16)
        p_lo = (prod - p_hi.astype(F32)).astype(BF16)
        sums = _dot(jnp.concatenate([p_hi, p_lo], axis=0), ones)
        sums = sums[:sub * sub] + sums[sub * sub:]
        attn = jnp.zeros((sub, LANES), F32)
        for s in range(sub):
            attn = attn + jnp.where(lane == s, sums[s * sub:(s + 1) * sub], 0.0)
        o_blk = _dot(attn[:, :sub].astype(BF16), vb[lo:lo + sub])
        if blk:
            qa = qi * jnp.exp(bi - bi[0:1])
            ka = k[:lo] * jnp.exp(bi[0:1] - b[:lo])
            o_blk = o_blk + _dot(_dot_t(qa.astype(BF16), ka.astype(BF16)).astype(BF16), vb[:lo])
        outs.append(o_blk)
    o = o_inter + jnp.concatenate(outs, axis=0)

    b_last = b[c_len - 1:c_len]
    k_dec = (k * jnp.exp(b_last - b)).astype(BF16)
    state_ref[...] = state_t * jnp.exp(b_last) + lax.dot_general(
        vb, k_dec, (((0,), (0,)), ((), ())), preferred_element_type=F32)
    return o


GLA_HEADS_PER_STEP = 4


def _gla(pc, w_gate2, b_gate, norm_g, batch, seq):
    t = pc.shape[0]
    heads = norm_g.shape[0] // GLA_DV
    hps = math.gcd(heads, GLA_HEADS_PER_STEP)
    groups = heads // hps
    wg = jnp.pad(w_gate2, ((0, LANES - GLA_GATE_RANK), (0, 0))).astype(BF16)
    nc = seq // GLA_CHUNK
    blk = lambda w, off: pl.BlockSpec((None, GLA_CHUNK, hps * w), lambda b, g, c: (b, c, off + g))
    par = lambda rows, w: pl.BlockSpec((rows, hps * w), lambda b, g, c: (0, g))
    pc3 = pc.reshape(batch, seq, pc.shape[1])
    out = pl.pallas_call(
        _gla_kernel,
        grid=(batch, groups, nc),
        in_specs=[blk(GLA_DK, 4 * groups), blk(GLA_DK, 5 * groups), blk(GLA_DV, 0),
                  pl.BlockSpec((None, GLA_CHUNK, LANES), lambda b, g, c: (b, c, 6 * heads)),
                  blk(GLA_DV, groups), par(LANES, GLA_DK), par(1, GLA_DK), par(1, GLA_DV)],
        out_specs=blk(GLA_DV, 0),
        out_shape=jax.ShapeDtypeStruct((batch, seq, heads * GLA_DV), BF16),
        scratch_shapes=[pltpu.VMEM((hps, GLA_DV, GLA_DK), F32)],
        compiler_params=_cparams("parallel", "parallel", "arbitrary"),
        name="gla",
    )(pc3, pc3, pc3, pc3, pc3, wg, b_gate.reshape(1, -1), norm_g.reshape(1, -1))
    return out.reshape(t, heads * GLA_DV)


_DIMS = {'nn': (((1,), (0,)), ((), ())), 'nt': (((1,), (1,)), ((), ())), 'tn': (((0,), (0,)), ((), ()))}


def _mm(a, b, form):
    return lax.dot_general(a.astype(BF16), b.astype(BF16), _DIMS[form], preferred_element_type=F32)


def _seg_sum(x, width):
    gi = lax.broadcasted_iota(jnp.int32, (LANES, LANES), 0) // width
    gj = lax.broadcasted_iota(jnp.int32, (LANES, LANES), 1) // width
    ones = (gi == gj).astype(F32)
    return jnp.concatenate([_dot_f32(x[:, c:c + LANES], ones) for c in range(0, x.shape[1], LANES)], axis=1)


def _rwkv_prep_kernel(*refs, width, seq, has_vres):
    if has_vres:
        (p_ref, pp_ref, prm_ref, ml_ref, w2_ref, a2_ref, g2_ref, vf_ref, v0_ref, v1_ref, v2_ref,
         r_ref, lw_ref, k_ref, v_ref, kk_ref, bb_ref, bonus_ref, g_ref) = refs
    else:
        (p_ref, pp_ref, prm_ref, ml_ref, w2_ref, a2_ref, g2_ref,
         r_ref, lw_ref, k_ref, v_ref, kk_ref, bb_ref, bonus_ref, g_ref) = refs
    w = width
    p, prm, ml = p_ref[...], prm_ref[...], ml_ref[...]
    tm = p.shape[0]
    above = jnp.where((pl.program_id(0) * tm) % seq == 0, 0.0, pp_ref[7:8, :])
    first = lax.broadcasted_iota(jnp.int32, p.shape, 0) == 0
    pp = jnp.where(first, jnp.broadcast_to(above, p.shape), pltpu.roll(p, 1, axis=0))
    lerp = lambda lo, hi, mu: p[:, lo:hi] + (pp[:, lo:hi] - p[:, lo:hi]) * mu
    r = lerp(0, w, prm[0:1])
    k = lerp(w, 2 * w, prm[1:2])
    v = lerp(2 * w, 3 * w, prm[2:3])
    x_wa = lerp(3 * w, 3 * w + LANES, ml[0:1])
    x_g = lerp(3 * w + LANES, 3 * w + 2 * LANES, ml[1:2])
    w_log = -jax.nn.softplus(-(prm[3:4] + _dot(jnp.tanh(x_wa).astype(BF16), w2_ref[...]))) - 0.5
    lw_ref[...] = -jnp.exp(w_log)
    a = jax.nn.sigmoid(prm[4:5] + _dot(x_wa.astype(BF16), a2_ref[...]))
    g_ref[...] = _dot(jax.nn.sigmoid(x_g).astype(BF16), g2_ref[...])
    if has_vres:
        mix = jax.nn.sigmoid(v0_ref[...] + _dot(_dot(v.astype(BF16), v1_ref[...]).astype(BF16), v2_ref[...]))
        v = v + (vf_ref[...] - v) * mix
    kk = k * prm[5:6]
    kk = kk * lax.rsqrt(jnp.maximum(_seg_sum(kk * kk, RWKV_HS), 1e-24))
    k2 = k * (1.0 + (a - 1.0) * prm[6:7])
    r_ref[...] = r
    k_ref[...] = k2
    v_ref[...] = v
    kk_ref[...] = kk
    bb_ref[...] = kk * a
    bonus_ref[...] = _seg_sum(r * k2 * prm[7:8], RWKV_HS) * v


RWKV_HEADS_PER_STEP = 8


def _rwkv_chunk_kernel(r_ref, lw_ref, k_ref, v_ref, kk_ref, bb_ref, y_ref, s_ref, *, n_chunk):
    c_len, hs = RWKV_CHUNK, RWKV_HS
    rows_n = n_chunk * c_len

    @pl.when(pl.program_id(2) == 0)
    def _():
        s_ref[...] = jnp.zeros_like(s_ref)

    ri = lax.broadcasted_iota(jnp.int32, (rows_n, rows_n), 0)
    ci = lax.broadcasted_iota(jnp.int32, (rows_n, rows_n), 1)
    same = (ri // c_len) == (ci // c_len)
    incl = jnp.logical_and(same, ci <= ri)
    strict = jnp.logical_and(same, ci < ri)
    eye = (ri == ci).astype(F32)
    e_r = lax.broadcasted_iota(jnp.int32, (hs, hs), 0)
    e_c = lax.broadcasted_iota(jnp.int32, (hs, hs), 1)
    sums = _masked_sums(jnp.concatenate([incl, same], axis=0).astype(BF16), lw_ref[...])
    cum_all, cum_c_all = sums[:rows_n], sums[rows_n:]
    heads = range(s_ref.shape[0])
    hd = []
    for hh in heads:
        sl = slice(hh * hs, (hh + 1) * hs)
        r, lw, k, v, kk, bb = (ref[...][:, sl] for ref in (r_ref, lw_ref, k_ref, v_ref, kk_ref, bb_ref))
        cum, cum_c = cum_all[:, sl], cum_c_all[:, sl]
        g_inv, g_end = jnp.exp(-cum), jnp.exp(cum_c - cum)
        kap, rt = kk * jnp.exp(cum - lw), r * jnp.exp(cum)
        bet, kt = bb * g_inv, k * g_inv
        gram = _mm(jnp.concatenate([kap, rt], axis=0), jnp.concatenate([bet, kt], axis=0), 'nt')
        hd.append(dict(v=v, kap=kap, rt=rt, bet_c=bb * g_end, kt_c=k * g_end, cum_c=cum_c,
                       n_m=jnp.where(strict, gram[:rows_n, :rows_n], 0.0),
                       a_kk=jnp.where(strict, gram[:rows_n, rows_n:], 0.0),
                       a_rb=jnp.where(incl, gram[rows_n:, :rows_n], 0.0),
                       a_rk=jnp.where(incl, gram[rows_n:, rows_n:], 0.0)))
    t_inv = [eye - h['n_m'] for h in hd]
    pw = [-h['n_m'] for h in hd]
    for _ in range(int(math.log2(c_len)) - 1):
        pw = [_mm(p, p, 'nn') for p in pw]
        t_inv = [t + _mm(t, p, 'nn') for t, p in zip(t_inv, pw)]
    av = [_mm(jnp.concatenate([h['a_kk'], h['a_rk']], axis=0), h['v'], 'nn') for h in hd]
    z = [_mm(t, jnp.concatenate([h['kap'], a[:rows_n]], axis=1), 'nn') for t, h, a in zip(t_inv, hd, av)]
    az = [_mm(h['a_rb'], zz, 'nn') for h, zz in zip(hd, z)]
    r_p = [h['rt'] - a[:, :hs] for h, a in zip(hd, az)]
    y0 = [a[rows_n:] - b[:, hs:] for a, b in zip(av, az)]
    s = [s_ref[hh] for hh in heads]
    outs = [[] for _ in heads]
    for c in range(n_chunk):
        rows = slice(c * c_len, (c + 1) * c_len)
        for hh in heads:
            h = hd[hh]
            outs[hh].append(_mm(r_p[hh][rows], s[hh], 'nt') + y0[hh][rows])
            zb = _mm(z[hh][rows], h['bet_c'][rows], 'tn')
            g_chunk = jnp.exp(h['cum_c'][c * c_len:c * c_len + 1])
            m = jnp.where(e_r == e_c, jnp.broadcast_to(g_chunk, (hs, hs)), 0.0) - zb[:hs]
            s[hh] = _mm(s[hh], m, 'nn') + _mm(h['v'][rows], h['kt_c'][rows], 'tn') - zb[hs:]
    for hh in heads:
        s_ref[hh] = s[hh]
    y_ref[...] = jnp.concatenate([jnp.concatenate(o, axis=0) for o in outs], axis=1)


def _rwkv_post_kernel(y_ref, bonus_ref, g_ref, lg_ref, lb_ref, o_ref):
    y = y_ref[...]
    mu = _seg_sum(y, RWKV_HS) * (1.0 / RWKV_HS)
    yc = y - mu
    var = _seg_sum(yc * yc, RWKV_HS) * (1.0 / RWKV_HS)
    yn = yc * lax.rsqrt(var + RWKV_GN_EPS) * lg_ref[...] + lb_ref[...]
    o_ref[...] = ((yn + bonus_ref[...]) * g_ref[...]).astype(o_ref.dtype)


def _rwkv(pd, mu_rkv, mu_lora, w0, w2, a0, a2, g2, k_k, k_a, r_k, lnx_g, lnx_b, v_first, vres, batch, seq):
    t, wd = pd.shape
    w = w0.shape[0]
    prm = jnp.stack([mu_rkv[0], mu_rkv[1], mu_rkv[2], w0, a0, k_k, k_a, r_k.reshape(-1)])
    zl = jnp.zeros((RWKV_LORA,), F32)
    ml = jnp.stack([jnp.concatenate([mu_lora[0], mu_lora[1]]), jnp.concatenate([mu_lora[2], zl])])
    zw = jnp.zeros((LANES - RWKV_LORA, w), F32)
    w2p = jnp.concatenate([w2, zw]).astype(BF16)
    a2p = jnp.concatenate([zw, a2]).astype(BF16)
    g2p = jnp.concatenate([g2, zw]).astype(BF16)
    has_vres = vres is not None
    tm = _tile(seq, 256, 8)
    row = lambda width: pl.BlockSpec((tm, width), lambda i: (i, 0))
    full = lambda a: pl.BlockSpec(a.shape, lambda i: (0, 0))
    above = pl.BlockSpec((8, wd), lambda i: (jnp.maximum(i * (tm // 8) - 1, 0), 0))
    args = [pd, pd, prm, ml, w2p, a2p, g2p]
    specs = [row(wd), above, full(prm), full(ml), full(w2p), full(a2p), full(g2p)]
    if has_vres:
        v0, v1, v2 = vres
        v1p = _pad_cols(v1, LANES).astype(BF16)
        v2p = jnp.pad(v2, ((0, LANES - v2.shape[0]), (0, 0))).astype(BF16)
        v0r = v0.reshape(1, w)
        args += [v_first, v0r, v1p, v2p]
        specs += [row(w), full(v0r), full(v1p), full(v2p)]
    r, lw, k2, v, kk, bb, bonus, g = pl.pallas_call(
        functools.partial(_rwkv_prep_kernel, width=w, seq=seq, has_vres=has_vres),
        grid=(t // tm,),
        in_specs=specs,
        out_specs=[row(w)] * 8,
        out_shape=[jax.ShapeDtypeStruct((t, w), F32)] * 8,
        compiler_params=_cparams("parallel"),
        name="rwkv_prep",
    )(*args)

    n_chunk = 4 if seq % (4 * RWKV_CHUNK) == 0 else 1
    rows_n = n_chunk * RWKV_CHUNK
    bw = math.gcd(w, RWKV_HEADS_PER_STEP * RWKV_HS)
    blk = pl.BlockSpec((None, rows_n, bw), lambda b, h, c: (b, c, h))
    r3 = lambda a: a.reshape(batch, seq, w)
    y = pl.pallas_call(
        functools.partial(_rwkv_chunk_kernel, n_chunk=n_chunk),
        grid=(batch, w // bw, seq // rows_n),
        in_specs=[blk] * 6,
        out_specs=blk,
        out_shape=jax.ShapeDtypeStruct((batch, seq, w), F32),
        scratch_shapes=[pltpu.VMEM((bw // RWKV_HS, RWKV_HS, RWKV_HS), F32)],
        compiler_params=_cparams("parallel", "parallel", "arbitrary"),
        name="rwkv_chunk",
    )(r3(r), r3(lw), r3(k2), r3(v), r3(kk), r3(bb))

    lg, lb = lnx_g.reshape(1, w), lnx_b.reshape(1, w)
    out = pl.pallas_call(
        _rwkv_post_kernel,
        grid=(t // tm,),
        in_specs=[row(w), row(w), row(w), full(lg), full(lb)],
        out_specs=row(w),
        out_shape=jax.ShapeDtypeStruct((t, w), BF16),
        compiler_params=_cparams("parallel"),
        name="rwkv_post",
    )(y.reshape(t, w), bonus, g, lg, lb)
    return out, (v_first if has_vres else v)


N_BRANCH = 4


def _merge_kernel(*refs):
    x_ref = refs[0]
    y_refs = refs[1:1 + N_BRANCH]
    wg_refs = refs[1 + N_BRANCH:1 + 2 * N_BRANCH]
    wb_refs = refs[1 + 2 * N_BRANCH:1 + 3 * N_BRANCH]
    o_ref = refs[-1]
    x = x_ref[...]
    acc = None
    for y_ref, wg_ref, wb_ref in zip(y_refs, wg_refs, wb_refs):
        term = jax.nn.sigmoid(_dot(x, wg_ref[...])) * _dot(y_ref[...], wb_ref[...])
        acc = term if acc is None else acc + term
    o_ref[...] = acc.astype(o_ref.dtype)


def _merge(xb, ys, w_gate, w_branch):
    t, d = xb.shape
    w = ys[0].shape[1]
    tm, tn = _tile(t, 512, 8), _tile(d, 256)
    nj = d // tn
    gate_spec = lambda i: pl.BlockSpec((d, tn), lambda r, j: (0, i * nj + j))
    br_spec = lambda i: pl.BlockSpec((None, w, tn), lambda r, j: (i, 0, j))
    return pl.pallas_call(
        _merge_kernel,
        grid=(t // tm, nj),
        in_specs=[pl.BlockSpec((tm, d), lambda r, j: (r, 0))] + [pl.BlockSpec((tm, w), lambda r, j: (r, 0))] * N_BRANCH
        + [gate_spec(i) for i in range(N_BRANCH)] + [br_spec(i) for i in range(N_BRANCH)],
        out_specs=pl.BlockSpec((tm, tn), lambda r, j: (r, j)),
        out_shape=jax.ShapeDtypeStruct((t, d), BF16),
        compiler_params=_cparams("parallel", "parallel"),
        name="merge",
    )(xb, *ys, *([w_gate] * N_BRANCH), *([w_branch] * N_BRANCH))


def _router_kernel(x_ref, w_ref, b_ref, ids_ref, wts_ref, rank_ref, cnt_ref, run_ref, *, n_exp):
    tm = x_ref.shape[0]

    @pl.when(pl.program_id(0) == 0)
    def _():
        run_ref[...] = jnp.zeros_like(run_ref)

    scores = jax.nn.sigmoid(_dot(x_ref[...], w_ref[...]))
    lane = lax.broadcasted_iota(jnp.int32, (tm, LANES), 1)
    work = jnp.where(lane < n_exp, scores + b_ref[...], -jnp.inf)
    chosen = jnp.zeros((tm, LANES), F32)
    sel = jnp.zeros((tm, LANES), F32)
    ids = jnp.zeros((tm, LANES), jnp.int32)
    wts = jnp.zeros((tm, LANES), F32)
    for it in range(TOP_K):
        best = jnp.max(work, axis=1, keepdims=True)
        first = jnp.min(jnp.where(work == best, lane, LANES), axis=1, keepdims=True)
        hit = lane == first
        chosen = jnp.where(hit, scores, chosen)
        sel = jnp.where(hit, 1.0, sel)
        ids = jnp.where(lane == it, first, ids)
        wts = jnp.where(lane == it, jnp.sum(jnp.where(hit, scores, 0.0), axis=1, keepdims=True), wts)
        work = jnp.where(hit, -jnp.inf, work)
    ids_ref[...] = ids
    wts_ref[...] = wts * (ROUTED_SCALE / jnp.sum(chosen, axis=1, keepdims=True))
    ri = lax.broadcasted_iota(jnp.int32, (tm, tm), 0)
    ci = lax.broadcasted_iota(jnp.int32, (tm, tm), 1)
    before = _dot((ci < ri).astype(BF16), sel.astype(BF16))
    run = run_ref[0:1, :]
    rank_ref[...] = (before + run).astype(jnp.int32)
    run = run + jnp.sum(sel, axis=0, keepdims=True)
    run_ref[...] = jnp.broadcast_to(run, run_ref.shape)
    cnt_ref[...] = jnp.broadcast_to(run, cnt_ref.shape).astype(jnp.int32)


def _router(xb, router_w, router_bias):
    t, d = xb.shape
    n_exp = router_w.shape[1]
    assert n_exp < LANES
    tm = _tile(t, 512, 8)
    wr = _pad_cols(router_w, LANES).astype(BF16)
    br = _pad_cols(router_bias.reshape(1, n_exp), LANES)
    tok = pl.BlockSpec((tm, LANES), lambda i: (i, 0))
    tok_shape = lambda dt: jax.ShapeDtypeStruct((t, LANES), dt)
    return pl.pallas_call(
        functools.partial(_router_kernel, n_exp=n_exp),
        grid=(t // tm,),
        in_specs=[pl.BlockSpec((tm, d), lambda i: (i, 0)), pl.BlockSpec((d, LANES), lambda i: (0, 0)),
                  pl.BlockSpec((1, LANES), lambda i: (0, 0))],
        out_specs=[tok, tok, tok, pl.BlockSpec((8, LANES), lambda i: (0, 0))],
        out_shape=[tok_shape(jnp.int32), tok_shape(F32), tok_shape(jnp.int32),
                   jax.ShapeDtypeStruct((8, LANES), jnp.int32)],
        scratch_shapes=[pltpu.VMEM((8, LANES), F32)],
        compiler_params=_cparams("arbitrary"),
        name="router",
    )(xb, wr, br)


def _shared_kernel(x_ref, wgu_ref, wd_ref, o_ref, *, hid):
    gu = _dot(x_ref[...], wgu_ref[...])
    gate = gu[:, :hid]
    h = (gate * jax.nn.sigmoid(gate) * gu[:, hid:]).astype(BF16)
    o_ref[...] = _dot(h, wd_ref[...])


def _shared_expert(xb, w_gate, w_up, w_down):
    t, d = xb.shape
    hid = w_gate.shape[1]
    wgu = jnp.concatenate([w_gate, w_up], axis=1).astype(BF16)
    wd = w_down.astype(BF16)
    tm = _tile(t, 512, 8)
    full = lambda a: pl.BlockSpec(a.shape, lambda i: (0, 0))
    return pl.pallas_call(
        functools.partial(_shared_kernel, hid=hid),
        grid=(t // tm,),
        in_specs=[pl.BlockSpec((tm, d), lambda i: (i, 0)), full(wgu), full(wd)],
        out_specs=pl.BlockSpec((tm, d), lambda i: (i, 0)),
        out_shape=jax.ShapeDtypeStruct((t, d), F32),
        compiler_params=_cparams("parallel"),
        name="shared_expert",
    )(xb, wgu, wd)


MOE_ROWS = 256
SC_WINDOW = 16


def _sc_mesh():
    return plsc.VectorSubcoreMesh(core_axis_name="core", subcore_axis_name="subcore")


def _sc_windows(n_items):
    mesh = _sc_mesh()
    workers = mesh.num_cores * mesh.num_subcores
    n_win = n_items // LANES
    assert n_items % LANES == 0
    return mesh, n_win, -(-n_win // workers)


def _sc_worker(mesh):
    return lax.axis_index("core") * mesh.num_subcores + lax.axis_index("subcore")


def _sc_scatter_rows(x, pos, n_rows):
    t, d = x.shape
    k = pos.shape[0]
    mesh, n_win, per = _sc_windows(t)

    n_sub = LANES // SC_WINDOW

    @pl.kernel(out_type=jax.ShapeDtypeStruct((n_rows, d), x.dtype), mesh=mesh,
               scratch_types=[pltpu.VMEM((k, LANES), jnp.int32), pltpu.VMEM((2, SC_WINDOW, d), x.dtype),
                              pltpu.SemaphoreType.DMA, pltpu.SemaphoreType.DMA((2,))])
    def scatter(x_hbm, pos_hbm, o_hbm, idx_vmem, buf, sem_in, sem_out):
        first = _sc_worker(mesh) * per

        @pl.loop(0, per)
        def _(step):
            win = first + step

            @pl.when(win < n_win)
            def _():
                base = win * LANES
                pltpu.sync_copy(pos_hbm.at[:, pl.ds(base, LANES)], idx_vmem)
                load = lambda j: pltpu.make_async_copy(
                    x_hbm.at[pl.ds(base + j * SC_WINDOW, SC_WINDOW)], buf.at[j % 2], sem_in)
                store = lambda j, kk: pltpu.make_async_copy(
                    buf.at[j % 2], o_hbm.at[idx_vmem[kk, pl.ds(j * SC_WINDOW, SC_WINDOW)]], sem_out.at[j % 2])
                load(0).start()
                for j in range(n_sub):
                    load(j).wait()
                    if j >= 1:
                        for kk in range(k):
                            store(j - 1, kk).wait()
                    if j + 1 < n_sub:
                        load(j + 1).start()
                    for kk in range(k):
                        store(j, kk).start()
                for kk in range(k):
                    store(n_sub - 1, kk).wait()

    return scatter(x, pos)


def _sc_gather_rows(y, idx):
    m = idx.shape[1]
    d = y.shape[1]
    mesh, n_win, per = _sc_windows(m)

    n_sub = LANES // SC_WINDOW

    @pl.kernel(out_type=jax.ShapeDtypeStruct((m, d), y.dtype), mesh=mesh,
               scratch_types=[pltpu.VMEM((1, LANES), jnp.int32), pltpu.VMEM((2, SC_WINDOW, d), y.dtype),
                              pltpu.SemaphoreType.DMA, pltpu.SemaphoreType.DMA((2,))])
    def gather(y_hbm, i_hbm, o_hbm, idx_vmem, buf, sem_in, sem_out):
        first = _sc_worker(mesh) * per

        @pl.loop(0, per)
        def _(step):
            win = first + step

            @pl.when(win < n_win)
            def _():
                base = win * LANES
                pltpu.sync_copy(i_hbm.at[:, pl.ds(base, LANES)], idx_vmem)
                load = lambda j: pltpu.make_async_copy(
                    y_hbm.at[idx_vmem[0, pl.ds(j * SC_WINDOW, SC_WINDOW)]], buf.at[j % 2], sem_in)
                store = lambda j: pltpu.make_async_copy(
                    buf.at[j % 2], o_hbm.at[pl.ds(base + j * SC_WINDOW, SC_WINDOW)], sem_out.at[j % 2])
                load(0).start()
                for j in range(n_sub):
                    load(j).wait()
                    if j >= 1:
                        store(j - 1).wait()
                    if j + 1 < n_sub:
                        load(j + 1).start()
                    store(j).start()
                store(n_sub - 1).wait()

    return gather(y, idx)


def _grouped_kernel(te_ref, nu_ref, x_ref, wg_ref, wu_ref, wd_ref, o_ref, wgu_bf, wd_bf, *, hid):
    i = pl.program_id(0)

    @pl.when(jnp.logical_or(i == 0, te_ref[i] != te_ref[jnp.maximum(i - 1, 0)]))
    def _():
        wgu_bf[:, :hid] = wg_ref[...].astype(BF16)
        wgu_bf[:, hid:] = wu_ref[...].astype(BF16)
        wd_bf[...] = wd_ref[...].astype(BF16)

    @pl.when(i < nu_ref[0])
    def _():
        lo, hi = _unpack_pairs(x_ref[...])
        x = jnp.concatenate([lo.astype(BF16), hi.astype(BF16)], axis=1)
        gu = _dot(x, wgu_bf[...])
        gate = gu[:, :hid]
        h = (gate * jax.nn.sigmoid(gate) * gu[:, hid:]).astype(BF16)
        o_ref[...] = _pack_pairs(_dot(h, wd_bf[...]))


def _grouped_swiglu(xs, tile_expert, n_used, w_gate, w_up, w_down, layer):
    n, half = xs.shape
    d = 2 * half
    hid = w_down.shape[2]
    grid_spec = pltpu.PrefetchScalarGridSpec(
        num_scalar_prefetch=2,
        grid=(n // MOE_ROWS,),
        in_specs=[pl.BlockSpec((MOE_ROWS, half), lambda i, te, nu: (i, 0)),
                  pl.BlockSpec((None, None, d, hid), lambda i, te, nu: (layer, te[i], 0, 0)),
                  pl.BlockSpec((None, None, d, hid), lambda i, te, nu: (layer, te[i], 0, 0)),
                  pl.BlockSpec((None, None, hid, d), lambda i, te, nu: (layer, te[i], 0, 0))],
        out_specs=pl.BlockSpec((MOE_ROWS, half), lambda i, te, nu: (i, 0)),
        scratch_shapes=[pltpu.VMEM((d, 2 * hid), BF16), pltpu.VMEM((hid, d), BF16)],
    )
    return pl.pallas_call(
        functools.partial(_grouped_kernel, hid=hid),
        grid_spec=grid_spec,
        out_shape=jax.ShapeDtypeStruct((n, half), jnp.int32),
        compiler_params=_cparams("arbitrary"),
        name="moe_grouped",
    )(tile_expert, n_used, xs, w_gate, w_up, w_down)


def _combine_ln_kernel(x_ref, yg_ref, w_ref, fs_ref, g_ref, b_ref, o_ref, ob_ref, *, alpha):
    w = w_ref[...]
    acc_lo, acc_hi = None, None
    for k in range(TOP_K):
        lo, hi = _unpack_pairs(yg_ref[k])
        wk = w[:, k:k + 1]
        acc_lo = wk * lo if k == 0 else acc_lo + wk * lo
        acc_hi = wk * hi if k == 0 else acc_hi + wk * hi
    acc = alpha * x_ref[...] + fs_ref[...] + jnp.concatenate([acc_lo, acc_hi], axis=1)
    mu = jnp.mean(acc, axis=-1, keepdims=True)
    xc = acc - mu
    var = jnp.mean(xc * xc, axis=-1, keepdims=True)
    y = xc * lax.rsqrt(var + LN_EPS) * g_ref[...] + b_ref[...]
    o_ref[...] = y
    ob_ref[...] = y.astype(BF16)


def _combine_ln(x, yg, wts, f_shared, g, b, alpha):
    t, d = x.shape
    tm = _tile(t, 64, 8)
    row = lambda width: pl.BlockSpec((tm, width), lambda i: (i, 0))
    vec = pl.BlockSpec((1, d), lambda i: (0, 0))
    return pl.pallas_call(
        functools.partial(_combine_ln_kernel, alpha=alpha),
        grid=(t // tm,),
        in_specs=[row(d), pl.BlockSpec((TOP_K, tm, d // 2), lambda i: (0, i, 0)), row(LANES), row(d), vec, vec],
        out_specs=[row(d), row(d)],
        out_shape=[jax.ShapeDtypeStruct((t, d), F32), jax.ShapeDtypeStruct((t, d), BF16)],
        compiler_params=_cparams("parallel"),
        name="moe_combine_ln",
    )(x, yg, wts, f_shared, g.reshape(1, d), b.reshape(1, d))


def _routed_ln(xf, xpk, ids, wts, rank, counts, w_gate, w_up, w_down, layer, f_shared, g, b, alpha):
    t, d = xf.shape
    n_exp = w_gate.shape[1]
    n_rows = t * TOP_K + n_exp * MOE_ROWS
    n_tiles = n_rows // MOE_ROWS
    cnt = counts[0, :n_exp]
    padded = (cnt + MOE_ROWS - 1) // MOE_ROWS * MOE_ROWS
    ends = jnp.cumsum(padded)
    starts = ends - padded
    tile_start = jnp.arange(n_tiles, dtype=jnp.int32) * MOE_ROWS
    tile_expert = jnp.minimum(jnp.sum((ends[None, :] <= tile_start[:, None]).astype(jnp.int32), axis=1), n_exp - 1)
    n_used = (ends[-1:] // MOE_ROWS).astype(jnp.int32)
    top = ids[:, :TOP_K]
    pos_t = (starts[top] + jnp.take_along_axis(rank, top, axis=1)).astype(jnp.int32).T
    xs = _sc_scatter_rows(xpk, pos_t, n_rows)
    ys = _grouped_swiglu(xs, tile_expert, n_used, w_gate, w_up, w_down, layer)
    yg = _sc_gather_rows(ys, pos_t.reshape(1, TOP_K * t)).reshape(TOP_K, t, d // 2)
    return _combine_ln(xf, yg, wts, f_shared, g, b, alpha)


PROJ_ALIGN = 512


def _w_in_plan(d, n_cols):
    wdt = d // N_BRANCH
    gla_heads = wdt // GLA_DV
    names = (('gate', N_BRANCH * d), ('mla_cq', MLA_Q_RANK), ('mla_ckv', MLA_KV_RANK), ('mla_kr', MLA_ROPE),
             ('dsa_q', wdt), ('dsa_k', DSA_DH), ('dsa_v', DSA_DH),
             ('idx_q', IDX_HEADS * IDX_DIM), ('idx_k', IDX_DIM), ('idx_w', IDX_HEADS),
             ('gla_q', gla_heads * GLA_DK), ('gla_k', gla_heads * GLA_DK), ('gla_v', wdt),
             ('gla_a', GLA_GATE_RANK), ('gla_r', wdt),
             ('rwkv_r', wdt), ('rwkv_k', wdt), ('rwkv_v', wdt),
             ('rwkv_w', RWKV_LORA), ('rwkv_a', RWKV_LORA), ('rwkv_g', RWKV_LORA))
    src, off = {}, 0
    for name, width in names:
        src[name] = (off, width)
        off += width
    assert off == n_cols, (off, n_cols)

    def layout(items):
        copies, dst = [], 0
        for name, slot, scale in items:
            if name is not None:
                copies.append((dst, src[name][0], src[name][1], scale))
            dst += slot
        return copies, -(-dst // PROJ_ALIGN) * PROJ_ALIGN

    half = MLA_ROPE // 2
    kr0 = src['mla_kr'][0]
    lat, lat_w = layout([('mla_cq', MLA_Q_RANK, None), ('mla_ckv', MLA_KV_RANK, None), ('mla_kr', LANES, None),
                         (None, LANES, None)])
    swap_dst = MLA_Q_RANK + MLA_KV_RANK + LANES
    lat += [(swap_dst, kr0 + half, half, None), (swap_dst + half, kr0, half, None)]
    iq0 = src['idx_q'][0]
    dsa, dsa_w = layout([(None, IDX_HEADS * LANES, None), ('dsa_q', wdt, DSA_DH ** -0.5 * LOG2E),
                         ('dsa_k', DSA_DH, None), ('dsa_v', DSA_DH, None), ('idx_k', LANES, None),
                         ('idx_w', LANES, None)])
    dsa += [(h * LANES, iq0 + h * IDX_DIM, IDX_DIM, None) for h in range(IDX_HEADS)]
    gla, gla_w = layout([('gla_v', wdt, None), ('gla_r', wdt, None), ('gla_q', gla_heads * GLA_DK, None),
                         ('gla_k', gla_heads * GLA_DK, None), ('gla_a', LANES, None)])
    rwkv, rwkv_w = layout([('rwkv_r', wdt, None), ('rwkv_k', wdt, None), ('rwkv_v', wdt, None),
                           ('rwkv_w', RWKV_LORA, None), ('rwkv_a', RWKV_LORA, None), ('rwkv_g', LANES, None)])
    return (lat, dsa, gla, rwkv), (lat_w, dsa_w, gla_w, rwkv_w)


def _regroup_kernel(w_ref, gate_ref, *out_refs, plans, gate_w):
    gate_ref[...] = w_ref[:, :gate_w].astype(BF16)
    for o_ref, plan in zip(out_refs, plans):
        o_ref[...] = jnp.zeros_like(o_ref)
        for dst, src, width, scale in plan:
            piece = w_ref[:, src:src + width]
            if scale is not None:
                piece = piece * scale
            o_ref[:, dst:dst + width] = piece.astype(BF16)


def _regroup_w_in(w_in, layer, d):
    n_cols = w_in.shape[2]
    plans, widths = _w_in_plan(d, n_cols)
    gate_w = N_BRANCH * d
    tr = _tile(d, 128, 8)
    out = lambda width: pl.BlockSpec((tr, width), lambda i: (i, 0))
    return pl.pallas_call(
        functools.partial(_regroup_kernel, plans=plans, gate_w=gate_w),
        grid=(d // tr,),
        in_specs=[pl.BlockSpec((None, tr, n_cols), lambda i: (layer, i, 0))],
        out_specs=[out(gate_w)] + [out(wd) for wd in widths],
        out_shape=[jax.ShapeDtypeStruct((d, gate_w), BF16)] + [jax.ShapeDtypeStruct((d, wd), BF16) for wd in widths],
        compiler_params=_cparams("parallel"),
        name="regroup_w_in",
    )(w_in)


def kernel(x, positions, ln_in_g, ln_in_b, w_in, w_branch, w_out, mla_q_norm, mla_w_uq, mla_kv_norm, mla_w_ukv,
           gla_w_gate2, gla_b_gate, gla_norm_g, rwkv_mu_rkv, rwkv_mu_lora, rwkv_w0, rwkv_w2, rwkv_a0, rwkv_a2,
           rwkv_g2, rwkv_k_k, rwkv_k_a, rwkv_r_k, rwkv_lnx_g, rwkv_lnx_b, rwkv_v0, rwkv_v1, rwkv_v2,
           ln_mix_g, ln_mix_b, router_w, router_bias, exp_w_gate, exp_w_up, exp_w_down,
           sh_w_gate, sh_w_up, sh_w_down, ln_ffn_g, ln_ffn_b):
    batch, seq, d = x.shape
    depth = w_in.shape[0]
    t, wdt = batch * seq, d // N_BRANCH
    alpha = (2 * depth) ** 0.25
    ct, st = _rope_tables(positions)
    xf, xb = _layer_norm(x.reshape(t, d), (), ln_in_g, ln_in_b)
    v_first = None
    for l in range(depth):
        w_gate, w_lat, w_dsa, w_gla, w_rwkv = _regroup_w_in(w_in, l, d)
        lat = _matmul(xb, w_lat, F32)
        pb = _matmul(xb, w_dsa, BF16)
        pc = _matmul(xb, w_gla, F32)
        pd = _matmul(xb, w_rwkv, F32)

        y_mla = _mla(lat, ct, st, mla_q_norm[l], mla_w_uq[l], mla_kv_norm[l], mla_w_ukv[l], batch, seq)
        y_dsa = _dsa(pb, batch, seq, wdt)
        y_gla = _gla(pc, gla_w_gate2[l], gla_b_gate[l], gla_norm_g[l], batch, seq)
        vres = None if l == 0 else (rwkv_v0[l - 1], rwkv_v1[l - 1], rwkv_v2[l - 1])
        y_rwkv, v_first = _rwkv(pd, rwkv_mu_rkv[l], rwkv_mu_lora[l], rwkv_w0[l], rwkv_w2[l], rwkv_a0[l], rwkv_a2[l],
                                rwkv_g2[l], rwkv_k_k[l], rwkv_k_a[l], rwkv_r_k[l], rwkv_lnx_g[l], rwkv_lnx_b[l],
                                v_first, vres, batch, seq)

        merged = _merge(xb, (y_mla, y_dsa, y_gla, y_rwkv), w_gate, w_branch[l].astype(BF16))
        xf, xb, xpk = _layer_norm(xf, (_matmul(merged, w_out[l].astype(BF16), F32),), ln_mix_g[l], ln_mix_b[l], alpha,
                                  packed=True)

        ids, wts, rank, counts = _router(xb, router_w[l], router_bias[l])
        f_shared = _shared_expert(xb, sh_w_gate[l], sh_w_up[l], sh_w_down[l])
        xf, xb = _routed_ln(xf, xpk, ids, wts, rank, counts, exp_w_gate, exp_w_up, exp_w_down, l,
                            f_shared, ln_ffn_g[l], ln_ffn_b[l], alpha)
    return xf.reshape(batch, seq, d)
```

```python
import functools
import math

import jax
import jax.numpy as jnp
from jax import lax
from jax.experimental import pallas as pl
from jax.experimental.pallas import tpu as pltpu
from jax.experimental.pallas import tpu_sc as plsc

F32 = jnp.float32
BF16 = jnp.bfloat16

LANES = 128

MLA_NOPE, MLA_ROPE, MLA_V = 128, 64, 128
MLA_Q_RANK, MLA_KV_RANK = 768, 256
ROPE_THETA = 10000.0
DSA_DH = 128
IDX_HEADS, IDX_DIM, IDX_TOPK_MAX = 16, 64, 256
GLA_DV, GLA_DK, GLA_GATE_RANK, GLA_TAU, GLA_CHUNK = 256, 128, 16, 16.0, 64
GLA_SUB = 16
RWKV_HS, RWKV_LORA, RWKV_GN_EPS = 64, 64, 64e-5
RWKV_CHUNK = 64
TOP_K, ROUTED_SCALE = 8, 2.5
LN_EPS, RMS_EPS = 1e-5, 1e-6
NEG_BIG = -1e30
LOG2E = math.log2(math.e)

VMEM_LIMIT = 56 * 1024 * 1024


def _cparams(*sem):
    return pltpu.CompilerParams(dimension_semantics=sem, vmem_limit_bytes=VMEM_LIMIT)


def _tile(n, pref, unit=LANES):
    if n <= pref:
        return n
    t = (pref // unit) * unit
    while t > unit and n % t:
        t -= unit
    assert n % t == 0, (n, pref, unit)
    return t


def _dot(a, b):
    return jnp.dot(a, b, preferred_element_type=F32)


def _dot_t(a, b):
    return lax.dot_general(a, b, (((1,), (1,)), ((), ())), preferred_element_type=F32)


def _dot_f32(a, b):
    return jnp.dot(a, b, preferred_element_type=F32, precision=lax.Precision.HIGHEST)


def _masked_sums(mask, x):
    w = x.shape[1]
    hi = x.astype(BF16)
    rest = x - hi.astype(F32)
    mid = rest.astype(BF16)
    lo = (rest - mid.astype(F32)).astype(BF16)
    s = _dot(mask, jnp.concatenate([hi, mid, lo], axis=1))
    return s[:, :w] + s[:, w:2 * w] + s[:, 2 * w:]


def _mm_kernel(a_ref, b_ref, o_ref):
    o_ref[...] = _dot(a_ref[...], b_ref[...]).astype(o_ref.dtype)


def _matmul(a, b, out_dtype, tm=1024, tn=512):
    m, k = a.shape
    n = b.shape[1]
    tm, tn = _tile(m, tm, 8), _tile(n, tn)
    return pl.pallas_call(
        _mm_kernel,
        grid=(m // tm, n // tn),
        in_specs=[pl.BlockSpec((tm, k), lambda i, j: (i, 0)), pl.BlockSpec((k, tn), lambda i, j: (0, j))],
        out_specs=pl.BlockSpec((tm, tn), lambda i, j: (i, j)),
        out_shape=jax.ShapeDtypeStruct((m, n), out_dtype),
        compiler_params=_cparams("parallel", "parallel"),
        name="matmul",
    )(a, b)


def _pack_pairs(v):
    half = v.shape[1] // 2
    bits = lax.bitcast_convert_type(v.astype(BF16).astype(F32), jnp.int32)
    return (bits[:, half:] & jnp.int32(-65536)) | lax.shift_right_logical(bits[:, :half], 16)


def _unpack_pairs(w):
    return (lax.bitcast_convert_type(lax.shift_left(w, 16), F32),
            lax.bitcast_convert_type(w & jnp.int32(-65536), F32))


def _ln_kernel(*refs, alpha, n_res, packed):
    x_ref, res_refs = refs[0], refs[1:1 + n_res]
    g_ref, b_ref, o_ref, ob_ref = refs[1 + n_res:5 + n_res]
    x = x_ref[...]
    if n_res:
        x = alpha * x
        for f_ref in res_refs:
            x = x + f_ref[...]
    mu = jnp.mean(x, axis=-1, keepdims=True)
    xc = x - mu
    var = jnp.mean(xc * xc, axis=-1, keepdims=True)
    y = xc * lax.rsqrt(var + LN_EPS) * g_ref[...] + b_ref[...]
    o_ref[...] = y
    ob_ref[...] = y.astype(BF16)
    if packed:
        refs[-1][...] = _pack_pairs(y)


def _layer_norm(x, res, g, b, alpha=1.0, packed=False):
    t, d = x.shape
    tm = _tile(t, 128, 8)
    row = pl.BlockSpec((tm, d), lambda i: (i, 0))
    vec = pl.BlockSpec((1, d), lambda i: (0, 0))
    args = (x,) + tuple(res)
    extra_spec = [pl.BlockSpec((tm, d // 2), lambda i: (i, 0))] if packed else []
    extra_shape = [jax.ShapeDtypeStruct((t, d // 2), jnp.int32)] if packed else []
    return pl.pallas_call(
        functools.partial(_ln_kernel, alpha=alpha, n_res=len(res), packed=packed),
        grid=(t // tm,),
        in_specs=[row] * len(args) + [vec, vec],
        out_specs=[row, row] + extra_spec,
        out_shape=[jax.ShapeDtypeStruct((t, d), F32), jax.ShapeDtypeStruct((t, d), BF16)] + extra_shape,
        compiler_params=_cparams("parallel"),
        name="layer_norm",
    )(*args, g.reshape(1, d), b.reshape(1, d))


def _pad_cols(w, width):
    return jnp.pad(w, ((0, 0), (0, width - w.shape[1])))


def _rms(x, g):
    return x * lax.rsqrt(jnp.mean(x * x, axis=-1, keepdims=True) + RMS_EPS) * g


def _mla_prep_kernel(lat_ref, ct_ref, st_ref, qg_ref, kg_ref, wqn_ref, wqr_ref, wqs_ref, wkn_ref, wv_ref,
                     qn_ref, qr_ref, kn_ref, kr_ref, v_ref, *, heads, scale):
    lat = lat_ref[...]
    cq = lat[:, :MLA_Q_RANK]
    ckv = lat[:, MLA_Q_RANK:MLA_Q_RANK + MLA_KV_RANK]
    kr = lat[:, MLA_Q_RANK + MLA_KV_RANK:MLA_Q_RANK + MLA_KV_RANK + LANES]
    krs = lat[:, MLA_Q_RANK + MLA_KV_RANK + LANES:MLA_Q_RANK + MLA_KV_RANK + 2 * LANES]
    ct, st = ct_ref[...], st_ref[...]
    nq = _rms(cq, qg_ref[...]).astype(BF16)
    nkv = _rms(ckv, kg_ref[...]).astype(BF16)
    qn_ref[...] = (_dot(nq, wqn_ref[...]) * scale).astype(BF16)
    cth = jnp.concatenate([ct] * heads, axis=1)
    sth = jnp.concatenate([st] * heads, axis=1)
    qr_ref[...] = ((_dot(nq, wqr_ref[...]) * cth + _dot(nq, wqs_ref[...]) * sth) * scale).astype(BF16)
    kn_ref[...] = _dot(nkv, wkn_ref[...]).astype(BF16)
    v_ref[...] = _dot(nkv, wv_ref[...]).astype(BF16)
    kr_ref[...] = (kr * ct + krs * st).astype(BF16)


def _flash_kernel(qn_ref, qr_ref, kn_ref, kr_ref, v_ref, o_ref, m_ref, l_ref, acc_ref, *, tq, tk):
    i, j = pl.program_id(2), pl.program_id(3)

    @pl.when(j == 0)
    def _():
        m_ref[...] = jnp.full_like(m_ref, NEG_BIG)
        l_ref[...] = jnp.zeros_like(l_ref)
        acc_ref[...] = jnp.zeros_like(acc_ref)

    def step(masked):
        q = jnp.concatenate([qn_ref[...], qr_ref[...]], axis=1)
        k = jnp.concatenate([kn_ref[...], kr_ref[...]], axis=1)
        s = _dot_t(q, k)
        if masked:
            qpos = i * tq + lax.broadcasted_iota(jnp.int32, (tq, tk), 0)
            kpos = j * tk + lax.broadcasted_iota(jnp.int32, (tq, tk), 1)
            s = jnp.where(kpos <= qpos, s, -jnp.inf)
        m_prev = m_ref[...]
        m_new = jnp.maximum(m_prev, jnp.max(s, axis=1, keepdims=True))
        alpha = jnp.exp2(m_prev - m_new)
        p = jnp.exp2(s - jnp.concatenate([m_new] * (tk // LANES), axis=1))
        l_ref[...] = alpha * l_ref[...] + jnp.sum(p, axis=1, keepdims=True)
        acc_ref[...] = alpha * acc_ref[...] + _dot(p.astype(BF16), v_ref[...])
        m_ref[...] = m_new

    below = j * tk + tk - 1 <= i * tq
    touches = j * tk <= i * tq + tq - 1

    @pl.when(below)
    def _():
        step(False)

    @pl.when(jnp.logical_and(touches, jnp.logical_not(below)))
    def _():
        step(True)

    @pl.when(j == pl.num_programs(3) - 1)
    def _():
        o_ref[...] = (acc_ref[...] / l_ref[...]).astype(o_ref.dtype)


def _rope_tables(positions):
    half = MLA_ROPE // 2
    inv = ROPE_THETA ** (-jnp.arange(0, MLA_ROPE, 2, dtype=F32) / MLA_ROPE)
    ang = positions.reshape(-1).astype(F32)[:, None] * inv
    cos, sin = jnp.cos(ang), jnp.sin(ang)
    zero = jnp.zeros((ang.shape[0], LANES - 2 * half), F32)
    return jnp.concatenate([cos, cos, zero], 1), jnp.concatenate([-sin, sin, zero], 1)


def _swap_halves(w):
    half = w.shape[-1] // 2
    return jnp.concatenate([w[..., half:], w[..., :half]], -1)


def _mla(lat, ct, st, q_norm, w_uq, kv_norm, w_ukv, batch, seq):
    t = lat.shape[0]
    heads = w_uq.shape[1] // (MLA_NOPE + MLA_ROPE)
    hw = heads * LANES
    wq = w_uq.reshape(MLA_Q_RANK, heads, MLA_NOPE + MLA_ROPE)
    wqn = wq[:, :, :MLA_NOPE].reshape(MLA_Q_RANK, hw).astype(BF16)
    rope_pad = ((0, 0), (0, 0), (0, LANES - MLA_ROPE))
    wqr = jnp.pad(wq[:, :, MLA_NOPE:], rope_pad).reshape(MLA_Q_RANK, hw).astype(BF16)
    wqs = jnp.pad(_swap_halves(wq[:, :, MLA_NOPE:]), rope_pad).reshape(MLA_Q_RANK, hw).astype(BF16)
    wkv = w_ukv.reshape(MLA_KV_RANK, heads, MLA_NOPE + MLA_V)
    wkn = wkv[:, :, :MLA_NOPE].reshape(MLA_KV_RANK, hw).astype(BF16)
    wv = wkv[:, :, MLA_NOPE:].reshape(MLA_KV_RANK, hw).astype(BF16)

    tm = _tile(t, 512, 8)
    row = lambda w: pl.BlockSpec((tm, w), lambda i: (i, 0))
    full = lambda a: pl.BlockSpec(a.shape, lambda i: (0, 0))
    qg, kg = q_norm.reshape(1, -1), kv_norm.reshape(1, -1)
    qn, qr, kn, kr, v = pl.pallas_call(
        functools.partial(_mla_prep_kernel, heads=heads, scale=(MLA_NOPE + MLA_ROPE) ** -0.5 * LOG2E),
        grid=(t // tm,),
        in_specs=[row(lat.shape[1]), row(LANES), row(LANES), full(qg), full(kg),
                  full(wqn), full(wqr), full(wqs), full(wkn), full(wv)],
        out_specs=[row(hw), row(hw), row(hw), row(LANES), row(hw)],
        out_shape=[jax.ShapeDtypeStruct((t, hw), BF16)] * 3 + [jax.ShapeDtypeStruct((t, LANES), BF16),
                                                               jax.ShapeDtypeStruct((t, hw), BF16)],
        compiler_params=_cparams("parallel"),
        name="mla_prep",
    )(lat, ct, st, qg, kg, wqn, wqr, wqs, wkn, wv)

    tq = tk = _tile(seq, 1024, 8)
    nq, nk = seq // tq, seq // tk
    r3 = lambda a: a.reshape(batch, seq, a.shape[1])
    last = lambda i, j: jnp.minimum(j, ((i + 1) * tq - 1) // tk)
    q_spec = pl.BlockSpec((None, tq, LANES), lambda b, h, i, j: (b, i, h))
    k_spec = pl.BlockSpec((None, tk, LANES), lambda b, h, i, j: (b, last(i, j), h))
    kr_spec = pl.BlockSpec((None, tk, LANES), lambda b, h, i, j: (b, last(i, j), 0))
    out = pl.pallas_call(
        functools.partial(_flash_kernel, tq=tq, tk=tk),
        grid=(batch, heads, nq, nk),
        in_specs=[q_spec, q_spec, k_spec, kr_spec, k_spec],
        out_specs=q_spec,
        out_shape=jax.ShapeDtypeStruct((batch, seq, hw), BF16),
        scratch_shapes=[pltpu.VMEM((tq, LANES), F32), pltpu.VMEM((tq, LANES), F32), pltpu.VMEM((tq, LANES), F32)],
        compiler_params=_cparams("parallel", "parallel", "parallel", "arbitrary"),
        name="mla_flash",
    )(r3(qn), r3(qr), r3(kn), r3(kr), r3(v))
    return out.reshape(t, hw)


DSA_TQ = 256
DSA_TK = 1024
INT_MIN = -2 ** 31


def _dsa_kernel(iq_ref, q_ref, k_ref, v_ref, ik_ref, iw_ref, o_ref, key_ref, m_ref, l_ref, acc_ref,
                *, tq, tk, n_sel, heads, pos_bits):
    i = pl.program_id(1)
    n_kt = (i * tq + tq - 1) // tk + 1
    reps = tk // LANES
    wide = lambda a: jnp.concatenate([a] * reps, axis=1)
    qpos = i * tq + lax.broadcasted_iota(jnp.int32, (tq, tk), 0)
    lane_pos = lax.broadcasted_iota(jnp.int32, (tq, tk), 1)

    iq = iq_ref[...]
    iw = iw_ref[...].astype(F32) * (IDX_HEADS ** -0.5 * IDX_DIM ** -0.5)
    iw_b = [jnp.broadcast_to(iw[:, h:h + 1], (tq, LANES)) for h in range(IDX_HEADS)]

    def score_tile(j, carry):
        ikj = ik_ref[pl.ds(pl.multiple_of(j * tk, tk), tk), :]
        sc = jnp.zeros((tq, tk), F32)
        for h in range(IDX_HEADS):
            logit = _dot_t(iq[:, h * LANES:(h + 1) * LANES], ikj)
            sc = sc + wide(iw_b[h]) * jnp.maximum(logit, 0.0)
        sc = jnp.where(j * tk + lane_pos <= qpos, sc + 0.0, -jnp.inf)
        bits = lax.bitcast_convert_type(sc, jnp.int32)
        key_ref[j] = jnp.where(bits >= 0, bits, bits ^ 0x7FFFFFFF)
        return carry

    lax.fori_loop(0, n_kt, score_tile, 0)

    def count(pred):
        def body(j, acc):
            c = pred(key_ref[j], j * tk + lane_pos).astype(jnp.int32)
            for rep in range(reps):
                acc = acc + c[:, rep * LANES:(rep + 1) * LANES]
            return acc
        acc = lax.fori_loop(0, n_kt, body, jnp.zeros((tq, LANES), jnp.int32))
        return jnp.broadcast_to(jnp.sum(acc, axis=1, keepdims=True), (tq, LANES))

    def thr_bit(bit_i, thr):
        cand = thr ^ jnp.left_shift(jnp.int32(1), 31 - bit_i)
        cnt = count(lambda key, pos: key >= wide(cand))
        return jnp.where(cnt >= n_sel, cand, thr)

    thr = lax.fori_loop(0, 32, thr_bit, jnp.full((tq, LANES), INT_MIN, jnp.int32))
    cnt_gt = count(lambda key, pos: key > wide(thr))
    cnt_ge = count(lambda key, pos: key >= wide(thr))
    need = n_sel - cnt_gt

    def tie_bit(bit_i, cut):
        cand = cut | jnp.left_shift(jnp.int32(1), pos_bits - 1 - bit_i)
        cnt = count(lambda key, pos: jnp.logical_and(key == wide(thr), pos < wide(cand)))
        return jnp.where(cnt < need, cand, cut)

    surplus = jnp.max(cnt_ge - cnt_gt - need) > 0
    cut = lax.cond(surplus,
                   lambda: lax.fori_loop(0, pos_bits, tie_bit, jnp.zeros((tq, LANES), jnp.int32)),
                   lambda: jnp.full((tq, LANES), 2 ** 31 - 1, jnp.int32))

    q = q_ref[...]
    q_all = jnp.concatenate([q[:, h * LANES:(h + 1) * LANES] for h in range(heads)], axis=0)
    m_ref[...] = jnp.full_like(m_ref, NEG_BIG)
    l_ref[...] = jnp.zeros_like(l_ref)
    acc_ref[...] = jnp.zeros_like(acc_ref)

    def attend(j, carry):
        rows = pl.ds(pl.multiple_of(j * tk, tk), tk)
        s = _dot_t(q_all, k_ref[rows, :])
        key, pos = key_ref[j], j * tk + lane_pos
        keep = jnp.where(key > wide(thr), 0.0,
                         jnp.where(jnp.logical_and(key == wide(thr), pos <= wide(cut)), 0.0, -jnp.inf))
        keep = jnp.where(pos <= qpos, keep, -jnp.inf)
        s = s + jnp.concatenate([keep] * heads, axis=0)
        m_prev = m_ref[...]
        m_new = jnp.maximum(m_prev, jnp.max(s, axis=1, keepdims=True))
        alpha = jnp.exp2(m_prev - m_new)
        p = jnp.exp2(s - jnp.concatenate([m_new] * reps, axis=1))
        l_ref[...] = alpha * l_ref[...] + jnp.sum(p, axis=1, keepdims=True)
        acc_ref[...] = alpha * acc_ref[...] + _dot(p.astype(BF16), v_ref[rows, :])
        m_ref[...] = m_new
        return carry

    lax.fori_loop(0, n_kt, attend, 0)
    out = acc_ref[...] / l_ref[...]
    o_ref[...] = jnp.concatenate([out[h * tq:(h + 1) * tq] for h in range(heads)], axis=1).astype(o_ref.dtype)


def _dsa(pb, batch, seq, width):
    t, wb = pb.shape
    heads = width // DSA_DH
    iqw = IDX_HEADS * LANES
    tq, tk = _tile(seq, DSA_TQ, 8), _tile(seq, DSA_TK)
    n_sel = min(IDX_TOPK_MAX, seq // 4)
    c0 = (iqw + width) // LANES
    pb3 = pb.reshape(batch, seq, wb)
    qblk = lambda w, idx: pl.BlockSpec((None, tq, w), lambda b, i: (b, i, idx))
    seqblk = lambda idx: pl.BlockSpec((None, seq, LANES), lambda b, i: (b, 0, idx))
    out = pl.pallas_call(
        functools.partial(_dsa_kernel, tq=tq, tk=tk, n_sel=n_sel, heads=heads,
                          pos_bits=max(1, (seq - 1).bit_length())),
        grid=(batch, seq // tq),
        in_specs=[qblk(iqw, 0), qblk(width, iqw // width), seqblk(c0), seqblk(c0 + 1), seqblk(c0 + 2),
                  qblk(LANES, c0 + 3)],
        out_specs=qblk(width, 0),
        out_shape=jax.ShapeDtypeStruct((batch, seq, width), BF16),
        scratch_shapes=[pltpu.VMEM((seq // tk, tq, tk), jnp.int32), pltpu.VMEM((heads * tq, LANES), F32),
                        pltpu.VMEM((heads * tq, LANES), F32), pltpu.VMEM((heads * tq, LANES), F32)],
        compiler_params=_cparams("parallel", "arbitrary"),
        name="dsa",
    )(pb3, pb3, pb3, pb3, pb3, pb3)
    return out.reshape(t, width)


def _gla_kernel(q_ref, k_ref, v_ref, a_ref, r_ref, wg_ref, bg_ref, ng_ref, o_ref, state_ref):
    c_len, sub = GLA_CHUNK, GLA_SUB

    @pl.when(pl.program_id(2) == 0)
    def _():
        state_ref[...] = jnp.zeros_like(state_ref)

    z_all = _dot(a_ref[...].astype(BF16), wg_ref[...]) + bg_ref[...]
    g_all = jax.nn.log_sigmoid(z_all) / GLA_TAU
    ri = lax.broadcasted_iota(jnp.int32, (c_len, c_len), 0)
    ci = lax.broadcasted_iota(jnp.int32, (c_len, c_len), 1)
    b_all = _masked_sums((ri >= ci).astype(BF16), g_all)
    ones = jnp.ones((GLA_DK, LANES), BF16)
    trow = lax.broadcasted_iota(jnp.int32, (sub, GLA_DK), 0)
    lane = lax.broadcasted_iota(jnp.int32, (sub, LANES), 1)
    for hh in range(state_ref.shape[0]):
        ksl = slice(hh * GLA_DK, (hh + 1) * GLA_DK)
        vsl = slice(hh * GLA_DV, (hh + 1) * GLA_DV)
        o = _gla_head(q_ref[:, ksl] * GLA_DK ** -0.5, k_ref[:, ksl], v_ref[:, vsl].astype(BF16), b_all[:, ksl],
                      state_ref.at[hh], ones, trow, lane)
        r = r_ref[:, vsl]
        o_ref[:, vsl] = (_rms(o, ng_ref[:, vsl]) * (r * jax.nn.sigmoid(r))).astype(o_ref.dtype)


def _gla_head(q, k, vb, b, state_ref, ones, trow, lane):
    c_len, sub = GLA_CHUNK, GLA_SUB
    state_t = state_ref[...]
    o_inter = _dot_t((q * jnp.exp(b)).astype(BF16), state_t.astype(BF16))
    outs = []
    for blk in range(c_len // sub):
        lo = blk * sub
        qi, bi, ki = q[lo:lo + sub], b[lo:lo + sub], k[lo:lo + sub]
        rows = [qi * ki[s:s + 1] * jnp.exp(jnp.where(trow >= s, bi - bi[s:s + 1], -jnp.inf)) for s in range(sub)]
        prod = jnp.concatenate(rows, axis=0)
        p_hi = prod.astype(BF16)
        p_lo = (prod - p_hi.astype(F32)).astype(BF16)
        sums = _dot(jnp.concatenate([p_hi, p_lo], axis=0), ones)
        sums = sums[:sub * sub] + sums[sub * sub:]
        attn = jnp.zeros((sub, LANES), F32)
        for s in range(sub):
            attn = attn + jnp.where(lane == s, sums[s * sub:(s + 1) * sub], 0.0)
        o_blk = _dot(attn[:, :sub].astype(BF16), vb[lo:lo + sub])
        if blk:
            qa = qi * jnp.exp(bi - bi[0:1])
            ka = k[:lo] * jnp.exp(bi[0:1] - b[:lo])
            o_blk = o_blk + _dot(_dot_t(qa.astype(BF16), ka.astype(BF16)).astype(BF16), vb[:lo])
        outs.append(o_blk)
    o = o_inter + jnp.concatenate(outs, axis=0)

    b_last = b[c_len - 1:c_len]
    k_dec = (k * jnp.exp(b_last - b)).astype(BF16)
    state_ref[...] = state_t * jnp.exp(b_last) + lax.dot_general(
        vb, k_dec, (((0,), (0,)), ((), ())), preferred_element_type=F32)
    return o


GLA_HEADS_PER_STEP = 4


def _gla(pc, w_gate2, b_gate, norm_g, batch, seq):
    t = pc.shape[0]
    heads = norm_g.shape[0] // GLA_DV
    hps = math.gcd(heads, GLA_HEADS_PER_STEP)
    groups = heads // hps
    wg = jnp.pad(w_gate2, ((0, LANES - GLA_GATE_RANK), (0, 0))).astype(BF16)
    nc = seq // GLA_CHUNK
    blk = lambda w, off: pl.BlockSpec((None, GLA_CHUNK, hps * w), lambda b, g, c: (b, c, off + g))
    par = lambda rows, w: pl.BlockSpec((rows, hps * w), lambda b, g, c: (0, g))
    pc3 = pc.reshape(batch, seq, pc.shape[1])
    out = pl.pallas_call(
        _gla_kernel,
        grid=(batch, groups, nc),
        in_specs=[blk(GLA_DK, 4 * groups), blk(GLA_DK, 5 * groups), blk(GLA_DV, 0),
                  pl.BlockSpec((None, GLA_CHUNK, LANES), lambda b, g, c: (b, c, 6 * heads)),
                  blk(GLA_DV, groups), par(LANES, GLA_DK), par(1, GLA_DK), par(1, GLA_DV)],
        out_specs=blk(GLA_DV, 0),
        out_shape=jax.ShapeDtypeStruct((batch, seq, heads * GLA_DV), BF16),
        scratch_shapes=[pltpu.VMEM((hps, GLA_DV, GLA_DK), F32)],
        compiler_params=_cparams("parallel", "parallel", "arbitrary"),
        name="gla",
    )(pc3, pc3, pc3, pc3, pc3, wg, b_gate.reshape(1, -1), norm_g.reshape(1, -1))
    return out.reshape(t, heads * GLA_DV)


_DIMS = {'nn': (((1,), (0,)), ((), ())), 'nt': (((1,), (1,)), ((), ())), 'tn': (((0,), (0,)), ((), ()))}


def _mm(a, b, form):
    return lax.dot_general(a.astype(BF16), b.astype(BF16), _DIMS[form], preferred_element_type=F32)


def _seg_sum(x, width):
    gi = lax.broadcasted_iota(jnp.int32, (LANES, LANES), 0) // width
    gj = lax.broadcasted_iota(jnp.int32, (LANES, LANES), 1) // width
    ones = (gi == gj).astype(F32)
    return jnp.concatenate([_dot_f32(x[:, c:c + LANES], ones) for c in range(0, x.shape[1], LANES)], axis=1)


def _rwkv_prep_kernel(*refs, width, seq, has_vres):
    if has_vres:
        (p_ref, pp_ref, prm_ref, ml_ref, w2_ref, a2_ref, g2_ref, vf_ref, v0_ref, v1_ref, v2_ref,
         r_ref, lw_ref, k_ref, v_ref, kk_ref, bb_ref, bonus_ref, g_ref) = refs
    else:
        (p_ref, pp_ref, prm_ref, ml_ref, w2_ref, a2_ref, g2_ref,
         r_ref, lw_ref, k_ref, v_ref, kk_ref, bb_ref, bonus_ref, g_ref) = refs
    w = width
    p, prm, ml = p_ref[...], prm_ref[...], ml_ref[...]
    tm = p.shape[0]
    above = jnp.where((pl.program_id(0) * tm) % seq == 0, 0.0, pp_ref[7:8, :])
    first = lax.broadcasted_iota(jnp.int32, p.shape, 0) == 0
    pp = jnp.where(first, jnp.broadcast_to(above, p.shape), pltpu.roll(p, 1, axis=0))
    lerp = lambda lo, hi, mu: p[:, lo:hi] + (pp[:, lo:hi] - p[:, lo:hi]) * mu
    r = lerp(0, w, prm[0:1])
    k = lerp(w, 2 * w, prm[1:2])
    v = lerp(2 * w, 3 * w, prm[2:3])
    x_wa = lerp(3 * w, 3 * w + LANES, ml[0:1])
    x_g = lerp(3 * w + LANES, 3 * w + 2 * LANES, ml[1:2])
    w_log = -jax.nn.softplus(-(prm[3:4] + _dot(jnp.tanh(x_wa).astype(BF16), w2_ref[...]))) - 0.5
    lw_ref[...] = -jnp.exp(w_log)
    a = jax.nn.sigmoid(prm[4:5] + _dot(x_wa.astype(BF16), a2_ref[...]))
    g_ref[...] = _dot(jax.nn.sigmoid(x_g).astype(BF16), g2_ref[...])
    if has_vres:
        mix = jax.nn.sigmoid(v0_ref[...] + _dot(_dot(v.astype(BF16), v1_ref[...]).astype(BF16), v2_ref[...]))
        v = v + (vf_ref[...] - v) * mix
    kk = k * prm[5:6]
    kk = kk * lax.rsqrt(jnp.maximum(_seg_sum(kk * kk, RWKV_HS), 1e-24))
    k2 = k * (1.0 + (a - 1.0) * prm[6:7])
    r_ref[...] = r
    k_ref[...] = k2
    v_ref[...] = v
    kk_ref[...] = kk
    bb_ref[...] = kk * a
    bonus_ref[...] = _seg_sum(r * k2 * prm[7:8], RWKV_HS) * v


RWKV_HEADS_PER_STEP = 8


def _rwkv_chunk_kernel(r_ref, lw_ref, k_ref, v_ref, kk_ref, bb_ref, y_ref, s_ref, *, n_chunk):
    c_len, hs = RWKV_CHUNK, RWKV_HS
    rows_n = n_chunk * c_len

    @pl.when(pl.program_id(2) == 0)
    def _():
        s_ref[...] = jnp.zeros_like(s_ref)

    ri = lax.broadcasted_iota(jnp.int32, (rows_n, rows_n), 0)
    ci = lax.broadcasted_iota(jnp.int32, (rows_n, rows_n), 1)
    same = (ri // c_len) == (ci // c_len)
    incl = jnp.logical_and(same, ci <= ri)
    strict = jnp.logical_and(same, ci < ri)
    eye = (ri == ci).astype(F32)
    e_r = lax.broadcasted_iota(jnp.int32, (hs, hs), 0)
    e_c = lax.broadcasted_iota(jnp.int32, (hs, hs), 1)
    sums = _masked_sums(jnp.concatenate([incl, same], axis=0).astype(BF16), lw_ref[...])
    cum_all, cum_c_all = sums[:rows_n], sums[rows_n:]
    heads = range(s_ref.shape[0])
    hd = []
    for hh in heads:
        sl = slice(hh * hs, (hh + 1) * hs)
        r, lw, k, v, kk, bb = (ref[...][:, sl] for ref in (r_ref, lw_ref, k_ref, v_ref, kk_ref, bb_ref))
        cum, cum_c = cum_all[:, sl], cum_c_all[:, sl]
        g_inv, g_end = jnp.exp(-cum), jnp.exp(cum_c - cum)
        kap, rt = kk * jnp.exp(cum - lw), r * jnp.exp(cum)
        bet, kt = bb * g_inv, k * g_inv
        gram = _mm(jnp.concatenate([kap, rt], axis=0), jnp.concatenate([bet, kt], axis=0), 'nt')
        hd.append(dict(v=v, kap=kap, rt=rt, bet_c=bb * g_end, kt_c=k * g_end, cum_c=cum_c,
                       n_m=jnp.where(strict, gram[:rows_n, :rows_n], 0.0),
                       a_kk=jnp.where(strict, gram[:rows_n, rows_n:], 0.0),
                       a_rb=jnp.where(incl, gram[rows_n:, :rows_n], 0.0),
                       a_rk=jnp.where(incl, gram[rows_n:, rows_n:], 0.0)))
    t_inv = [eye - h['n_m'] for h in hd]
    pw = [-h['n_m'] for h in hd]
    for _ in range(int(math.log2(c_len)) - 1):
        pw = [_mm(p, p, 'nn') for p in pw]
        t_inv = [t + _mm(t, p, 'nn') for t, p in zip(t_inv, pw)]
    av = [_mm(jnp.concatenate([h['a_kk'], h['a_rk']], axis=0), h['v'], 'nn') for h in hd]
    z = [_mm(t, jnp.concatenate([h['kap'], a[:rows_n]], axis=1), 'nn') for t, h, a in zip(t_inv, hd, av)]
    az = [_mm(h['a_rb'], zz, 'nn') for h, zz in zip(hd, z)]
    r_p = [h['rt'] - a[:, :hs] for h, a in zip(hd, az)]
    y0 = [a[rows_n:] - b[:, hs:] for a, b in zip(av, az)]
    s = [s_ref[hh] for hh in heads]
    outs = [[] for _ in heads]
    for c in range(n_chunk):
        rows = slice(c * c_len, (c + 1) * c_len)
        for hh in heads:
            h = hd[hh]
            outs[hh].append(_mm(r_p[hh][rows], s[hh], 'nt') + y0[hh][rows])
            zb = _mm(z[hh][rows], h['bet_c'][rows], 'tn')
            g_chunk = jnp.exp(h['cum_c'][c * c_len:c * c_len + 1])
            m = jnp.where(e_r == e_c, jnp.broadcast_to(g_chunk, (hs, hs)), 0.0) - zb[:hs]
            s[hh] = _mm(s[hh], m, 'nn') + _mm(h['v'][rows], h['kt_c'][rows], 'tn') - zb[hs:]
    for hh in heads:
        s_ref[hh] = s[hh]
    y_ref[...] = jnp.concatenate([jnp.concatenate(o, axis=0) for o in outs], axis=1)


def _rwkv_post_kernel(y_ref, bonus_ref, g_ref, lg_ref, lb_ref, o_ref):
    y = y_ref[...]
    mu = _seg_sum(y, RWKV_HS) * (1.0 / RWKV_HS)
    yc = y - mu
    var = _seg_sum(yc * yc, RWKV_HS) * (1.0 / RWKV_HS)
    yn = yc * lax.rsqrt(var + RWKV_GN_EPS) * lg_ref[...] + lb_ref[...]
    o_ref[...] = ((yn + bonus_ref[...]) * g_ref[...]).astype(o_ref.dtype)


def _rwkv(pd, mu_rkv, mu_lora, w0, w2, a0, a2, g2, k_k, k_a, r_k, lnx_g, lnx_b, v_first, vres, batch, seq):
    t, wd = pd.shape
    w = w0.shape[0]
    prm = jnp.stack([mu_rkv[0], mu_rkv[1], mu_rkv[2], w0, a0, k_k, k_a, r_k.reshape(-1)])
    zl = jnp.zeros((RWKV_LORA,), F32)
    ml = jnp.stack([jnp.concatenate([mu_lora[0], mu_lora[1]]), jnp.concatenate([mu_lora[2], zl])])
    zw = jnp.zeros((LANES - RWKV_LORA, w), F32)
    w2p = jnp.concatenate([w2, zw]).astype(BF16)
    a2p = jnp.concatenate([zw, a2]).astype(BF16)
    g2p = jnp.concatenate([g2, zw]).astype(BF16)
    has_vres = vres is not None
    tm = _tile(seq, 256, 8)
    row = lambda width: pl.BlockSpec((tm, width), lambda i: (i, 0))
    full = lambda a: pl.BlockSpec(a.shape, lambda i: (0, 0))
    above = pl.BlockSpec((8, wd), lambda i: (jnp.maximum(i * (tm // 8) - 1, 0), 0))
    args = [pd, pd, prm, ml, w2p, a2p, g2p]
    specs = [row(wd), above, full(prm), full(ml), full(w2p), full(a2p), full(g2p)]
    if has_vres:
        v0, v1, v2 = vres
        v1p = _pad_cols(v1, LANES).astype(BF16)
        v2p = jnp.pad(v2, ((0, LANES - v2.shape[0]), (0, 0))).astype(BF16)
        v0r = v0.reshape(1, w)
        args += [v_first, v0r, v1p, v2p]
        specs += [row(w), full(v0r), full(v1p), full(v2p)]
    r, lw, k2, v, kk, bb, bonus, g = pl.pallas_call(
        functools.partial(_rwkv_prep_kernel, width=w, seq=seq, has_vres=has_vres),
        grid=(t // tm,),
        in_specs=specs,
        out_specs=[row(w)] * 8,
        out_shape=[jax.ShapeDtypeStruct((t, w), F32)] * 8,
        compiler_params=_cparams("parallel"),
        name="rwkv_prep",
    )(*args)

    n_chunk = 4 if seq % (4 * RWKV_CHUNK) == 0 else 1
    rows_n = n_chunk * RWKV_CHUNK
    bw = math.gcd(w, RWKV_HEADS_PER_STEP * RWKV_HS)
    blk = pl.BlockSpec((None, rows_n, bw), lambda b, h, c: (b, c, h))
    r3 = lambda a: a.reshape(batch, seq, w)
    y = pl.pallas_call(
        functools.partial(_rwkv_chunk_kernel, n_chunk=n_chunk),
        grid=(batch, w // bw, seq // rows_n),
        in_specs=[blk] * 6,
        out_specs=blk,
        out_shape=jax.ShapeDtypeStruct((batch, seq, w), F32),
        scratch_shapes=[pltpu.VMEM((bw // RWKV_HS, RWKV_HS, RWKV_HS), F32)],
        compiler_params=_cparams("parallel", "parallel", "arbitrary"),
        name="rwkv_chunk",
    )(r3(r), r3(lw), r3(k2), r3(v), r3(kk), r3(bb))

    lg, lb = lnx_g.reshape(1, w), lnx_b.reshape(1, w)
    out = pl.pallas_call(
        _rwkv_post_kernel,
        grid=(t // tm,),
        in_specs=[row(w), row(w), row(w), full(lg), full(lb)],
        out_specs=row(w),
        out_shape=jax.ShapeDtypeStruct((t, w), BF16),
        compiler_params=_cparams("parallel"),
        name="rwkv_post",
    )(y.reshape(t, w), bonus, g, lg, lb)
    return out, (v_first if has_vres else v)


N_BRANCH = 4


def _merge_kernel(*refs):
    x_ref = refs[0]
    y_refs = refs[1:1 + N_BRANCH]
    wg_refs = refs[1 + N_BRANCH:1 + 2 * N_BRANCH]
    wb_refs = refs[1 + 2 * N_BRANCH:1 + 3 * N_BRANCH]
    o_ref = refs[-1]
    x = x_ref[...]
    acc = None
    for y_ref, wg_ref, wb_ref in zip(y_refs, wg_refs, wb_refs):
        term = jax.nn.sigmoid(_dot(x, wg_ref[...])) * _dot(y_ref[...], wb_ref[...])
        acc = term if acc is None else acc + term
    o_ref[...] = acc.astype(o_ref.dtype)


def _merge(xb, ys, w_gate, w_branch):
    t, d = xb.shape
    w = ys[0].shape[1]
    tm, tn = _tile(t, 512, 8), _tile(d, 256)
    nj = d // tn
    gate_spec = lambda i: pl.BlockSpec((d, tn), lambda r, j: (0, i * nj + j))
    br_spec = lambda i: pl.BlockSpec((None, w, tn), lambda r, j: (i, 0, j))
    return pl.pallas_call(
        _merge_kernel,
        grid=(t // tm, nj),
        in_specs=[pl.BlockSpec((tm, d), lambda r, j: (r, 0))] + [pl.BlockSpec((tm, w), lambda r, j: (r, 0))] * N_BRANCH
        + [gate_spec(i) for i in range(N_BRANCH)] + [br_spec(i) for i in range(N_BRANCH)],
        out_specs=pl.BlockSpec((tm, tn), lambda r, j: (r, j)),
        out_shape=jax.ShapeDtypeStruct((t, d), BF16),
        compiler_params=_cparams("parallel", "parallel"),
        name="merge",
    )(xb, *ys, *([w_gate] * N_BRANCH), *([w_branch] * N_BRANCH))


def _router_kernel(x_ref, w_ref, b_ref, ids_ref, wts_ref, rank_ref, cnt_ref, run_ref, *, n_exp):
    tm = x_ref.shape[0]

    @pl.when(pl.program_id(0) == 0)
    def _():
        run_ref[...] = jnp.zeros_like(run_ref)

    scores = jax.nn.sigmoid(_dot(x_ref[...], w_ref[...]))
    lane = lax.broadcasted_iota(jnp.int32, (tm, LANES), 1)
    work = jnp.where(lane < n_exp, scores + b_ref[...], -jnp.inf)
    chosen = jnp.zeros((tm, LANES), F32)
    sel = jnp.zeros((tm, LANES), F32)
    ids = jnp.zeros((tm, LANES), jnp.int32)
    wts = jnp.zeros((tm, LANES), F32)
    for it in range(TOP_K):
        best = jnp.max(work, axis=1, keepdims=True)
        first = jnp.min(jnp.where(work == best, lane, LANES), axis=1, keepdims=True)
        hit = lane == first
        chosen = jnp.where(hit, scores, chosen)
        sel = jnp.where(hit, 1.0, sel)
        ids = jnp.where(lane == it, first, ids)
        wts = jnp.where(lane == it, jnp.sum(jnp.where(hit, scores, 0.0), axis=1, keepdims=True), wts)
        work = jnp.where(hit, -jnp.inf, work)
    ids_ref[...] = ids
    wts_ref[...] = wts * (ROUTED_SCALE / jnp.sum(chosen, axis=1, keepdims=True))
    ri = lax.broadcasted_iota(jnp.int32, (tm, tm), 0)
    ci = lax.broadcasted_iota(jnp.int32, (tm, tm), 1)
    before = _dot((ci < ri).astype(BF16), sel.astype(BF16))
    run = run_ref[0:1, :]
    rank_ref[...] = (before + run).astype(jnp.int32)
    run = run + jnp.sum(sel, axis=0, keepdims=True)
    run_ref[...] = jnp.broadcast_to(run, run_ref.shape)
    cnt_ref[...] = jnp.broadcast_to(run, cnt_ref.shape).astype(jnp.int32)


def _router(xb, router_w, router_bias):
    t, d = xb.shape
    n_exp = router_w.shape[1]
    assert n_exp < LANES
    tm = _tile(t, 512, 8)
    wr = _pad_cols(router_w, LANES).astype(BF16)
    br = _pad_cols(router_bias.reshape(1, n_exp), LANES)
    tok = pl.BlockSpec((tm, LANES), lambda i: (i, 0))
    tok_shape = lambda dt: jax.ShapeDtypeStruct((t, LANES), dt)
    return pl.pallas_call(
        functools.partial(_router_kernel, n_exp=n_exp),
        grid=(t // tm,),
        in_specs=[pl.BlockSpec((tm, d), lambda i: (i, 0)), pl.BlockSpec((d, LANES), lambda i: (0, 0)),
                  pl.BlockSpec((1, LANES), lambda i: (0, 0))],
        out_specs=[tok, tok, tok, pl.BlockSpec((8, LANES), lambda i: (0, 0))],
        out_shape=[tok_shape(jnp.int32), tok_shape(F32), tok_shape(jnp.int32),
                   jax.ShapeDtypeStruct((8, LANES), jnp.int32)],
        scratch_shapes=[pltpu.VMEM((8, LANES), F32)],
        compiler_params=_cparams("arbitrary"),
        name="router",
    )(xb, wr, br)


def _shared_kernel(x_ref, wgu_ref, wd_ref, o_ref, *, hid):
    gu = _dot(x_ref[...], wgu_ref[...])
    gate = gu[:, :hid]
    h = (gate * jax.nn.sigmoid(gate) * gu[:, hid:]).astype(BF16)
    o_ref[...] = _dot(h, wd_ref[...])


def _shared_expert(xb, w_gate, w_up, w_down):
    t, d = xb.shape
    hid = w_gate.shape[1]
    wgu = jnp.concatenate([w_gate, w_up], axis=1).astype(BF16)
    wd = w_down.astype(BF16)
    tm = _tile(t, 512, 8)
    full = lambda a: pl.BlockSpec(a.shape, lambda i: (0, 0))
    return pl.pallas_call(
        functools.partial(_shared_kernel, hid=hid),
        grid=(t // tm,),
        in_specs=[pl.BlockSpec((tm, d), lambda i: (i, 0)), full(wgu), full(wd)],
        out_specs=pl.BlockSpec((tm, d), lambda i: (i, 0)),
        out_shape=jax.ShapeDtypeStruct((t, d), F32),
        compiler_params=_cparams("parallel"),
        name="shared_expert",
    )(xb, wgu, wd)


MOE_ROWS = 256
SC_WINDOW = 16


def _sc_mesh():
    return plsc.VectorSubcoreMesh(core_axis_name="core", subcore_axis_name="subcore")


def _sc_windows(n_items):
    mesh = _sc_mesh()
    workers = mesh.num_cores * mesh.num_subcores
    n_win = n_items // LANES
    assert n_items % LANES == 0
    return mesh, n_win, -(-n_win // workers)


def _sc_worker(mesh):
    return lax.axis_index("core") * mesh.num_subcores + lax.axis_index("subcore")


def _sc_scatter_rows(x, pos, n_rows):
    t, d = x.shape
    k = pos.shape[0]
    mesh, n_win, per = _sc_windows(t)

    n_sub = LANES // SC_WINDOW

    @pl.kernel(out_type=jax.ShapeDtypeStruct((n_rows, d), x.dtype), mesh=mesh,
               scratch_types=[pltpu.VMEM((k, LANES), jnp.int32), pltpu.VMEM((2, SC_WINDOW, d), x.dtype),
                              pltpu.SemaphoreType.DMA, pltpu.SemaphoreType.DMA((2,))])
    def scatter(x_hbm, pos_hbm, o_hbm, idx_vmem, buf, sem_in, sem_out):
        first = _sc_worker(mesh) * per

        @pl.loop(0, per)
        def _(step):
            win = first + step

            @pl.when(win < n_win)
            def _():
                base = win * LANES
                pltpu.sync_copy(pos_hbm.at[:, pl.ds(base, LANES)], idx_vmem)
                load = lambda j: pltpu.make_async_copy(
                    x_hbm.at[pl.ds(base + j * SC_WINDOW, SC_WINDOW)], buf.at[j % 2], sem_in)
                store = lambda j, kk: pltpu.make_async_copy(
                    buf.at[j % 2], o_hbm.at[idx_vmem[kk, pl.ds(j * SC_WINDOW, SC_WINDOW)]], sem_out.at[j % 2])
                load(0).start()
                for j in range(n_sub):
                    load(j).wait()
                    if j >= 1:
                        for kk in range(k):
                            store(j - 1, kk).wait()
                    if j + 1 < n_sub:
                        load(j + 1).start()
                    for kk in range(k):
                        store(j, kk).start()
                for kk in range(k):
                    store(n_sub - 1, kk).wait()

    return scatter(x, pos)


def _sc_gather_rows(y, idx):
    m = idx.shape[1]
    d = y.shape[1]
    mesh, n_win, per = _sc_windows(m)

    n_sub = LANES // SC_WINDOW

    @pl.kernel(out_type=jax.ShapeDtypeStruct((m, d), y.dtype), mesh=mesh,
               scratch_types=[pltpu.VMEM((1, LANES), jnp.int32), pltpu.VMEM((2, SC_WINDOW, d), y.dtype),
                              pltpu.SemaphoreType.DMA, pltpu.SemaphoreType.DMA((2,))])
    def gather(y_hbm, i_hbm, o_hbm, idx_vmem, buf, sem_in, sem_out):
        first = _sc_worker(mesh) * per

        @pl.loop(0, per)
        def _(step):
            win = first + step

            @pl.when(win < n_win)
            def _():
                base = win * LANES
                pltpu.sync_copy(i_hbm.at[:, pl.ds(base, LANES)], idx_vmem)
                load = lambda j: pltpu.make_async_copy(
                    y_hbm.at[idx_vmem[0, pl.ds(j * SC_WINDOW, SC_WINDOW)]], buf.at[j % 2], sem_in)
                store = lambda j: pltpu.make_async_copy(
                    buf.at[j % 2], o_hbm.at[pl.ds(base + j * SC_WINDOW, SC_WINDOW)], sem_out.at[j % 2])
                load(0).start()
                for j in range(n_sub):
                    load(j).wait()
                    if j >= 1:
                        store(j - 1).wait()
                    if j + 1 < n_sub:
                        load(j + 1).start()
                    store(j).start()
                store(n_sub - 1).wait()

    return gather(y, idx)


def _grouped_kernel(te_ref, nu_ref, x_ref, wg_ref, wu_ref, wd_ref, o_ref, wgu_bf, wd_bf, *, hid):
    i = pl.program_id(0)

    @pl.when(jnp.logical_or(i == 0, te_ref[i] != te_ref[jnp.maximum(i - 1, 0)]))
    def _():
        wgu_bf[:, :hid] = wg_ref[...].astype(BF16)
        wgu_bf[:, hid:] = wu_ref[...].astype(BF16)
        wd_bf[...] = wd_ref[...].astype(BF16)

    @pl.when(i < nu_ref[0])
    def _():
        lo, hi = _unpack_pairs(x_ref[...])
        x = jnp.concatenate([lo.astype(BF16), hi.astype(BF16)], axis=1)
        gu = _dot(x, wgu_bf[...])
        gate = gu[:, :hid]
        h = (gate * jax.nn.sigmoid(gate) * gu[:, hid:]).astype(BF16)
        o_ref[...] = _pack_pairs(_dot(h, wd_bf[...]))


def _grouped_swiglu(xs, tile_expert, n_used, w_gate, w_up, w_down, layer):
    n, half = xs.shape
    d = 2 * half
    hid = w_down.shape[2]
    grid_spec = pltpu.PrefetchScalarGridSpec(
        num_scalar_prefetch=2,
        grid=(n // MOE_ROWS,),
        in_specs=[pl.BlockSpec((MOE_ROWS, half), lambda i, te, nu: (i, 0)),
                  pl.BlockSpec((None, None, d, hid), lambda i, te, nu: (layer, te[i], 0, 0)),
                  pl.BlockSpec((None, None, d, hid), lambda i, te, nu: (layer, te[i], 0, 0)),
                  pl.BlockSpec((None, None, hid, d), lambda i, te, nu: (layer, te[i], 0, 0))],
        out_specs=pl.BlockSpec((MOE_ROWS, half), lambda i, te, nu: (i, 0)),
        scratch_shapes=[pltpu.VMEM((d, 2 * hid), BF16), pltpu.VMEM((hid, d), BF16)],
    )
    return pl.pallas_call(
        functools.partial(_grouped_kernel, hid=hid),
        grid_spec=grid_spec,
        out_shape=jax.ShapeDtypeStruct((n, half), jnp.int32),
        compiler_params=_cparams("arbitrary"),
        name="moe_grouped",
    )(tile_expert, n_used, xs, w_gate, w_up, w_down)


def _combine_ln_kernel(x_ref, yg_ref, w_ref, fs_ref, g_ref, b_ref, o_ref, ob_ref, *, alpha):
    w = w_ref[...]
    acc_lo, acc_hi = None, None
    for k in range(TOP_K):
        lo, hi = _unpack_pairs(yg_ref[k])
        wk = w[:, k:k + 1]
        acc_lo = wk * lo if k == 0 else acc_lo + wk * lo
        acc_hi = wk * hi if k == 0 else acc_hi + wk * hi
    acc = alpha * x_ref[...] + fs_ref[...] + jnp.concatenate([acc_lo, acc_hi], axis=1)
    mu = jnp.mean(acc, axis=-1, keepdims=True)
    xc = acc - mu
    var = jnp.mean(xc * xc, axis=-1, keepdims=True)
    y = xc * lax.rsqrt(var + LN_EPS) * g_ref[...] + b_ref[...]
    o_ref[...] = y
    ob_ref[...] = y.astype(BF16)


def _combine_ln(x, yg, wts, f_shared, g, b, alpha):
    t, d = x.shape
    tm = _tile(t, 64, 8)
    row = lambda width: pl.BlockSpec((tm, width), lambda i: (i, 0))
    vec = pl.BlockSpec((1, d), lambda i: (0, 0))
    return pl.pallas_call(
        functools.partial(_combine_ln_kernel, alpha=alpha),
        grid=(t // tm,),
        in_specs=[row(d), pl.BlockSpec((TOP_K, tm, d // 2), lambda i: (0, i, 0)), row(LANES), row(d), vec, vec],
        out_specs=[row(d), row(d)],
        out_shape=[jax.ShapeDtypeStruct((t, d), F32), jax.ShapeDtypeStruct((t, d), BF16)],
        compiler_params=_cparams("parallel"),
        name="moe_combine_ln",
    )(x, yg, wts, f_shared, g.reshape(1, d), b.reshape(1, d))


def _routed_ln(xf, xpk, ids, wts, rank, counts, w_gate, w_up, w_down, layer, f_shared, g, b, alpha):
    t, d = xf.shape
    n_exp = w_gate.shape[1]
    n_rows = t * TOP_K + n_exp * MOE_ROWS
    n_tiles = n_rows // MOE_ROWS
    cnt = counts[0, :n_exp]
    padded = (cnt + MOE_ROWS - 1) // MOE_ROWS * MOE_ROWS
    ends = jnp.cumsum(padded)
    starts = ends - padded
    tile_start = jnp.arange(n_tiles, dtype=jnp.int32) * MOE_ROWS
    tile_expert = jnp.minimum(jnp.sum((ends[None, :] <= tile_start[:, None]).astype(jnp.int32), axis=1), n_exp - 1)
    n_used = (ends[-1:] // MOE_ROWS).astype(jnp.int32)
    top = ids[:, :TOP_K]
    pos_t = (starts[top] + jnp.take_along_axis(rank, top, axis=1)).astype(jnp.int32).T
    xs = _sc_scatter_rows(xpk, pos_t, n_rows)
    ys = _grouped_swiglu(xs, tile_expert, n_used, w_gate, w_up, w_down, layer)
    yg = _sc_gather_rows(ys, pos_t.reshape(1, TOP_K * t)).reshape(TOP_K, t, d // 2)
    return _combine_ln(xf, yg, wts, f_shared, g, b, alpha)


PROJ_ALIGN = 512


def _w_in_plan(d, n_cols):
    wdt = d // N_BRANCH
    gla_heads = wdt // GLA_DV
    names = (('gate', N_BRANCH * d), ('mla_cq', MLA_Q_RANK), ('mla_ckv', MLA_KV_RANK), ('mla_kr', MLA_ROPE),
             ('dsa_q', wdt), ('dsa_k', DSA_DH), ('dsa_v', DSA_DH),
             ('idx_q', IDX_HEADS * IDX_DIM), ('idx_k', IDX_DIM), ('idx_w', IDX_HEADS),
             ('gla_q', gla_heads * GLA_DK), ('gla_k', gla_heads * GLA_DK), ('gla_v', wdt),
             ('gla_a', GLA_GATE_RANK), ('gla_r', wdt),
             ('rwkv_r', wdt), ('rwkv_k', wdt), ('rwkv_v', wdt),
             ('rwkv_w', RWKV_LORA), ('rwkv_a', RWKV_LORA), ('rwkv_g', RWKV_LORA))
    src, off = {}, 0
    for name, width in names:
        src[name] = (off, width)
        off += width
    assert off == n_cols, (off, n_cols)

    def layout(items):
        copies, dst = [], 0
        for name, slot, scale in items:
            if name is not None:
                copies.append((dst, src[name][0], src[name][1], scale))
            dst += slot
        return copies, -(-dst // PROJ_ALIGN) * PROJ_ALIGN

    half = MLA_ROPE // 2
    kr0 = src['mla_kr'][0]
    lat, lat_w = layout([('mla_cq', MLA_Q_RANK, None), ('mla_ckv', MLA_KV_RANK, None), ('mla_kr', LANES, None),
                         (None, LANES, None)])
    swap_dst = MLA_Q_RANK + MLA_KV_RANK + LANES
    lat += [(swap_dst, kr0 + half, half, None), (swap_dst + half, kr0, half, None)]
    iq0 = src['idx_q'][0]
    dsa, dsa_w = layout([(None, IDX_HEADS * LANES, None), ('dsa_q', wdt, DSA_DH ** -0.5 * LOG2E),
                         ('dsa_k', DSA_DH, None), ('dsa_v', DSA_DH, None), ('idx_k', LANES, None),
                         ('idx_w', LANES, None)])
    dsa += [(h * LANES, iq0 + h * IDX_DIM, IDX_DIM, None) for h in range(IDX_HEADS)]
    gla, gla_w = layout([('gla_v', wdt, None), ('gla_r', wdt, None), ('gla_q', gla_heads * GLA_DK, None),
                         ('gla_k', gla_heads * GLA_DK, None), ('gla_a', LANES, None)])
    rwkv, rwkv_w = layout([('rwkv_r', wdt, None), ('rwkv_k', wdt, None), ('rwkv_v', wdt, None),
                           ('rwkv_w', RWKV_LORA, None), ('rwkv_a', RWKV_LORA, None), ('rwkv_g', LANES, None)])
    return (lat, dsa, gla, rwkv), (lat_w, dsa_w, gla_w, rwkv_w)


def _regroup_kernel(wt_ref, gate_ref, *out_refs, plans, widths, gate_w):
    tr = wt_ref.shape[1]
    gate_ref[...] = wt_ref[:gate_w, :].T.astype(BF16)
    for o_ref, plan, width in zip(out_refs, plans, widths):
        rows, at = [], 0
        for dst, src, n, scale in sorted(plan):
            if dst > at:
                rows.append(jnp.zeros((dst - at, tr), F32))
            piece = wt_ref[src:src + n, :]
            rows.append(piece if scale is None else piece * scale)
            at = dst + n
        if width > at:
            rows.append(jnp.zeros((width - at, tr), F32))
        o_ref[...] = jnp.concatenate(rows, axis=0).T.astype(BF16)


def _regroup_w_in(w_in, layer, d):
    n_cols = w_in.shape[2]
    plans, widths = _w_in_plan(d, n_cols)
    gate_w = N_BRANCH * d
    tr = _tile(d, LANES, LANES)
    out = lambda width: pl.BlockSpec((tr, width), lambda i: (i, 0))
    return pl.pallas_call(
        functools.partial(_regroup_kernel, plans=plans, widths=widths, gate_w=gate_w),
        grid=(d // tr,),
        in_specs=[pl.BlockSpec((None, n_cols, tr), lambda i: (layer, 0, i))],
        out_specs=[out(gate_w)] + [out(wd) for wd in widths],
        out_shape=[jax.ShapeDtypeStruct((d, gate_w), BF16)] + [jax.ShapeDtypeStruct((d, wd), BF16) for wd in widths],
        compiler_params=_cparams("parallel"),
        name="regroup_w_in",
    )(jnp.swapaxes(w_in, 1, 2))


def kernel(x, positions, ln_in_g, ln_in_b, w_in, w_branch, w_out, mla_q_norm, mla_w_uq, mla_kv_norm, mla_w_ukv,
           gla_w_gate2, gla_b_gate, gla_norm_g, rwkv_mu_rkv, rwkv_mu_lora, rwkv_w0, rwkv_w2, rwkv_a0, rwkv_a2,
           rwkv_g2, rwkv_k_k, rwkv_k_a, rwkv_r_k, rwkv_lnx_g, rwkv_lnx_b, rwkv_v0, rwkv_v1, rwkv_v2,
           ln_mix_g, ln_mix_b, router_w, router_bias, exp_w_gate, exp_w_up, exp_w_down,
           sh_w_gate, sh_w_up, sh_w_down, ln_ffn_g, ln_ffn_b):
    batch, seq, d = x.shape
    depth = w_in.shape[0]
    t, wdt = batch * seq, d // N_BRANCH
    alpha = (2 * depth) ** 0.25
    ct, st = _rope_tables(positions)
    xf, xb = _layer_norm(x.reshape(t, d), (), ln_in_g, ln_in_b)
    v_first = None
    for l in range(depth):
        w_gate, w_lat, w_dsa, w_gla, w_rwkv = _regroup_w_in(w_in, l, d)
        lat = _matmul(xb, w_lat, F32)
        pb = _matmul(xb, w_dsa, BF16)
        pc = _matmul(xb, w_gla, F32)
        pd = _matmul(xb, w_rwkv, F32)

        y_mla = _mla(lat, ct, st, mla_q_norm[l], mla_w_uq[l], mla_kv_norm[l], mla_w_ukv[l], batch, seq)
        y_dsa = _dsa(pb, batch, seq, wdt)
        y_gla = _gla(pc, gla_w_gate2[l], gla_b_gate[l], gla_norm_g[l], batch, seq)
        vres = None if l == 0 else (rwkv_v0[l - 1], rwkv_v1[l - 1], rwkv_v2[l - 1])
        y_rwkv, v_first = _rwkv(pd, rwkv_mu_rkv[l], rwkv_mu_lora[l], rwkv_w0[l], rwkv_w2[l], rwkv_a0[l], rwkv_a2[l],
                                rwkv_g2[l], rwkv_k_k[l], rwkv_k_a[l], rwkv_r_k[l], rwkv_lnx_g[l], rwkv_lnx_b[l],
                                v_first, vres, batch, seq)

        merged = _merge(xb, (y_mla, y_dsa, y_gla, y_rwkv), w_gate, w_branch[l].astype(BF16))
        xf, xb, xpk = _layer_norm(xf, (_matmul(merged, w_out[l].astype(BF16), F32),), ln_mix_g[l], ln_mix_b[l], alpha,
                                  packed=True)

        ids, wts, rank, counts = _router(xb, router_w[l], router_bias[l])
        f_shared = _shared_expert(xb, sh_w_gate[l], sh_w_up[l], sh_w_down[l])
        xf, xb = _routed_ln(xf, xpk, ids, wts, rank, counts, exp_w_gate, exp_w_up, exp_w_down, l,
                            f_shared, ln_ffn_g[l], ln_ffn_b[l], alpha)
    return xf.reshape(batch, seq, d)
```
